```python
import math
import jax, jax.numpy as jnp
from jax import lax
import numpy as np


D_MODEL = 1024
BATCH = 8
SEQ = 2048
DEPTH = 1

CHUNK = 64
Q_BLOCK = 128
D_CONV = D_MODEL // 2
CONV_WIDTH = 31
N_DIFF_HEADS = 4
DIFF_HEAD_DIM = 64
VALUE_HEAD_DIM = 2 * DIFF_HEAD_DIM
D_ATTN = N_DIFF_HEADS * VALUE_HEAD_DIM
D_IN = 2 * D_CONV + 3 * D_ATTN
REL_BUCKETS = 32
REL_MAX_DIST = 128
N_EXPERTS = 32
TOP_K = 4
D_FF = D_MODEL
SWIGLU_LIMIT = 7.0
SWIGLU_ALPHA = 1.702
MOE_BLOCK = 128
EPS = 1e-5

kernel_name = "hybrid_conv_diffattn_moe_block"


def rms_norm(x, g):
    xf = x.astype(jnp.float32)
    y = xf * lax.rsqrt(jnp.mean(xf * xf, axis=-1, keepdims=True) + EPS)
    return (y * g.astype(jnp.float32)).astype(x.dtype)


def layer_norm(x, g, b):
    xf = x.astype(jnp.float32)
    mu = jnp.mean(xf, axis=-1, keepdims=True)
    xc = xf - mu
    y = xc * lax.rsqrt(jnp.mean(xc * xc, axis=-1, keepdims=True) + EPS)
    return (y * g.astype(jnp.float32) + b.astype(jnp.float32)).astype(x.dtype)


def t5_bucket(rel):
    nb = REL_BUCKETS // 2
    max_exact = nb // 2
    ret = jnp.where(rel > 0, nb, 0)
    n = jnp.abs(rel)
    nf = jnp.maximum(n, 1).astype(jnp.float32)
    large = max_exact + (jnp.log(nf / max_exact) / math.log(REL_MAX_DIST / max_exact)
                         * (nb - max_exact)).astype(jnp.int32)
    large = jnp.minimum(large, nb - 1)
    return ret + jnp.where(n < max_exact, n, large)


def conv_mixer(a, g, conv_w, conv_b, ln_g, ln_b):
    u = a * jax.nn.sigmoid(g)
    kern = conv_w[:, None, :].astype(u.dtype)
    y = lax.conv_general_dilated(u, kern, window_strides=(1,),
                                 padding=[(CONV_WIDTH - 1, 0)],
                                 dimension_numbers=('NWC', 'WIO', 'NWC'),
                                 feature_group_count=D_CONV)
    y = layer_norm(y + conv_b, ln_g, ln_b)
    return jax.nn.silu(y)


def diff_attention(q, k, v, q_g, k_g, lam_qk, subln_g, rel_bias, lambda_init):
    B, S = q.shape[0], q.shape[1]
    q = rms_norm(q, q_g).transpose(0, 2, 3, 1, 4)
    k = rms_norm(k, k_g).transpose(0, 2, 3, 1, 4)
    v = v.transpose(0, 2, 1, 3)
    lf = lam_qk.astype(jnp.float32)
    lam = jnp.exp(jnp.sum(lf[0] * lf[1])) - jnp.exp(jnp.sum(lf[2] * lf[3])) + lambda_init
    scale = DIFF_HEAD_DIM ** -0.5
    table = rel_bias.astype(jnp.float32)
    outs = []
    for qb in range(S // Q_BLOCK):
        q0 = qb * Q_BLOCK
        end = q0 + Q_BLOCK
        qi = q[:, :, :, q0:end]
        kj = k[:, :, :, :end]
        vj = v[:, :, :end]
        s = jnp.einsum('bhmqd,bhmkd->bhmqk', qi, kj).astype(jnp.float32) * scale
        qpos = jnp.arange(q0, end, dtype=jnp.int32)[:, None]
        kpos = jnp.arange(end, dtype=jnp.int32)[None, :]
        bias = table[t5_bucket(kpos - qpos)].transpose(2, 0, 1)
        s = s + bias[None, :, None]
        allowed = (kpos // CHUNK) <= (qpos // CHUNK)
        s = jnp.where(allowed, s, -jnp.inf)
        p = jax.nn.softmax(s, axis=-1)
        pd = p[:, :, 0] - lam * p[:, :, 1]
        outs.append(jnp.einsum('bhqk,bhkv->bhqv', pd.astype(v.dtype), vj))
    o = jnp.concatenate(outs, axis=2)
    o = rms_norm(o, subln_g) * (1.0 - lambda_init)
    return o.transpose(0, 2, 1, 3).reshape(B, S, D_ATTN)


def moe_ffn(h, router_w, router_b, w_gate_up, b_gate_up, w_down, b_down):
    B, S, D = h.shape
    N = B * S
    xt = h.reshape(N, D)
    logits = (xt @ router_w).astype(jnp.float32) + router_b.astype(jnp.float32)
    top_vals, top_idx = lax.top_k(logits, TOP_K)
    gates = jax.nn.softmax(top_vals, axis=-1)
    NK = N * TOP_K
    e_flat = top_idx.reshape(-1)
    t_flat = jnp.arange(NK, dtype=jnp.int32) // TOP_K
    g_flat = gates.reshape(-1)
    order = jnp.argsort(e_flat)
    e_sorted = e_flat[order]
    counts = jnp.bincount(e_flat, length=N_EXPERTS)
    padded = (counts + MOE_BLOCK - 1) // MOE_BLOCK * MOE_BLOCK
    pad_end = jnp.cumsum(padded)
    pad_start = pad_end - padded
    start = jnp.cumsum(counts) - counts
    rank = jnp.arange(NK, dtype=jnp.int32) - start[e_sorted]
    dest = pad_start[e_sorted] + rank
    n_blocks = -(-NK // MOE_BLOCK) + N_EXPERTS
    L = n_blocks * MOE_BLOCK
    buf_tok = jnp.full((L,), N, dtype=jnp.int32).at[dest].set(t_flat[order])
    buf_gate = jnp.zeros((L,), jnp.float32).at[dest].set(g_flat[order])
    block_start = jnp.arange(n_blocks, dtype=jnp.int32) * MOE_BLOCK
    block_expert = jnp.minimum(jnp.searchsorted(pad_end, block_start, side='right'), N_EXPERTS - 1)
    x_pad = jnp.concatenate([xt, jnp.zeros((1, D), xt.dtype)], axis=0)
    xb = x_pad[buf_tok].reshape(n_blocks, MOE_BLOCK, D)

    def expert_block(args):
        xblk, e = args
        gu = xblk @ w_gate_up[e] + b_gate_up[e]
        gate = jnp.minimum(gu[:, 0::2], SWIGLU_LIMIT)
        lin = jnp.clip(gu[:, 1::2], -SWIGLU_LIMIT, SWIGLU_LIMIT)
        act = gate * jax.nn.sigmoid(SWIGLU_ALPHA * gate) * (lin + 1.0)
        return act @ w_down[e] + b_down[e]

    yb = lax.map(expert_block, (xb, block_expert)).reshape(L, D)
    y = jnp.zeros((N + 1, D), jnp.float32).at[buf_tok].add(yb.astype(jnp.float32) * buf_gate[:, None])
    return y[:N].astype(h.dtype).reshape(B, S, D)


def setup_inputs(seed: int = 0) -> dict:
    key = jax.random.key(seed)
    ks = jax.random.split(key, 21)
    f32 = jnp.float32

    def nrm(k, shape, scale):
        return jax.random.normal(k, shape, f32) * scale

    return {
        "x": nrm(ks[0], (BATCH, SEQ, D_MODEL), 1.0),
        "norm_mix_g": 1.0 + nrm(ks[1], (DEPTH, D_MODEL), 0.02),
        "w_in": nrm(ks[2], (DEPTH, D_MODEL, D_IN), D_MODEL ** -0.5),
        "conv_w": nrm(ks[3], (DEPTH, CONV_WIDTH, D_CONV), CONV_WIDTH ** -0.5),
        "conv_b": nrm(ks[4], (DEPTH, D_CONV), 0.01),
        "conv_ln_g": 1.0 + nrm(ks[5], (DEPTH, D_CONV), 0.02),
        "conv_ln_b": nrm(ks[6], (DEPTH, D_CONV), 0.01),
        "q_norm_g": 1.0 + nrm(ks[7], (DEPTH, 2, DIFF_HEAD_DIM), 0.02),
        "k_norm_g": 1.0 + nrm(ks[8], (DEPTH, 2, DIFF_HEAD_DIM), 0.02),
        "lambda_qk": nrm(ks[9], (DEPTH, 4, DIFF_HEAD_DIM), 0.1),
        "subln_g": 1.0 + nrm(ks[10], (DEPTH, VALUE_HEAD_DIM), 0.02),
        "rel_bias": nrm(ks[11], (REL_BUCKETS, N_DIFF_HEADS), 0.5),
        "w_out": nrm(ks[12], (DEPTH, D_CONV + D_ATTN, D_MODEL), (D_CONV + D_ATTN) ** -0.5),
        "norm_ffn_g": 1.0 + nrm(ks[13], (DEPTH, D_MODEL), 0.02),
        "router_w": nrm(ks[14], (DEPTH, D_MODEL, N_EXPERTS), D_MODEL ** -0.5),
        "router_b": nrm(ks[15], (DEPTH, N_EXPERTS), 0.01),
        "w_gate_up": nrm(ks[16], (DEPTH, N_EXPERTS, D_MODEL, 2 * D_FF), D_MODEL ** -0.5),
        "b_gate_up": nrm(ks[17], (DEPTH, N_EXPERTS, 2 * D_FF), 0.01),
        "w_down": nrm(ks[18], (DEPTH, N_EXPERTS, D_FF, D_MODEL), D_FF ** -0.5),
        "b_down": nrm(ks[19], (DEPTH, N_EXPERTS, D_MODEL), 0.01),
    }


def reference(x, norm_mix_g, w_in, conv_w, conv_b, conv_ln_g, conv_ln_b, q_norm_g, k_norm_g,
              lambda_qk, subln_g, rel_bias, w_out, norm_ffn_g, router_w, router_b,
              w_gate_up, b_gate_up, w_down, b_down):
    B, S, _ = x.shape
    h = x
    for layer in range(DEPTH):
        lambda_init = 0.8 - 0.6 * math.exp(-0.3 * layer)
        u = rms_norm(h, norm_mix_g[layer])
        proj = u @ w_in[layer]
        a, g, q, k, v = jnp.split(
            proj, [D_CONV, 2 * D_CONV, 2 * D_CONV + D_ATTN, 2 * D_CONV + 2 * D_ATTN], axis=-1)
        conv_out = conv_mixer(a, g, conv_w[layer], conv_b[layer], conv_ln_g[layer], conv_ln_b[layer])
        q = q.reshape(B, S, N_DIFF_HEADS, 2, DIFF_HEAD_DIM)
        k = k.reshape(B, S, N_DIFF_HEADS, 2, DIFF_HEAD_DIM)
        v = v.reshape(B, S, N_DIFF_HEADS, VALUE_HEAD_DIM)
        attn_out = diff_attention(q, k, v, q_norm_g[layer], k_norm_g[layer], lambda_qk[layer],
                                  subln_g[layer], rel_bias, lambda_init)
        mixed = jnp.concatenate([conv_out, attn_out], axis=-1)
        h = h + mixed @ w_out[layer]
        hn = rms_norm(h, norm_ffn_g[layer])
        h = h + moe_ffn(hn, router_w[layer], router_b[layer], w_gate_up[layer], b_gate_up[layer],
                        w_down[layer], b_down[layer])
    return h
```

```python
import functools
import math

import jax
import jax.numpy as jnp
import numpy as np
from jax import lax
from jax.experimental import pallas as pl
from jax.experimental.pallas import tpu as pltpu

F32 = jnp.float32
BF16 = jnp.bfloat16
I32 = jnp.int32

CHUNK = 64
CONV_WIDTH = 31
N_HEADS = 4
HEAD_DIM = 64
VALUE_DIM = 2 * HEAD_DIM
REL_BUCKETS = 32
REL_MAX_DIST = 128
N_EXPERTS = 32
TOP_K = 4
SWIGLU_LIMIT = 7.0
SWIGLU_ALPHA = 1.702
EPS = 1e-5
LOG2E = 1.4426950408889634

LANES = 128
SUBLANES = 8
MXU_DIM = 256
VMEM_LIMIT = 56 * 1024 * 1024

NEG_INF = float("-inf")


def _cparams(sem, vmem=VMEM_LIMIT):
    return pltpu.CompilerParams(dimension_semantics=sem, vmem_limit_bytes=vmem)


def _dot(a, b):
    return jnp.dot(a, b, preferred_element_type=F32)


def _dot_nt(a, b):
    return lax.dot_general(a, b, (((1,), (1,)), ((), ())), preferred_element_type=F32)


def _in_proj_kernel(x_ref, g_ref, w_ref, u_ref, q_ref, k_ref, v_ref, *, d_conv, d_attn):
    x = x_ref[...]
    ms = jnp.mean(x * x, axis=-1, keepdims=True)
    y = (x * lax.rsqrt(ms + EPS) * g_ref[...]).astype(BF16)
    proj = _dot(y, w_ref[...])
    a = proj[:, :d_conv]
    g = proj[:, d_conv:2 * d_conv]
    u_ref[...] = a * jax.nn.sigmoid(g)
    o = 2 * d_conv
    q_ref[...] = proj[:, o:o + d_attn]
    k_ref[...] = proj[:, o + d_attn:o + 2 * d_attn]
    v_ref[...] = proj[:, o + 2 * d_attn:o + 3 * d_attn].astype(BF16)


def _in_proj(x2, g, w_bf, d_conv, d_attn, tm):
    n, d = x2.shape
    d_in = w_bf.shape[1]
    row = lambda i: (i, 0)
    fixed = lambda i: (0, 0)
    return pl.pallas_call(
        functools.partial(_in_proj_kernel, d_conv=d_conv, d_attn=d_attn),
        grid=(n // tm,),
        in_specs=[pl.BlockSpec((tm, d), row),
                  pl.BlockSpec((1, d), fixed),
                  pl.BlockSpec((d, d_in), fixed)],
        out_specs=[pl.BlockSpec((tm, d_conv), row),
                   pl.BlockSpec((tm, d_attn), row),
                   pl.BlockSpec((tm, d_attn), row),
                   pl.BlockSpec((tm, d_attn), row)],
        out_shape=[jax.ShapeDtypeStruct((n, d_conv), F32),
                   jax.ShapeDtypeStruct((n, d_attn), F32),
                   jax.ShapeDtypeStruct((n, d_attn), F32),
                   jax.ShapeDtypeStruct((n, d_attn), BF16)],
        compiler_params=_cparams(("parallel",)),
        name="in_proj",
    )(x2, g, w_bf)


CONV_PAD = 32
CONV_SEQ_TILE = 512
CONV_TILE = 64


def _conv_kernel(u_ref, prev_ref, w_ref, cb_ref, lg_ref, lb_ref, o_ref, sh_ref):
    ts, c = u_ref.shape[1], u_ref.shape[2]
    plen = ts + CONV_PAD
    hist = prev_ref[0]
    hist = jnp.where(pl.program_id(1) > 0, hist, jnp.zeros_like(hist))
    sh_ref[0, pl.ds(0, CONV_PAD), :] = hist
    sh_ref[0, pl.ds(CONV_PAD, ts), :] = u_ref[0]
    sh_ref[0, pl.ds(plen, SUBLANES), :] = jnp.zeros((SUBLANES, c), F32)

    bt = CONV_PAD

    def shift(i, _):
        p0 = pl.multiple_of(i * bt, bt)
        win = sh_ref[0, pl.ds(p0, bt + SUBLANES), :]
        for r in range(1, SUBLANES):
            sh_ref[r, pl.ds(p0, bt), :] = win[r:r + bt]
        return 0

    lax.fori_loop(0, plen // bt, shift, 0)

    off0 = CONV_PAD - (CONV_WIDTH - 1)

    def body(i, _):
        t0 = pl.multiple_of(i * CONV_TILE, CONV_TILE)
        acc = jnp.zeros((CONV_TILE, c), F32)
        for j in range(CONV_WIDTH):
            off = off0 + j
            a, r = off // SUBLANES, off % SUBLANES
            start = pl.multiple_of(t0 + a * SUBLANES, SUBLANES)
            acc = acc + sh_ref[r, pl.ds(start, CONV_TILE), :] * w_ref[pl.ds(j, 1), :]
        y = acc + cb_ref[...]
        mu = jnp.mean(y, axis=-1, keepdims=True)
        yc = y - mu
        var = jnp.mean(yc * yc, axis=-1, keepdims=True)
        z = yc * lax.rsqrt(var + EPS) * lg_ref[...] + lb_ref[...]
        o_ref[0, pl.ds(t0, CONV_TILE), :] = (z * jax.nn.sigmoid(z)).astype(o_ref.dtype)
        return 0

    lax.fori_loop(0, ts // CONV_TILE, body, 0)


def _conv(u3, conv_w, conv_b, ln_g, ln_b):
    b, s, c = u3.shape
    ts = min(CONV_SEQ_TILE, s)
    hist_per_tile = ts // CONV_PAD
    fixed = lambda i, j: (0, 0)
    return pl.pallas_call(
        _conv_kernel,
        grid=(b, s // ts),
        in_specs=[pl.BlockSpec((1, ts, c), lambda i, j: (i, j, 0)),
                  pl.BlockSpec((1, CONV_PAD, c),
                               lambda i, j: (i, jnp.maximum(j * hist_per_tile - 1, 0), 0)),
                  pl.BlockSpec((CONV_WIDTH, c), fixed),
                  pl.BlockSpec((1, c), fixed),
                  pl.BlockSpec((1, c), fixed),
                  pl.BlockSpec((1, c), fixed)],
        out_specs=pl.BlockSpec((1, ts, c), lambda i, j: (i, j, 0)),
        out_shape=jax.ShapeDtypeStruct((b, s, c), BF16),
        scratch_shapes=[pltpu.VMEM((SUBLANES, ts + CONV_PAD + SUBLANES, c), F32)],
        compiler_params=_cparams(("parallel", "parallel")),
        name="conv_mixer",
    )(u3, u3, conv_w, conv_b, ln_g, ln_b)


ATT_TILE = 256
FAR_BUCKET = REL_BUCKETS // 2 - 1


def _t5_bucket(rel):
    nb = REL_BUCKETS // 2
    max_exact = nb // 2
    ret = jnp.where(rel > 0, nb, 0)
    n = jnp.abs(rel)
    nf = jnp.maximum(n, 1).astype(jnp.float32)
    large = max_exact + (jnp.log(nf / max_exact) / math.log(REL_MAX_DIST / max_exact)
                         * (nb - max_exact)).astype(jnp.int32)
    large = jnp.minimum(large, nb - 1)
    return ret + jnp.where(n < max_exact, n, large)


def _near_buckets():
    assert ATT_TILE >= REL_MAX_DIST and ATT_TILE % CHUNK == 0
    qpos = jnp.arange(ATT_TILE, dtype=I32)[:, None]
    kpos = jnp.arange(ATT_TILE, dtype=I32)[None, :]
    prev = _t5_bucket(kpos - ATT_TILE - qpos)
    diag = _t5_bucket(kpos - qpos)
    diag = jnp.where(kpos // CHUNK <= qpos // CHUNK, diag, -1)
    return jnp.stack([prev, diag]).astype(I32)


def _attn_kernel(tab_ref, q_ref, k_ref, v_ref, bkt_ref, qg_ref, kg_ref, lqk_ref, sg_ref, o_ref,
                 qz_ref, kn_ref, bias_ref, m_ref, l_ref, acc_ref, *, seq, lambda_init):
    h = pl.program_id(1)
    t = ATT_TILE
    n_tiles = seq // t
    lane = lax.broadcasted_iota(I32, (1, VALUE_DIM), 1)
    first = lane < HEAD_DIM

    far = tab_ref[FAR_BUCKET, h]
    for d in range(2):
        bkt = bkt_ref[d]
        tile = jnp.full((t, t), NEG_INF, F32)
        for b in range(REL_BUCKETS):
            tile = jnp.where(bkt == b, (tab_ref[b, h] - far) * LOG2E, tile)
        bias_ref[d] = tile

    def half_norm(x, g):
        x2 = x * x
        s0 = jnp.sum(jnp.where(first, x2, 0.0), axis=-1, keepdims=True)
        s1 = jnp.sum(jnp.where(first, 0.0, x2), axis=-1, keepdims=True)
        ms = jnp.where(first, s0, s1) * (1.0 / HEAD_DIM)
        return x * lax.rsqrt(ms + EPS) * g

    q_scale = HEAD_DIM ** -0.5 * LOG2E

    def prep(i, _):
        r0 = pl.multiple_of(i * t, t)
        qn = half_norm(q_ref[0, pl.ds(r0, t), :], qg_ref[...]) * q_scale
        qz_ref[0, pl.ds(r0, t), :] = jnp.where(first, qn, 0.0).astype(BF16)
        qz_ref[1, pl.ds(r0, t), :] = jnp.where(first, 0.0, qn).astype(BF16)
        kn_ref[pl.ds(r0, t), :] = half_norm(k_ref[0, pl.ds(r0, t), :], kg_ref[...]).astype(BF16)
        return 0

    lax.fori_loop(0, n_tiles, prep, 0)

    lqk = lqk_ref[...]
    lam = (jnp.exp(jnp.sum(lqk[0:1] * lqk[1:2], axis=-1, keepdims=True))
           - jnp.exp(jnp.sum(lqk[2:3] * lqk[3:4], axis=-1, keepdims=True)) + lambda_init)

    def kv_block(q0, k0, bias_idx):
        kb = kn_ref[pl.ds(k0, t), :]
        vb = v_ref[0, pl.ds(k0, t), :]
        for m in range(2):
            s = _dot_nt(qz_ref[m, pl.ds(q0, t), :], kb)
            if bias_idx is not None:
                s = s + bias_ref[bias_idx]
            m_prev = m_ref[m]
            m_new = jnp.maximum(m_prev, jnp.max(s, axis=-1, keepdims=True))
            p = jnp.exp2(s - jnp.concatenate([m_new] * (t // LANES), axis=1))
            alpha = jnp.exp2(m_prev - m_new)
            psum = p[:, :LANES]
            for c in range(1, t // LANES):
                psum = psum + p[:, c * LANES:(c + 1) * LANES]
            l_ref[m] = alpha * l_ref[m] + psum
            acc_ref[m] = alpha * acc_ref[m] + _dot(p.astype(BF16), vb)
            m_ref[m] = m_new

    def q_tile(i, _):
        q0 = pl.multiple_of(i * t, t)
        m_ref[...] = jnp.full(m_ref.shape, NEG_INF, F32)
        l_ref[...] = jnp.zeros(l_ref.shape, F32)
        acc_ref[...] = jnp.zeros(acc_ref.shape, F32)

        def far_block(j, _):
            kv_block(q0, pl.multiple_of(j * t, t), None)
            return 0

        lax.fori_loop(0, jnp.maximum(i - 1, 0), far_block, 0)

        @pl.when(i >= 1)
        def _():
            kv_block(q0, pl.multiple_of(q0 - t, t), 0)

        kv_block(q0, q0, 1)

        l0 = jnp.sum(l_ref[0], axis=-1, keepdims=True)
        l1 = jnp.sum(l_ref[1], axis=-1, keepdims=True)
        o = acc_ref[0] / l0 - lam * (acc_ref[1] / l1)
        ms = jnp.mean(o * o, axis=-1, keepdims=True)
        o = o * lax.rsqrt(ms + EPS) * sg_ref[...] * (1.0 - lambda_init)
        o_ref[0, pl.ds(q0, t), :] = o.astype(o_ref.dtype)
        return 0

    lax.fori_loop(0, n_tiles, q_tile, 0)


def _attention(q3, k3, v3, rel_bias, q_g, k_g, lam_qk, subln_g, lambda_init):
    b, s, _ = q3.shape
    t = ATT_TILE
    head = lambda i, j: (i, 0, j)
    fixed2 = lambda i, j: (0, 0)
    fixed3 = lambda i, j: (0, 0, 0)
    return pl.pallas_call(
        functools.partial(_attn_kernel, seq=s, lambda_init=lambda_init),
        grid=(b, N_HEADS),
        in_specs=[pl.BlockSpec(memory_space=pltpu.SMEM),
                  pl.BlockSpec((1, s, VALUE_DIM), head),
                  pl.BlockSpec((1, s, VALUE_DIM), head),
                  pl.BlockSpec((1, s, VALUE_DIM), head),
                  pl.BlockSpec((2, t, t), fixed3),
                  pl.BlockSpec((1, VALUE_DIM), fixed2),
                  pl.BlockSpec((1, VALUE_DIM), fixed2),
                  pl.BlockSpec((4, HEAD_DIM), fixed2),
                  pl.BlockSpec((1, VALUE_DIM), fixed2)],
        out_specs=pl.BlockSpec((1, s, VALUE_DIM), head),
        out_shape=jax.ShapeDtypeStruct((b, s, N_HEADS * VALUE_DIM), BF16),
        scratch_shapes=[pltpu.VMEM((2, s, VALUE_DIM), BF16),
                        pltpu.VMEM((s, VALUE_DIM), BF16),
                        pltpu.VMEM((2, t, t), F32),
                        pltpu.VMEM((2, t, LANES), F32),
                        pltpu.VMEM((2, t, LANES), F32),
                        pltpu.VMEM((2, t, VALUE_DIM), F32)],
        compiler_params=_cparams(("parallel", "parallel")),
        name="diff_attn",
    )(rel_bias, q3, k3, v3, _near_buckets(), q_g, k_g, lam_qk, subln_g)


def _out_proj_kernel(x_ref, c_ref, a_ref, wc_ref, wa_ref, g_ref, rwh_ref, rwl_ref, rb_ref, tri_ref,
                     h_ref, hn_ref, idx_ref, gate_ref, rank_ref, cnt_ref, carry_ref):
    @pl.when(pl.program_id(0) == 0)
    def _():
        carry_ref[...] = jnp.zeros_like(carry_ref)

    h = x_ref[...] + _dot(c_ref[...], wc_ref[...]) + _dot(a_ref[...], wa_ref[...])
    h_ref[...] = h
    ms = jnp.mean(h * h, axis=-1, keepdims=True)
    hn = h * lax.rsqrt(ms + EPS) * g_ref[...]
    hn_ref[...] = hn

    hi = hn.astype(BF16)
    lo = (hn - hi.astype(F32)).astype(BF16)
    logits = (_dot(hi, rwh_ref[...]) + (_dot(hi, rwl_ref[...]) + _dot(lo, rwh_ref[...]))) + rb_ref[...]

    tm = logits.shape[0]
    lane = lax.broadcasted_iota(I32, (tm, LANES), 1)
    work = logits
    vals, idxs = [], []
    for _ in range(TOP_K):
        mx = jnp.max(work, axis=-1, keepdims=True)
        ix = jnp.min(jnp.where(work == mx, lane, LANES), axis=-1, keepdims=True)
        vals.append(mx)
        idxs.append(ix)
        work = jnp.where(lane == ix, NEG_INF, work)
    exps = [jnp.exp(v - vals[0]) for v in vals]
    denom = exps[0]
    for e in exps[1:]:
        denom = denom + e

    sel = jnp.zeros((tm, LANES), F32)
    for ix in idxs:
        sel = sel + jnp.where(lane == ix, 1.0, 0.0)
    rank = _dot(tri_ref[...], sel.astype(BF16)) + carry_ref[...]
    carry_ref[...] = carry_ref[...] + jnp.sum(sel, axis=0, keepdims=True)
    cnt_ref[...] = carry_ref[...]

    idx_out = jnp.zeros((tm, LANES), I32)
    gate_out = jnp.zeros((tm, LANES), F32)
    rank_out = jnp.zeros((tm, LANES), F32)
    for k in range(TOP_K):
        rk = jnp.sum(jnp.where(lane == idxs[k], rank, 0.0), axis=-1, keepdims=True)
        idx_out = jnp.where(lane == k, idxs[k], idx_out)
        gate_out = jnp.where(lane == k, exps[k] / denom, gate_out)
        rank_out = jnp.where(lane == k, rk, rank_out)
    idx_ref[...] = idx_out
    gate_ref[...] = gate_out
    rank_ref[...] = rank_out.astype(I32)


def _out_proj(x2, conv_o, attn_o, wc, wa, g, rwh, rwl, rb, tm):
    n, d = x2.shape
    dc, da = conv_o.shape[1], attn_o.shape[1]
    row = lambda i: (i, 0)
    fixed = lambda i: (0, 0)
    tri = jnp.tril(jnp.ones((tm, tm), F32), -1).astype(BF16)
    return pl.pallas_call(
        _out_proj_kernel,
        grid=(n // tm,),
        in_specs=[pl.BlockSpec((tm, d), row),
                  pl.BlockSpec((tm, dc), row),
                  pl.BlockSpec((tm, da), row),
                  pl.BlockSpec((dc, d), fixed),
                  pl.BlockSpec((da, d), fixed),
                  pl.BlockSpec((1, d), fixed),
                  pl.BlockSpec((d, LANES), fixed),
                  pl.BlockSpec((d, LANES), fixed),
                  pl.BlockSpec((1, LANES), fixed),
                  pl.BlockSpec((tm, tm), fixed)],
        out_specs=[pl.BlockSpec((tm, d), row),
                   pl.BlockSpec((tm, d), row),
                   pl.BlockSpec((tm, LANES), row),
                   pl.BlockSpec((tm, LANES), row),
                   pl.BlockSpec((tm, LANES), row),
                   pl.BlockSpec((1, LANES), fixed)],
        out_shape=[jax.ShapeDtypeStruct((n, d), F32),
                   jax.ShapeDtypeStruct((n, d), F32),
                   jax.ShapeDtypeStruct((n, LANES), I32),
                   jax.ShapeDtypeStruct((n, LANES), F32),
                   jax.ShapeDtypeStruct((n, LANES), I32),
                   jax.ShapeDtypeStruct((1, LANES), F32)],
        scratch_shapes=[pltpu.VMEM((1, LANES), F32)],
        compiler_params=_cparams(("arbitrary",)),
        name="out_proj_router",
    )(x2, conv_o, attn_o, wc, wa, g, rwh, rwl, rb, tri)


MOE_BLOCK = 256


def _moe_kernel(be_ref, nused_ref, tok_next_ref, tok_first_ref, hn_hbm,
                wgu_ref, bgu_ref, wd_ref, bd_ref, perm_ref, y_ref,
                xbuf, sem, wgu_bf, wd_bf):
    i = pl.program_id(0)
    bm = MOE_BLOCK
    n_used = nused_ref[0]
    d_ff = wd_ref.shape[1]

    def row_copy(tok, slot, r):
        return pltpu.make_async_copy(hn_hbm.at[pl.ds(tok, 1)], xbuf.at[slot, pl.ds(r, 1)], sem.at[slot])

    def gather(tok_ref, slot):
        def body(r, _):
            row_copy(tok_ref[0, 0, r], slot, r).start()
            return 0
        lax.fori_loop(0, bm, body, 0, unroll=8)

    @pl.when(jnp.logical_and(i == 0, n_used > 0))
    def _():
        gather(tok_first_ref, 0)

    @pl.when(i + 1 < n_used)
    def _():
        gather(tok_next_ref, (i + 1) % 2)

    @pl.when(i < n_used)
    def _():
        changed = jnp.logical_or(i == 0, be_ref[i] != be_ref[jnp.maximum(i - 1, 0)])

        @pl.when(changed)
        def _():
            for g in range(2 * d_ff // MXU_DIM):
                cols = pl.ds(g * MXU_DIM, MXU_DIM)
                wgu_bf[:, cols] = _dot(wgu_ref[0, :, cols].astype(BF16), perm_ref[...]).astype(BF16)
            wd_bf[...] = wd_ref[0].astype(BF16)

        slot = i % 2
        pltpu.make_async_copy(hn_hbm.at[pl.ds(0, bm)], xbuf.at[slot], sem.at[slot]).wait()
        x = xbuf[slot].astype(BF16)
        gu = _dot(x, wgu_bf[...]) + bgu_ref[0]
        acts = []
        for g in range(d_ff // LANES):
            gate = jnp.minimum(gu[:, g * MXU_DIM:g * MXU_DIM + LANES], SWIGLU_LIMIT)
            lin = jnp.clip(gu[:, g * MXU_DIM + LANES:(g + 1) * MXU_DIM], -SWIGLU_LIMIT, SWIGLU_LIMIT)
            acts.append((gate * jax.nn.sigmoid(SWIGLU_ALPHA * gate) * (lin + 1.0)).astype(BF16))
        act = jnp.concatenate(acts, axis=1)
        y_ref[...] = _dot(act, wd_bf[...]) + bd_ref[0]

    @pl.when(i >= n_used)
    def _():
        y_ref[...] = jnp.zeros_like(y_ref)


def _deinterleave_perm():
    src = np.arange(MXU_DIM)
    dst = np.where(src % 2 == 0, src // 2, LANES + src // 2)
    p = np.zeros((MXU_DIM, MXU_DIM), np.float32)
    p[src, dst] = 1.0
    return jnp.asarray(p, dtype=BF16)


def _moe(block_expert, n_used, tok_sorted, hn, w_gate_up, bgu_p, w_down, b_down):
    n_exp, d, d_gu = w_gate_up.shape
    d_ff = w_down.shape[1]
    bm = MOE_BLOCK
    nblk = tok_sorted.shape[0]
    grid_spec = pltpu.PrefetchScalarGridSpec(
        num_scalar_prefetch=2,
        grid=(nblk,),
        in_specs=[
            pl.BlockSpec((1, 1, bm), lambda i, be, nu: (jnp.minimum(i + 1, nblk - 1), 0, 0),
                         memory_space=pltpu.SMEM),
            pl.BlockSpec((1, 1, bm), lambda i, be, nu: (0, 0, 0), memory_space=pltpu.SMEM),
            pl.BlockSpec(memory_space=pl.ANY),
            pl.BlockSpec((1, d, d_gu), lambda i, be, nu: (be[i], 0, 0)),
            pl.BlockSpec((1, 1, d_gu), lambda i, be, nu: (be[i], 0, 0)),
            pl.BlockSpec((1, d_ff, d), lambda i, be, nu: (be[i], 0, 0)),
            pl.BlockSpec((1, 1, d), lambda i, be, nu: (be[i], 0, 0)),
            pl.BlockSpec((MXU_DIM, MXU_DIM), lambda i, be, nu: (0, 0)),
        ],
        out_specs=pl.BlockSpec((bm, d), lambda i, be, nu: (i, 0)),
        scratch_shapes=[pltpu.VMEM((2, bm, d), F32),
                        pltpu.SemaphoreType.DMA((2,)),
                        pltpu.VMEM((d, d_gu), BF16),
                        pltpu.VMEM((d_ff, d), BF16)],
    )
    return pl.pallas_call(
        _moe_kernel,
        grid_spec=grid_spec,
        out_shape=jax.ShapeDtypeStruct((nblk * bm, d), F32),
        compiler_params=_cparams(("arbitrary",)),
        name="moe_experts",
    )(block_expert, n_used, tok_sorted, tok_sorted, hn, w_gate_up, bgu_p, w_down, b_down,
      _deinterleave_perm())


COMBINE_TILE = 128


def _combine_kernel(pos_next_ref, pos_first_ref, h_ref, gate_ref, y_hbm, o_ref, ybuf, sem):
    i = pl.program_id(0)
    n = pl.num_programs(0)
    tc = COMBINE_TILE

    def gather(pos_ref, slot):
        def body(r, _):
            for k in range(TOP_K):
                pltpu.make_async_copy(y_hbm.at[pl.ds(pos_ref[0, 0, r * TOP_K + k], 1)],
                                      ybuf.at[slot, k, pl.ds(r, 1)], sem.at[slot]).start()
            return 0
        lax.fori_loop(0, tc, body, 0, unroll=2)

    @pl.when(i == 0)
    def _():
        gather(pos_first_ref, 0)

    @pl.when(i + 1 < n)
    def _():
        gather(pos_next_ref, (i + 1) % 2)

    slot = i % 2
    for k in range(TOP_K):
        pltpu.make_async_copy(y_hbm.at[pl.ds(0, tc)], ybuf.at[slot, k], sem.at[slot]).wait()
    gates = gate_ref[...]
    out = h_ref[...]
    for k in range(TOP_K):
        out = out + gates[:, k:k + 1] * ybuf[slot, k]
    o_ref[...] = out


def _combine(pos, h, gates, yb):
    n, d = h.shape
    tc = COMBINE_TILE
    nt = n // tc
    pos3 = pos.reshape(nt, 1, tc * TOP_K)
    row = lambda i: (i, 0)
    return pl.pallas_call(
        _combine_kernel,
        grid=(nt,),
        in_specs=[pl.BlockSpec((1, 1, tc * TOP_K), lambda i: (jnp.minimum(i + 1, nt - 1), 0, 0),
                               memory_space=pltpu.SMEM),
                  pl.BlockSpec((1, 1, tc * TOP_K), lambda i: (0, 0, 0), memory_space=pltpu.SMEM),
                  pl.BlockSpec((tc, d), row),
                  pl.BlockSpec((tc, LANES), row),
                  pl.BlockSpec(memory_space=pl.ANY)],
        out_specs=pl.BlockSpec((tc, d), row),
        out_shape=jax.ShapeDtypeStruct((n, d), F32),
        scratch_shapes=[pltpu.VMEM((2, TOP_K, tc, d), F32),
                        pltpu.SemaphoreType.DMA((2,))],
        compiler_params=_cparams(("arbitrary",)),
        name="moe_combine",
    )(pos3, pos3, h, gates, yb)


def _layer(h3, layer, norm_mix_g, w_in, conv_w, conv_b, conv_ln_g, conv_ln_b, q_norm_g, k_norm_g,
           lambda_qk, subln_g, rel_bias, w_out, norm_ffn_g, router_w, router_b,
           w_gate_up, b_gate_up, w_down, b_down):
    b, s, d = h3.shape
    n = b * s
    d_conv = conv_w.shape[-1]
    d_attn = N_HEADS * VALUE_DIM
    lambda_init = 0.8 - 0.6 * math.exp(-0.3 * layer)
    x2 = h3.reshape(n, d)

    u, q, k, v = _in_proj(x2, norm_mix_g.reshape(1, d), w_in.astype(BF16), d_conv, d_attn, tm=512)
    conv_o = _conv(u.reshape(b, s, d_conv), conv_w, conv_b.reshape(1, d_conv),
                   conv_ln_g.reshape(1, d_conv), conv_ln_b.reshape(1, d_conv))
    attn_o = _attention(q.reshape(b, s, d_attn), k.reshape(b, s, d_attn), v.reshape(b, s, d_attn),
                        rel_bias, q_norm_g.reshape(1, VALUE_DIM), k_norm_g.reshape(1, VALUE_DIM),
                        lambda_qk, subln_g.reshape(1, VALUE_DIM), lambda_init)

    n_exp = router_w.shape[1]
    rw = jnp.zeros((d, LANES), F32).at[:, :n_exp].set(router_w)
    rwh = rw.astype(BF16)
    rwl = (rw - rwh.astype(F32)).astype(BF16)
    rb = jnp.full((1, LANES), NEG_INF, F32).at[0, :n_exp].set(router_b)
    w_out_bf = w_out.astype(BF16)
    hres, hn, idx, gates, rank, cnt = _out_proj(
        x2, conv_o.reshape(n, d_conv), attn_o.reshape(n, d_attn),
        w_out_bf[:d_conv], w_out_bf[d_conv:], norm_ffn_g.reshape(1, d), rwh, rwl, rb, tm=256)

    bm = MOE_BLOCK
    nblk = n * TOP_K // bm + n_exp
    counts = cnt[0, :n_exp].astype(I32)
    padded = (counts + bm - 1) // bm * bm
    pad_end = jnp.cumsum(padded)
    pad_start = pad_end - padded
    pos = pad_start[idx[:, :TOP_K]] + rank[:, :TOP_K]
    tok = jnp.broadcast_to(jnp.arange(n, dtype=I32)[:, None], (n, TOP_K))
    tok_sorted = jnp.zeros((nblk * bm,), I32).at[pos.reshape(-1)].set(tok.reshape(-1))
    block_start = jnp.arange(nblk, dtype=I32) * bm
    block_expert = jnp.minimum(jnp.searchsorted(pad_end, block_start, side="right"),
                               n_exp - 1).astype(I32)
    n_used = (pad_end[-1:] // bm).astype(I32)

    d_ff = w_down.shape[1]
    bgu_p = b_gate_up.reshape(n_exp, d_ff // LANES, LANES, 2).transpose(0, 1, 3, 2).reshape(n_exp, 1, 2 * d_ff)
    yb = _moe(block_expert, n_used, tok_sorted.reshape(nblk, 1, bm), hn, w_gate_up, bgu_p,
              w_down, b_down.reshape(n_exp, 1, d))
    out = _combine(pos, hres, gates, yb)
    return out.reshape(b, s, d)


def kernel(x, norm_mix_g, w_in, conv_w, conv_b, conv_ln_g, conv_ln_b, q_norm_g, k_norm_g, lambda_qk,
           subln_g, rel_bias, w_out, norm_ffn_g, router_w, router_b, w_gate_up, b_gate_up, w_down,
           b_down):
    h = x
    for layer in range(norm_mix_g.shape[0]):
        h = _layer(h, layer, norm_mix_g[layer], w_in[layer], conv_w[layer], conv_b[layer],
                   conv_ln_g[layer], conv_ln_b[layer], q_norm_g[layer], k_norm_g[layer],
                   lambda_qk[layer], subln_g[layer], rel_bias, w_out[layer], norm_ffn_g[layer],
                   router_w[layer], router_b[layer], w_gate_up[layer], b_gate_up[layer],
                   w_down[layer], b_down[layer])
    return h
```

```python
import functools
import math

import jax
import jax.numpy as jnp
import numpy as np
from jax import lax
from jax.experimental import pallas as pl
from jax.experimental.pallas import tpu as pltpu

F32 = jnp.float32
BF16 = jnp.bfloat16
I32 = jnp.int32

CHUNK = 64
CONV_WIDTH = 31
N_HEADS = 4
HEAD_DIM = 64
VALUE_DIM = 2 * HEAD_DIM
REL_BUCKETS = 32
REL_MAX_DIST = 128
N_EXPERTS = 32
TOP_K = 4
SWIGLU_LIMIT = 7.0
SWIGLU_ALPHA = 1.702
EPS = 1e-5
LOG2E = 1.4426950408889634

LANES = 128
SUBLANES = 8
MXU_DIM = 256
VMEM_LIMIT = 56 * 1024 * 1024

NEG_INF = float("-inf")


def _cparams(sem, vmem=VMEM_LIMIT):
    return pltpu.CompilerParams(dimension_semantics=sem, vmem_limit_bytes=vmem)


def _dot(a, b):
    return jnp.dot(a, b, preferred_element_type=F32)


def _dot_nt(a, b):
    return lax.dot_general(a, b, (((1,), (1,)), ((), ())), preferred_element_type=F32)


def _in_proj_kernel(x_ref, g_ref, w_ref, u_ref, q_ref, k_ref, v_ref, *, d_conv, d_attn):
    x = x_ref[...]
    ms = jnp.mean(x * x, axis=-1, keepdims=True)
    y = (x * lax.rsqrt(ms + EPS) * g_ref[...]).astype(BF16)
    proj = _dot(y, w_ref[...])
    a = proj[:, :d_conv]
    g = proj[:, d_conv:2 * d_conv]
    u_ref[...] = a * jax.nn.sigmoid(g)
    o = 2 * d_conv
    q_ref[...] = proj[:, o:o + d_attn]
    k_ref[...] = proj[:, o + d_attn:o + 2 * d_attn]
    v_ref[...] = proj[:, o + 2 * d_attn:o + 3 * d_attn].astype(BF16)


def _in_proj(x2, g, w_bf, d_conv, d_attn, tm):
    n, d = x2.shape
    d_in = w_bf.shape[1]
    row = lambda i: (i, 0)
    fixed = lambda i: (0, 0)
    return pl.pallas_call(
        functools.partial(_in_proj_kernel, d_conv=d_conv, d_attn=d_attn),
        grid=(n // tm,),
        in_specs=[pl.BlockSpec((tm, d), row),
                  pl.BlockSpec((1, d), fixed),
                  pl.BlockSpec((d, d_in), fixed)],
        out_specs=[pl.BlockSpec((tm, d_conv), row),
                   pl.BlockSpec((tm, d_attn), row),
                   pl.BlockSpec((tm, d_attn), row),
                   pl.BlockSpec((tm, d_attn), row)],
        out_shape=[jax.ShapeDtypeStruct((n, d_conv), F32),
                   jax.ShapeDtypeStruct((n, d_attn), F32),
                   jax.ShapeDtypeStruct((n, d_attn), F32),
                   jax.ShapeDtypeStruct((n, d_attn), BF16)],
        compiler_params=_cparams(("parallel",)),
        name="in_proj",
    )(x2, g, w_bf)


CONV_PAD = 32
CONV_SEQ_TILE = 512
CONV_TILE = 64


def _conv_kernel(u_ref, prev_ref, w_ref, cb_ref, lg_ref, lb_ref, o_ref, sh_ref):
    ts, c = u_ref.shape[1], u_ref.shape[2]
    plen = ts + CONV_PAD
    hist = prev_ref[0]
    hist = jnp.where(pl.program_id(1) > 0, hist, jnp.zeros_like(hist))
    sh_ref[0, pl.ds(0, CONV_PAD), :] = hist
    sh_ref[0, pl.ds(CONV_PAD, ts), :] = u_ref[0]
    sh_ref[0, pl.ds(plen, SUBLANES), :] = jnp.zeros((SUBLANES, c), F32)

    bt = CONV_PAD

    def shift(i, _):
        p0 = pl.multiple_of(i * bt, bt)
        win = sh_ref[0, pl.ds(p0, bt + SUBLANES), :]
        for r in range(1, SUBLANES):
            sh_ref[r, pl.ds(p0, bt), :] = win[r:r + bt]
        return 0

    lax.fori_loop(0, plen // bt, shift, 0)

    off0 = CONV_PAD - (CONV_WIDTH - 1)

    def body(i, _):
        t0 = pl.multiple_of(i * CONV_TILE, CONV_TILE)
        acc = jnp.zeros((CONV_TILE, c), F32)
        for j in range(CONV_WIDTH):
            off = off0 + j
            a, r = off // SUBLANES, off % SUBLANES
            start = pl.multiple_of(t0 + a * SUBLANES, SUBLANES)
            acc = acc + sh_ref[r, pl.ds(start, CONV_TILE), :] * w_ref[pl.ds(j, 1), :]
        y = acc + cb_ref[...]
        mu = jnp.mean(y, axis=-1, keepdims=True)
        yc = y - mu
        var = jnp.mean(yc * yc, axis=-1, keepdims=True)
        z = yc * lax.rsqrt(var + EPS) * lg_ref[...] + lb_ref[...]
        o_ref[0, pl.ds(t0, CONV_TILE), :] = (z * jax.nn.sigmoid(z)).astype(o_ref.dtype)
        return 0

    lax.fori_loop(0, ts // CONV_TILE, body, 0)


def _conv(u3, conv_w, conv_b, ln_g, ln_b):
    b, s, c = u3.shape
    ts = min(CONV_SEQ_TILE, s)
    hist_per_tile = ts // CONV_PAD
    fixed = lambda i, j: (0, 0)
    return pl.pallas_call(
        _conv_kernel,
        grid=(b, s // ts),
        in_specs=[pl.BlockSpec((1, ts, c), lambda i, j: (i, j, 0)),
                  pl.BlockSpec((1, CONV_PAD, c),
                               lambda i, j: (i, jnp.maximum(j * hist_per_tile - 1, 0), 0)),
                  pl.BlockSpec((CONV_WIDTH, c), fixed),
                  pl.BlockSpec((1, c), fixed),
                  pl.BlockSpec((1, c), fixed),
                  pl.BlockSpec((1, c), fixed)],
        out_specs=pl.BlockSpec((1, ts, c), lambda i, j: (i, j, 0)),
        out_shape=jax.ShapeDtypeStruct((b, s, c), BF16),
        scratch_shapes=[pltpu.VMEM((SUBLANES, ts + CONV_PAD + SUBLANES, c), F32)],
        compiler_params=_cparams(("parallel", "parallel")),
        name="conv_mixer",
    )(u3, u3, conv_w, conv_b, ln_g, ln_b)


ATT_TILE = 256
FAR_BUCKET = REL_BUCKETS // 2 - 1


def _t5_bucket(rel):
    nb = REL_BUCKETS // 2
    max_exact = nb // 2
    ret = jnp.where(rel > 0, nb, 0)
    n = jnp.abs(rel)
    nf = jnp.maximum(n, 1).astype(jnp.float32)
    large = max_exact + (jnp.log(nf / max_exact) / math.log(REL_MAX_DIST / max_exact)
                         * (nb - max_exact)).astype(jnp.int32)
    large = jnp.minimum(large, nb - 1)
    return ret + jnp.where(n < max_exact, n, large)


def _near_buckets():
    assert ATT_TILE >= REL_MAX_DIST and ATT_TILE % CHUNK == 0
    qpos = jnp.arange(ATT_TILE, dtype=I32)[:, None]
    kpos = jnp.arange(ATT_TILE, dtype=I32)[None, :]
    prev = _t5_bucket(kpos - ATT_TILE - qpos)
    diag = _t5_bucket(kpos - qpos)
    diag = jnp.where(kpos // CHUNK <= qpos // CHUNK, diag, -1)
    return jnp.stack([prev, diag]).astype(I32)


def _attn_kernel(tab_ref, q_ref, k_ref, v_ref, bkt_ref, bd_ref, qg_ref, kg_ref, lqk_ref, sg_ref, o_ref,
                 qz_ref, kn_ref, bias_ref, *, seq, lambda_init):
    h = pl.program_id(0)
    t = ATT_TILE
    n_tiles = seq // t
    lane = lax.broadcasted_iota(I32, (1, VALUE_DIM), 1)
    first = lane < HEAD_DIM

    @pl.when(pl.program_id(1) == 0)
    def _():
        far = tab_ref[FAR_BUCKET, h]
        for d in range(2):
            bkt = bkt_ref[d]
            tile = jnp.full((t, t), NEG_INF, F32)
            for b in range(REL_BUCKETS):
                tile = jnp.where(bkt == b, (tab_ref[b, h] - far) * LOG2E, tile)
            bias_ref[d] = tile

    def half_norm(x, g):
        x2 = x * x
        hi = x2.astype(BF16)
        lo = (x2 - hi.astype(F32)).astype(BF16)
        ms = (_dot(hi, bd_ref[...]) + _dot(lo, bd_ref[...])) * (1.0 / HEAD_DIM)
        return x * lax.rsqrt(ms + EPS) * g

    q_scale = HEAD_DIM ** -0.5 * LOG2E

    def prep(i, _):
        r0 = pl.multiple_of(i * t, t)
        qn = half_norm(q_ref[0, pl.ds(r0, t), :], qg_ref[...]) * q_scale
        qz_ref[0, pl.ds(r0, t), :] = jnp.where(first, qn, 0.0).astype(BF16)
        qz_ref[1, pl.ds(r0, t), :] = jnp.where(first, 0.0, qn).astype(BF16)
        kn_ref[pl.ds(r0, t), :] = half_norm(k_ref[0, pl.ds(r0, t), :], kg_ref[...]).astype(BF16)
        return 0

    lax.fori_loop(0, n_tiles, prep, 0)

    lqk = lqk_ref[...]
    lam = (jnp.exp(jnp.sum(lqk[0:1] * lqk[1:2], axis=-1, keepdims=True))
           - jnp.exp(jnp.sum(lqk[2:3] * lqk[3:4], axis=-1, keepdims=True)) + lambda_init)

    for i in range(n_tiles):
        q0, kend = i * t, (i + 1) * t
        keys = kn_ref[0:kend, :]
        vals = v_ref[0, 0:kend, :]
        maps = []
        for m in range(2):
            s = _dot_nt(qz_ref[m, q0:q0 + t, :], keys)
            parts = [s[:, kend - t:] + bias_ref[1]]
            if i >= 1:
                parts.insert(0, s[:, kend - 2 * t:kend - t] + bias_ref[0])
            if i >= 2:
                parts.insert(0, s[:, :kend - 2 * t])
            s = jnp.concatenate(parts, axis=1) if len(parts) > 1 else parts[0]
            p = jnp.exp2(s - jnp.max(s, axis=-1, keepdims=True))
            l = jnp.sum(p, axis=-1, keepdims=True)
            maps.append(_dot(p.astype(BF16), vals) / l)
        o = maps[0] - lam * maps[1]
        ms = jnp.mean(o * o, axis=-1, keepdims=True)
        o = o * lax.rsqrt(ms + EPS) * sg_ref[...] * (1.0 - lambda_init)
        o_ref[0, q0:q0 + t, :] = o.astype(o_ref.dtype)


def _attention(q3, k3, v3, rel_bias, q_g, k_g, lam_qk, subln_g, lambda_init):
    b, s, _ = q3.shape
    t = ATT_TILE
    head = lambda j, i: (i, 0, j)
    fixed2 = lambda j, i: (0, 0)
    fixed3 = lambda j, i: (0, 0, 0)
    half = np.arange(VALUE_DIM) // HEAD_DIM
    blockdiag = jnp.asarray(half[:, None] == half[None, :], dtype=BF16)
    return pl.pallas_call(
        functools.partial(_attn_kernel, seq=s, lambda_init=lambda_init),
        grid=(N_HEADS, b),
        in_specs=[pl.BlockSpec(memory_space=pltpu.SMEM),
                  pl.BlockSpec((1, s, VALUE_DIM), head),
                  pl.BlockSpec((1, s, VALUE_DIM), head),
                  pl.BlockSpec((1, s, VALUE_DIM), head),
                  pl.BlockSpec((2, t, t), fixed3),
                  pl.BlockSpec((VALUE_DIM, VALUE_DIM), fixed2),
                  pl.BlockSpec((1, VALUE_DIM), fixed2),
                  pl.BlockSpec((1, VALUE_DIM), fixed2),
                  pl.BlockSpec((4, HEAD_DIM), fixed2),
                  pl.BlockSpec((1, VALUE_DIM), fixed2)],
        out_specs=pl.BlockSpec((1, s, VALUE_DIM), head),
        out_shape=jax.ShapeDtypeStruct((b, s, N_HEADS * VALUE_DIM), BF16),
        scratch_shapes=[pltpu.VMEM((2, s, VALUE_DIM), BF16),
                        pltpu.VMEM((s, VALUE_DIM), BF16),
                        pltpu.VMEM((2, t, t), F32)],
        compiler_params=_cparams(("arbitrary", "arbitrary")),
        name="diff_attn",
    )(rel_bias, q3, k3, v3, _near_buckets(), blockdiag, q_g, k_g, lam_qk, subln_g)


def _out_proj_kernel(x_ref, c_ref, a_ref, wc_ref, wa_ref, g_ref, rwh_ref, rwl_ref, rb_ref, tri_ref,
                     h_ref, hn_ref, idx_ref, gate_ref, rank_ref, cnt_ref, carry_ref):
    @pl.when(pl.program_id(0) == 0)
    def _():
        carry_ref[...] = jnp.zeros_like(carry_ref)

    h = x_ref[...] + _dot(c_ref[...], wc_ref[...]) + _dot(a_ref[...], wa_ref[...])
    h_ref[...] = h
    ms = jnp.mean(h * h, axis=-1, keepdims=True)
    hn = h * lax.rsqrt(ms + EPS) * g_ref[...]
    hn_ref[...] = hn

    hi = hn.astype(BF16)
    lo = (hn - hi.astype(F32)).astype(BF16)
    logits = (_dot(hi, rwh_ref[...]) + (_dot(hi, rwl_ref[...]) + _dot(lo, rwh_ref[...]))) + rb_ref[...]

    tm = logits.shape[0]
    lane = lax.broadcasted_iota(I32, (tm, LANES), 1)
    work = logits
    vals, idxs = [], []
    for _ in range(TOP_K):
        mx = jnp.max(work, axis=-1, keepdims=True)
        ix = jnp.min(jnp.where(work == mx, lane, LANES), axis=-1, keepdims=True)
        vals.append(mx)
        idxs.append(ix)
        work = jnp.where(lane == ix, NEG_INF, work)
    exps = [jnp.exp(v - vals[0]) for v in vals]
    denom = exps[0]
    for e in exps[1:]:
        denom = denom + e

    sel = jnp.zeros((tm, LANES), F32)
    for ix in idxs:
        sel = sel + jnp.where(lane == ix, 1.0, 0.0)
    rank = _dot(tri_ref[...], sel.astype(BF16)) + carry_ref[...]
    carry_ref[...] = carry_ref[...] + jnp.sum(sel, axis=0, keepdims=True)
    cnt_ref[...] = carry_ref[...]

    idx_out = jnp.zeros((tm, LANES), I32)
    gate_out = jnp.zeros((tm, LANES), F32)
    rank_out = jnp.zeros((tm, LANES), F32)
    for k in range(TOP_K):
        rk = jnp.sum(jnp.where(lane == idxs[k], rank, 0.0), axis=-1, keepdims=True)
        idx_out = jnp.where(lane == k, idxs[k], idx_out)
        gate_out = jnp.where(lane == k, exps[k] / denom, gate_out)
        rank_out = jnp.where(lane == k, rk, rank_out)
    idx_ref[...] = idx_out
    gate_ref[...] = gate_out
    rank_ref[...] = rank_out.astype(I32)


def _out_proj(x2, conv_o, attn_o, wc, wa, g, rwh, rwl, rb, tm):
    n, d = x2.shape
    dc, da = conv_o.shape[1], attn_o.shape[1]
    row = lambda i: (i, 0)
    fixed = lambda i: (0, 0)
    tri = jnp.tril(jnp.ones((tm, tm), F32), -1).astype(BF16)
    return pl.pallas_call(
        _out_proj_kernel,
        grid=(n // tm,),
        in_specs=[pl.BlockSpec((tm, d), row),
                  pl.BlockSpec((tm, dc), row),
                  pl.BlockSpec((tm, da), row),
                  pl.BlockSpec((dc, d), fixed),
                  pl.BlockSpec((da, d), fixed),
                  pl.BlockSpec((1, d), fixed),
                  pl.BlockSpec((d, LANES), fixed),
                  pl.BlockSpec((d, LANES), fixed),
                  pl.BlockSpec((1, LANES), fixed),
                  pl.BlockSpec((tm, tm), fixed)],
        out_specs=[pl.BlockSpec((tm, d), row),
                   pl.BlockSpec((tm, d), row),
                   pl.BlockSpec((tm, LANES), row),
                   pl.BlockSpec((tm, LANES), row),
                   pl.BlockSpec((tm, LANES), row),
                   pl.BlockSpec((1, LANES), fixed)],
        out_shape=[jax.ShapeDtypeStruct((n, d), F32),
                   jax.ShapeDtypeStruct((n, d), F32),
                   jax.ShapeDtypeStruct((n, LANES), I32),
                   jax.ShapeDtypeStruct((n, LANES), F32),
                   jax.ShapeDtypeStruct((n, LANES), I32),
                   jax.ShapeDtypeStruct((1, LANES), F32)],
        scratch_shapes=[pltpu.VMEM((1, LANES), F32)],
        compiler_params=_cparams(("arbitrary",)),
        name="out_proj_router",
    )(x2, conv_o, attn_o, wc, wa, g, rwh, rwl, rb, tri)


MOE_BLOCK = 256


def _moe_kernel(be_ref, first_ref, wslot_ref, nxt_ref, nused_ref,
                tok_next_ref, tok_first_ref, hn_hbm, wgu_hbm, wd_hbm, bgu_ref, bd_ref, perm_ref,
                y_ref, xbuf, xsem, wgu_f32, wd_f32, wsem, wgu_bf, wd_bf, x_bf, act_bf):
    i = pl.program_id(0)
    nblk = pl.num_programs(0)
    bm = MOE_BLOCK
    n_used = nused_ref[0]
    d_ff = wd_bf.shape[0]
    n_groups = d_ff // LANES

    def row_copy(tok, slot, r):
        return pltpu.make_async_copy(hn_hbm.at[pl.ds(tok, 1)], xbuf.at[slot, pl.ds(r, 1)], xsem.at[slot])

    def weight_copies(e, slot):
        return (pltpu.make_async_copy(wgu_hbm.at[e], wgu_f32.at[slot], wsem.at[0, slot]),
                pltpu.make_async_copy(wd_hbm.at[e], wd_f32.at[slot], wsem.at[1, slot]))

    def wait_rows(slot):
        pltpu.make_async_copy(hn_hbm.at[pl.ds(0, bm)], xbuf.at[slot], xsem.at[slot]).wait()

    @pl.when(i == 0)
    def _():
        for c in weight_copies(be_ref[0], 0):
            c.start()
        for r in range(bm):
            row_copy(tok_first_ref[0, 0, r], 0, r).start()

    active = i < n_used

    @pl.when(jnp.logical_and(active, first_ref[i] == 1))
    def _():
        ws = wslot_ref[i]
        for c in weight_copies(be_ref[i], ws):
            c.wait()
        for g in range(n_groups):
            cols = pl.ds(g * MXU_DIM, MXU_DIM)
            wgu_bf[:, cols] = _dot(wgu_f32[ws, :, cols].astype(BF16), perm_ref[...]).astype(BF16)
        wd_bf[...] = wd_f32[ws].astype(BF16)

        @pl.when(nxt_ref[i] >= 0)
        def _():
            for c in weight_copies(nxt_ref[i], 1 - ws):
                c.start()

    def activate(g, gu):
        gu = gu + bgu_ref[0, :, g * MXU_DIM:(g + 1) * MXU_DIM]
        gate = jnp.minimum(gu[:, :LANES], SWIGLU_LIMIT)
        lin = jnp.clip(gu[:, LANES:], -SWIGLU_LIMIT, SWIGLU_LIMIT)
        act_bf[:, g * LANES:(g + 1) * LANES] = (
            gate * jax.nn.sigmoid(SWIGLU_ALPHA * gate) * (lin + 1.0)).astype(BF16)

    def block(slot):
        wait_rows(slot)
        x_bf[...] = xbuf[slot].astype(BF16)
        rows_per_group = bm // n_groups
        pending = None
        for g in range(n_groups):
            for r in range(g * rows_per_group, (g + 1) * rows_per_group):
                row_copy(tok_next_ref[0, 0, r], 1 - slot, r).start()
            gu = _dot(x_bf[...], wgu_bf[:, g * MXU_DIM:(g + 1) * MXU_DIM])
            if pending is not None:
                activate(*pending)
            pending = (g, gu)
        activate(*pending)
        y_ref[...] = _dot(act_bf[...], wd_bf[...]) + bd_ref[0]

        @pl.when(i + 1 == nblk)
        def _():
            wait_rows(1 - slot)

    for parity in range(2):
        @pl.when(jnp.logical_and(active, i % 2 == parity))
        def _(parity=parity):
            block(parity)

    @pl.when(i >= n_used)
    def _():
        y_ref[...] = jnp.zeros_like(y_ref)

        @pl.when(i == n_used)
        def _():
            wait_rows(i % 2)


def _deinterleave_perm():
    src = np.arange(MXU_DIM)
    dst = np.where(src % 2 == 0, src // 2, LANES + src // 2)
    p = np.zeros((MXU_DIM, MXU_DIM), np.float32)
    p[src, dst] = 1.0
    return jnp.asarray(p, dtype=BF16)


def _moe(plan, tok_sorted, hn, w_gate_up, bgu_p, w_down, b_down):
    n_exp, d, d_gu = w_gate_up.shape
    d_ff = w_down.shape[1]
    bm = MOE_BLOCK
    nblk = tok_sorted.shape[0]
    by_expert = lambda i, be, *_: (be[i], 0, 0)
    grid_spec = pltpu.PrefetchScalarGridSpec(
        num_scalar_prefetch=5,
        grid=(nblk,),
        in_specs=[
            pl.BlockSpec((1, 1, bm), lambda i, *_: (jnp.minimum(i + 1, nblk - 1), 0, 0),
                         memory_space=pltpu.SMEM),
            pl.BlockSpec((1, 1, bm), lambda i, *_: (0, 0, 0), memory_space=pltpu.SMEM),
            pl.BlockSpec(memory_space=pl.ANY),
            pl.BlockSpec(memory_space=pl.ANY),
            pl.BlockSpec(memory_space=pl.ANY),
            pl.BlockSpec((1, 1, d_gu), by_expert),
            pl.BlockSpec((1, 1, d), by_expert),
            pl.BlockSpec((MXU_DIM, MXU_DIM), lambda i, *_: (0, 0)),
        ],
        out_specs=pl.BlockSpec((bm, d), lambda i, *_: (i, 0)),
        scratch_shapes=[pltpu.VMEM((2, bm, d), F32),
                        pltpu.SemaphoreType.DMA((2,)),
                        pltpu.VMEM((2, d, d_gu), F32),
                        pltpu.VMEM((2, d_ff, d), F32),
                        pltpu.SemaphoreType.DMA((2, 2)),
                        pltpu.VMEM((d, d_gu), BF16),
                        pltpu.VMEM((d_ff, d), BF16),
                        pltpu.VMEM((bm, d), BF16),
                        pltpu.VMEM((bm, d_ff), BF16)],
    )
    return pl.pallas_call(
        _moe_kernel,
        grid_spec=grid_spec,
        out_shape=jax.ShapeDtypeStruct((nblk * bm, d), F32),
        compiler_params=_cparams(("arbitrary",)),
        name="moe_experts",
    )(*plan, tok_sorted, tok_sorted, hn, w_gate_up, w_down, bgu_p, b_down, _deinterleave_perm())


COMBINE_TILE = 128


def _combine_kernel(pos_next_ref, pos_first_ref, h_ref, gate_ref, y_hbm, o_ref, ybuf, sem):
    i = pl.program_id(0)
    n = pl.num_programs(0)
    tc = COMBINE_TILE

    def gather(pos_ref, slot):
        for r in range(tc):
            for k in range(TOP_K):
                pltpu.make_async_copy(y_hbm.at[pl.ds(pos_ref[0, 0, r * TOP_K + k], 1)],
                                      ybuf.at[slot, k, pl.ds(r, 1)], sem.at[slot]).start()

    @pl.when(i == 0)
    def _():
        gather(pos_first_ref, 0)

    def step(slot):
        @pl.when(i + 1 < n)
        def _():
            gather(pos_next_ref, 1 - slot)

        for k in range(TOP_K):
            pltpu.make_async_copy(y_hbm.at[pl.ds(0, tc)], ybuf.at[slot, k], sem.at[slot]).wait()
        gates = gate_ref[...]
        out = h_ref[...]
        for k in range(TOP_K):
            out = out + gates[:, k:k + 1] * ybuf[slot, k]
        o_ref[...] = out

    for parity in range(2):
        @pl.when(i % 2 == parity)
        def _(parity=parity):
            step(parity)


def _combine(pos, h, gates, yb):
    n, d = h.shape
    tc = COMBINE_TILE
    nt = n // tc
    pos3 = pos.reshape(nt, 1, tc * TOP_K)
    row = lambda i: (i, 0)
    return pl.pallas_call(
        _combine_kernel,
        grid=(nt,),
        in_specs=[pl.BlockSpec((1, 1, tc * TOP_K), lambda i: (jnp.minimum(i + 1, nt - 1), 0, 0),
                               memory_space=pltpu.SMEM),
                  pl.BlockSpec((1, 1, tc * TOP_K), lambda i: (0, 0, 0), memory_space=pltpu.SMEM),
                  pl.BlockSpec((tc, d), row),
                  pl.BlockSpec((tc, LANES), row),
                  pl.BlockSpec(memory_space=pl.ANY)],
        out_specs=pl.BlockSpec((tc, d), row),
        out_shape=jax.ShapeDtypeStruct((n, d), F32),
        scratch_shapes=[pltpu.VMEM((2, TOP_K, tc, d), F32),
                        pltpu.SemaphoreType.DMA((2,))],
        compiler_params=_cparams(("arbitrary",)),
        name="moe_combine",
    )(pos3, pos3, h, gates, yb)


def _layer(h3, layer, norm_mix_g, w_in, conv_w, conv_b, conv_ln_g, conv_ln_b, q_norm_g, k_norm_g,
           lambda_qk, subln_g, rel_bias, w_out, norm_ffn_g, router_w, router_b,
           w_gate_up, b_gate_up, w_down, b_down):
    b, s, d = h3.shape
    n = b * s
    d_conv = conv_w.shape[-1]
    d_attn = N_HEADS * VALUE_DIM
    lambda_init = 0.8 - 0.6 * math.exp(-0.3 * layer)
    x2 = h3.reshape(n, d)

    u, q, k, v = _in_proj(x2, norm_mix_g.reshape(1, d), w_in.astype(BF16), d_conv, d_attn, tm=512)
    conv_o = _conv(u.reshape(b, s, d_conv), conv_w, conv_b.reshape(1, d_conv),
                   conv_ln_g.reshape(1, d_conv), conv_ln_b.reshape(1, d_conv))
    attn_o = _attention(q.reshape(b, s, d_attn), k.reshape(b, s, d_attn), v.reshape(b, s, d_attn),
                        rel_bias, q_norm_g.reshape(1, VALUE_DIM), k_norm_g.reshape(1, VALUE_DIM),
                        lambda_qk, subln_g.reshape(1, VALUE_DIM), lambda_init)

    n_exp = router_w.shape[1]
    rw = jnp.zeros((d, LANES), F32).at[:, :n_exp].set(router_w)
    rwh = rw.astype(BF16)
    rwl = (rw - rwh.astype(F32)).astype(BF16)
    rb = jnp.full((1, LANES), NEG_INF, F32).at[0, :n_exp].set(router_b)
    w_out_bf = w_out.astype(BF16)
    hres, hn, idx, gates, rank, cnt = _out_proj(
        x2, conv_o.reshape(n, d_conv), attn_o.reshape(n, d_attn),
        w_out_bf[:d_conv], w_out_bf[d_conv:], norm_ffn_g.reshape(1, d), rwh, rwl, rb, tm=256)

    bm = MOE_BLOCK
    nblk = n * TOP_K // bm + n_exp
    counts = cnt[0, :n_exp].astype(I32)
    padded = (counts + bm - 1) // bm * bm
    pad_end = jnp.cumsum(padded)
    pad_start = pad_end - padded
    pos = pad_start[idx[:, :TOP_K]] + rank[:, :TOP_K]
    tok = jnp.broadcast_to(jnp.arange(n, dtype=I32)[:, None], (n, TOP_K))
    tok_sorted = jnp.zeros((nblk * bm,), I32).at[pos.reshape(-1)].set(tok.reshape(-1))
    block_start = jnp.arange(nblk, dtype=I32) * bm
    block_expert = jnp.minimum(jnp.sum(block_start[:, None] >= pad_end[None, :], axis=1),
                               n_exp - 1).astype(I32)
    n_used = (pad_end[-1:] // bm).astype(I32)
    blk = jnp.arange(nblk, dtype=I32)
    first = jnp.logical_and(
        jnp.concatenate([jnp.ones((1,), bool), block_expert[1:] != block_expert[:-1]]),
        blk < n_used[0]).astype(I32)
    wslot = ((jnp.cumsum(first) - 1) % 2).astype(I32)
    eid = jnp.arange(n_exp, dtype=I32)
    later_used = jnp.logical_and(eid[None, :] > eid[:, None], counts[None, :] > 0)
    next_used = jnp.min(jnp.where(later_used, eid[None, :], n_exp), axis=1)
    next_used = jnp.where(next_used == n_exp, -1, next_used).astype(I32)
    plan = (block_expert, first, wslot, next_used[block_expert], n_used)

    d_ff = w_down.shape[1]
    bgu_p = b_gate_up.reshape(n_exp, d_ff // LANES, LANES, 2).transpose(0, 1, 3, 2).reshape(n_exp, 1, 2 * d_ff)
    yb = _moe(plan, tok_sorted.reshape(nblk, 1, bm), hn, w_gate_up, bgu_p,
              w_down, b_down.reshape(n_exp, 1, d))
    out = _combine(pos, hres, gates, yb)
    return out.reshape(b, s, d)


def kernel(x, norm_mix_g, w_in, conv_w, conv_b, conv_ln_g, conv_ln_b, q_norm_g, k_norm_g, lambda_qk,
           subln_g, rel_bias, w_out, norm_ffn_g, router_w, router_b, w_gate_up, b_gate_up, w_down,
           b_down):
    h = x
    for layer in range(norm_mix_g.shape[0]):
        h = _layer(h, layer, norm_mix_g[layer], w_in[layer], conv_w[layer], conv_b[layer],
                   conv_ln_g[layer], conv_ln_b[layer], q_norm_g[layer], k_norm_g[layer],
                   lambda_qk[layer], subln_g[layer], rel_bias, w_out[layer], norm_ffn_g[layer],
                   router_w[layer], router_b[layer], w_gate_up[layer], b_gate_up[layer],
                   w_down[layer], b_down[layer])
    return h
```

```python
import functools
import math

import jax
import jax.numpy as jnp
import numpy as np
from jax import lax
from jax.experimental import pallas as pl
from jax.experimental.pallas import tpu as pltpu
from jax.experimental.pallas import tpu_sc as plsc

F32 = jnp.float32
BF16 = jnp.bfloat16
I32 = jnp.int32

CHUNK = 64
CONV_WIDTH = 31
N_HEADS = 4
HEAD_DIM = 64
VALUE_DIM = 2 * HEAD_DIM
REL_BUCKETS = 32
REL_MAX_DIST = 128
N_EXPERTS = 32
TOP_K = 4
SWIGLU_LIMIT = 7.0
SWIGLU_ALPHA = 1.702
EPS = 1e-5
LOG2E = 1.4426950408889634

LANES = 128
SUBLANES = 8
MXU_DIM = 256
VMEM_LIMIT = 56 * 1024 * 1024

NEG_INF = float("-inf")


def _cparams(sem, vmem=VMEM_LIMIT):
    return pltpu.CompilerParams(dimension_semantics=sem, vmem_limit_bytes=vmem)


def _dot(a, b):
    return jnp.dot(a, b, preferred_element_type=F32)


def _dot_nt(a, b):
    return lax.dot_general(a, b, (((1,), (1,)), ((), ())), preferred_element_type=F32)


def _store_rows(ref, lead, x):
    rows, k = x.shape[0], x.shape[1] // LANES
    for j in range(k):
        ref[(*lead, pl.ds(j, rows, stride=k), slice(None))] = x[:, j * LANES:(j + 1) * LANES]


def _load_row_tile(ref, lead, rows, k, j):
    return ref[(*lead, pl.ds(j, rows, stride=k), slice(None))]


def _in_proj_kernel(x_ref, g_ref, w_ref, u_ref, q_ref, k_ref, v_ref, *, d_conv, d_attn):
    x = x_ref[...]
    ms = jnp.mean(x * x, axis=-1, keepdims=True)
    y = (x * lax.rsqrt(ms + EPS) * g_ref[...]).astype(BF16)
    proj = _dot(y, w_ref[...])
    a = proj[:, :d_conv]
    g = proj[:, d_conv:2 * d_conv]
    u_ref[...] = a * jax.nn.sigmoid(g)
    o = 2 * d_conv
    q_ref[...] = proj[:, o:o + d_attn]
    k_ref[...] = proj[:, o + d_attn:o + 2 * d_attn]
    v_ref[...] = proj[:, o + 2 * d_attn:o + 3 * d_attn].astype(BF16)


def _in_proj(x2, g, w_bf, d_conv, d_attn, tm):
    n, d = x2.shape
    d_in = w_bf.shape[1]
    row = lambda i: (i, 0)
    fixed = lambda i: (0, 0)
    return pl.pallas_call(
        functools.partial(_in_proj_kernel, d_conv=d_conv, d_attn=d_attn),
        grid=(n // tm,),
        in_specs=[pl.BlockSpec((tm, d), row),
                  pl.BlockSpec((1, d), fixed),
                  pl.BlockSpec((d, d_in), fixed)],
        out_specs=[pl.BlockSpec((tm, d_conv), row),
                   pl.BlockSpec((tm, d_attn), row),
                   pl.BlockSpec((tm, d_attn), row),
                   pl.BlockSpec((tm, d_attn), row)],
        out_shape=[jax.ShapeDtypeStruct((n, d_conv), F32),
                   jax.ShapeDtypeStruct((n, d_attn), F32),
                   jax.ShapeDtypeStruct((n, d_attn), F32),
                   jax.ShapeDtypeStruct((n, d_attn), BF16)],
        compiler_params=_cparams(("parallel",)),
        name="in_proj",
    )(x2, g, w_bf)


CONV_PAD = 32
CONV_SEQ_TILE = 512
CONV_TILE = 64


def _conv_kernel(u_ref, prev_ref, w_ref, cb_ref, lg_ref, lb_ref, o_ref, sh_ref):
    ts, c = u_ref.shape[1], u_ref.shape[2]
    plen = ts + CONV_PAD
    hist = prev_ref[0]
    hist = jnp.where(pl.program_id(1) > 0, hist, jnp.zeros_like(hist))
    sh_ref[0, pl.ds(0, CONV_PAD), :] = hist
    sh_ref[0, pl.ds(CONV_PAD, ts), :] = u_ref[0]
    sh_ref[0, pl.ds(plen, SUBLANES), :] = jnp.zeros((SUBLANES, c), F32)

    bt = CONV_PAD

    def shift(i, _):
        p0 = pl.multiple_of(i * bt, bt)
        win = sh_ref[0, pl.ds(p0, bt + SUBLANES), :]
        for r in range(1, SUBLANES):
            sh_ref[r, pl.ds(p0, bt), :] = win[r:r + bt]
        return 0

    lax.fori_loop(0, plen // bt, shift, 0)

    off0 = CONV_PAD - (CONV_WIDTH - 1)

    def body(i, _):
        t0 = pl.multiple_of(i * CONV_TILE, CONV_TILE)
        acc = jnp.zeros((CONV_TILE, c), F32)
        for j in range(CONV_WIDTH):
            off = off0 + j
            a, r = off // SUBLANES, off % SUBLANES
            start = pl.multiple_of(t0 + a * SUBLANES, SUBLANES)
            acc = acc + sh_ref[r, pl.ds(start, CONV_TILE), :] * w_ref[pl.ds(j, 1), :]
        y = acc + cb_ref[...]
        mu = jnp.mean(y, axis=-1, keepdims=True)
        yc = y - mu
        var = jnp.mean(yc * yc, axis=-1, keepdims=True)
        z = yc * lax.rsqrt(var + EPS) * lg_ref[...] + lb_ref[...]
        o_ref[0, pl.ds(t0, CONV_TILE), :] = (z * jax.nn.sigmoid(z)).astype(o_ref.dtype)
        return 0

    lax.fori_loop(0, ts // CONV_TILE, body, 0)


def _conv(u3, conv_w, conv_b, ln_g, ln_b):
    b, s, c = u3.shape
    ts = min(CONV_SEQ_TILE, s)
    hist_per_tile = ts // CONV_PAD
    fixed = lambda i, j: (0, 0)
    return pl.pallas_call(
        _conv_kernel,
        grid=(b, s // ts),
        in_specs=[pl.BlockSpec((1, ts, c), lambda i, j: (i, j, 0)),
                  pl.BlockSpec((1, CONV_PAD, c),
                               lambda i, j: (i, jnp.maximum(j * hist_per_tile - 1, 0), 0)),
                  pl.BlockSpec((CONV_WIDTH, c), fixed),
                  pl.BlockSpec((1, c), fixed),
                  pl.BlockSpec((1, c), fixed),
                  pl.BlockSpec((1, c), fixed)],
        out_specs=pl.BlockSpec((1, ts, c), lambda i, j: (i, j, 0)),
        out_shape=jax.ShapeDtypeStruct((b, s, c), BF16),
        scratch_shapes=[pltpu.VMEM((SUBLANES, ts + CONV_PAD + SUBLANES, c), F32)],
        compiler_params=_cparams(("parallel", "parallel")),
        name="conv_mixer",
    )(u3, u3, conv_w, conv_b, ln_g, ln_b)


ATT_TILE = 256
FAR_BUCKET = REL_BUCKETS // 2 - 1


def _t5_bucket(rel):
    nb = REL_BUCKETS // 2
    max_exact = nb // 2
    ret = jnp.where(rel > 0, nb, 0)
    n = jnp.abs(rel)
    nf = jnp.maximum(n, 1).astype(jnp.float32)
    large = max_exact + (jnp.log(nf / max_exact) / math.log(REL_MAX_DIST / max_exact)
                         * (nb - max_exact)).astype(jnp.int32)
    large = jnp.minimum(large, nb - 1)
    return ret + jnp.where(n < max_exact, n, large)


def _near_buckets():
    assert ATT_TILE >= REL_MAX_DIST and ATT_TILE % CHUNK == 0
    qpos = jnp.arange(ATT_TILE, dtype=I32)[:, None]
    kpos = jnp.arange(ATT_TILE, dtype=I32)[None, :]
    prev = _t5_bucket(kpos - ATT_TILE - qpos)
    diag = _t5_bucket(kpos - qpos)
    diag = jnp.where(kpos // CHUNK <= qpos // CHUNK, diag, -1)
    return jnp.stack([prev, diag]).astype(I32)


def _attn_kernel(tab_ref, q_ref, k_ref, v_ref, bkt_ref, bd_ref, qg_ref, kg_ref, lqk_ref, sg_ref, o_ref,
                 qz_ref, kn_ref, bias_ref, *, seq, lambda_init):
    h = pl.program_id(0)
    t = ATT_TILE
    n_tiles = seq // t
    lane = lax.broadcasted_iota(I32, (1, VALUE_DIM), 1)
    first = lane < HEAD_DIM

    @pl.when(pl.program_id(1) == 0)
    def _():
        far = tab_ref[FAR_BUCKET, h]
        for d in range(2):
            bkt = bkt_ref[d]
            tile = jnp.full((t, t), NEG_INF, F32)
            for b in range(REL_BUCKETS):
                tile = jnp.where(bkt == b, (tab_ref[b, h] - far) * LOG2E, tile)
            bias_ref[d] = tile

    def half_norm(x, g):
        x2 = x * x
        hi = x2.astype(BF16)
        lo = (x2 - hi.astype(F32)).astype(BF16)
        ms = (_dot(hi, bd_ref[...]) + _dot(lo, bd_ref[...])) * (1.0 / HEAD_DIM)
        return x * lax.rsqrt(ms + EPS) * g

    q_scale = HEAD_DIM ** -0.5 * LOG2E

    def prep(i, _):
        r0 = pl.multiple_of(i * t, t)
        qn = half_norm(q_ref[0, pl.ds(r0, t), :], qg_ref[...]) * q_scale
        qz_ref[0, pl.ds(r0, t), :] = jnp.where(first, qn, 0.0).astype(BF16)
        qz_ref[1, pl.ds(r0, t), :] = jnp.where(first, 0.0, qn).astype(BF16)
        kn_ref[pl.ds(r0, t), :] = half_norm(k_ref[0, pl.ds(r0, t), :], kg_ref[...]).astype(BF16)
        return 0

    lax.fori_loop(0, n_tiles, prep, 0)

    lqk = lqk_ref[...]
    lam = (jnp.exp(jnp.sum(lqk[0:1] * lqk[1:2], axis=-1, keepdims=True))
           - jnp.exp(jnp.sum(lqk[2:3] * lqk[3:4], axis=-1, keepdims=True)) + lambda_init)

    for i in range(n_tiles):
        q0, kend = i * t, (i + 1) * t
        keys = kn_ref[0:kend, :]
        vals = v_ref[0, 0:kend, :]
        maps = []
        for m in range(2):
            s = _dot_nt(qz_ref[m, q0:q0 + t, :], keys)
            parts = [s[:, kend - t:] + bias_ref[1]]
            if i >= 1:
                parts.insert(0, s[:, kend - 2 * t:kend - t] + bias_ref[0])
            if i >= 2:
                parts.insert(0, s[:, :kend - 2 * t])
            s = jnp.concatenate(parts, axis=1) if len(parts) > 1 else parts[0]
            p = jnp.exp2(s - jnp.max(s, axis=-1, keepdims=True))
            l = jnp.sum(p, axis=-1, keepdims=True)
            maps.append(_dot(p.astype(BF16), vals) / l)
        o = maps[0] - lam * maps[1]
        ms = jnp.mean(o * o, axis=-1, keepdims=True)
        o = o * lax.rsqrt(ms + EPS) * sg_ref[...] * (1.0 - lambda_init)
        o_ref[0, q0:q0 + t, :] = o.astype(o_ref.dtype)


def _attention(q3, k3, v3, rel_bias, q_g, k_g, lam_qk, subln_g, lambda_init):
    b, s, _ = q3.shape
    t = ATT_TILE
    head = lambda j, i: (i, 0, j)
    fixed2 = lambda j, i: (0, 0)
    fixed3 = lambda j, i: (0, 0, 0)
    half = np.arange(VALUE_DIM) // HEAD_DIM
    blockdiag = jnp.asarray(half[:, None] == half[None, :], dtype=BF16)
    return pl.pallas_call(
        functools.partial(_attn_kernel, seq=s, lambda_init=lambda_init),
        grid=(N_HEADS, b),
        in_specs=[pl.BlockSpec(memory_space=pltpu.SMEM),
                  pl.BlockSpec((1, s, VALUE_DIM), head),
                  pl.BlockSpec((1, s, VALUE_DIM), head),
                  pl.BlockSpec((1, s, VALUE_DIM), head),
                  pl.BlockSpec((2, t, t), fixed3),
                  pl.BlockSpec((VALUE_DIM, VALUE_DIM), fixed2),
                  pl.BlockSpec((1, VALUE_DIM), fixed2),
                  pl.BlockSpec((1, VALUE_DIM), fixed2),
                  pl.BlockSpec((4, HEAD_DIM), fixed2),
                  pl.BlockSpec((1, VALUE_DIM), fixed2)],
        out_specs=pl.BlockSpec((1, s, VALUE_DIM), head),
        out_shape=jax.ShapeDtypeStruct((b, s, N_HEADS * VALUE_DIM), BF16),
        scratch_shapes=[pltpu.VMEM((2, s, VALUE_DIM), BF16),
                        pltpu.VMEM((s, VALUE_DIM), BF16),
                        pltpu.VMEM((2, t, t), F32)],
        compiler_params=_cparams(("arbitrary", "arbitrary")),
        name="diff_attn",
    )(rel_bias, q3, k3, v3, _near_buckets(), blockdiag, q_g, k_g, lam_qk, subln_g)


def _out_proj_kernel(x_ref, c_ref, a_ref, wc_ref, wa_ref, g_ref, rwh_ref, rwl_ref, rb_ref, tri_ref,
                     h_ref, hn_ref, idx_ref, gate_ref, rank_ref, cnt_ref, carry_ref):
    @pl.when(pl.program_id(0) == 0)
    def _():
        carry_ref[...] = jnp.zeros_like(carry_ref)

    h = x_ref[...] + _dot(c_ref[...], wc_ref[...]) + _dot(a_ref[...], wa_ref[...])
    h_ref[...] = h
    ms = jnp.mean(h * h, axis=-1, keepdims=True)
    hn = h * lax.rsqrt(ms + EPS) * g_ref[...]
    _store_rows(hn_ref, (), hn)

    hi = hn.astype(BF16)
    lo = (hn - hi.astype(F32)).astype(BF16)
    logits = (_dot(hi, rwh_ref[...]) + (_dot(hi, rwl_ref[...]) + _dot(lo, rwh_ref[...]))) + rb_ref[...]

    tm = logits.shape[0]
    lane = lax.broadcasted_iota(I32, (tm, LANES), 1)
    work = logits
    vals, idxs = [], []
    for _ in range(TOP_K):
        mx = jnp.max(work, axis=-1, keepdims=True)
        ix = jnp.min(jnp.where(work == mx, lane, LANES), axis=-1, keepdims=True)
        vals.append(mx)
        idxs.append(ix)
        work = jnp.where(lane == ix, NEG_INF, work)
    exps = [jnp.exp(v - vals[0]) for v in vals]
    denom = exps[0]
    for e in exps[1:]:
        denom = denom + e

    sel = jnp.zeros((tm, LANES), F32)
    for ix in idxs:
        sel = sel + jnp.where(lane == ix, 1.0, 0.0)
    rank = _dot(tri_ref[...], sel.astype(BF16)) + carry_ref[...]
    carry_ref[...] = carry_ref[...] + jnp.sum(sel, axis=0, keepdims=True)
    cnt_ref[...] = carry_ref[...]

    idx_out = jnp.zeros((tm, LANES), I32)
    gate_out = jnp.zeros((tm, LANES), F32)
    rank_out = jnp.zeros((tm, LANES), F32)
    for k in range(TOP_K):
        rk = jnp.sum(jnp.where(lane == idxs[k], rank, 0.0), axis=-1, keepdims=True)
        idx_out = jnp.where(lane == k, idxs[k], idx_out)
        gate_out = jnp.where(lane == k, exps[k] / denom, gate_out)
        rank_out = jnp.where(lane == k, rk, rank_out)
    idx_ref[...] = idx_out
    gate_ref[...] = gate_out
    rank_ref[...] = rank_out.astype(I32)


def _out_proj(x2, conv_o, attn_o, wc, wa, g, rwh, rwl, rb, tm):
    n, d = x2.shape
    dc, da = conv_o.shape[1], attn_o.shape[1]
    row = lambda i: (i, 0)
    fixed = lambda i: (0, 0)
    tri = jnp.tril(jnp.ones((tm, tm), F32), -1).astype(BF16)
    return pl.pallas_call(
        _out_proj_kernel,
        grid=(n // tm,),
        in_specs=[pl.BlockSpec((tm, d), row),
                  pl.BlockSpec((tm, dc), row),
                  pl.BlockSpec((tm, da), row),
                  pl.BlockSpec((dc, d), fixed),
                  pl.BlockSpec((da, d), fixed),
                  pl.BlockSpec((1, d), fixed),
                  pl.BlockSpec((d, LANES), fixed),
                  pl.BlockSpec((d, LANES), fixed),
                  pl.BlockSpec((1, LANES), fixed),
                  pl.BlockSpec((tm, tm), fixed)],
        out_specs=[pl.BlockSpec((tm, d), row),
                   pl.BlockSpec((tm * d // LANES, LANES), row),
                   pl.BlockSpec((tm, LANES), row),
                   pl.BlockSpec((tm, LANES), row),
                   pl.BlockSpec((tm, LANES), row),
                   pl.BlockSpec((1, LANES), fixed)],
        out_shape=[jax.ShapeDtypeStruct((n, d), F32),
                   jax.ShapeDtypeStruct((n * d // LANES, LANES), F32),
                   jax.ShapeDtypeStruct((n, LANES), I32),
                   jax.ShapeDtypeStruct((n, LANES), F32),
                   jax.ShapeDtypeStruct((n, LANES), I32),
                   jax.ShapeDtypeStruct((1, LANES), F32)],
        scratch_shapes=[pltpu.VMEM((1, LANES), F32)],
        compiler_params=_cparams(("arbitrary",)),
        name="out_proj_router",
    )(x2, conv_o, attn_o, wc, wa, g, rwh, rwl, rb, tri)


MOE_BLOCK = 256


def _moe_kernel(be_ref, first_ref, wslot_ref, nxt_ref, nused_ref,
                x_ref, wgu_hbm, wd_hbm, bgu_ref, bd_ref, perm_ref,
                y_ref, wgu_f32, wd_f32, wsem, wgu_bf, wd_bf, x_bf, act_bf):
    i = pl.program_id(0)
    bm = MOE_BLOCK
    n_used = nused_ref[0]
    d = wd_bf.shape[1]
    d_ff = wd_bf.shape[0]
    n_groups = d_ff // LANES

    def weight_copies(e, slot):
        return (pltpu.make_async_copy(wgu_hbm.at[e], wgu_f32.at[slot], wsem.at[0, slot]),
                pltpu.make_async_copy(wd_hbm.at[e], wd_f32.at[slot], wsem.at[1, slot]))

    @pl.when(i == 0)
    def _():
        for c in weight_copies(be_ref[0], 0):
            c.start()

    active = i < n_used

    @pl.when(jnp.logical_and(active, first_ref[i] == 1))
    def _():
        ws = wslot_ref[i]
        for c in weight_copies(be_ref[i], ws):
            c.wait()
        for g in range(n_groups):
            cols = pl.ds(g * MXU_DIM, MXU_DIM)
            wgu_bf[:, cols] = _dot(wgu_f32[ws, :, cols].astype(BF16), perm_ref[...]).astype(BF16)
        wd_bf[...] = wd_f32[ws].astype(BF16)

        @pl.when(nxt_ref[i] >= 0)
        def _():
            for c in weight_copies(nxt_ref[i], 1 - ws):
                c.start()

    def activate(g, gu):
        gu = gu + bgu_ref[0, :, g * MXU_DIM:(g + 1) * MXU_DIM]
        gate = jnp.minimum(gu[:, :LANES], SWIGLU_LIMIT)
        lin = jnp.clip(gu[:, LANES:], -SWIGLU_LIMIT, SWIGLU_LIMIT)
        act_bf[:, g * LANES:(g + 1) * LANES] = (
            gate * jax.nn.sigmoid(SWIGLU_ALPHA * gate) * (lin + 1.0)).astype(BF16)

    @pl.when(active)
    def _():
        for j in range(d // LANES):
            x_bf[:, j * LANES:(j + 1) * LANES] = _load_row_tile(x_ref, (), bm, d // LANES, j).astype(BF16)
        for g in range(n_groups):
            activate(g, _dot(x_bf[...], wgu_bf[:, g * MXU_DIM:(g + 1) * MXU_DIM]))
        _store_rows(y_ref, (), _dot(act_bf[...], wd_bf[...]) + bd_ref[0])

    @pl.when(i >= n_used)
    def _():
        y_ref[...] = jnp.zeros_like(y_ref)


def _deinterleave_perm():
    src = np.arange(MXU_DIM)
    dst = np.where(src % 2 == 0, src // 2, LANES + src // 2)
    p = np.zeros((MXU_DIM, MXU_DIM), np.float32)
    p[src, dst] = 1.0
    return jnp.asarray(p, dtype=BF16)


def _moe(plan, xs, w_gate_up, bgu_p, w_down, b_down):
    n_exp, d, d_gu = w_gate_up.shape
    d_ff = w_down.shape[1]
    bm = MOE_BLOCK
    nblk = plan[0].shape[0]
    slab = bm * d // LANES
    by_expert = lambda i, be, *_: (be[i], 0, 0)
    grid_spec = pltpu.PrefetchScalarGridSpec(
        num_scalar_prefetch=5,
        grid=(nblk,),
        in_specs=[
            pl.BlockSpec((slab, LANES), lambda i, *_: (i, 0)),
            pl.BlockSpec(memory_space=pl.ANY),
            pl.BlockSpec(memory_space=pl.ANY),
            pl.BlockSpec((1, 1, d_gu), by_expert),
            pl.BlockSpec((1, 1, d), by_expert),
            pl.BlockSpec((MXU_DIM, MXU_DIM), lambda i, *_: (0, 0)),
        ],
        out_specs=pl.BlockSpec((slab, LANES), lambda i, *_: (i, 0)),
        scratch_shapes=[pltpu.VMEM((2, d, d_gu), F32),
                        pltpu.VMEM((2, d_ff, d), F32),
                        pltpu.SemaphoreType.DMA((2, 2)),
                        pltpu.VMEM((d, d_gu), BF16),
                        pltpu.VMEM((d_ff, d), BF16),
                        pltpu.VMEM((bm, d), BF16),
                        pltpu.VMEM((bm, d_ff), BF16)],
    )
    return pl.pallas_call(
        _moe_kernel,
        grid_spec=grid_spec,
        out_shape=jax.ShapeDtypeStruct((nblk * slab, LANES), F32),
        compiler_params=_cparams(("arbitrary",)),
        name="moe_experts",
    )(*plan, xs, w_gate_up, w_down, bgu_p, b_down, _deinterleave_perm())


SC_CORES = 2
SC_SUBCORES = 16
SC_CHUNK = 64


def _sc_mesh():
    return plsc.VectorSubcoreMesh(core_axis_name="c", subcore_axis_name="s")


def _sc_dispatch(rows3, pos_chunks, n_slots):
    n, r, _ = rows3.shape
    workers = SC_CORES * SC_SUBCORES
    per_worker = n // SC_CHUNK // workers

    def body(rows_hbm, pos_hbm, out_hbm, idx_v, rows_v, sem):
        wid = lax.axis_index("s") * SC_CORES + lax.axis_index("c")

        @pl.loop(0, per_worker)
        def _(c):
            chunk = wid * per_worker + c
            pltpu.sync_copy(pos_hbm.at[chunk], idx_v)
            pltpu.sync_copy(rows_hbm.at[pl.ds(chunk * SC_CHUNK, SC_CHUNK)], rows_v)
            copies = [pltpu.async_copy(rows_v, out_hbm.at[idx_v.at[k]], sem) for k in range(TOP_K)]
            for cp in copies:
                cp.wait()

    return pl.kernel(
        body,
        out_type=jax.ShapeDtypeStruct((n_slots, r, LANES), F32),
        mesh=_sc_mesh(),
        scratch_types=[pltpu.VMEM((TOP_K, SC_CHUNK), I32),
                       pltpu.VMEM((SC_CHUNK, r, LANES), F32),
                       pltpu.SemaphoreType.DMA],
        name="sc_dispatch",
    )(rows3, pos_chunks)


def _sc_gather(rows3, pos_chunks, n):
    _, r, _ = rows3.shape
    workers = SC_CORES * SC_SUBCORES
    per_worker = n // SC_CHUNK // workers

    def body(rows_hbm, pos_hbm, out_hbm, idx_v, rows_v, sem):
        wid = lax.axis_index("s") * SC_CORES + lax.axis_index("c")

        @pl.loop(0, per_worker)
        def _(c):
            chunk = wid * per_worker + c
            pltpu.sync_copy(pos_hbm.at[chunk], idx_v)
            for k in range(TOP_K):
                pltpu.async_copy(rows_hbm.at[idx_v.at[k]], rows_v, sem).wait()
                pltpu.sync_copy(rows_v, out_hbm.at[k, pl.ds(chunk * SC_CHUNK, SC_CHUNK)])

    return pl.kernel(
        body,
        out_type=jax.ShapeDtypeStruct((TOP_K, n, r, LANES), F32),
        mesh=_sc_mesh(),
        scratch_types=[pltpu.VMEM((TOP_K, SC_CHUNK), I32),
                       pltpu.VMEM((SC_CHUNK, r, LANES), F32),
                       pltpu.SemaphoreType.DMA],
        name="sc_gather",
    )(rows3, pos_chunks)


COMBINE_TILE = 128


def _combine_kernel(h_ref, gate_ref, y_ref, o_ref):
    tc, d = h_ref.shape
    k_tiles = d // LANES
    gates = gate_ref[...]
    for j in range(k_tiles):
        acc = h_ref[:, j * LANES:(j + 1) * LANES]
        for k in range(TOP_K):
            acc = acc + gates[:, k:k + 1] * _load_row_tile(y_ref, (k,), tc, k_tiles, j)
        o_ref[:, j * LANES:(j + 1) * LANES] = acc


def _combine(h, gates, y4):
    n, d = h.shape
    tc = COMBINE_TILE
    row = lambda i: (i, 0)
    return pl.pallas_call(
        _combine_kernel,
        grid=(n // tc,),
        in_specs=[pl.BlockSpec((tc, d), row),
                  pl.BlockSpec((tc, LANES), row),
                  pl.BlockSpec((TOP_K, tc * d // LANES, LANES), lambda i: (0, i, 0))],
        out_specs=pl.BlockSpec((tc, d), row),
        out_shape=jax.ShapeDtypeStruct((n, d), F32),
        compiler_params=_cparams(("parallel",)),
        name="moe_combine",
    )(h, gates, y4)


def _layer(h3, layer, norm_mix_g, w_in, conv_w, conv_b, conv_ln_g, conv_ln_b, q_norm_g, k_norm_g,
           lambda_qk, subln_g, rel_bias, w_out, norm_ffn_g, router_w, router_b,
           w_gate_up, b_gate_up, w_down, b_down):
    b, s, d = h3.shape
    n = b * s
    d_conv = conv_w.shape[-1]
    d_attn = N_HEADS * VALUE_DIM
    lambda_init = 0.8 - 0.6 * math.exp(-0.3 * layer)
    x2 = h3.reshape(n, d)

    u, q, k, v = _in_proj(x2, norm_mix_g.reshape(1, d), w_in.astype(BF16), d_conv, d_attn, tm=512)
    conv_o = _conv(u.reshape(b, s, d_conv), conv_w, conv_b.reshape(1, d_conv),
                   conv_ln_g.reshape(1, d_conv), conv_ln_b.reshape(1, d_conv))
    attn_o = _attention(q.reshape(b, s, d_attn), k.reshape(b, s, d_attn), v.reshape(b, s, d_attn),
                        rel_bias, q_norm_g.reshape(1, VALUE_DIM), k_norm_g.reshape(1, VALUE_DIM),
                        lambda_qk, subln_g.reshape(1, VALUE_DIM), lambda_init)

    n_exp = router_w.shape[1]
    rw = jnp.zeros((d, LANES), F32).at[:, :n_exp].set(router_w)
    rwh = rw.astype(BF16)
    rwl = (rw - rwh.astype(F32)).astype(BF16)
    rb = jnp.full((1, LANES), NEG_INF, F32).at[0, :n_exp].set(router_b)
    w_out_bf = w_out.astype(BF16)
    hres, hn, idx, gates, rank, cnt = _out_proj(
        x2, conv_o.reshape(n, d_conv), attn_o.reshape(n, d_attn),
        w_out_bf[:d_conv], w_out_bf[d_conv:], norm_ffn_g.reshape(1, d), rwh, rwl, rb, tm=256)

    bm = MOE_BLOCK
    nblk = n * TOP_K // bm + n_exp
    counts = cnt[0, :n_exp].astype(I32)
    padded = (counts + bm - 1) // bm * bm
    pad_end = jnp.cumsum(padded)
    pad_start = pad_end - padded
    pos = pad_start[idx[:, :TOP_K]] + rank[:, :TOP_K]
    pos_chunks = pos.reshape(n // SC_CHUNK, SC_CHUNK, TOP_K).transpose(0, 2, 1)
    block_start = jnp.arange(nblk, dtype=I32) * bm
    block_expert = jnp.minimum(jnp.sum(block_start[:, None] >= pad_end[None, :], axis=1),
                               n_exp - 1).astype(I32)
    n_used = (pad_end[-1:] // bm).astype(I32)
    blk = jnp.arange(nblk, dtype=I32)
    first = jnp.logical_and(
        jnp.concatenate([jnp.ones((1,), bool), block_expert[1:] != block_expert[:-1]]),
        blk < n_used[0]).astype(I32)
    wslot = ((jnp.cumsum(first) - 1) % 2).astype(I32)
    eid = jnp.arange(n_exp, dtype=I32)
    later_used = jnp.logical_and(eid[None, :] > eid[:, None], counts[None, :] > 0)
    next_used = jnp.min(jnp.where(later_used, eid[None, :], n_exp), axis=1)
    next_used = jnp.where(next_used == n_exp, -1, next_used).astype(I32)
    plan = (block_expert, first, wslot, next_used[block_expert], n_used)

    d_ff = w_down.shape[1]
    bgu_p = b_gate_up.reshape(n_exp, d_ff // LANES, LANES, 2).transpose(0, 1, 3, 2).reshape(n_exp, 1, 2 * d_ff)
    r = d // LANES
    xs = _sc_dispatch(hn.reshape(n, r, LANES), pos_chunks, nblk * bm)
    yb = _moe(plan, xs.reshape(nblk * bm * r, LANES), w_gate_up, bgu_p, w_down,
              b_down.reshape(n_exp, 1, d))
    y4 = _sc_gather(yb.reshape(nblk * bm, r, LANES), pos_chunks, n)
    out = _combine(hres, gates, y4.reshape(TOP_K, n * r, LANES))
    return out.reshape(b, s, d)


def kernel(x, norm_mix_g, w_in, conv_w, conv_b, conv_ln_g, conv_ln_b, q_norm_g, k_norm_g, lambda_qk,
           subln_g, rel_bias, w_out, norm_ffn_g, router_w, router_b, w_gate_up, b_gate_up, w_down,
           b_down):
    h = x
    for layer in range(norm_mix_g.shape[0]):
        h = _layer(h, layer, norm_mix_g[layer], w_in[layer], conv_w[layer], conv_b[layer],
                   conv_ln_g[layer], conv_ln_b[layer], q_norm_g[layer], k_norm_g[layer],
                   lambda_qk[layer], subln_g[layer], rel_bias, w_out[layer], norm_ffn_g[layer],
                   router_w[layer], router_b[layer], w_gate_up[layer], b_gate_up[layer],
                   w_down[layer], b_down[layer])
    return h
```

```python
import functools
import math

import jax
import jax.numpy as jnp
import numpy as np
from jax import lax
from jax.experimental import pallas as pl
from jax.experimental.pallas import tpu as pltpu
from jax.experimental.pallas import tpu_sc as plsc

F32 = jnp.float32
BF16 = jnp.bfloat16
I32 = jnp.int32
U32 = jnp.uint32

CHUNK = 64
CONV_WIDTH = 31
N_HEADS = 4
HEAD_DIM = 64
VALUE_DIM = 2 * HEAD_DIM
REL_BUCKETS = 32
REL_MAX_DIST = 128
N_EXPERTS = 32
TOP_K = 4
SWIGLU_LIMIT = 7.0
SWIGLU_ALPHA = 1.702
EPS = 1e-5
LOG2E = 1.4426950408889634

LANES = 128
SUBLANES = 8
MXU_DIM = 256
VMEM_LIMIT = 56 * 1024 * 1024

NEG_INF = float("-inf")


def _cparams(sem, vmem=VMEM_LIMIT):
    return pltpu.CompilerParams(dimension_semantics=sem, vmem_limit_bytes=vmem)


def _dot(a, b):
    return jnp.dot(a, b, preferred_element_type=F32)


def _dot_nt(a, b):
    return lax.dot_general(a, b, (((1,), (1,)), ((), ())), preferred_element_type=F32)


def _pack_pair(lo, hi):
    lo_bits = lax.bitcast_convert_type(lo.astype(BF16).astype(F32), U32)
    hi_bits = lax.bitcast_convert_type(hi.astype(BF16).astype(F32), U32)
    return (lo_bits >> 16) | hi_bits


def _unpack_pair(w):
    return (lax.bitcast_convert_type(w << 16, F32),
            lax.bitcast_convert_type(w & jnp.uint32(0xFFFF0000), F32))


def _store_packed_rows(ref, lead, x):
    rows, k = x.shape[0], x.shape[1] // (2 * LANES)
    for j in range(k):
        w = _pack_pair(x[:, j * LANES:(j + 1) * LANES], x[:, (k + j) * LANES:(k + j + 1) * LANES])
        ref[(*lead, pl.ds(j, rows, stride=k), slice(None))] = w


def _load_packed_tile(ref, lead, rows, k, j):
    return _unpack_pair(ref[(*lead, pl.ds(j, rows, stride=k), slice(None))])


def _in_proj_kernel(x_ref, g_ref, w_ref, u_ref, q_ref, k_ref, v_ref, *, d_conv, d_attn):
    x = x_ref[...]
    ms = jnp.mean(x * x, axis=-1, keepdims=True)
    y = (x * lax.rsqrt(ms + EPS) * g_ref[...]).astype(BF16)
    proj = _dot(y, w_ref[...])
    a = proj[:, :d_conv]
    g = proj[:, d_conv:2 * d_conv]
    u_ref[...] = a * jax.nn.sigmoid(g)
    o = 2 * d_conv
    q_ref[...] = proj[:, o:o + d_attn]
    k_ref[...] = proj[:, o + d_attn:o + 2 * d_attn]
    v_ref[...] = proj[:, o + 2 * d_attn:o + 3 * d_attn].astype(BF16)


def _in_proj(x2, g, w_bf, d_conv, d_attn, tm):
    n, d = x2.shape
    d_in = w_bf.shape[1]
    row = lambda i: (i, 0)
    fixed = lambda i: (0, 0)
    return pl.pallas_call(
        functools.partial(_in_proj_kernel, d_conv=d_conv, d_attn=d_attn),
        grid=(n // tm,),
        in_specs=[pl.BlockSpec((tm, d), row),
                  pl.BlockSpec((1, d), fixed),
                  pl.BlockSpec((d, d_in), fixed)],
        out_specs=[pl.BlockSpec((tm, d_conv), row),
                   pl.BlockSpec((tm, d_attn), row),
                   pl.BlockSpec((tm, d_attn), row),
                   pl.BlockSpec((tm, d_attn), row)],
        out_shape=[jax.ShapeDtypeStruct((n, d_conv), F32),
                   jax.ShapeDtypeStruct((n, d_attn), F32),
                   jax.ShapeDtypeStruct((n, d_attn), F32),
                   jax.ShapeDtypeStruct((n, d_attn), BF16)],
        compiler_params=_cparams(("parallel",)),
        name="in_proj",
    )(x2, g, w_bf)


CONV_PAD = 32
CONV_SEQ_TILE = 512
CONV_TILE = 64


def _conv_kernel(u_ref, prev_ref, w_ref, cb_ref, lg_ref, lb_ref, o_ref, sh_ref):
    ts, c = u_ref.shape[1], u_ref.shape[2]
    plen = ts + CONV_PAD
    hist = prev_ref[0]
    hist = jnp.where(pl.program_id(1) > 0, hist, jnp.zeros_like(hist))
    sh_ref[0, pl.ds(0, CONV_PAD), :] = hist
    sh_ref[0, pl.ds(CONV_PAD, ts), :] = u_ref[0]
    sh_ref[0, pl.ds(plen, SUBLANES), :] = jnp.zeros((SUBLANES, c), F32)

    bt = CONV_PAD

    def shift(i, _):
        p0 = pl.multiple_of(i * bt, bt)
        win = sh_ref[0, pl.ds(p0, bt + SUBLANES), :]
        for r in range(1, SUBLANES):
            sh_ref[r, pl.ds(p0, bt), :] = win[r:r + bt]
        return 0

    lax.fori_loop(0, plen // bt, shift, 0)

    off0 = CONV_PAD - (CONV_WIDTH - 1)

    def body(i, _):
        t0 = pl.multiple_of(i * CONV_TILE, CONV_TILE)
        acc = jnp.zeros((CONV_TILE, c), F32)
        for j in range(CONV_WIDTH):
            off = off0 + j
            a, r = off // SUBLANES, off % SUBLANES
            start = pl.multiple_of(t0 + a * SUBLANES, SUBLANES)
            acc = acc + sh_ref[r, pl.ds(start, CONV_TILE), :] * w_ref[pl.ds(j, 1), :]
        y = acc + cb_ref[...]
        mu = jnp.mean(y, axis=-1, keepdims=True)
        yc = y - mu
        var = jnp.mean(yc * yc, axis=-1, keepdims=True)
        z = yc * lax.rsqrt(var + EPS) * lg_ref[...] + lb_ref[...]
        o_ref[0, pl.ds(t0, CONV_TILE), :] = (z * jax.nn.sigmoid(z)).astype(o_ref.dtype)
        return 0

    lax.fori_loop(0, ts // CONV_TILE, body, 0)


def _conv(u3, conv_w, conv_b, ln_g, ln_b):
    b, s, c = u3.shape
    ts = min(CONV_SEQ_TILE, s)
    hist_per_tile = ts // CONV_PAD
    fixed = lambda i, j: (0, 0)
    return pl.pallas_call(
        _conv_kernel,
        grid=(b, s // ts),
        in_specs=[pl.BlockSpec((1, ts, c), lambda i, j: (i, j, 0)),
                  pl.BlockSpec((1, CONV_PAD, c),
                               lambda i, j: (i, jnp.maximum(j * hist_per_tile - 1, 0), 0)),
                  pl.BlockSpec((CONV_WIDTH, c), fixed),
                  pl.BlockSpec((1, c), fixed),
                  pl.BlockSpec((1, c), fixed),
                  pl.BlockSpec((1, c), fixed)],
        out_specs=pl.BlockSpec((1, ts, c), lambda i, j: (i, j, 0)),
        out_shape=jax.ShapeDtypeStruct((b, s, c), BF16),
        scratch_shapes=[pltpu.VMEM((SUBLANES, ts + CONV_PAD + SUBLANES, c), F32)],
        compiler_params=_cparams(("parallel", "parallel")),
        name="conv_mixer",
    )(u3, u3, conv_w, conv_b, ln_g, ln_b)


ATT_TILE = 256
FAR_BUCKET = REL_BUCKETS // 2 - 1


def _t5_bucket(rel):
    nb = REL_BUCKETS // 2
    max_exact = nb // 2
    ret = jnp.where(rel > 0, nb, 0)
    n = jnp.abs(rel)
    nf = jnp.maximum(n, 1).astype(jnp.float32)
    large = max_exact + (jnp.log(nf / max_exact) / math.log(REL_MAX_DIST / max_exact)
                         * (nb - max_exact)).astype(jnp.int32)
    large = jnp.minimum(large, nb - 1)
    return ret + jnp.where(n < max_exact, n, large)


def _near_buckets():
    assert ATT_TILE >= REL_MAX_DIST and ATT_TILE % CHUNK == 0
    qpos = jnp.arange(ATT_TILE, dtype=I32)[:, None]
    kpos = jnp.arange(ATT_TILE, dtype=I32)[None, :]
    prev = _t5_bucket(kpos - ATT_TILE - qpos)
    diag = _t5_bucket(kpos - qpos)
    diag = jnp.where(kpos // CHUNK <= qpos // CHUNK, diag, -1)
    return jnp.stack([prev, diag]).astype(I32)


def _attn_kernel(tab_ref, q_ref, k_ref, v_ref, bkt_ref, bd_ref, qg_ref, kg_ref, lqk_ref, sg_ref, o_ref,
                 qz_ref, kn_ref, bias_ref, *, seq, lambda_init):
    h = pl.program_id(0)
    t = ATT_TILE
    n_tiles = seq // t
    lane = lax.broadcasted_iota(I32, (1, VALUE_DIM), 1)
    first = lane < HEAD_DIM

    @pl.when(pl.program_id(1) == 0)
    def _():
        far = tab_ref[FAR_BUCKET, h]
        for d in range(2):
            bkt = bkt_ref[d]
            tile = jnp.full((t, t), NEG_INF, F32)
            for b in range(REL_BUCKETS):
                tile = jnp.where(bkt == b, (tab_ref[b, h] - far) * LOG2E, tile)
            bias_ref[d] = tile

    def half_norm(x, g):
        x2 = x * x
        hi = x2.astype(BF16)
        lo = (x2 - hi.astype(F32)).astype(BF16)
        ms = (_dot(hi, bd_ref[...]) + _dot(lo, bd_ref[...])) * (1.0 / HEAD_DIM)
        return x * lax.rsqrt(ms + EPS) * g

    q_scale = HEAD_DIM ** -0.5 * LOG2E

    def prep(i, _):
        r0 = pl.multiple_of(i * t, t)
        qn = half_norm(q_ref[0, pl.ds(r0, t), :], qg_ref[...]) * q_scale
        qz_ref[0, pl.ds(r0, t), :] = jnp.where(first, qn, 0.0).astype(BF16)
        qz_ref[1, pl.ds(r0, t), :] = jnp.where(first, 0.0, qn).astype(BF16)
        kn_ref[pl.ds(r0, t), :] = half_norm(k_ref[0, pl.ds(r0, t), :], kg_ref[...]).astype(BF16)
        return 0

    lax.fori_loop(0, n_tiles, prep, 0)

    lqk = lqk_ref[...]
    lam = (jnp.exp(jnp.sum(lqk[0:1] * lqk[1:2], axis=-1, keepdims=True))
           - jnp.exp(jnp.sum(lqk[2:3] * lqk[3:4], axis=-1, keepdims=True)) + lambda_init)

    for i in range(n_tiles):
        q0, kend = i * t, (i + 1) * t
        keys = kn_ref[0:kend, :]
        vals = v_ref[0, 0:kend, :]
        maps = []
        for m in range(2):
            s = _dot_nt(qz_ref[m, q0:q0 + t, :], keys)
            parts = [s[:, kend - t:] + bias_ref[1]]
            if i >= 1:
                parts.insert(0, s[:, kend - 2 * t:kend - t] + bias_ref[0])
            if i >= 2:
                parts.insert(0, s[:, :kend - 2 * t])
            s = jnp.concatenate(parts, axis=1) if len(parts) > 1 else parts[0]
            p = jnp.exp2(s - jnp.max(s, axis=-1, keepdims=True))
            l = jnp.sum(p, axis=-1, keepdims=True)
            maps.append(_dot(p.astype(BF16), vals) / l)
        o = maps[0] - lam * maps[1]
        ms = jnp.mean(o * o, axis=-1, keepdims=True)
        o = o * lax.rsqrt(ms + EPS) * sg_ref[...] * (1.0 - lambda_init)
        o_ref[0, q0:q0 + t, :] = o.astype(o_ref.dtype)


def _attention(q3, k3, v3, rel_bias, q_g, k_g, lam_qk, subln_g, lambda_init):
    b, s, _ = q3.shape
    t = ATT_TILE
    head = lambda j, i: (i, 0, j)
    fixed2 = lambda j, i: (0, 0)
    fixed3 = lambda j, i: (0, 0, 0)
    half = np.arange(VALUE_DIM) // HEAD_DIM
    blockdiag = jnp.asarray(half[:, None] == half[None, :], dtype=BF16)
    return pl.pallas_call(
        functools.partial(_attn_kernel, seq=s, lambda_init=lambda_init),
        grid=(N_HEADS, b),
        in_specs=[pl.BlockSpec(memory_space=pltpu.SMEM),
                  pl.BlockSpec((1, s, VALUE_DIM), head),
                  pl.BlockSpec((1, s, VALUE_DIM), head),
                  pl.BlockSpec((1, s, VALUE_DIM), head),
                  pl.BlockSpec((2, t, t), fixed3),
                  pl.BlockSpec((VALUE_DIM, VALUE_DIM), fixed2),
                  pl.BlockSpec((1, VALUE_DIM), fixed2),
                  pl.BlockSpec((1, VALUE_DIM), fixed2),
                  pl.BlockSpec((4, HEAD_DIM), fixed2),
                  pl.BlockSpec((1, VALUE_DIM), fixed2)],
        out_specs=pl.BlockSpec((1, s, VALUE_DIM), head),
        out_shape=jax.ShapeDtypeStruct((b, s, N_HEADS * VALUE_DIM), BF16),
        scratch_shapes=[pltpu.VMEM((2, s, VALUE_DIM), BF16),
                        pltpu.VMEM((s, VALUE_DIM), BF16),
                        pltpu.VMEM((2, t, t), F32)],
        compiler_params=_cparams(("arbitrary", "arbitrary")),
        name="diff_attn",
    )(rel_bias, q3, k3, v3, _near_buckets(), blockdiag, q_g, k_g, lam_qk, subln_g)


def _out_proj_kernel(x_ref, c_ref, a_ref, wc_ref, wa_ref, g_ref, rwh_ref, rwl_ref, rb_ref, tri_ref,
                     h_ref, hn_ref, idx_ref, gate_ref, rank_ref, cnt_ref, carry_ref):
    @pl.when(pl.program_id(0) == 0)
    def _():
        carry_ref[...] = jnp.zeros_like(carry_ref)

    h = x_ref[...] + _dot(c_ref[...], wc_ref[...]) + _dot(a_ref[...], wa_ref[...])
    h_ref[...] = h
    ms = jnp.mean(h * h, axis=-1, keepdims=True)
    hn = h * lax.rsqrt(ms + EPS) * g_ref[...]
    _store_packed_rows(hn_ref, (), hn)

    hi = hn.astype(BF16)
    lo = (hn - hi.astype(F32)).astype(BF16)
    logits = (_dot(hi, rwh_ref[...]) + (_dot(hi, rwl_ref[...]) + _dot(lo, rwh_ref[...]))) + rb_ref[...]

    tm = logits.shape[0]
    lane = lax.broadcasted_iota(I32, (tm, LANES), 1)
    work = logits
    vals, idxs = [], []
    for _ in range(TOP_K):
        mx = jnp.max(work, axis=-1, keepdims=True)
        ix = jnp.min(jnp.where(work == mx, lane, LANES), axis=-1, keepdims=True)
        vals.append(mx)
        idxs.append(ix)
        work = jnp.where(lane == ix, NEG_INF, work)
    exps = [jnp.exp(v - vals[0]) for v in vals]
    denom = exps[0]
    for e in exps[1:]:
        denom = denom + e

    sel = jnp.zeros((tm, LANES), F32)
    for ix in idxs:
        sel = sel + jnp.where(lane == ix, 1.0, 0.0)
    rank = _dot(tri_ref[...], sel.astype(BF16)) + carry_ref[...]
    carry_ref[...] = carry_ref[...] + jnp.sum(sel, axis=0, keepdims=True)
    cnt_ref[...] = carry_ref[...]

    idx_out = jnp.zeros((tm, LANES), I32)
    gate_out = jnp.zeros((tm, LANES), F32)
    rank_out = jnp.zeros((tm, LANES), F32)
    for k in range(TOP_K):
        rk = jnp.sum(jnp.where(lane == idxs[k], rank, 0.0), axis=-1, keepdims=True)
        idx_out = jnp.where(lane == k, idxs[k], idx_out)
        gate_out = jnp.where(lane == k, exps[k] / denom, gate_out)
        rank_out = jnp.where(lane == k, rk, rank_out)
    idx_ref[...] = idx_out
    gate_ref[...] = gate_out
    rank_ref[...] = rank_out.astype(I32)


def _out_proj(x2, conv_o, attn_o, wc, wa, g, rwh, rwl, rb, tm):
    n, d = x2.shape
    dc, da = conv_o.shape[1], attn_o.shape[1]
    row = lambda i: (i, 0)
    fixed = lambda i: (0, 0)
    tri = jnp.tril(jnp.ones((tm, tm), F32), -1).astype(BF16)
    return pl.pallas_call(
        _out_proj_kernel,
        grid=(n // tm,),
        in_specs=[pl.BlockSpec((tm, d), row),
                  pl.BlockSpec((tm, dc), row),
                  pl.BlockSpec((tm, da), row),
                  pl.BlockSpec((dc, d), fixed),
                  pl.BlockSpec((da, d), fixed),
                  pl.BlockSpec((1, d), fixed),
                  pl.BlockSpec((d, LANES), fixed),
                  pl.BlockSpec((d, LANES), fixed),
                  pl.BlockSpec((1, LANES), fixed),
                  pl.BlockSpec((tm, tm), fixed)],
        out_specs=[pl.BlockSpec((tm, d), row),
                   pl.BlockSpec((tm * d // (2 * LANES), LANES), row),
                   pl.BlockSpec((tm, LANES), row),
                   pl.BlockSpec((tm, LANES), row),
                   pl.BlockSpec((tm, LANES), row),
                   pl.BlockSpec((1, LANES), fixed)],
        out_shape=[jax.ShapeDtypeStruct((n, d), F32),
                   jax.ShapeDtypeStruct((n * d // (2 * LANES), LANES), U32),
                   jax.ShapeDtypeStruct((n, LANES), I32),
                   jax.ShapeDtypeStruct((n, LANES), F32),
                   jax.ShapeDtypeStruct((n, LANES), I32),
                   jax.ShapeDtypeStruct((1, LANES), F32)],
        scratch_shapes=[pltpu.VMEM((1, LANES), F32)],
        compiler_params=_cparams(("arbitrary",)),
        name="out_proj_router",
    )(x2, conv_o, attn_o, wc, wa, g, rwh, rwl, rb, tri)


MOE_BLOCK = 512


def _moe_kernel(be_ref, first_ref, wslot_ref, nxt_ref, nused_ref,
                x_ref, wgu_hbm, wd_hbm, bgu_ref, bd_ref, perm_ref,
                y_ref, wgu_f32, wd_f32, wsem, wgu_bf, wd_bf, x_bf, act_bf):
    i = pl.program_id(0)
    bm = MOE_BLOCK
    n_used = nused_ref[0]
    d = wd_bf.shape[1]
    d_ff = wd_bf.shape[0]
    n_groups = d_ff // LANES

    def weight_copies(e, slot):
        return (pltpu.make_async_copy(wgu_hbm.at[e], wgu_f32.at[slot], wsem.at[0, slot]),
                pltpu.make_async_copy(wd_hbm.at[e], wd_f32.at[slot], wsem.at[1, slot]))

    @pl.when(i == 0)
    def _():
        for c in weight_copies(be_ref[0], 0):
            c.start()

    active = i < n_used

    @pl.when(jnp.logical_and(active, first_ref[i] == 1))
    def _():
        ws = wslot_ref[i]
        for c in weight_copies(be_ref[i], ws):
            c.wait()
        for g in range(n_groups):
            cols = pl.ds(g * MXU_DIM, MXU_DIM)
            wgu_bf[:, cols] = _dot(wgu_f32[ws, :, cols].astype(BF16), perm_ref[...]).astype(BF16)
        wd_bf[...] = wd_f32[ws].astype(BF16)

        @pl.when(nxt_ref[i] >= 0)
        def _():
            for c in weight_copies(nxt_ref[i], 1 - ws):
                c.start()

    def activate(g, gu):
        gu = gu + bgu_ref[0, :, g * MXU_DIM:(g + 1) * MXU_DIM]
        gate = jnp.minimum(gu[:, :LANES], SWIGLU_LIMIT)
        lin = jnp.clip(gu[:, LANES:], -SWIGLU_LIMIT, SWIGLU_LIMIT)
        act_bf[:, g * LANES:(g + 1) * LANES] = (
            gate * jax.nn.sigmoid(SWIGLU_ALPHA * gate) * (lin + 1.0)).astype(BF16)

    @pl.when(active)
    def _():
        k = d // (2 * LANES)
        for j in range(k):
            lo, hi = _load_packed_tile(x_ref, (), bm, k, j)
            x_bf[:, j * LANES:(j + 1) * LANES] = lo.astype(BF16)
            x_bf[:, (k + j) * LANES:(k + j + 1) * LANES] = hi.astype(BF16)
        for g in range(n_groups):
            activate(g, _dot(x_bf[...], wgu_bf[:, g * MXU_DIM:(g + 1) * MXU_DIM]))
        _store_packed_rows(y_ref, (), _dot(act_bf[...], wd_bf[...]) + bd_ref[0])

    @pl.when(i >= n_used)
    def _():
        y_ref[...] = jnp.zeros_like(y_ref)


def _deinterleave_perm():
    src = np.arange(MXU_DIM)
    dst = np.where(src % 2 == 0, src // 2, LANES + src // 2)
    p = np.zeros((MXU_DIM, MXU_DIM), np.float32)
    p[src, dst] = 1.0
    return jnp.asarray(p, dtype=BF16)


def _moe(plan, xs, w_gate_up, bgu_p, w_down, b_down):
    n_exp, d, d_gu = w_gate_up.shape
    d_ff = w_down.shape[1]
    bm = MOE_BLOCK
    nblk = plan[0].shape[0]
    slab = bm * d // (2 * LANES)
    by_expert = lambda i, be, *_: (be[i], 0, 0)
    grid_spec = pltpu.PrefetchScalarGridSpec(
        num_scalar_prefetch=5,
        grid=(nblk,),
        in_specs=[
            pl.BlockSpec((slab, LANES), lambda i, *_: (i, 0)),
            pl.BlockSpec(memory_space=pl.ANY),
            pl.BlockSpec(memory_space=pl.ANY),
            pl.BlockSpec((1, 1, d_gu), by_expert),
            pl.BlockSpec((1, 1, d), by_expert),
            pl.BlockSpec((MXU_DIM, MXU_DIM), lambda i, *_: (0, 0)),
        ],
        out_specs=pl.BlockSpec((slab, LANES), lambda i, *_: (i, 0)),
        scratch_shapes=[pltpu.VMEM((2, d, d_gu), F32),
                        pltpu.VMEM((2, d_ff, d), F32),
                        pltpu.SemaphoreType.DMA((2, 2)),
                        pltpu.VMEM((d, d_gu), BF16),
                        pltpu.VMEM((d_ff, d), BF16),
                        pltpu.VMEM((bm, d), BF16),
                        pltpu.VMEM((bm, d_ff), BF16)],
    )
    return pl.pallas_call(
        _moe_kernel,
        grid_spec=grid_spec,
        out_shape=jax.ShapeDtypeStruct((nblk * slab, LANES), U32),
        compiler_params=_cparams(("arbitrary",)),
        name="moe_experts",
    )(*plan, xs, w_gate_up, w_down, bgu_p, b_down, _deinterleave_perm())


SC_CORES = 2
SC_SUBCORES = 16
SC_CHUNK = 64


def _sc_mesh():
    return plsc.VectorSubcoreMesh(core_axis_name="c", subcore_axis_name="s")


def _sc_dispatch(rows3, pos_chunks, n_slots):
    n, r, _ = rows3.shape
    workers = SC_CORES * SC_SUBCORES
    per_worker = n // SC_CHUNK // workers

    def body(rows_hbm, pos_hbm, out_hbm, idx_v, rows_v, sem):
        wid = lax.axis_index("s") * SC_CORES + lax.axis_index("c")

        @pl.loop(0, per_worker)
        def _(c):
            chunk = wid * per_worker + c
            pltpu.sync_copy(pos_hbm.at[chunk], idx_v)
            pltpu.sync_copy(rows_hbm.at[pl.ds(chunk * SC_CHUNK, SC_CHUNK)], rows_v)
            copies = [pltpu.async_copy(rows_v, out_hbm.at[idx_v.at[k]], sem) for k in range(TOP_K)]
            for cp in copies:
                cp.wait()

    return pl.kernel(
        body,
        out_type=jax.ShapeDtypeStruct((n_slots, r, LANES), rows3.dtype),
        mesh=_sc_mesh(),
        scratch_types=[pltpu.VMEM((TOP_K, SC_CHUNK), I32),
                       pltpu.VMEM((SC_CHUNK, r, LANES), rows3.dtype),
                       pltpu.SemaphoreType.DMA],
        name="sc_dispatch",
    )(rows3, pos_chunks)


def _sc_gather(rows3, pos_chunks, n):
    _, r, _ = rows3.shape
    workers = SC_CORES * SC_SUBCORES
    per_worker = n // SC_CHUNK // workers

    def body(rows_hbm, pos_hbm, out_hbm, idx_v, buf_a, buf_b, sem_a, sem_b):
        wid = lax.axis_index("s") * SC_CORES + lax.axis_index("c")
        bufs, sems = (buf_a, buf_b), (sem_a, sem_b)

        @pl.loop(0, per_worker)
        def _(c):
            chunk = wid * per_worker + c
            pltpu.sync_copy(pos_hbm.at[chunk], idx_v)
            gathers = [None] * TOP_K
            gathers[0] = pltpu.async_copy(rows_hbm.at[idx_v.at[0]], bufs[0], sems[0])
            for k in range(TOP_K):
                if k + 1 < TOP_K:
                    nxt = (k + 1) % 2
                    gathers[k + 1] = pltpu.async_copy(rows_hbm.at[idx_v.at[k + 1]], bufs[nxt], sems[nxt])
                gathers[k].wait()
                pltpu.sync_copy(bufs[k % 2], out_hbm.at[k, pl.ds(chunk * SC_CHUNK, SC_CHUNK)])

    return pl.kernel(
        body,
        out_type=jax.ShapeDtypeStruct((TOP_K, n, r, LANES), rows3.dtype),
        mesh=_sc_mesh(),
        scratch_types=[pltpu.VMEM((TOP_K, SC_CHUNK), I32),
                       pltpu.VMEM((SC_CHUNK, r, LANES), rows3.dtype),
                       pltpu.VMEM((SC_CHUNK, r, LANES), rows3.dtype),
                       pltpu.SemaphoreType.DMA,
                       pltpu.SemaphoreType.DMA],
        name="sc_gather",
    )(rows3, pos_chunks)


COMBINE_TILE = 128


def _combine_kernel(h_ref, gate_ref, y_ref, o_ref):
    tc, d = h_ref.shape
    kt = d // (2 * LANES)
    gates = gate_ref[...]
    for j in range(kt):
        lo_acc = h_ref[:, j * LANES:(j + 1) * LANES]
        hi_acc = h_ref[:, (kt + j) * LANES:(kt + j + 1) * LANES]
        for k in range(TOP_K):
            lo, hi = _load_packed_tile(y_ref, (k,), tc, kt, j)
            g = gates[:, k:k + 1]
            lo_acc = lo_acc + g * lo
            hi_acc = hi_acc + g * hi
        o_ref[:, j * LANES:(j + 1) * LANES] = lo_acc
        o_ref[:, (kt + j) * LANES:(kt + j + 1) * LANES] = hi_acc


def _combine(h, gates, y4):
    n, d = h.shape
    tc = COMBINE_TILE
    row = lambda i: (i, 0)
    return pl.pallas_call(
        _combine_kernel,
        grid=(n // tc,),
        in_specs=[pl.BlockSpec((tc, d), row),
                  pl.BlockSpec((tc, LANES), row),
                  pl.BlockSpec((TOP_K, tc * d // (2 * LANES), LANES), lambda i: (0, i, 0))],
        out_specs=pl.BlockSpec((tc, d), row),
        out_shape=jax.ShapeDtypeStruct((n, d), F32),
        compiler_params=_cparams(("parallel",)),
        name="moe_combine",
    )(h, gates, y4)


def _layer(h3, layer, norm_mix_g, w_in, conv_w, conv_b, conv_ln_g, conv_ln_b, q_norm_g, k_norm_g,
           lambda_qk, subln_g, rel_bias, w_out, norm_ffn_g, router_w, router_b,
           w_gate_up, b_gate_up, w_down, b_down):
    b, s, d = h3.shape
    n = b * s
    d_conv = conv_w.shape[-1]
    d_attn = N_HEADS * VALUE_DIM
    lambda_init = 0.8 - 0.6 * math.exp(-0.3 * layer)
    x2 = h3.reshape(n, d)

    u, q, k, v = _in_proj(x2, norm_mix_g.reshape(1, d), w_in.astype(BF16), d_conv, d_attn, tm=512)
    conv_o = _conv(u.reshape(b, s, d_conv), conv_w, conv_b.reshape(1, d_conv),
                   conv_ln_g.reshape(1, d_conv), conv_ln_b.reshape(1, d_conv))
    attn_o = _attention(q.reshape(b, s, d_attn), k.reshape(b, s, d_attn), v.reshape(b, s, d_attn),
                        rel_bias, q_norm_g.reshape(1, VALUE_DIM), k_norm_g.reshape(1, VALUE_DIM),
                        lambda_qk, subln_g.reshape(1, VALUE_DIM), lambda_init)

    n_exp = router_w.shape[1]
    rw = jnp.zeros((d, LANES), F32).at[:, :n_exp].set(router_w)
    rwh = rw.astype(BF16)
    rwl = (rw - rwh.astype(F32)).astype(BF16)
    rb = jnp.full((1, LANES), NEG_INF, F32).at[0, :n_exp].set(router_b)
    w_out_bf = w_out.astype(BF16)
    hres, hn, idx, gates, rank, cnt = _out_proj(
        x2, conv_o.reshape(n, d_conv), attn_o.reshape(n, d_attn),
        w_out_bf[:d_conv], w_out_bf[d_conv:], norm_ffn_g.reshape(1, d), rwh, rwl, rb, tm=256)

    bm = MOE_BLOCK
    nblk = n * TOP_K // bm + n_exp
    counts = cnt[0, :n_exp].astype(I32)
    padded = (counts + bm - 1) // bm * bm
    pad_end = jnp.cumsum(padded)
    pad_start = pad_end - padded
    pos = pad_start[idx[:, :TOP_K]] + rank[:, :TOP_K]
    pos_chunks = pos.reshape(n // SC_CHUNK, SC_CHUNK, TOP_K).transpose(0, 2, 1)
    block_start = jnp.arange(nblk, dtype=I32) * bm
    block_expert = jnp.minimum(jnp.sum(block_start[:, None] >= pad_end[None, :], axis=1),
                               n_exp - 1).astype(I32)
    n_used = (pad_end[-1:] // bm).astype(I32)
    blk = jnp.arange(nblk, dtype=I32)
    first = jnp.logical_and(
        jnp.concatenate([jnp.ones((1,), bool), block_expert[1:] != block_expert[:-1]]),
        blk < n_used[0]).astype(I32)
    wslot = ((jnp.cumsum(first) - 1) % 2).astype(I32)
    eid = jnp.arange(n_exp, dtype=I32)
    later_used = jnp.logical_and(eid[None, :] > eid[:, None], counts[None, :] > 0)
    next_used = jnp.min(jnp.where(later_used, eid[None, :], n_exp), axis=1)
    next_used = jnp.where(next_used == n_exp, -1, next_used).astype(I32)
    plan = (block_expert, first, wslot, next_used[block_expert], n_used)

    d_ff = w_down.shape[1]
    bgu_p = b_gate_up.reshape(n_exp, d_ff // LANES, LANES, 2).transpose(0, 1, 3, 2).reshape(n_exp, 1, 2 * d_ff)
    r = d // (2 * LANES)
    xs = _sc_dispatch(hn.reshape(n, r, LANES), pos_chunks, nblk * bm)
    yb = _moe(plan, xs.reshape(nblk * bm * r, LANES), w_gate_up, bgu_p, w_down,
              b_down.reshape(n_exp, 1, d))
    y4 = _sc_gather(yb.reshape(nblk * bm, r, LANES), pos_chunks, n)
    out = _combine(hres, gates, y4.reshape(TOP_K, n * r, LANES))
    return out.reshape(b, s, d)


def kernel(x, norm_mix_g, w_in, conv_w, conv_b, conv_ln_g, conv_ln_b, q_norm_g, k_norm_g, lambda_qk,
           subln_g, rel_bias, w_out, norm_ffn_g, router_w, router_b, w_gate_up, b_gate_up, w_down,
           b_down):
    h = x
    for layer in range(norm_mix_g.shape[0]):
        h = _layer(h, layer, norm_mix_g[layer], w_in[layer], conv_w[layer], conv_b[layer],
                   conv_ln_g[layer], conv_ln_b[layer], q_norm_g[layer], k_norm_g[layer],
                   lambda_qk[layer], subln_g[layer], rel_bias, w_out[layer], norm_ffn_g[layer],
                   router_w[layer], router_b[layer], w_gate_up[layer], b_gate_up[layer],
                   w_down[layer], b_down[layer])
    return h
```

```python
import functools
import math

import jax
import jax.numpy as jnp
import numpy as np
from jax import lax
from jax.experimental import pallas as pl
from jax.experimental.pallas import tpu as pltpu
from jax.experimental.pallas import tpu_sc as plsc

F32 = jnp.float32
BF16 = jnp.bfloat16
I32 = jnp.int32
U32 = jnp.uint32

CHUNK = 64
CONV_WIDTH = 31
N_HEADS = 4
HEAD_DIM = 64
VALUE_DIM = 2 * HEAD_DIM
REL_BUCKETS = 32
REL_MAX_DIST = 128
N_EXPERTS = 32
TOP_K = 4
SWIGLU_LIMIT = 7.0
SWIGLU_ALPHA = 1.702
EPS = 1e-5
LOG2E = 1.4426950408889634

LANES = 128
SUBLANES = 8
MXU_DIM = 256
VMEM_LIMIT = 56 * 1024 * 1024

NEG_INF = float("-inf")


def _cparams(sem, vmem=VMEM_LIMIT, flags=None):
    return pltpu.CompilerParams(dimension_semantics=sem, vmem_limit_bytes=vmem, flags=flags)


def _dot(a, b):
    return jnp.dot(a, b, preferred_element_type=F32)


def _dot_nt(a, b):
    return lax.dot_general(a, b, (((1,), (1,)), ((), ())), preferred_element_type=F32)


def _pack_pair(lo, hi):
    lo_bits = lax.bitcast_convert_type(lo.astype(BF16).astype(F32), U32)
    hi_bits = lax.bitcast_convert_type(hi.astype(BF16).astype(F32), U32)
    return (lo_bits >> 16) | hi_bits


def _unpack_pair(w):
    return (lax.bitcast_convert_type(w << 16, F32),
            lax.bitcast_convert_type(w & jnp.uint32(0xFFFF0000), F32))


def _store_packed_rows(ref, lead, x):
    rows, k = x.shape[0], x.shape[1] // (2 * LANES)
    for j in range(k):
        w = _pack_pair(x[:, j * LANES:(j + 1) * LANES], x[:, (k + j) * LANES:(k + j + 1) * LANES])
        ref[(*lead, pl.ds(j, rows, stride=k), slice(None))] = w


def _load_packed_tile(ref, lead, rows, k, j):
    return _unpack_pair(ref[(*lead, pl.ds(j, rows, stride=k), slice(None))])


def _in_proj_kernel(x_ref, g_ref, w_ref, u_ref, q_ref, k_ref, v_ref, *, d_conv, d_attn):
    x = x_ref[...]
    ms = jnp.mean(x * x, axis=-1, keepdims=True)
    y = (x * lax.rsqrt(ms + EPS) * g_ref[...]).astype(BF16)
    proj = _dot(y, w_ref[...])
    a = proj[:, :d_conv]
    g = proj[:, d_conv:2 * d_conv]
    u_ref[...] = a * jax.nn.sigmoid(g)
    o = 2 * d_conv
    q_ref[...] = proj[:, o:o + d_attn]
    k_ref[...] = proj[:, o + d_attn:o + 2 * d_attn]
    v_ref[...] = proj[:, o + 2 * d_attn:o + 3 * d_attn].astype(BF16)


def _in_proj(x2, g, w_bf, d_conv, d_attn, tm):
    n, d = x2.shape
    d_in = w_bf.shape[1]
    row = lambda i: (i, 0)
    fixed = lambda i: (0, 0)
    return pl.pallas_call(
        functools.partial(_in_proj_kernel, d_conv=d_conv, d_attn=d_attn),
        grid=(n // tm,),
        in_specs=[pl.BlockSpec((tm, d), row),
                  pl.BlockSpec((1, d), fixed),
                  pl.BlockSpec((d, d_in), fixed)],
        out_specs=[pl.BlockSpec((tm, d_conv), row),
                   pl.BlockSpec((tm, d_attn), row),
                   pl.BlockSpec((tm, d_attn), row),
                   pl.BlockSpec((tm, d_attn), row)],
        out_shape=[jax.ShapeDtypeStruct((n, d_conv), F32),
                   jax.ShapeDtypeStruct((n, d_attn), F32),
                   jax.ShapeDtypeStruct((n, d_attn), F32),
                   jax.ShapeDtypeStruct((n, d_attn), BF16)],
        compiler_params=_cparams(("parallel",)),
        name="in_proj",
    )(x2, g, w_bf)


CONV_PAD = 32
CONV_SEQ_TILE = 512
CONV_TILE = 64


def _conv_kernel(u_ref, prev_ref, w_ref, cb_ref, lg_ref, lb_ref, o_ref, sh_ref):
    ts, c = u_ref.shape[1], u_ref.shape[2]
    plen = ts + CONV_PAD
    hist = prev_ref[0]
    hist = jnp.where(pl.program_id(1) > 0, hist, jnp.zeros_like(hist))
    sh_ref[0, pl.ds(0, CONV_PAD), :] = hist
    sh_ref[0, pl.ds(CONV_PAD, ts), :] = u_ref[0]
    sh_ref[0, pl.ds(plen, SUBLANES), :] = jnp.zeros((SUBLANES, c), F32)

    bt = CONV_PAD

    def shift(i, _):
        p0 = pl.multiple_of(i * bt, bt)
        win = sh_ref[0, pl.ds(p0, bt + SUBLANES), :]
        for r in range(1, SUBLANES):
            sh_ref[r, pl.ds(p0, bt), :] = win[r:r + bt]
        return 0

    lax.fori_loop(0, plen // bt, shift, 0)

    off0 = CONV_PAD - (CONV_WIDTH - 1)

    def body(i, _):
        t0 = pl.multiple_of(i * CONV_TILE, CONV_TILE)
        acc = jnp.zeros((CONV_TILE, c), F32)
        for j in range(CONV_WIDTH):
            off = off0 + j
            a, r = off // SUBLANES, off % SUBLANES
            start = pl.multiple_of(t0 + a * SUBLANES, SUBLANES)
            acc = acc + sh_ref[r, pl.ds(start, CONV_TILE), :] * w_ref[pl.ds(j, 1), :]
        y = acc + cb_ref[...]
        mu = jnp.mean(y, axis=-1, keepdims=True)
        yc = y - mu
        var = jnp.mean(yc * yc, axis=-1, keepdims=True)
        z = yc * lax.rsqrt(var + EPS) * lg_ref[...] + lb_ref[...]
        o_ref[0, pl.ds(t0, CONV_TILE), :] = (z * jax.nn.sigmoid(z)).astype(o_ref.dtype)
        return 0

    lax.fori_loop(0, ts // CONV_TILE, body, 0)


def _conv(u3, conv_w, conv_b, ln_g, ln_b):
    b, s, c = u3.shape
    ts = min(CONV_SEQ_TILE, s)
    hist_per_tile = ts // CONV_PAD
    fixed = lambda i, j: (0, 0)
    return pl.pallas_call(
        _conv_kernel,
        grid=(b, s // ts),
        in_specs=[pl.BlockSpec((1, ts, c), lambda i, j: (i, j, 0)),
                  pl.BlockSpec((1, CONV_PAD, c),
                               lambda i, j: (i, jnp.maximum(j * hist_per_tile - 1, 0), 0)),
                  pl.BlockSpec((CONV_WIDTH, c), fixed),
                  pl.BlockSpec((1, c), fixed),
                  pl.BlockSpec((1, c), fixed),
                  pl.BlockSpec((1, c), fixed)],
        out_specs=pl.BlockSpec((1, ts, c), lambda i, j: (i, j, 0)),
        out_shape=jax.ShapeDtypeStruct((b, s, c), BF16),
        scratch_shapes=[pltpu.VMEM((SUBLANES, ts + CONV_PAD + SUBLANES, c), F32)],
        compiler_params=_cparams(("parallel", "parallel")),
        name="conv_mixer",
    )(u3, u3, conv_w, conv_b, ln_g, ln_b)


ATT_TILE = 256
FAR_BUCKET = REL_BUCKETS // 2 - 1


def _t5_bucket(rel):
    nb = REL_BUCKETS // 2
    max_exact = nb // 2
    ret = jnp.where(rel > 0, nb, 0)
    n = jnp.abs(rel)
    nf = jnp.maximum(n, 1).astype(jnp.float32)
    large = max_exact + (jnp.log(nf / max_exact) / math.log(REL_MAX_DIST / max_exact)
                         * (nb - max_exact)).astype(jnp.int32)
    large = jnp.minimum(large, nb - 1)
    return ret + jnp.where(n < max_exact, n, large)


def _near_buckets():
    assert ATT_TILE >= REL_MAX_DIST and ATT_TILE % CHUNK == 0
    qpos = jnp.arange(ATT_TILE, dtype=I32)[:, None]
    kpos = jnp.arange(ATT_TILE, dtype=I32)[None, :]
    prev = _t5_bucket(kpos - ATT_TILE - qpos)
    diag = _t5_bucket(kpos - qpos)
    diag = jnp.where(kpos // CHUNK <= qpos // CHUNK, diag, -1)
    return jnp.stack([prev, diag]).astype(I32)


def _attn_kernel(tab_ref, q_ref, k_ref, v_ref, bkt_ref, bd_ref, qg_ref, kg_ref, lqk_ref, sg_ref, o_ref,
                 qz_ref, kn_ref, v1_ref, bias_ref, *, seq, lambda_init):
    h = pl.program_id(0)
    t = ATT_TILE
    n_tiles = seq // t
    lane = lax.broadcasted_iota(I32, (1, VALUE_DIM), 1)
    first = lane < HEAD_DIM

    @pl.when(pl.program_id(1) == 0)
    def _():
        far = tab_ref[FAR_BUCKET, h]
        for d in range(2):
            bkt = bkt_ref[d]
            tile = jnp.full((t, t), NEG_INF, F32)
            for b in range(REL_BUCKETS):
                tile = jnp.where(bkt == b, (tab_ref[b, h] - far) * LOG2E, tile)
            bias_ref[d] = tile

    def half_norm(x, g):
        x2 = x * x
        hi = x2.astype(BF16)
        lo = (x2 - hi.astype(F32)).astype(BF16)
        ms = (_dot(hi, bd_ref[...]) + _dot(lo, bd_ref[...])) * (1.0 / HEAD_DIM)
        return x * lax.rsqrt(ms + EPS) * g

    q_scale = HEAD_DIM ** -0.5 * LOG2E

    def prep(i, _):
        r0 = pl.multiple_of(i * t, t)
        qn = half_norm(q_ref[0, pl.ds(r0, t), :], qg_ref[...]) * q_scale
        qz_ref[0, pl.ds(r0, t), :] = jnp.where(first, qn, 0.0).astype(BF16)
        qz_ref[1, pl.ds(r0, t), :] = jnp.where(first, 0.0, qn).astype(BF16)
        kn_ref[pl.ds(r0, t), :] = half_norm(k_ref[0, pl.ds(r0, t), :], kg_ref[...]).astype(BF16)
        ones_col = jnp.broadcast_to(jnp.where(lane == 0, 1.0, 0.0).astype(BF16), (t, VALUE_DIM))
        v1_ref[pl.ds(r0, t), :] = jnp.concatenate([v_ref[0, pl.ds(r0, t), :], ones_col], axis=1)
        return 0

    lax.fori_loop(0, n_tiles, prep, 0)

    lqk = lqk_ref[...]
    lam = (jnp.exp(jnp.sum(lqk[0:1] * lqk[1:2], axis=-1, keepdims=True))
           - jnp.exp(jnp.sum(lqk[2:3] * lqk[3:4], axis=-1, keepdims=True)) + lambda_init)

    for i in range(n_tiles):
        q0, kend = i * t, (i + 1) * t
        keys = kn_ref[0:kend, :]
        vals = v1_ref[0:kend, :]
        maps = []
        for m in range(2):
            s = _dot_nt(qz_ref[m, q0:q0 + t, :], keys)
            parts = [s[:, kend - t:] + bias_ref[1]]
            if i >= 1:
                parts.insert(0, s[:, kend - 2 * t:kend - t] + bias_ref[0])
            if i >= 2:
                parts.insert(0, s[:, :kend - 2 * t])
            s = jnp.concatenate(parts, axis=1) if len(parts) > 1 else parts[0]
            p = jnp.exp2(s - jnp.max(s, axis=-1, keepdims=True))
            pv = _dot(p.astype(BF16), vals)
            maps.append(pv[:, :VALUE_DIM] / pv[:, VALUE_DIM:VALUE_DIM + 1])
        o = maps[0] - lam * maps[1]
        ms = jnp.mean(o * o, axis=-1, keepdims=True)
        o = o * lax.rsqrt(ms + EPS) * sg_ref[...] * (1.0 - lambda_init)
        o_ref[0, q0:q0 + t, :] = o.astype(o_ref.dtype)


def _attention(q3, k3, v3, rel_bias, q_g, k_g, lam_qk, subln_g, lambda_init):
    b, s, _ = q3.shape
    t = ATT_TILE
    head = lambda j, i: (i, 0, j)
    fixed2 = lambda j, i: (0, 0)
    fixed3 = lambda j, i: (0, 0, 0)
    half = np.arange(VALUE_DIM) // HEAD_DIM
    blockdiag = jnp.asarray(half[:, None] == half[None, :], dtype=BF16)
    return pl.pallas_call(
        functools.partial(_attn_kernel, seq=s, lambda_init=lambda_init),
        grid=(N_HEADS, b),
        in_specs=[pl.BlockSpec(memory_space=pltpu.SMEM),
                  pl.BlockSpec((1, s, VALUE_DIM), head),
                  pl.BlockSpec((1, s, VALUE_DIM), head),
                  pl.BlockSpec((1, s, VALUE_DIM), head),
                  pl.BlockSpec((2, t, t), fixed3),
                  pl.BlockSpec((VALUE_DIM, VALUE_DIM), fixed2),
                  pl.BlockSpec((1, VALUE_DIM), fixed2),
                  pl.BlockSpec((1, VALUE_DIM), fixed2),
                  pl.BlockSpec((4, HEAD_DIM), fixed2),
                  pl.BlockSpec((1, VALUE_DIM), fixed2)],
        out_specs=pl.BlockSpec((1, s, VALUE_DIM), head),
        out_shape=jax.ShapeDtypeStruct((b, s, N_HEADS * VALUE_DIM), BF16),
        scratch_shapes=[pltpu.VMEM((2, s, VALUE_DIM), BF16),
                        pltpu.VMEM((s, VALUE_DIM), BF16),
                        pltpu.VMEM((s, 2 * VALUE_DIM), BF16),
                        pltpu.VMEM((2, t, t), F32)],
        compiler_params=_cparams(("arbitrary", "arbitrary")),
        name="diff_attn",
    )(rel_bias, q3, k3, v3, _near_buckets(), blockdiag, q_g, k_g, lam_qk, subln_g)


def _out_proj_kernel(x_ref, c_ref, a_ref, wc_ref, wa_ref, g_ref, rwh_ref, rwl_ref, rb_ref, tri_ref,
                     h_ref, hn_ref, idx_ref, gate_ref, rank_ref, cnt_ref, carry_ref):
    @pl.when(pl.program_id(0) == 0)
    def _():
        carry_ref[...] = jnp.zeros_like(carry_ref)

    h = x_ref[...] + _dot(c_ref[...], wc_ref[...]) + _dot(a_ref[...], wa_ref[...])
    h_ref[...] = h
    ms = jnp.mean(h * h, axis=-1, keepdims=True)
    hn = h * lax.rsqrt(ms + EPS) * g_ref[...]
    _store_packed_rows(hn_ref, (), hn)

    hi = hn.astype(BF16)
    lo = (hn - hi.astype(F32)).astype(BF16)
    logits = (_dot(hi, rwh_ref[...]) + (_dot(hi, rwl_ref[...]) + _dot(lo, rwh_ref[...]))) + rb_ref[...]

    tm = logits.shape[0]
    lane = lax.broadcasted_iota(I32, (tm, LANES), 1).astype(F32)
    work = logits
    vals, idxs = [], []
    for _ in range(TOP_K):
        mx = jnp.max(work, axis=-1, keepdims=True)
        ix = jnp.min(jnp.where(work == mx, lane, float(LANES)), axis=-1, keepdims=True)
        vals.append(mx)
        idxs.append(ix)
        work = jnp.where(lane == ix, NEG_INF, work)
    exps = [jnp.exp(v - vals[0]) for v in vals]
    denom = exps[0]
    for e in exps[1:]:
        denom = denom + e

    sel = jnp.zeros((tm, LANES), F32)
    for ix in idxs:
        sel = sel + jnp.where(lane == ix, 1.0, 0.0)
    rank = _dot(tri_ref[...], sel.astype(BF16)) + carry_ref[...]
    carry_ref[...] = carry_ref[...] + jnp.sum(sel, axis=0, keepdims=True)
    cnt_ref[...] = carry_ref[...]

    idx_out = jnp.zeros((tm, LANES), F32)
    gate_out = jnp.zeros((tm, LANES), F32)
    rank_out = jnp.zeros((tm, LANES), F32)
    for k in range(TOP_K):
        rk = jnp.sum(jnp.where(lane == idxs[k], rank, 0.0), axis=-1, keepdims=True)
        idx_out = jnp.where(lane == k, idxs[k], idx_out)
        gate_out = jnp.where(lane == k, exps[k] / denom, gate_out)
        rank_out = jnp.where(lane == k, rk, rank_out)
    idx_ref[...] = idx_out.astype(I32)
    gate_ref[...] = gate_out
    rank_ref[...] = rank_out.astype(I32)


def _out_proj(x2, conv_o, attn_o, wc, wa, g, rwh, rwl, rb, tm):
    n, d = x2.shape
    dc, da = conv_o.shape[1], attn_o.shape[1]
    row = lambda i: (i, 0)
    fixed = lambda i: (0, 0)
    tri = jnp.tril(jnp.ones((tm, tm), F32), -1).astype(BF16)
    return pl.pallas_call(
        _out_proj_kernel,
        grid=(n // tm,),
        in_specs=[pl.BlockSpec((tm, d), row),
                  pl.BlockSpec((tm, dc), row),
                  pl.BlockSpec((tm, da), row),
                  pl.BlockSpec((dc, d), fixed),
                  pl.BlockSpec((da, d), fixed),
                  pl.BlockSpec((1, d), fixed),
                  pl.BlockSpec((d, LANES), fixed),
                  pl.BlockSpec((d, LANES), fixed),
                  pl.BlockSpec((1, LANES), fixed),
                  pl.BlockSpec((tm, tm), fixed)],
        out_specs=[pl.BlockSpec((tm, d), row),
                   pl.BlockSpec((tm * d // (2 * LANES), LANES), row),
                   pl.BlockSpec((tm, LANES), row),
                   pl.BlockSpec((tm, LANES), row),
                   pl.BlockSpec((tm, LANES), row),
                   pl.BlockSpec((1, LANES), fixed)],
        out_shape=[jax.ShapeDtypeStruct((n, d), F32),
                   jax.ShapeDtypeStruct((n * d // (2 * LANES), LANES), U32),
                   jax.ShapeDtypeStruct((n, LANES), I32),
                   jax.ShapeDtypeStruct((n, LANES), F32),
                   jax.ShapeDtypeStruct((n, LANES), I32),
                   jax.ShapeDtypeStruct((1, LANES), F32)],
        scratch_shapes=[pltpu.VMEM((1, LANES), F32)],
        compiler_params=_cparams(("arbitrary",)),
        name="out_proj_router",
    )(x2, conv_o, attn_o, wc, wa, g, rwh, rwl, rb, tri)


MOE_BLOCK = 512


def _moe_kernel(be_ref, first_ref, wslot_ref, nxt_ref, nused_ref,
                x_ref, wgu_hbm, wd_hbm, bgu_ref, bd_ref, perm_ref,
                y_ref, wgu_f32, wd_f32, wsem, wgu_bf, wd_bf, x_bf, act_bf):
    i = pl.program_id(0)
    bm = MOE_BLOCK
    n_used = nused_ref[0]
    d = wd_bf.shape[1]
    d_ff = wd_bf.shape[0]
    n_groups = d_ff // LANES

    def weight_copies(e, slot):
        return (pltpu.make_async_copy(wgu_hbm.at[e], wgu_f32.at[slot], wsem.at[0, slot]),
                pltpu.make_async_copy(wd_hbm.at[e], wd_f32.at[slot], wsem.at[1, slot]))

    @pl.when(i == 0)
    def _():
        for c in weight_copies(be_ref[0], 0):
            c.start()

    active = i < n_used

    @pl.when(jnp.logical_and(active, first_ref[i] == 1))
    def _():
        ws = wslot_ref[i]
        for c in weight_copies(be_ref[i], ws):
            c.wait()
        for g in range(n_groups):
            cols = pl.ds(g * MXU_DIM, MXU_DIM)
            wgu_bf[:, cols] = _dot(wgu_f32[ws, :, cols].astype(BF16), perm_ref[...]).astype(BF16)
        wd_bf[...] = wd_f32[ws].astype(BF16)

        @pl.when(nxt_ref[i] >= 0)
        def _():
            for c in weight_copies(nxt_ref[i], 1 - ws):
                c.start()

    def activate(g, gu):
        gu = gu + bgu_ref[0, :, g * MXU_DIM:(g + 1) * MXU_DIM]
        gate = jnp.minimum(gu[:, :LANES], SWIGLU_LIMIT)
        lin = jnp.clip(gu[:, LANES:], -SWIGLU_LIMIT, SWIGLU_LIMIT)
        act_bf[:, g * LANES:(g + 1) * LANES] = (
            gate * jax.nn.sigmoid(SWIGLU_ALPHA * gate) * (lin + 1.0)).astype(BF16)

    @pl.when(active)
    def _():
        k = d // (2 * LANES)
        for j in range(k):
            lo, hi = _load_packed_tile(x_ref, (), bm, k, j)
            x_bf[:, j * LANES:(j + 1) * LANES] = lo.astype(BF16)
            x_bf[:, (k + j) * LANES:(k + j + 1) * LANES] = hi.astype(BF16)
        for g in range(n_groups):
            activate(g, _dot(x_bf[...], wgu_bf[:, g * MXU_DIM:(g + 1) * MXU_DIM]))
        _store_packed_rows(y_ref, (), _dot(act_bf[...], wd_bf[...]) + bd_ref[0])

    @pl.when(i >= n_used)
    def _():
        y_ref[...] = jnp.zeros_like(y_ref)


def _deinterleave_perm():
    src = np.arange(MXU_DIM)
    dst = np.where(src % 2 == 0, src // 2, LANES + src // 2)
    p = np.zeros((MXU_DIM, MXU_DIM), np.float32)
    p[src, dst] = 1.0
    return jnp.asarray(p, dtype=BF16)


def _moe(plan, xs, w_gate_up, bgu_p, w_down, b_down):
    n_exp, d, d_gu = w_gate_up.shape
    d_ff = w_down.shape[1]
    bm = MOE_BLOCK
    nblk = plan[0].shape[0]
    slab = bm * d // (2 * LANES)
    by_expert = lambda i, be, *_: (be[i], 0, 0)
    grid_spec = pltpu.PrefetchScalarGridSpec(
        num_scalar_prefetch=5,
        grid=(nblk,),
        in_specs=[
            pl.BlockSpec((slab, LANES), lambda i, *_: (i, 0)),
            pl.BlockSpec(memory_space=pl.ANY),
            pl.BlockSpec(memory_space=pl.ANY),
            pl.BlockSpec((1, 1, d_gu), by_expert),
            pl.BlockSpec((1, 1, d), by_expert),
            pl.BlockSpec((MXU_DIM, MXU_DIM), lambda i, *_: (0, 0)),
        ],
        out_specs=pl.BlockSpec((slab, LANES), lambda i, *_: (i, 0)),
        scratch_shapes=[pltpu.VMEM((2, d, d_gu), F32),
                        pltpu.VMEM((2, d_ff, d), F32),
                        pltpu.SemaphoreType.DMA((2, 2)),
                        pltpu.VMEM((d, d_gu), BF16),
                        pltpu.VMEM((d_ff, d), BF16),
                        pltpu.VMEM((bm, d), BF16),
                        pltpu.VMEM((bm, d_ff), BF16)],
    )
    return pl.pallas_call(
        _moe_kernel,
        grid_spec=grid_spec,
        out_shape=jax.ShapeDtypeStruct((nblk * slab, LANES), U32),
        compiler_params=_cparams(("arbitrary",)),
        name="moe_experts",
    )(*plan, xs, w_gate_up, w_down, bgu_p, b_down, _deinterleave_perm())


SC_CORES = 2
SC_SUBCORES = 16
SC_CHUNK = 64


def _sc_mesh():
    return plsc.VectorSubcoreMesh(core_axis_name="c", subcore_axis_name="s")


def _sc_dispatch(rows3, pos_chunks, n_slots):
    n, r, _ = rows3.shape
    workers = SC_CORES * SC_SUBCORES
    per_worker = n // SC_CHUNK // workers

    def body(rows_hbm, pos_hbm, out_hbm, idx_v, rows_v, sem):
        wid = lax.axis_index("s") * SC_CORES + lax.axis_index("c")

        @pl.loop(0, per_worker)
        def _(c):
            chunk = wid * per_worker + c
            pltpu.sync_copy(pos_hbm.at[chunk], idx_v)
            pltpu.sync_copy(rows_hbm.at[pl.ds(chunk * SC_CHUNK, SC_CHUNK)], rows_v)
            copies = [pltpu.async_copy(rows_v, out_hbm.at[idx_v.at[k]], sem) for k in range(TOP_K)]
            for cp in copies:
                cp.wait()

    return pl.kernel(
        body,
        out_type=jax.ShapeDtypeStruct((n_slots, r, LANES), rows3.dtype),
        mesh=_sc_mesh(),
        scratch_types=[pltpu.VMEM((TOP_K, SC_CHUNK), I32),
                       pltpu.VMEM((SC_CHUNK, r, LANES), rows3.dtype),
                       pltpu.SemaphoreType.DMA],
        name="sc_dispatch",
    )(rows3, pos_chunks)


def _sc_gather(rows3, pos_chunks, n):
    _, r, _ = rows3.shape
    workers = SC_CORES * SC_SUBCORES
    per_worker = n // SC_CHUNK // workers

    def body(rows_hbm, pos_hbm, out_hbm, idx_v, buf_a, buf_b, sem_a, sem_b):
        wid = lax.axis_index("s") * SC_CORES + lax.axis_index("c")
        bufs, sems = (buf_a, buf_b), (sem_a, sem_b)

        @pl.loop(0, per_worker)
        def _(c):
            chunk = wid * per_worker + c
            pltpu.sync_copy(pos_hbm.at[chunk], idx_v)
            gathers = [None] * TOP_K
            gathers[0] = pltpu.async_copy(rows_hbm.at[idx_v.at[0]], bufs[0], sems[0])
            for k in range(TOP_K):
                if k + 1 < TOP_K:
                    nxt = (k + 1) % 2
                    gathers[k + 1] = pltpu.async_copy(rows_hbm.at[idx_v.at[k + 1]], bufs[nxt], sems[nxt])
                gathers[k].wait()
                pltpu.sync_copy(bufs[k % 2], out_hbm.at[k, pl.ds(chunk * SC_CHUNK, SC_CHUNK)])

    return pl.kernel(
        body,
        out_type=jax.ShapeDtypeStruct((TOP_K, n, r, LANES), rows3.dtype),
        mesh=_sc_mesh(),
        scratch_types=[pltpu.VMEM((TOP_K, SC_CHUNK), I32),
                       pltpu.VMEM((SC_CHUNK, r, LANES), rows3.dtype),
                       pltpu.VMEM((SC_CHUNK, r, LANES), rows3.dtype),
                       pltpu.SemaphoreType.DMA,
                       pltpu.SemaphoreType.DMA],
        name="sc_gather",
    )(rows3, pos_chunks)


COMBINE_TILE = 512


def _combine_kernel(h_ref, gate_ref, y_ref, o_ref):
    tc, d = h_ref.shape
    kt = d // (2 * LANES)
    gates = gate_ref[...]
    for j in range(kt):
        lo_acc = h_ref[:, j * LANES:(j + 1) * LANES]
        hi_acc = h_ref[:, (kt + j) * LANES:(kt + j + 1) * LANES]
        for k in range(TOP_K):
            lo, hi = _load_packed_tile(y_ref, (k,), tc, kt, j)
            g = gates[:, k:k + 1]
            lo_acc = lo_acc + g * lo
            hi_acc = hi_acc + g * hi
        o_ref[:, j * LANES:(j + 1) * LANES] = lo_acc
        o_ref[:, (kt + j) * LANES:(kt + j + 1) * LANES] = hi_acc


def _combine(h, gates, y4):
    n, d = h.shape
    tc = COMBINE_TILE
    row = lambda i: (i, 0)
    return pl.pallas_call(
        _combine_kernel,
        grid=(n // tc,),
        in_specs=[pl.BlockSpec((tc, d), row),
                  pl.BlockSpec((tc, LANES), row),
                  pl.BlockSpec((TOP_K, tc * d // (2 * LANES), LANES), lambda i: (0, i, 0))],
        out_specs=pl.BlockSpec((tc, d), row),
        out_shape=jax.ShapeDtypeStruct((n, d), F32),
        compiler_params=_cparams(("parallel",)),
        name="moe_combine",
    )(h, gates, y4)


def _layer(h3, layer, norm_mix_g, w_in, conv_w, conv_b, conv_ln_g, conv_ln_b, q_norm_g, k_norm_g,
           lambda_qk, subln_g, rel_bias, w_out, norm_ffn_g, router_w, router_b,
           w_gate_up, b_gate_up, w_down, b_down):
    b, s, d = h3.shape
    n = b * s
    d_conv = conv_w.shape[-1]
    d_attn = N_HEADS * VALUE_DIM
    lambda_init = 0.8 - 0.6 * math.exp(-0.3 * layer)
    x2 = h3.reshape(n, d)

    u, q, k, v = _in_proj(x2, norm_mix_g.reshape(1, d), w_in.astype(BF16), d_conv, d_attn, tm=512)
    conv_o = _conv(u.reshape(b, s, d_conv), conv_w, conv_b.reshape(1, d_conv),
                   conv_ln_g.reshape(1, d_conv), conv_ln_b.reshape(1, d_conv))
    attn_o = _attention(q.reshape(b, s, d_attn), k.reshape(b, s, d_attn), v.reshape(b, s, d_attn),
                        rel_bias, q_norm_g.reshape(1, VALUE_DIM), k_norm_g.reshape(1, VALUE_DIM),
                        lambda_qk, subln_g.reshape(1, VALUE_DIM), lambda_init)

    n_exp = router_w.shape[1]
    rw = jnp.zeros((d, LANES), F32).at[:, :n_exp].set(router_w)
    rwh = rw.astype(BF16)
    rwl = (rw - rwh.astype(F32)).astype(BF16)
    rb = jnp.full((1, LANES), NEG_INF, F32).at[0, :n_exp].set(router_b)
    w_out_bf = w_out.astype(BF16)
    hres, hn, idx, gates, rank, cnt = _out_proj(
        x2, conv_o.reshape(n, d_conv), attn_o.reshape(n, d_attn),
        w_out_bf[:d_conv], w_out_bf[d_conv:], norm_ffn_g.reshape(1, d), rwh, rwl, rb, tm=512)

    bm = MOE_BLOCK
    nblk = n * TOP_K // bm + n_exp
    counts = cnt[0, :n_exp].astype(I32)
    padded = (counts + bm - 1) // bm * bm
    pad_end = jnp.cumsum(padded)
    pad_start = pad_end - padded
    pos = pad_start[idx[:, :TOP_K]] + rank[:, :TOP_K]
    pos_chunks = pos.reshape(n // SC_CHUNK, SC_CHUNK, TOP_K).transpose(0, 2, 1)
    block_start = jnp.arange(nblk, dtype=I32) * bm
    block_expert = jnp.minimum(jnp.sum(block_start[:, None] >= pad_end[None, :], axis=1),
                               n_exp - 1).astype(I32)
    n_used = (pad_end[-1:] // bm).astype(I32)
    blk = jnp.arange(nblk, dtype=I32)
    first = jnp.logical_and(
        jnp.concatenate([jnp.ones((1,), bool), block_expert[1:] != block_expert[:-1]]),
        blk < n_used[0]).astype(I32)
    wslot = ((jnp.cumsum(first) - 1) % 2).astype(I32)
    eid = jnp.arange(n_exp, dtype=I32)
    later_used = jnp.logical_and(eid[None, :] > eid[:, None], counts[None, :] > 0)
    next_used = jnp.min(jnp.where(later_used, eid[None, :], n_exp), axis=1)
    next_used = jnp.where(next_used == n_exp, -1, next_used).astype(I32)
    plan = (block_expert, first, wslot, next_used[block_expert], n_used)

    d_ff = w_down.shape[1]
    bgu_p = b_gate_up.reshape(n_exp, d_ff // LANES, LANES, 2).transpose(0, 1, 3, 2).reshape(n_exp, 1, 2 * d_ff)
    r = d // (2 * LANES)
    xs = _sc_dispatch(hn.reshape(n, r, LANES), pos_chunks, nblk * bm)
    yb = _moe(plan, xs.reshape(nblk * bm * r, LANES), w_gate_up, bgu_p, w_down,
              b_down.reshape(n_exp, 1, d))
    y4 = _sc_gather(yb.reshape(nblk * bm, r, LANES), pos_chunks, n)
    out = _combine(hres, gates, y4.reshape(TOP_K, n * r, LANES))
    return out.reshape(b, s, d)


def kernel(x, norm_mix_g, w_in, conv_w, conv_b, conv_ln_g, conv_ln_b, q_norm_g, k_norm_g, lambda_qk,
           subln_g, rel_bias, w_out, norm_ffn_g, router_w, router_b, w_gate_up, b_gate_up, w_down,
           b_down):
    h = x
    for layer in range(norm_mix_g.shape[0]):
        h = _layer(h, layer, norm_mix_g[layer], w_in[layer], conv_w[layer], conv_b[layer],
                   conv_ln_g[layer], conv_ln_b[layer], q_norm_g[layer], k_norm_g[layer],
                   lambda_qk[layer], subln_g[layer], rel_bias, w_out[layer], norm_ffn_g[layer],
                   router_w[layer], router_b[layer], w_gate_up[layer], b_gate_up[layer],
                   w_down[layer], b_down[layer])
    return h
```

```python
import functools
import math

import jax
import jax.numpy as jnp
import numpy as np
from jax import lax
from jax.experimental import pallas as pl
from jax.experimental.pallas import tpu as pltpu
from jax.experimental.pallas import tpu_sc as plsc

F32 = jnp.float32
BF16 = jnp.bfloat16
I32 = jnp.int32
U32 = jnp.uint32

CHUNK = 64
CONV_WIDTH = 31
N_HEADS = 4
HEAD_DIM = 64
VALUE_DIM = 2 * HEAD_DIM
REL_BUCKETS = 32
REL_MAX_DIST = 128
N_EXPERTS = 32
TOP_K = 4
SWIGLU_LIMIT = 7.0
SWIGLU_ALPHA = 1.702
EPS = 1e-5
LOG2E = 1.4426950408889634

LANES = 128
SUBLANES = 8
MXU_DIM = 256
VMEM_LIMIT = 56 * 1024 * 1024

NEG_INF = float("-inf")


def _cparams(sem, vmem=VMEM_LIMIT, flags=None):
    return pltpu.CompilerParams(dimension_semantics=sem, vmem_limit_bytes=vmem, flags=flags)


def _dot(a, b):
    return jnp.dot(a, b, preferred_element_type=F32)


def _dot_nt(a, b):
    return lax.dot_general(a, b, (((1,), (1,)), ((), ())), preferred_element_type=F32)


def _pack_pair(lo, hi):
    lo_bits = lax.bitcast_convert_type(lo.astype(BF16).astype(F32), U32)
    hi_bits = lax.bitcast_convert_type(hi.astype(BF16).astype(F32), U32)
    return (lo_bits >> 16) | hi_bits


def _unpack_pair(w):
    return (lax.bitcast_convert_type(w << 16, F32),
            lax.bitcast_convert_type(w & jnp.uint32(0xFFFF0000), F32))


def _store_packed_rows(ref, lead, x):
    rows, k = x.shape[0], x.shape[1] // (2 * LANES)
    for j in range(k):
        w = _pack_pair(x[:, j * LANES:(j + 1) * LANES], x[:, (k + j) * LANES:(k + j + 1) * LANES])
        ref[(*lead, pl.ds(j, rows, stride=k), slice(None))] = w


def _load_packed_tile(ref, lead, rows, k, j):
    return _unpack_pair(ref[(*lead, pl.ds(j, rows, stride=k), slice(None))])


def _in_proj_kernel(x_ref, g_ref, w_ref, u_ref, q_ref, k_ref, v_ref, *, d_conv, d_attn):
    x = x_ref[...]
    ms = jnp.mean(x * x, axis=-1, keepdims=True)
    y = (x * lax.rsqrt(ms + EPS) * g_ref[...]).astype(BF16)
    proj = _dot(y, w_ref[...])
    a = proj[:, :d_conv]
    g = proj[:, d_conv:2 * d_conv]
    u_ref[...] = a * jax.nn.sigmoid(g)
    o = 2 * d_conv
    q_ref[...] = proj[:, o:o + d_attn]
    k_ref[...] = proj[:, o + d_attn:o + 2 * d_attn]
    v_ref[...] = proj[:, o + 2 * d_attn:o + 3 * d_attn].astype(BF16)


def _in_proj(x2, g, w_bf, d_conv, d_attn, tm):
    n, d = x2.shape
    d_in = w_bf.shape[1]
    row = lambda i: (i, 0)
    fixed = lambda i: (0, 0)
    return pl.pallas_call(
        functools.partial(_in_proj_kernel, d_conv=d_conv, d_attn=d_attn),
        grid=(n // tm,),
        in_specs=[pl.BlockSpec((tm, d), row),
                  pl.BlockSpec((1, d), fixed),
                  pl.BlockSpec((d, d_in), fixed)],
        out_specs=[pl.BlockSpec((tm, d_conv), row),
                   pl.BlockSpec((tm, d_attn), row),
                   pl.BlockSpec((tm, d_attn), row),
                   pl.BlockSpec((tm, d_attn), row)],
        out_shape=[jax.ShapeDtypeStruct((n, d_conv), F32),
                   jax.ShapeDtypeStruct((n, d_attn), F32),
                   jax.ShapeDtypeStruct((n, d_attn), F32),
                   jax.ShapeDtypeStruct((n, d_attn), BF16)],
        compiler_params=_cparams(("parallel",)),
        name="in_proj",
    )(x2, g, w_bf)


CONV_PAD = 32
CONV_SEQ_TILE = 512
CONV_TILE = 128
CONV_NORM_TILE = 256


def _conv_kernel(u_ref, prev_ref, w_ref, cb_ref, lg_ref, lb_ref, o_ref, sh_ref, y_ref):
    ts, c = u_ref.shape[1], u_ref.shape[2]
    plen = ts + CONV_PAD
    hist = prev_ref[0]
    hist = jnp.where(pl.program_id(1) > 0, hist, jnp.zeros_like(hist))
    sh_ref[0, pl.ds(0, CONV_PAD), :] = hist
    sh_ref[0, pl.ds(CONV_PAD, ts), :] = u_ref[0]
    sh_ref[0, pl.ds(plen, SUBLANES), :] = jnp.zeros((SUBLANES, c), F32)

    bt = CONV_PAD

    def shift(i, _):
        p0 = pl.multiple_of(i * bt, bt)
        win = sh_ref[0, pl.ds(p0, bt + SUBLANES), :]
        for r in range(1, SUBLANES):
            sh_ref[r, pl.ds(p0, bt), :] = win[r:r + bt]
        return 0

    lax.fori_loop(0, plen // bt, shift, 0)

    off0 = CONV_PAD - (CONV_WIDTH - 1)

    by_shift = {}
    for j in range(CONV_WIDTH):
        a, r = divmod(off0 + j, SUBLANES)
        by_shift.setdefault(r, []).append((a, j))

    for lt in range(c // LANES):
        lanes = slice(lt * LANES, (lt + 1) * LANES)
        taps = [w_ref[pl.ds(j, 1), lanes] for j in range(CONV_WIDTH)]

        def tap_body(i, _, lanes=lanes, taps=taps):
            t0 = pl.multiple_of(i * CONV_TILE, CONV_TILE)
            acc = jnp.zeros((CONV_TILE, LANES), F32)
            for r, group in by_shift.items():
                a_lo = min(a for a, _ in group)
                a_hi = max(a for a, _ in group)
                rows = CONV_TILE + (a_hi - a_lo) * SUBLANES
                start = pl.multiple_of(t0 + a_lo * SUBLANES, SUBLANES)
                win = sh_ref[r, pl.ds(start, rows), lanes]
                for a, j in group:
                    lo = (a - a_lo) * SUBLANES
                    acc = acc + win[lo:lo + CONV_TILE] * taps[j]
            y_ref[pl.ds(t0, CONV_TILE), lanes] = acc
            return 0

        lax.fori_loop(0, ts // CONV_TILE, tap_body, 0)

    def norm_body(i, _):
        t0 = pl.multiple_of(i * CONV_NORM_TILE, CONV_NORM_TILE)
        y = y_ref[pl.ds(t0, CONV_NORM_TILE), :] + cb_ref[...]
        mu = jnp.mean(y, axis=-1, keepdims=True)
        yc = y - mu
        var = jnp.mean(yc * yc, axis=-1, keepdims=True)
        z = yc * lax.rsqrt(var + EPS) * lg_ref[...] + lb_ref[...]
        o_ref[0, pl.ds(t0, CONV_NORM_TILE), :] = (z * jax.nn.sigmoid(z)).astype(o_ref.dtype)
        return 0

    lax.fori_loop(0, ts // CONV_NORM_TILE, norm_body, 0)


def _conv(u3, conv_w, conv_b, ln_g, ln_b):
    b, s, c = u3.shape
    ts = min(CONV_SEQ_TILE, s)
    hist_per_tile = ts // CONV_PAD
    fixed = lambda i, j: (0, 0)
    return pl.pallas_call(
        _conv_kernel,
        grid=(b, s // ts),
        in_specs=[pl.BlockSpec((1, ts, c), lambda i, j: (i, j, 0)),
                  pl.BlockSpec((1, CONV_PAD, c),
                               lambda i, j: (i, jnp.maximum(j * hist_per_tile - 1, 0), 0)),
                  pl.BlockSpec((CONV_WIDTH, c), fixed),
                  pl.BlockSpec((1, c), fixed),
                  pl.BlockSpec((1, c), fixed),
                  pl.BlockSpec((1, c), fixed)],
        out_specs=pl.BlockSpec((1, ts, c), lambda i, j: (i, j, 0)),
        out_shape=jax.ShapeDtypeStruct((b, s, c), BF16),
        scratch_shapes=[pltpu.VMEM((SUBLANES, ts + CONV_PAD + SUBLANES, c), F32),
                        pltpu.VMEM((ts, c), F32)],
        compiler_params=_cparams(("parallel", "parallel")),
        name="conv_mixer",
    )(u3, u3, conv_w, conv_b, ln_g, ln_b)


ATT_TILE = 256
FAR_BUCKET = REL_BUCKETS // 2 - 1


def _t5_bucket(rel):
    nb = REL_BUCKETS // 2
    max_exact = nb // 2
    ret = jnp.where(rel > 0, nb, 0)
    n = jnp.abs(rel)
    nf = jnp.maximum(n, 1).astype(jnp.float32)
    large = max_exact + (jnp.log(nf / max_exact) / math.log(REL_MAX_DIST / max_exact)
                         * (nb - max_exact)).astype(jnp.int32)
    large = jnp.minimum(large, nb - 1)
    return ret + jnp.where(n < max_exact, n, large)


def _near_buckets():
    assert ATT_TILE >= REL_MAX_DIST and ATT_TILE % CHUNK == 0
    qpos = jnp.arange(ATT_TILE, dtype=I32)[:, None]
    kpos = jnp.arange(ATT_TILE, dtype=I32)[None, :]
    prev = _t5_bucket(kpos - ATT_TILE - qpos)
    diag = _t5_bucket(kpos - qpos)
    diag = jnp.where(kpos // CHUNK <= qpos // CHUNK, diag, -1)
    return jnp.stack([prev, diag]).astype(I32)


def _attn_kernel(tab_ref, q_ref, k_ref, v_ref, bkt_ref, bd_ref, qg_ref, kg_ref, lqk_ref, sg_ref, o_ref,
                 qz_ref, kn_ref, v1_ref, bias_ref, *, seq, lambda_init):
    h = pl.program_id(0)
    t = ATT_TILE
    n_tiles = seq // t
    lane = lax.broadcasted_iota(I32, (1, VALUE_DIM), 1)
    first = lane < HEAD_DIM

    @pl.when(pl.program_id(1) == 0)
    def _():
        far = tab_ref[FAR_BUCKET, h]
        for d in range(2):
            bkt = bkt_ref[d]
            tile = jnp.full((t, t), NEG_INF, F32)
            for b in range(REL_BUCKETS):
                tile = jnp.where(bkt == b, (tab_ref[b, h] - far) * LOG2E, tile)
            bias_ref[d] = tile

    def half_norm(x, g):
        x2 = x * x
        hi = x2.astype(BF16)
        lo = (x2 - hi.astype(F32)).astype(BF16)
        ms = (_dot(hi, bd_ref[...]) + _dot(lo, bd_ref[...])) * (1.0 / HEAD_DIM)
        return x * lax.rsqrt(ms + EPS) * g

    q_scale = HEAD_DIM ** -0.5 * LOG2E

    def prep(i, _):
        r0 = pl.multiple_of(i * t, t)
        qn = half_norm(q_ref[0, pl.ds(r0, t), :], qg_ref[...]) * q_scale
        qz_ref[0, pl.ds(r0, t), :] = jnp.where(first, qn, 0.0).astype(BF16)
        qz_ref[1, pl.ds(r0, t), :] = jnp.where(first, 0.0, qn).astype(BF16)
        kn_ref[pl.ds(r0, t), :] = half_norm(k_ref[0, pl.ds(r0, t), :], kg_ref[...]).astype(BF16)
        ones_col = jnp.broadcast_to(jnp.where(lane == 0, 1.0, 0.0).astype(BF16), (t, VALUE_DIM))
        v1_ref[pl.ds(r0, t), :] = jnp.concatenate([v_ref[0, pl.ds(r0, t), :], ones_col], axis=1)
        return 0

    lax.fori_loop(0, n_tiles, prep, 0)

    lqk = lqk_ref[...]
    lam = (jnp.exp(jnp.sum(lqk[0:1] * lqk[1:2], axis=-1, keepdims=True))
           - jnp.exp(jnp.sum(lqk[2:3] * lqk[3:4], axis=-1, keepdims=True)) + lambda_init)

    for i in range(n_tiles):
        q0, kend = i * t, (i + 1) * t
        keys = kn_ref[0:kend, :]
        vals = v1_ref[0:kend, :]
        maps = []
        for m in range(2):
            s = _dot_nt(qz_ref[m, q0:q0 + t, :], keys)
            parts = [s[:, kend - t:] + bias_ref[1]]
            if i >= 1:
                parts.insert(0, s[:, kend - 2 * t:kend - t] + bias_ref[0])
            if i >= 2:
                parts.insert(0, s[:, :kend - 2 * t])
            s = jnp.concatenate(parts, axis=1) if len(parts) > 1 else parts[0]
            p = jnp.exp2(s - jnp.max(s, axis=-1, keepdims=True))
            pv = _dot(p.astype(BF16), vals)
            maps.append(pv[:, :VALUE_DIM] / pv[:, VALUE_DIM:VALUE_DIM + 1])
        o = maps[0] - lam * maps[1]
        ms = jnp.mean(o * o, axis=-1, keepdims=True)
        o = o * lax.rsqrt(ms + EPS) * sg_ref[...] * (1.0 - lambda_init)
        o_ref[0, q0:q0 + t, :] = o.astype(o_ref.dtype)


def _attention(q3, k3, v3, rel_bias, q_g, k_g, lam_qk, subln_g, lambda_init):
    b, s, _ = q3.shape
    t = ATT_TILE
    head = lambda j, i: (i, 0, j)
    fixed2 = lambda j, i: (0, 0)
    fixed3 = lambda j, i: (0, 0, 0)
    half = np.arange(VALUE_DIM) // HEAD_DIM
    blockdiag = jnp.asarray(half[:, None] == half[None, :], dtype=BF16)
    return pl.pallas_call(
        functools.partial(_attn_kernel, seq=s, lambda_init=lambda_init),
        grid=(N_HEADS, b),
        in_specs=[pl.BlockSpec(memory_space=pltpu.SMEM),
                  pl.BlockSpec((1, s, VALUE_DIM), head),
                  pl.BlockSpec((1, s, VALUE_DIM), head),
                  pl.BlockSpec((1, s, VALUE_DIM), head),
                  pl.BlockSpec((2, t, t), fixed3),
                  pl.BlockSpec((VALUE_DIM, VALUE_DIM), fixed2),
                  pl.BlockSpec((1, VALUE_DIM), fixed2),
                  pl.BlockSpec((1, VALUE_DIM), fixed2),
                  pl.BlockSpec((4, HEAD_DIM), fixed2),
                  pl.BlockSpec((1, VALUE_DIM), fixed2)],
        out_specs=pl.BlockSpec((1, s, VALUE_DIM), head),
        out_shape=jax.ShapeDtypeStruct((b, s, N_HEADS * VALUE_DIM), BF16),
        scratch_shapes=[pltpu.VMEM((2, s, VALUE_DIM), BF16),
                        pltpu.VMEM((s, VALUE_DIM), BF16),
                        pltpu.VMEM((s, 2 * VALUE_DIM), BF16),
                        pltpu.VMEM((2, t, t), F32)],
        compiler_params=_cparams(("arbitrary", "arbitrary")),
        name="diff_attn",
    )(rel_bias, q3, k3, v3, _near_buckets(), blockdiag, q_g, k_g, lam_qk, subln_g)


def _out_proj_kernel(x_ref, c_ref, a_ref, wc_ref, wa_ref, g_ref, rwh_ref, rwl_ref, rb_ref, tri_ref,
                     h_ref, hn_ref, idx_ref, gate_ref, rank_ref, cnt_ref, carry_ref):
    @pl.when(pl.program_id(0) == 0)
    def _():
        carry_ref[...] = jnp.zeros_like(carry_ref)

    h = x_ref[...] + _dot(c_ref[...], wc_ref[...]) + _dot(a_ref[...], wa_ref[...])
    h_ref[...] = h
    ms = jnp.mean(h * h, axis=-1, keepdims=True)
    hn = h * lax.rsqrt(ms + EPS) * g_ref[...]
    _store_packed_rows(hn_ref, (), hn)

    hi = hn.astype(BF16)
    lo = (hn - hi.astype(F32)).astype(BF16)
    logits = (_dot(hi, rwh_ref[...]) + (_dot(hi, rwl_ref[...]) + _dot(lo, rwh_ref[...]))) + rb_ref[...]

    tm = logits.shape[0]
    lane = lax.broadcasted_iota(I32, (tm, LANES), 1).astype(F32)
    work = logits
    vals, idxs = [], []
    for _ in range(TOP_K):
        mx = jnp.max(work, axis=-1, keepdims=True)
        ix = jnp.min(jnp.where(work == mx, lane, float(LANES)), axis=-1, keepdims=True)
        vals.append(mx)
        idxs.append(ix)
        work = jnp.where(lane == ix, NEG_INF, work)
    exps = [jnp.exp(v - vals[0]) for v in vals]
    denom = exps[0]
    for e in exps[1:]:
        denom = denom + e

    sel = jnp.zeros((tm, LANES), F32)
    for ix in idxs:
        sel = sel + jnp.where(lane == ix, 1.0, 0.0)
    rank = _dot(tri_ref[...], sel.astype(BF16)) + carry_ref[...]
    carry_ref[...] = carry_ref[...] + jnp.sum(sel, axis=0, keepdims=True)
    cnt_ref[...] = carry_ref[...]

    idx_out = jnp.zeros((tm, LANES), F32)
    gate_out = jnp.zeros((tm, LANES), F32)
    rank_out = jnp.zeros((tm, LANES), F32)
    for k in range(TOP_K):
        rk = jnp.sum(jnp.where(lane == idxs[k], rank, 0.0), axis=-1, keepdims=True)
        idx_out = jnp.where(lane == k, idxs[k], idx_out)
        gate_out = jnp.where(lane == k, exps[k] / denom, gate_out)
        rank_out = jnp.where(lane == k, rk, rank_out)
    idx_ref[...] = jnp.transpose(idx_out)[:SUBLANES].astype(I32)
    rank_ref[...] = jnp.transpose(rank_out)[:SUBLANES].astype(I32)
    gate_ref[...] = gate_out


def _out_proj(x2, conv_o, attn_o, wc, wa, g, rwh, rwl, rb, tm):
    n, d = x2.shape
    dc, da = conv_o.shape[1], attn_o.shape[1]
    row = lambda i: (i, 0)
    fixed = lambda i: (0, 0)
    tri = jnp.tril(jnp.ones((tm, tm), F32), -1).astype(BF16)
    return pl.pallas_call(
        _out_proj_kernel,
        grid=(n // tm,),
        in_specs=[pl.BlockSpec((tm, d), row),
                  pl.BlockSpec((tm, dc), row),
                  pl.BlockSpec((tm, da), row),
                  pl.BlockSpec((dc, d), fixed),
                  pl.BlockSpec((da, d), fixed),
                  pl.BlockSpec((1, d), fixed),
                  pl.BlockSpec((d, LANES), fixed),
                  pl.BlockSpec((d, LANES), fixed),
                  pl.BlockSpec((1, LANES), fixed),
                  pl.BlockSpec((tm, tm), fixed)],
        out_specs=[pl.BlockSpec((tm, d), row),
                   pl.BlockSpec((tm * d // (2 * LANES), LANES), row),
                   pl.BlockSpec((SUBLANES, tm), lambda i: (0, i)),
                   pl.BlockSpec((tm, LANES), row),
                   pl.BlockSpec((SUBLANES, tm), lambda i: (0, i)),
                   pl.BlockSpec((1, LANES), fixed)],
        out_shape=[jax.ShapeDtypeStruct((n, d), F32),
                   jax.ShapeDtypeStruct((n * d // (2 * LANES), LANES), U32),
                   jax.ShapeDtypeStruct((SUBLANES, n), I32),
                   jax.ShapeDtypeStruct((n, LANES), F32),
                   jax.ShapeDtypeStruct((SUBLANES, n), I32),
                   jax.ShapeDtypeStruct((1, LANES), F32)],
        scratch_shapes=[pltpu.VMEM((1, LANES), F32)],
        compiler_params=_cparams(("arbitrary",)),
        name="out_proj_router",
    )(x2, conv_o, attn_o, wc, wa, g, rwh, rwl, rb, tri)


MOE_BLOCK = 512


def _moe_kernel(be_ref, first_ref, wslot_ref, nxt_ref, nused_ref,
                x_ref, wgu_hbm, wd_hbm, bgu_ref, bd_ref, perm_ref,
                y_ref, wgu_f32, wd_f32, wsem, wgu_bf, wd_bf, x_bf, act_bf):
    i = pl.program_id(0)
    bm = MOE_BLOCK
    n_used = nused_ref[0]
    d = wd_bf.shape[1]
    d_ff = wd_bf.shape[0]
    n_groups = d_ff // LANES

    def weight_copies(e, slot):
        return (pltpu.make_async_copy(wgu_hbm.at[e], wgu_f32.at[slot], wsem.at[0, slot]),
                pltpu.make_async_copy(wd_hbm.at[e], wd_f32.at[slot], wsem.at[1, slot]))

    @pl.when(i == 0)
    def _():
        for c in weight_copies(be_ref[0], 0):
            c.start()

    active = i < n_used

    @pl.when(jnp.logical_and(active, first_ref[i] == 1))
    def _():
        ws = wslot_ref[i]
        for c in weight_copies(be_ref[i], ws):
            c.wait()
        for g in range(n_groups):
            cols = pl.ds(g * MXU_DIM, MXU_DIM)
            wgu_bf[:, cols] = _dot(wgu_f32[ws, :, cols].astype(BF16), perm_ref[...]).astype(BF16)
        wd_bf[...] = wd_f32[ws].astype(BF16)

        @pl.when(nxt_ref[i] >= 0)
        def _():
            for c in weight_copies(nxt_ref[i], 1 - ws):
                c.start()

    def activate(g, gu):
        gu = gu + bgu_ref[0, :, g * MXU_DIM:(g + 1) * MXU_DIM]
        gate = jnp.minimum(gu[:, :LANES], SWIGLU_LIMIT)
        lin = jnp.clip(gu[:, LANES:], -SWIGLU_LIMIT, SWIGLU_LIMIT)
        act_bf[:, g * LANES:(g + 1) * LANES] = (
            gate * jax.nn.sigmoid(SWIGLU_ALPHA * gate) * (lin + 1.0)).astype(BF16)

    @pl.when(active)
    def _():
        k = d // (2 * LANES)
        for j in range(k):
            lo, hi = _load_packed_tile(x_ref, (), bm, k, j)
            x_bf[:, j * LANES:(j + 1) * LANES] = lo.astype(BF16)
            x_bf[:, (k + j) * LANES:(k + j + 1) * LANES] = hi.astype(BF16)
        for g in range(n_groups):
            activate(g, _dot(x_bf[...], wgu_bf[:, g * MXU_DIM:(g + 1) * MXU_DIM]))
        _store_packed_rows(y_ref, (), _dot(act_bf[...], wd_bf[...]) + bd_ref[0])

    @pl.when(i >= n_used)
    def _():
        y_ref[...] = jnp.zeros_like(y_ref)


def _deinterleave_perm():
    src = np.arange(MXU_DIM)
    dst = np.where(src % 2 == 0, src // 2, LANES + src // 2)
    p = np.zeros((MXU_DIM, MXU_DIM), np.float32)
    p[src, dst] = 1.0
    return jnp.asarray(p, dtype=BF16)


def _moe(plan, xs, w_gate_up, bgu_p, w_down, b_down):
    n_exp, d, d_gu = w_gate_up.shape
    d_ff = w_down.shape[1]
    bm = MOE_BLOCK
    nblk = plan[0].shape[0]
    slab = bm * d // (2 * LANES)
    by_expert = lambda i, be, *_: (be[i], 0, 0)
    grid_spec = pltpu.PrefetchScalarGridSpec(
        num_scalar_prefetch=5,
        grid=(nblk,),
        in_specs=[
            pl.BlockSpec((slab, LANES), lambda i, *_: (i, 0)),
            pl.BlockSpec(memory_space=pl.ANY),
            pl.BlockSpec(memory_space=pl.ANY),
            pl.BlockSpec((1, 1, d_gu), by_expert),
            pl.BlockSpec((1, 1, d), by_expert),
            pl.BlockSpec((MXU_DIM, MXU_DIM), lambda i, *_: (0, 0)),
        ],
        out_specs=pl.BlockSpec((slab, LANES), lambda i, *_: (i, 0)),
        scratch_shapes=[pltpu.VMEM((2, d, d_gu), F32),
                        pltpu.VMEM((2, d_ff, d), F32),
                        pltpu.SemaphoreType.DMA((2, 2)),
                        pltpu.VMEM((d, d_gu), BF16),
                        pltpu.VMEM((d_ff, d), BF16),
                        pltpu.VMEM((bm, d), BF16),
                        pltpu.VMEM((bm, d_ff), BF16)],
    )
    return pl.pallas_call(
        _moe_kernel,
        grid_spec=grid_spec,
        out_shape=jax.ShapeDtypeStruct((nblk * slab, LANES), U32),
        compiler_params=_cparams(("arbitrary",)),
        name="moe_experts",
    )(*plan, xs, w_gate_up, w_down, bgu_p, b_down, _deinterleave_perm())


SC_CORES = 2
SC_SUBCORES = 16
SC_CHUNK = 64


def _sc_mesh():
    return plsc.VectorSubcoreMesh(core_axis_name="c", subcore_axis_name="s")


def _sc_dispatch(rows3, pos_chunks, n_slots):
    n, r, _ = rows3.shape
    workers = SC_CORES * SC_SUBCORES
    per_worker = n // SC_CHUNK // workers

    def body(rows_hbm, pos_hbm, out_hbm, idx_v, rows_v, sem):
        wid = lax.axis_index("s") * SC_CORES + lax.axis_index("c")

        @pl.loop(0, per_worker)
        def _(c):
            chunk = wid * per_worker + c
            pltpu.sync_copy(pos_hbm.at[chunk], idx_v)
            pltpu.sync_copy(rows_hbm.at[pl.ds(chunk * SC_CHUNK, SC_CHUNK)], rows_v)
            copies = [pltpu.async_copy(rows_v, out_hbm.at[idx_v.at[k]], sem) for k in range(TOP_K)]
            for cp in copies:
                cp.wait()

    return pl.kernel(
        body,
        out_type=jax.ShapeDtypeStruct((n_slots, r, LANES), rows3.dtype),
        mesh=_sc_mesh(),
        scratch_types=[pltpu.VMEM((TOP_K, SC_CHUNK), I32),
                       pltpu.VMEM((SC_CHUNK, r, LANES), rows3.dtype),
                       pltpu.SemaphoreType.DMA],
        name="sc_dispatch",
    )(rows3, pos_chunks)


def _sc_gather(rows3, pos_chunks, n):
    _, r, _ = rows3.shape
    workers = SC_CORES * SC_SUBCORES
    per_worker = n // SC_CHUNK // workers

    def body(rows_hbm, pos_hbm, out_hbm, idx_v, buf_a, buf_b, sem_a, sem_b):
        wid = lax.axis_index("s") * SC_CORES + lax.axis_index("c")
        bufs, sems = (buf_a, buf_b), (sem_a, sem_b)

        @pl.loop(0, per_worker)
        def _(c):
            chunk = wid * per_worker + c
            pltpu.sync_copy(pos_hbm.at[chunk], idx_v)
            gathers = [None] * TOP_K
            gathers[0] = pltpu.async_copy(rows_hbm.at[idx_v.at[0]], bufs[0], sems[0])
            for k in range(TOP_K):
                if k + 1 < TOP_K:
                    nxt = (k + 1) % 2
                    gathers[k + 1] = pltpu.async_copy(rows_hbm.at[idx_v.at[k + 1]], bufs[nxt], sems[nxt])
                gathers[k].wait()
                pltpu.sync_copy(bufs[k % 2], out_hbm.at[k, pl.ds(chunk * SC_CHUNK, SC_CHUNK)])

    return pl.kernel(
        body,
        out_type=jax.ShapeDtypeStruct((TOP_K, n, r, LANES), rows3.dtype),
        mesh=_sc_mesh(),
        scratch_types=[pltpu.VMEM((TOP_K, SC_CHUNK), I32),
                       pltpu.VMEM((SC_CHUNK, r, LANES), rows3.dtype),
                       pltpu.VMEM((SC_CHUNK, r, LANES), rows3.dtype),
                       pltpu.SemaphoreType.DMA,
                       pltpu.SemaphoreType.DMA],
        name="sc_gather",
    )(rows3, pos_chunks)


COMBINE_TILE = 512


def _combine_kernel(h_ref, gate_ref, y_ref, o_ref):
    tc, d = h_ref.shape
    kt = d // (2 * LANES)
    gates = gate_ref[...]
    for j in range(kt):
        lo_acc = h_ref[:, j * LANES:(j + 1) * LANES]
        hi_acc = h_ref[:, (kt + j) * LANES:(kt + j + 1) * LANES]
        for k in range(TOP_K):
            lo, hi = _load_packed_tile(y_ref, (k,), tc, kt, j)
            g = gates[:, k:k + 1]
            lo_acc = lo_acc + g * lo
            hi_acc = hi_acc + g * hi
        o_ref[:, j * LANES:(j + 1) * LANES] = lo_acc
        o_ref[:, (kt + j) * LANES:(kt + j + 1) * LANES] = hi_acc


def _combine(h, gates, y4):
    n, d = h.shape
    tc = COMBINE_TILE
    row = lambda i: (i, 0)
    return pl.pallas_call(
        _combine_kernel,
        grid=(n // tc,),
        in_specs=[pl.BlockSpec((tc, d), row),
                  pl.BlockSpec((tc, LANES), row),
                  pl.BlockSpec((TOP_K, tc * d // (2 * LANES), LANES), lambda i: (0, i, 0))],
        out_specs=pl.BlockSpec((tc, d), row),
        out_shape=jax.ShapeDtypeStruct((n, d), F32),
        compiler_params=_cparams(("parallel",)),
        name="moe_combine",
    )(h, gates, y4)


def _layer(h3, layer, norm_mix_g, w_in, conv_w, conv_b, conv_ln_g, conv_ln_b, q_norm_g, k_norm_g,
           lambda_qk, subln_g, rel_bias, w_out, norm_ffn_g, router_w, router_b,
           w_gate_up, b_gate_up, w_down, b_down):
    b, s, d = h3.shape
    n = b * s
    d_conv = conv_w.shape[-1]
    d_attn = N_HEADS * VALUE_DIM
    lambda_init = 0.8 - 0.6 * math.exp(-0.3 * layer)
    x2 = h3.reshape(n, d)

    u, q, k, v = _in_proj(x2, norm_mix_g.reshape(1, d), w_in.astype(BF16), d_conv, d_attn, tm=512)
    conv_o = _conv(u.reshape(b, s, d_conv), conv_w, conv_b.reshape(1, d_conv),
                   conv_ln_g.reshape(1, d_conv), conv_ln_b.reshape(1, d_conv))
    attn_o = _attention(q.reshape(b, s, d_attn), k.reshape(b, s, d_attn), v.reshape(b, s, d_attn),
                        rel_bias, q_norm_g.reshape(1, VALUE_DIM), k_norm_g.reshape(1, VALUE_DIM),
                        lambda_qk, subln_g.reshape(1, VALUE_DIM), lambda_init)

    n_exp = router_w.shape[1]
    rw = jnp.zeros((d, LANES), F32).at[:, :n_exp].set(router_w)
    rwh = rw.astype(BF16)
    rwl = (rw - rwh.astype(F32)).astype(BF16)
    rb = jnp.full((1, LANES), NEG_INF, F32).at[0, :n_exp].set(router_b)
    w_out_bf = w_out.astype(BF16)
    hres, hn, idx, gates, rank, cnt = _out_proj(
        x2, conv_o.reshape(n, d_conv), attn_o.reshape(n, d_attn),
        w_out_bf[:d_conv], w_out_bf[d_conv:], norm_ffn_g.reshape(1, d), rwh, rwl, rb, tm=512)

    bm = MOE_BLOCK
    nblk = n * TOP_K // bm + n_exp
    counts = cnt[0, :n_exp].astype(I32)
    padded = (counts + bm - 1) // bm * bm
    pad_end = jnp.cumsum(padded)
    pad_start = pad_end - padded
    pos = pad_start[idx[:TOP_K]] + rank[:TOP_K]
    pos_chunks = pos.reshape(TOP_K, n // SC_CHUNK, SC_CHUNK).transpose(1, 0, 2)
    block_start = jnp.arange(nblk, dtype=I32) * bm
    block_expert = jnp.minimum(jnp.sum(block_start[:, None] >= pad_end[None, :], axis=1),
                               n_exp - 1).astype(I32)
    n_used = (pad_end[-1:] // bm).astype(I32)
    blk = jnp.arange(nblk, dtype=I32)
    first = jnp.logical_and(
        jnp.concatenate([jnp.ones((1,), bool), block_expert[1:] != block_expert[:-1]]),
        blk < n_used[0]).astype(I32)
    wslot = ((jnp.cumsum(first) - 1) % 2).astype(I32)
    eid = jnp.arange(n_exp, dtype=I32)
    later_used = jnp.logical_and(eid[None, :] > eid[:, None], counts[None, :] > 0)
    next_used = jnp.min(jnp.where(later_used, eid[None, :], n_exp), axis=1)
    next_used = jnp.where(next_used == n_exp, -1, next_used).astype(I32)
    plan = (block_expert, first, wslot, next_used[block_expert], n_used)

    d_ff = w_down.shape[1]
    bgu_p = b_gate_up.reshape(n_exp, d_ff // LANES, LANES, 2).transpose(0, 1, 3, 2).reshape(n_exp, 1, 2 * d_ff)
    r = d // (2 * LANES)
    xs = _sc_dispatch(hn.reshape(n, r, LANES), pos_chunks, nblk * bm)
    yb = _moe(plan, xs.reshape(nblk * bm * r, LANES), w_gate_up, bgu_p, w_down,
              b_down.reshape(n_exp, 1, d))
    y4 = _sc_gather(yb.reshape(nblk * bm, r, LANES), pos_chunks, n)
    out = _combine(hres, gates, y4.reshape(TOP_K, n * r, LANES))
    return out.reshape(b, s, d)


def kernel(x, norm_mix_g, w_in, conv_w, conv_b, conv_ln_g, conv_ln_b, q_norm_g, k_norm_g, lambda_qk,
           subln_g, rel_bias, w_out, norm_ffn_g, router_w, router_b, w_gate_up, b_gate_up, w_down,
           b_down):
    h = x
    for layer in range(norm_mix_g.shape[0]):
        h = _layer(h, layer, norm_mix_g[layer], w_in[layer], conv_w[layer], conv_b[layer],
                   conv_ln_g[layer], conv_ln_b[layer], q_norm_g[layer], k_norm_g[layer],
                   lambda_qk[layer], subln_g[layer], rel_bias, w_out[layer], norm_ffn_g[layer],
                   router_w[layer], router_b[layer], w_gate_up[layer], b_gate_up[layer],
                   w_down[layer], b_down[layer])
    return h
```

```python
import functools
import math

import jax
import jax.numpy as jnp
import numpy as np
from jax import lax
from jax.experimental import pallas as pl
from jax.experimental.pallas import tpu as pltpu
from jax.experimental.pallas import tpu_sc as plsc

F32 = jnp.float32
BF16 = jnp.bfloat16
I32 = jnp.int32
U32 = jnp.uint32

CHUNK = 64
CONV_WIDTH = 31
N_HEADS = 4
HEAD_DIM = 64
VALUE_DIM = 2 * HEAD_DIM
REL_BUCKETS = 32
REL_MAX_DIST = 128
N_EXPERTS = 32
TOP_K = 4
SWIGLU_LIMIT = 7.0
SWIGLU_ALPHA = 1.702
EPS = 1e-5
LOG2E = 1.4426950408889634

LANES = 128
SUBLANES = 8
MXU_DIM = 256
VMEM_LIMIT = 56 * 1024 * 1024

NEG_INF = float("-inf")


def _cparams(sem, vmem=VMEM_LIMIT, flags=None):
    return pltpu.CompilerParams(dimension_semantics=sem, vmem_limit_bytes=vmem, flags=flags)


def _dot(a, b):
    return jnp.dot(a, b, preferred_element_type=F32)


def _dot_nt(a, b):
    return lax.dot_general(a, b, (((1,), (1,)), ((), ())), preferred_element_type=F32)


def _pack_pair(lo, hi):
    lo_bits = lax.bitcast_convert_type(lo.astype(BF16).astype(F32), U32)
    hi_bits = lax.bitcast_convert_type(hi.astype(BF16).astype(F32), U32)
    return (lo_bits >> 16) | hi_bits


def _unpack_pair(w):
    return (lax.bitcast_convert_type(w << 16, F32),
            lax.bitcast_convert_type(w & jnp.uint32(0xFFFF0000), F32))


def _store_packed_rows(ref, lead, x):
    rows, k = x.shape[0], x.shape[1] // (2 * LANES)
    for j in range(k):
        w = _pack_pair(x[:, j * LANES:(j + 1) * LANES], x[:, (k + j) * LANES:(k + j + 1) * LANES])
        ref[(*lead, pl.ds(j, rows, stride=k), slice(None))] = w


def _load_packed_tile(ref, lead, rows, k, j):
    return _unpack_pair(ref[(*lead, pl.ds(j, rows, stride=k), slice(None))])


def _in_proj_kernel(x_ref, g_ref, w_ref, u_ref, q_ref, k_ref, v_ref, *, d_conv, d_attn):
    x = x_ref[...]
    ms = jnp.mean(x * x, axis=-1, keepdims=True)
    y = (x * lax.rsqrt(ms + EPS) * g_ref[...]).astype(BF16)
    proj = _dot(y, w_ref[...])
    a = proj[:, :d_conv]
    g = proj[:, d_conv:2 * d_conv]
    u_ref[...] = a * jax.nn.sigmoid(g)
    o = 2 * d_conv
    q_ref[...] = proj[:, o:o + d_attn]
    k_ref[...] = proj[:, o + d_attn:o + 2 * d_attn]
    v_ref[...] = proj[:, o + 2 * d_attn:o + 3 * d_attn].astype(BF16)


def _in_proj(x2, g, w_bf, d_conv, d_attn, tm):
    n, d = x2.shape
    d_in = w_bf.shape[1]
    row = lambda i: (i, 0)
    fixed = lambda i: (0, 0)
    return pl.pallas_call(
        functools.partial(_in_proj_kernel, d_conv=d_conv, d_attn=d_attn),
        grid=(n // tm,),
        in_specs=[pl.BlockSpec((tm, d), row),
                  pl.BlockSpec((1, d), fixed),
                  pl.BlockSpec((d, d_in), fixed)],
        out_specs=[pl.BlockSpec((tm, d_conv), row),
                   pl.BlockSpec((tm, d_attn), row),
                   pl.BlockSpec((tm, d_attn), row),
                   pl.BlockSpec((tm, d_attn), row)],
        out_shape=[jax.ShapeDtypeStruct((n, d_conv), F32),
                   jax.ShapeDtypeStruct((n, d_attn), F32),
                   jax.ShapeDtypeStruct((n, d_attn), F32),
                   jax.ShapeDtypeStruct((n, d_attn), BF16)],
        compiler_params=_cparams(("parallel",)),
        name="in_proj",
    )(x2, g, w_bf)


CONV_PAD = 32
CONV_SEQ_TILE = 512
CONV_TILE = 128
CONV_NORM_TILE = 256


def _conv_kernel(u_ref, prev_ref, w_ref, cb_ref, lg_ref, lb_ref, o_ref, sh_ref, y_ref):
    ts, c = u_ref.shape[1], u_ref.shape[2]
    plen = ts + CONV_PAD
    hist = prev_ref[0]
    hist = jnp.where(pl.program_id(1) > 0, hist, jnp.zeros_like(hist))
    sh_ref[0, pl.ds(0, CONV_PAD), :] = hist
    sh_ref[0, pl.ds(CONV_PAD, ts), :] = u_ref[0]
    sh_ref[0, pl.ds(plen, SUBLANES), :] = jnp.zeros((SUBLANES, c), F32)

    bt = CONV_PAD

    def shift(i, _):
        p0 = pl.multiple_of(i * bt, bt)
        win = sh_ref[0, pl.ds(p0, bt + SUBLANES), :]
        for r in range(1, SUBLANES):
            sh_ref[r, pl.ds(p0, bt), :] = win[r:r + bt]
        return 0

    lax.fori_loop(0, plen // bt, shift, 0)

    off0 = CONV_PAD - (CONV_WIDTH - 1)

    by_shift = {}
    for j in range(CONV_WIDTH):
        a, r = divmod(off0 + j, SUBLANES)
        by_shift.setdefault(r, []).append((a, j))

    for lt in range(c // LANES):
        lanes = slice(lt * LANES, (lt + 1) * LANES)
        taps = [w_ref[pl.ds(j, 1), lanes] for j in range(CONV_WIDTH)]

        def tap_body(i, _, lanes=lanes, taps=taps):
            t0 = pl.multiple_of(i * CONV_TILE, CONV_TILE)
            acc = jnp.zeros((CONV_TILE, LANES), F32)
            for r, group in by_shift.items():
                a_lo = min(a for a, _ in group)
                a_hi = max(a for a, _ in group)
                rows = CONV_TILE + (a_hi - a_lo) * SUBLANES
                start = pl.multiple_of(t0 + a_lo * SUBLANES, SUBLANES)
                win = sh_ref[r, pl.ds(start, rows), lanes]
                for a, j in group:
                    lo = (a - a_lo) * SUBLANES
                    acc = acc + win[lo:lo + CONV_TILE] * taps[j]
            y_ref[pl.ds(t0, CONV_TILE), lanes] = acc
            return 0

        lax.fori_loop(0, ts // CONV_TILE, tap_body, 0)

    def norm_body(i, _):
        t0 = pl.multiple_of(i * CONV_NORM_TILE, CONV_NORM_TILE)
        y = y_ref[pl.ds(t0, CONV_NORM_TILE), :] + cb_ref[...]
        mu = jnp.mean(y, axis=-1, keepdims=True)
        yc = y - mu
        var = jnp.mean(yc * yc, axis=-1, keepdims=True)
        z = yc * lax.rsqrt(var + EPS) * lg_ref[...] + lb_ref[...]
        o_ref[0, pl.ds(t0, CONV_NORM_TILE), :] = (z * jax.nn.sigmoid(z)).astype(o_ref.dtype)
        return 0

    lax.fori_loop(0, ts // CONV_NORM_TILE, norm_body, 0)


def _conv(u3, conv_w, conv_b, ln_g, ln_b):
    b, s, c = u3.shape
    ts = min(CONV_SEQ_TILE, s)
    hist_per_tile = ts // CONV_PAD
    fixed = lambda i, j: (0, 0)
    return pl.pallas_call(
        _conv_kernel,
        grid=(b, s // ts),
        in_specs=[pl.BlockSpec((1, ts, c), lambda i, j: (i, j, 0)),
                  pl.BlockSpec((1, CONV_PAD, c),
                               lambda i, j: (i, jnp.maximum(j * hist_per_tile - 1, 0), 0)),
                  pl.BlockSpec((CONV_WIDTH, c), fixed),
                  pl.BlockSpec((1, c), fixed),
                  pl.BlockSpec((1, c), fixed),
                  pl.BlockSpec((1, c), fixed)],
        out_specs=pl.BlockSpec((1, ts, c), lambda i, j: (i, j, 0)),
        out_shape=jax.ShapeDtypeStruct((b, s, c), BF16),
        scratch_shapes=[pltpu.VMEM((SUBLANES, ts + CONV_PAD + SUBLANES, c), F32),
                        pltpu.VMEM((ts, c), F32)],
        compiler_params=_cparams(("parallel", "parallel")),
        name="conv_mixer",
    )(u3, u3, conv_w, conv_b, ln_g, ln_b)


ATT_TILE = 256
FAR_BUCKET = REL_BUCKETS // 2 - 1


def _t5_bucket(rel):
    nb = REL_BUCKETS // 2
    max_exact = nb // 2
    ret = jnp.where(rel > 0, nb, 0)
    n = jnp.abs(rel)
    nf = jnp.maximum(n, 1).astype(jnp.float32)
    large = max_exact + (jnp.log(nf / max_exact) / math.log(REL_MAX_DIST / max_exact)
                         * (nb - max_exact)).astype(jnp.int32)
    large = jnp.minimum(large, nb - 1)
    return ret + jnp.where(n < max_exact, n, large)


def _near_buckets():
    assert ATT_TILE >= REL_MAX_DIST and ATT_TILE % CHUNK == 0
    qpos = jnp.arange(ATT_TILE, dtype=I32)[:, None]
    kpos = jnp.arange(ATT_TILE, dtype=I32)[None, :]
    prev = _t5_bucket(kpos - ATT_TILE - qpos)
    diag = _t5_bucket(kpos - qpos)
    diag = jnp.where(kpos // CHUNK <= qpos // CHUNK, diag, -1)
    return jnp.stack([prev, diag]).astype(I32)


def _attn_kernel(tab_ref, q_ref, k_ref, v_ref, bkt_ref, bd_ref, qg_ref, kg_ref, lqk_ref, sg_ref, o_ref,
                 qz_ref, kn_ref, v1_ref, bias_ref, *, seq, lambda_init):
    h = pl.program_id(0)
    t = ATT_TILE
    n_tiles = seq // t
    lane = lax.broadcasted_iota(I32, (1, VALUE_DIM), 1)
    first = lane < HEAD_DIM

    @pl.when(pl.program_id(1) == 0)
    def _():
        far = tab_ref[FAR_BUCKET, h]
        for d in range(2):
            bkt = bkt_ref[d]
            tile = jnp.full((t, t), NEG_INF, F32)
            for b in range(REL_BUCKETS):
                tile = jnp.where(bkt == b, (tab_ref[b, h] - far) * LOG2E, tile)
            bias_ref[d] = tile

    def half_norm(x, g):
        x2 = x * x
        hi = x2.astype(BF16)
        lo = (x2 - hi.astype(F32)).astype(BF16)
        ms = (_dot(hi, bd_ref[...]) + _dot(lo, bd_ref[...])) * (1.0 / HEAD_DIM)
        return x * lax.rsqrt(ms + EPS) * g

    q_scale = HEAD_DIM ** -0.5 * LOG2E

    def prep(i, _):
        r0 = pl.multiple_of(i * t, t)
        qn = half_norm(q_ref[0, pl.ds(r0, t), :], qg_ref[...]) * q_scale
        qz_ref[0, pl.ds(r0, t), :] = jnp.where(first, qn, 0.0).astype(BF16)
        qz_ref[1, pl.ds(r0, t), :] = jnp.where(first, 0.0, qn).astype(BF16)
        kn_ref[pl.ds(r0, t), :] = half_norm(k_ref[0, pl.ds(r0, t), :], kg_ref[...]).astype(BF16)
        ones_col = jnp.broadcast_to(jnp.where(lane == 0, 1.0, 0.0).astype(BF16), (t, VALUE_DIM))
        v1_ref[pl.ds(r0, t), :] = jnp.concatenate([v_ref[0, pl.ds(r0, t), :], ones_col], axis=1)
        return 0

    lax.fori_loop(0, n_tiles, prep, 0)

    lqk = lqk_ref[...]
    lam = (jnp.exp(jnp.sum(lqk[0:1] * lqk[1:2], axis=-1, keepdims=True))
           - jnp.exp(jnp.sum(lqk[2:3] * lqk[3:4], axis=-1, keepdims=True)) + lambda_init)

    for i in range(n_tiles):
        q0, kend = i * t, (i + 1) * t
        keys = kn_ref[0:kend, :]
        vals = v1_ref[0:kend, :]
        maps = []
        for m in range(2):
            s = _dot_nt(qz_ref[m, q0:q0 + t, :], keys)
            parts = [s[:, kend - t:] + bias_ref[1]]
            if i >= 1:
                parts.insert(0, s[:, kend - 2 * t:kend - t] + bias_ref[0])
            if i >= 2:
                parts.insert(0, s[:, :kend - 2 * t])
            s = jnp.concatenate(parts, axis=1) if len(parts) > 1 else parts[0]
            p = jnp.exp2(s - jnp.max(s, axis=-1, keepdims=True))
            pv = _dot(p.astype(BF16), vals)
            maps.append(pv[:, :VALUE_DIM] / pv[:, VALUE_DIM:VALUE_DIM + 1])
        o = maps[0] - lam * maps[1]
        ms = jnp.mean(o * o, axis=-1, keepdims=True)
        o = o * lax.rsqrt(ms + EPS) * sg_ref[...] * (1.0 - lambda_init)
        o_ref[0, q0:q0 + t, :] = o.astype(o_ref.dtype)


def _attention(q3, k3, v3, rel_bias, q_g, k_g, lam_qk, subln_g, lambda_init):
    b, s, _ = q3.shape
    t = ATT_TILE
    head = lambda j, i: (i, 0, j)
    fixed2 = lambda j, i: (0, 0)
    fixed3 = lambda j, i: (0, 0, 0)
    half = np.arange(VALUE_DIM) // HEAD_DIM
    blockdiag = jnp.asarray(half[:, None] == half[None, :], dtype=BF16)
    return pl.pallas_call(
        functools.partial(_attn_kernel, seq=s, lambda_init=lambda_init),
        grid=(N_HEADS, b),
        in_specs=[pl.BlockSpec(memory_space=pltpu.SMEM),
                  pl.BlockSpec((1, s, VALUE_DIM), head),
                  pl.BlockSpec((1, s, VALUE_DIM), head),
                  pl.BlockSpec((1, s, VALUE_DIM), head),
                  pl.BlockSpec((2, t, t), fixed3),
                  pl.BlockSpec((VALUE_DIM, VALUE_DIM), fixed2),
                  pl.BlockSpec((1, VALUE_DIM), fixed2),
                  pl.BlockSpec((1, VALUE_DIM), fixed2),
                  pl.BlockSpec((4, HEAD_DIM), fixed2),
                  pl.BlockSpec((1, VALUE_DIM), fixed2)],
        out_specs=pl.BlockSpec((1, s, VALUE_DIM), head),
        out_shape=jax.ShapeDtypeStruct((b, s, N_HEADS * VALUE_DIM), BF16),
        scratch_shapes=[pltpu.VMEM((2, s, VALUE_DIM), BF16),
                        pltpu.VMEM((s, VALUE_DIM), BF16),
                        pltpu.VMEM((s, 2 * VALUE_DIM), BF16),
                        pltpu.VMEM((2, t, t), F32)],
        compiler_params=_cparams(("arbitrary", "arbitrary")),
        name="diff_attn",
    )(rel_bias, q3, k3, v3, _near_buckets(), blockdiag, q_g, k_g, lam_qk, subln_g)


def _out_proj_kernel(x_ref, c_ref, a_ref, wc_ref, wa_ref, g_ref, rwh_ref, rwl_ref, rb_ref, tri_ref,
                     h_ref, hn_ref, idx_ref, gate_ref, rank_ref, cnt_ref, carry_ref):
    @pl.when(pl.program_id(0) == 0)
    def _():
        carry_ref[...] = jnp.zeros_like(carry_ref)

    h = x_ref[...] + _dot(c_ref[...], wc_ref[...]) + _dot(a_ref[...], wa_ref[...])
    h_ref[...] = h
    ms = jnp.mean(h * h, axis=-1, keepdims=True)
    hn = h * lax.rsqrt(ms + EPS) * g_ref[...]
    _store_packed_rows(hn_ref, (), hn)

    hi = hn.astype(BF16)
    lo = (hn - hi.astype(F32)).astype(BF16)
    logits = (_dot(hi, rwh_ref[...]) + (_dot(hi, rwl_ref[...]) + _dot(lo, rwh_ref[...]))) + rb_ref[...]

    tm = logits.shape[0]
    lane = lax.broadcasted_iota(I32, (tm, LANES), 1).astype(F32)
    work = logits
    vals, idxs = [], []
    for _ in range(TOP_K):
        mx = jnp.max(work, axis=-1, keepdims=True)
        ix = jnp.min(jnp.where(work == mx, lane, float(LANES)), axis=-1, keepdims=True)
        vals.append(mx)
        idxs.append(ix)
        work = jnp.where(lane == ix, NEG_INF, work)
    exps = [jnp.exp(v - vals[0]) for v in vals]
    denom = exps[0]
    for e in exps[1:]:
        denom = denom + e

    sel = jnp.zeros((tm, LANES), F32)
    for ix in idxs:
        sel = sel + jnp.where(lane == ix, 1.0, 0.0)
    rank = _dot(tri_ref[...], sel.astype(BF16)) + carry_ref[...]
    carry_ref[...] = carry_ref[...] + jnp.sum(sel, axis=0, keepdims=True)
    cnt_ref[...] = carry_ref[...]

    idx_out = jnp.zeros((tm, LANES), F32)
    gate_out = jnp.zeros((tm, LANES), F32)
    rank_out = jnp.zeros((tm, LANES), F32)
    for k in range(TOP_K):
        rk = jnp.sum(jnp.where(lane == idxs[k], rank, 0.0), axis=-1, keepdims=True)
        idx_out = jnp.where(lane == k, idxs[k], idx_out)
        gate_out = jnp.where(lane == k, exps[k] / denom, gate_out)
        rank_out = jnp.where(lane == k, rk, rank_out)
    idx_ref[...] = jnp.transpose(idx_out)[:SUBLANES].astype(I32)
    rank_ref[...] = jnp.transpose(rank_out)[:SUBLANES].astype(I32)
    gate_ref[...] = gate_out


def _out_proj(x2, conv_o, attn_o, wc, wa, g, rwh, rwl, rb, tm):
    n, d = x2.shape
    dc, da = conv_o.shape[1], attn_o.shape[1]
    row = lambda i: (i, 0)
    fixed = lambda i: (0, 0)
    tri = jnp.tril(jnp.ones((tm, tm), F32), -1).astype(BF16)
    return pl.pallas_call(
        _out_proj_kernel,
        grid=(n // tm,),
        in_specs=[pl.BlockSpec((tm, d), row),
                  pl.BlockSpec((tm, dc), row),
                  pl.BlockSpec((tm, da), row),
                  pl.BlockSpec((dc, d), fixed),
                  pl.BlockSpec((da, d), fixed),
                  pl.BlockSpec((1, d), fixed),
                  pl.BlockSpec((d, LANES), fixed),
                  pl.BlockSpec((d, LANES), fixed),
                  pl.BlockSpec((1, LANES), fixed),
                  pl.BlockSpec((tm, tm), fixed)],
        out_specs=[pl.BlockSpec((tm, d), row),
                   pl.BlockSpec((tm * d // (2 * LANES), LANES), row),
                   pl.BlockSpec((SUBLANES, tm), lambda i: (0, i)),
                   pl.BlockSpec((tm, LANES), row),
                   pl.BlockSpec((SUBLANES, tm), lambda i: (0, i)),
                   pl.BlockSpec((1, LANES), fixed)],
        out_shape=[jax.ShapeDtypeStruct((n, d), F32),
                   jax.ShapeDtypeStruct((n * d // (2 * LANES), LANES), U32),
                   jax.ShapeDtypeStruct((SUBLANES, n), I32),
                   jax.ShapeDtypeStruct((n, LANES), F32),
                   jax.ShapeDtypeStruct((SUBLANES, n), I32),
                   jax.ShapeDtypeStruct((1, LANES), F32)],
        scratch_shapes=[pltpu.VMEM((1, LANES), F32)],
        compiler_params=_cparams(("arbitrary",)),
        name="out_proj_router",
    )(x2, conv_o, attn_o, wc, wa, g, rwh, rwl, rb, tri)


MOE_BLOCK = 512


def _moe_kernel(be_ref, first_ref, wslot_ref, nxt_ref, nused_ref,
                x_ref, wgu_hbm, wd_hbm, bgu_ref, bd_ref, perm_ref,
                y_ref, wgu_f32, wd_f32, wsem, wgu_bf, wd_bf, x_bf, act_bf):
    i = pl.program_id(0)
    bm = MOE_BLOCK
    n_used = nused_ref[0]
    d = wd_bf.shape[1]
    d_ff = wd_bf.shape[0]
    n_groups = d_ff // LANES

    def weight_copies(e, slot):
        return (pltpu.make_async_copy(wgu_hbm.at[e], wgu_f32.at[slot], wsem.at[0, slot]),
                pltpu.make_async_copy(wd_hbm.at[e], wd_f32.at[slot], wsem.at[1, slot]))

    @pl.when(i == 0)
    def _():
        for c in weight_copies(be_ref[0], 0):
            c.start()

    active = i < n_used

    @pl.when(jnp.logical_and(active, first_ref[i] == 1))
    def _():
        ws = wslot_ref[i]
        for c in weight_copies(be_ref[i], ws):
            c.wait()
        for g in range(n_groups):
            cols = pl.ds(g * MXU_DIM, MXU_DIM)
            wgu_bf[:, cols] = _dot(wgu_f32[ws, :, cols].astype(BF16), perm_ref[...]).astype(BF16)
        wd_bf[...] = wd_f32[ws].astype(BF16)

        @pl.when(nxt_ref[i] >= 0)
        def _():
            for c in weight_copies(nxt_ref[i], 1 - ws):
                c.start()

    def activate(g, gu):
        gu = gu + bgu_ref[0, :, g * MXU_DIM:(g + 1) * MXU_DIM]
        gate = jnp.minimum(gu[:, :LANES], SWIGLU_LIMIT)
        lin = jnp.clip(gu[:, LANES:], -SWIGLU_LIMIT, SWIGLU_LIMIT)
        act_bf[:, g * LANES:(g + 1) * LANES] = (
            gate * jax.nn.sigmoid(SWIGLU_ALPHA * gate) * (lin + 1.0)).astype(BF16)

    @pl.when(active)
    def _():
        k = d // (2 * LANES)
        for j in range(k):
            lo, hi = _load_packed_tile(x_ref, (), bm, k, j)
            x_bf[:, j * LANES:(j + 1) * LANES] = lo.astype(BF16)
            x_bf[:, (k + j) * LANES:(k + j + 1) * LANES] = hi.astype(BF16)
        for g in range(n_groups):
            activate(g, _dot(x_bf[...], wgu_bf[:, g * MXU_DIM:(g + 1) * MXU_DIM]))
        _store_packed_rows(y_ref, (), _dot(act_bf[...], wd_bf[...]) + bd_ref[0])

    @pl.when(i >= n_used)
    def _():
        y_ref[...] = jnp.zeros_like(y_ref)


def _deinterleave_perm():
    src = np.arange(MXU_DIM)
    dst = np.where(src % 2 == 0, src // 2, LANES + src // 2)
    p = np.zeros((MXU_DIM, MXU_DIM), np.float32)
    p[src, dst] = 1.0
    return jnp.asarray(p, dtype=BF16)


def _moe(plan, xs, w_gate_up, bgu_p, w_down, b_down):
    n_exp, d, d_gu = w_gate_up.shape
    d_ff = w_down.shape[1]
    bm = MOE_BLOCK
    nblk = plan[0].shape[0]
    slab = bm * d // (2 * LANES)
    by_expert = lambda i, be, *_: (be[i], 0, 0)
    grid_spec = pltpu.PrefetchScalarGridSpec(
        num_scalar_prefetch=5,
        grid=(nblk,),
        in_specs=[
            pl.BlockSpec((slab, LANES), lambda i, *_: (i, 0)),
            pl.BlockSpec(memory_space=pl.ANY),
            pl.BlockSpec(memory_space=pl.ANY),
            pl.BlockSpec((1, 1, d_gu), by_expert),
            pl.BlockSpec((1, 1, d), by_expert),
            pl.BlockSpec((MXU_DIM, MXU_DIM), lambda i, *_: (0, 0)),
        ],
        out_specs=pl.BlockSpec((slab, LANES), lambda i, *_: (i, 0)),
        scratch_shapes=[pltpu.VMEM((2, d, d_gu), F32),
                        pltpu.VMEM((2, d_ff, d), F32),
                        pltpu.SemaphoreType.DMA((2, 2)),
                        pltpu.VMEM((d, d_gu), BF16),
                        pltpu.VMEM((d_ff, d), BF16),
                        pltpu.VMEM((bm, d), BF16),
                        pltpu.VMEM((bm, d_ff), BF16)],
    )
    return pl.pallas_call(
        _moe_kernel,
        grid_spec=grid_spec,
        out_shape=jax.ShapeDtypeStruct((nblk * slab, LANES), U32),
        compiler_params=_cparams(("arbitrary",)),
        name="moe_experts",
    )(*plan, xs, w_gate_up, w_down, bgu_p, b_down, _deinterleave_perm())


SC_CORES = 2
SC_SUBCORES = 16
SC_CHUNK = 64


def _sc_mesh():
    return plsc.VectorSubcoreMesh(core_axis_name="c", subcore_axis_name="s")


def _sc_dispatch(rows3, pos_chunks, n_slots):
    n, r, _ = rows3.shape
    workers = SC_CORES * SC_SUBCORES
    per_worker = n // SC_CHUNK // workers

    def body(rows_hbm, pos_hbm, out_hbm, idx_v, rows_v, sem):
        wid = lax.axis_index("s") * SC_CORES + lax.axis_index("c")

        @pl.loop(0, per_worker)
        def _(c):
            chunk = wid * per_worker + c
            pltpu.sync_copy(pos_hbm.at[chunk], idx_v)
            pltpu.sync_copy(rows_hbm.at[pl.ds(chunk * SC_CHUNK, SC_CHUNK)], rows_v)
            copies = [pltpu.async_copy(rows_v, out_hbm.at[idx_v.at[k]], sem) for k in range(TOP_K)]
            for cp in copies:
                cp.wait()

    return pl.kernel(
        body,
        out_type=jax.ShapeDtypeStruct((n_slots, r, LANES), rows3.dtype),
        mesh=_sc_mesh(),
        scratch_types=[pltpu.VMEM((TOP_K, SC_CHUNK), I32),
                       pltpu.VMEM((SC_CHUNK, r, LANES), rows3.dtype),
                       pltpu.SemaphoreType.DMA],
        name="sc_dispatch",
    )(rows3, pos_chunks)


def _sc_gather(rows3, pos_chunks, n):
    _, r, _ = rows3.shape
    workers = SC_CORES * SC_SUBCORES
    per_worker = n // SC_CHUNK // workers

    def body(rows_hbm, pos_hbm, out_hbm, idx_v, buf_a, buf_b, sem_a, sem_b):
        wid = lax.axis_index("s") * SC_CORES + lax.axis_index("c")
        bufs, sems = (buf_a, buf_b), (sem_a, sem_b)

        @pl.loop(0, per_worker)
        def _(c):
            chunk = wid * per_worker + c
            pltpu.sync_copy(pos_hbm.at[chunk], idx_v)
            gathers = [None] * TOP_K
            gathers[0] = pltpu.async_copy(rows_hbm.at[idx_v.at[0]], bufs[0], sems[0])
            for k in range(TOP_K):
                if k + 1 < TOP_K:
                    nxt = (k + 1) % 2
                    gathers[k + 1] = pltpu.async_copy(rows_hbm.at[idx_v.at[k + 1]], bufs[nxt], sems[nxt])
                gathers[k].wait()
                pltpu.sync_copy(bufs[k % 2], out_hbm.at[k, pl.ds(chunk * SC_CHUNK, SC_CHUNK)])

    return pl.kernel(
        body,
        out_type=jax.ShapeDtypeStruct((TOP_K, n, r, LANES), rows3.dtype),
        mesh=_sc_mesh(),
        scratch_types=[pltpu.VMEM((TOP_K, SC_CHUNK), I32),
                       pltpu.VMEM((SC_CHUNK, r, LANES), rows3.dtype),
                       pltpu.VMEM((SC_CHUNK, r, LANES), rows3.dtype),
                       pltpu.SemaphoreType.DMA,
                       pltpu.SemaphoreType.DMA],
        name="sc_gather",
    )(rows3, pos_chunks)


COMBINE_TILE = 512


def _combine_kernel(h_ref, gate_ref, y_ref, o_ref):
    tc, d = h_ref.shape
    kt = d // (2 * LANES)
    gates = gate_ref[...]
    for j in range(kt):
        lo_acc = h_ref[:, j * LANES:(j + 1) * LANES]
        hi_acc = h_ref[:, (kt + j) * LANES:(kt + j + 1) * LANES]
        for k in range(TOP_K):
            lo, hi = _load_packed_tile(y_ref, (k,), tc, kt, j)
            g = gates[:, k:k + 1]
            lo_acc = lo_acc + g * lo
            hi_acc = hi_acc + g * hi
        o_ref[:, j * LANES:(j + 1) * LANES] = lo_acc
        o_ref[:, (kt + j) * LANES:(kt + j + 1) * LANES] = hi_acc


def _combine(h, gates, y4):
    n, d = h.shape
    tc = COMBINE_TILE
    row = lambda i: (i, 0)
    return pl.pallas_call(
        _combine_kernel,
        grid=(n // tc,),
        in_specs=[pl.BlockSpec((tc, d), row),
                  pl.BlockSpec((tc, LANES), row),
                  pl.BlockSpec((TOP_K, tc * d // (2 * LANES), LANES), lambda i: (0, i, 0))],
        out_specs=pl.BlockSpec((tc, d), row),
        out_shape=jax.ShapeDtypeStruct((n, d), F32),
        compiler_params=_cparams(("parallel",)),
        name="moe_combine",
    )(h, gates, y4)


def _layer(h3, layer, norm_mix_g, w_in, conv_w, conv_b, conv_ln_g, conv_ln_b, q_norm_g, k_norm_g,
           lambda_qk, subln_g, rel_bias, w_out, norm_ffn_g, router_w, router_b,
           w_gate_up, b_gate_up, w_down, b_down):
    b, s, d = h3.shape
    n = b * s
    d_conv = conv_w.shape[-1]
    d_attn = N_HEADS * VALUE_DIM
    lambda_init = 0.8 - 0.6 * math.exp(-0.3 * layer)
    x2 = h3.reshape(n, d)

    u, q, k, v = _in_proj(x2, norm_mix_g.reshape(1, d), w_in.astype(BF16), d_conv, d_attn, tm=512)
    conv_o = _conv(u.reshape(b, s, d_conv), conv_w, conv_b.reshape(1, d_conv),
                   conv_ln_g.reshape(1, d_conv), conv_ln_b.reshape(1, d_conv))
    attn_o = _attention(q.reshape(b, s, d_attn), k.reshape(b, s, d_attn), v.reshape(b, s, d_attn),
                        rel_bias, q_norm_g.reshape(1, VALUE_DIM), k_norm_g.reshape(1, VALUE_DIM),
                        lambda_qk, subln_g.reshape(1, VALUE_DIM), lambda_init)

    n_exp = router_w.shape[1]
    rw = jnp.zeros((d, LANES), F32).at[:, :n_exp].set(router_w)
    rwh = rw.astype(BF16)
    rwl = (rw - rwh.astype(F32)).astype(BF16)
    rb = jnp.full((1, LANES), NEG_INF, F32).at[0, :n_exp].set(router_b)
    w_out_bf = w_out.astype(BF16)
    hres, hn, idx, gates, rank, cnt = _out_proj(
        x2, conv_o.reshape(n, d_conv), attn_o.reshape(n, d_attn),
        w_out_bf[:d_conv], w_out_bf[d_conv:], norm_ffn_g.reshape(1, d), rwh, rwl, rb, tm=512)

    bm = MOE_BLOCK
    nblk = n * TOP_K // bm + n_exp
    counts = cnt[0, :n_exp].astype(I32)
    padded = (counts + bm - 1) // bm * bm
    pad_end = jnp.cumsum(padded)
    pad_start = pad_end - padded
    eid = jnp.arange(n_exp, dtype=I32)
    start_of = jnp.sum(jnp.where(idx[None, :TOP_K] == eid[:, None, None],
                                 pad_start[:, None, None], 0), axis=0)
    pos = start_of + rank[:TOP_K]
    pos_chunks = pos.reshape(TOP_K, n // SC_CHUNK, SC_CHUNK).transpose(1, 0, 2)
    block_start = jnp.arange(nblk, dtype=I32) * bm
    block_expert = jnp.minimum(jnp.sum(block_start[:, None] >= pad_end[None, :], axis=1),
                               n_exp - 1).astype(I32)
    n_used = (pad_end[-1:] // bm).astype(I32)
    blk = jnp.arange(nblk, dtype=I32)
    first = jnp.logical_and(
        jnp.concatenate([jnp.ones((1,), bool), block_expert[1:] != block_expert[:-1]]),
        blk < n_used[0]).astype(I32)
    wslot = ((jnp.cumsum(first) - 1) % 2).astype(I32)
    later_used = jnp.logical_and(eid[None, :] > eid[:, None], counts[None, :] > 0)
    next_used = jnp.min(jnp.where(later_used, eid[None, :], n_exp), axis=1)
    next_used = jnp.where(next_used == n_exp, -1, next_used).astype(I32)
    plan = (block_expert, first, wslot, next_used[block_expert], n_used)

    d_ff = w_down.shape[1]
    bgu_p = b_gate_up.reshape(n_exp, d_ff // LANES, LANES, 2).transpose(0, 1, 3, 2).reshape(n_exp, 1, 2 * d_ff)
    r = d // (2 * LANES)
    xs = _sc_dispatch(hn.reshape(n, r, LANES), pos_chunks, nblk * bm)
    yb = _moe(plan, xs.reshape(nblk * bm * r, LANES), w_gate_up, bgu_p, w_down,
              b_down.reshape(n_exp, 1, d))
    y4 = _sc_gather(yb.reshape(nblk * bm, r, LANES), pos_chunks, n)
    out = _combine(hres, gates, y4.reshape(TOP_K, n * r, LANES))
    return out.reshape(b, s, d)


def kernel(x, norm_mix_g, w_in, conv_w, conv_b, conv_ln_g, conv_ln_b, q_norm_g, k_norm_g, lambda_qk,
           subln_g, rel_bias, w_out, norm_ffn_g, router_w, router_b, w_gate_up, b_gate_up, w_down,
           b_down):
    h = x
    for layer in range(norm_mix_g.shape[0]):
        h = _layer(h, layer, norm_mix_g[layer], w_in[layer], conv_w[layer], conv_b[layer],
                   conv_ln_g[layer], conv_ln_b[layer], q_norm_g[layer], k_norm_g[layer],
                   lambda_qk[layer], subln_g[layer], rel_bias, w_out[layer], norm_ffn_g[layer],
                   router_w[layer], router_b[layer], w_gate_up[layer], b_gate_up[layer],
                   w_down[layer], b_down[layer])
    return h
```

```python
import functools
import math

import jax
import jax.numpy as jnp
import numpy as np
from jax import lax
from jax.experimental import pallas as pl
from jax.experimental.pallas import tpu as pltpu
from jax.experimental.pallas import tpu_sc as plsc

F32 = jnp.float32
BF16 = jnp.bfloat16
I32 = jnp.int32
U32 = jnp.uint32

CHUNK = 64
CONV_WIDTH = 31
N_HEADS = 4
HEAD_DIM = 64
VALUE_DIM = 2 * HEAD_DIM
REL_BUCKETS = 32
REL_MAX_DIST = 128
N_EXPERTS = 32
TOP_K = 4
SWIGLU_LIMIT = 7.0
SWIGLU_ALPHA = 1.702
EPS = 1e-5
LOG2E = 1.4426950408889634

LANES = 128
SUBLANES = 8
MXU_DIM = 256
VMEM_LIMIT = 56 * 1024 * 1024

NEG_INF = float("-inf")


def _cparams(sem, vmem=VMEM_LIMIT, flags=None):
    return pltpu.CompilerParams(dimension_semantics=sem, vmem_limit_bytes=vmem, flags=flags)


def _dot(a, b):
    return jnp.dot(a, b, preferred_element_type=F32)


def _dot_nt(a, b):
    return lax.dot_general(a, b, (((1,), (1,)), ((), ())), preferred_element_type=F32)


def _pack_pair(lo, hi):
    lo_bits = lax.bitcast_convert_type(lo.astype(BF16).astype(F32), U32)
    hi_bits = lax.bitcast_convert_type(hi.astype(BF16).astype(F32), U32)
    return (lo_bits >> 16) | hi_bits


def _unpack_pair(w):
    return (lax.bitcast_convert_type(w << 16, F32),
            lax.bitcast_convert_type(w & jnp.uint32(0xFFFF0000), F32))


def _store_packed_rows(ref, lead, x):
    rows, k = x.shape[0], x.shape[1] // (2 * LANES)
    for j in range(k):
        w = _pack_pair(x[:, j * LANES:(j + 1) * LANES], x[:, (k + j) * LANES:(k + j + 1) * LANES])
        ref[(*lead, pl.ds(j, rows, stride=k), slice(None))] = w


def _load_packed_tile(ref, lead, rows, k, j):
    return _unpack_pair(ref[(*lead, pl.ds(j, rows, stride=k), slice(None))])


def _in_proj_kernel(x_ref, g_ref, w_ref, u_ref, q_ref, k_ref, v_ref, *, d_conv, d_attn):
    x = x_ref[...]
    ms = jnp.mean(x * x, axis=-1, keepdims=True)
    y = (x * lax.rsqrt(ms + EPS) * g_ref[...]).astype(BF16)
    proj = _dot(y, w_ref[...])
    a = proj[:, :d_conv]
    g = proj[:, d_conv:2 * d_conv]
    u_ref[...] = a * jax.nn.sigmoid(g)
    o = 2 * d_conv
    q_ref[...] = proj[:, o:o + d_attn]
    k_ref[...] = proj[:, o + d_attn:o + 2 * d_attn]
    v_ref[...] = proj[:, o + 2 * d_attn:o + 3 * d_attn].astype(BF16)


def _in_proj(x2, g, w_bf, d_conv, d_attn, tm):
    n, d = x2.shape
    d_in = w_bf.shape[1]
    row = lambda i: (i, 0)
    fixed = lambda i: (0, 0)
    return pl.pallas_call(
        functools.partial(_in_proj_kernel, d_conv=d_conv, d_attn=d_attn),
        grid=(n // tm,),
        in_specs=[pl.BlockSpec((tm, d), row),
                  pl.BlockSpec((1, d), fixed),
                  pl.BlockSpec((d, d_in), fixed)],
        out_specs=[pl.BlockSpec((tm, d_conv), row),
                   pl.BlockSpec((tm, d_attn), row),
                   pl.BlockSpec((tm, d_attn), row),
                   pl.BlockSpec((tm, d_attn), row)],
        out_shape=[jax.ShapeDtypeStruct((n, d_conv), F32),
                   jax.ShapeDtypeStruct((n, d_attn), F32),
                   jax.ShapeDtypeStruct((n, d_attn), F32),
                   jax.ShapeDtypeStruct((n, d_attn), BF16)],
        compiler_params=_cparams(("parallel",)),
        name="in_proj",
    )(x2, g, w_bf)


CONV_PAD = 32
CONV_SEQ_TILE = 512
CONV_TILE = 128
CONV_NORM_TILE = 256


def _conv_kernel(u_ref, prev_ref, w_ref, cb_ref, lg_ref, lb_ref, o_ref, sh_ref, y_ref):
    ts, c = u_ref.shape[1], u_ref.shape[2]
    plen = ts + CONV_PAD
    hist = prev_ref[0]
    hist = jnp.where(pl.program_id(1) > 0, hist, jnp.zeros_like(hist))
    sh_ref[0, pl.ds(0, CONV_PAD), :] = hist
    sh_ref[0, pl.ds(CONV_PAD, ts), :] = u_ref[0]
    sh_ref[0, pl.ds(plen, SUBLANES), :] = jnp.zeros((SUBLANES, c), F32)

    bt = CONV_PAD

    def shift(i, _):
        p0 = pl.multiple_of(i * bt, bt)
        win = sh_ref[0, pl.ds(p0, bt + SUBLANES), :]
        for r in range(1, SUBLANES):
            sh_ref[r, pl.ds(p0, bt), :] = win[r:r + bt]
        return 0

    lax.fori_loop(0, plen // bt, shift, 0)

    off0 = CONV_PAD - (CONV_WIDTH - 1)

    by_shift = {}
    for j in range(CONV_WIDTH):
        a, r = divmod(off0 + j, SUBLANES)
        by_shift.setdefault(r, []).append((a, j))

    for lt in range(c // LANES):
        lanes = slice(lt * LANES, (lt + 1) * LANES)
        taps = [w_ref[pl.ds(j, 1), lanes] for j in range(CONV_WIDTH)]

        def tap_body(i, _, lanes=lanes, taps=taps):
            t0 = pl.multiple_of(i * CONV_TILE, CONV_TILE)
            acc = jnp.zeros((CONV_TILE, LANES), F32)
            for r, group in by_shift.items():
                a_lo = min(a for a, _ in group)
                a_hi = max(a for a, _ in group)
                rows = CONV_TILE + (a_hi - a_lo) * SUBLANES
                start = pl.multiple_of(t0 + a_lo * SUBLANES, SUBLANES)
                win = sh_ref[r, pl.ds(start, rows), lanes]
                for a, j in group:
                    lo = (a - a_lo) * SUBLANES
                    acc = acc + win[lo:lo + CONV_TILE] * taps[j]
            y_ref[pl.ds(t0, CONV_TILE), lanes] = acc
            return 0

        lax.fori_loop(0, ts // CONV_TILE, tap_body, 0)

    def norm_body(i, _):
        t0 = pl.multiple_of(i * CONV_NORM_TILE, CONV_NORM_TILE)
        y = y_ref[pl.ds(t0, CONV_NORM_TILE), :] + cb_ref[...]
        mu = jnp.mean(y, axis=-1, keepdims=True)
        yc = y - mu
        var = jnp.mean(yc * yc, axis=-1, keepdims=True)
        z = yc * lax.rsqrt(var + EPS) * lg_ref[...] + lb_ref[...]
        o_ref[0, pl.ds(t0, CONV_NORM_TILE), :] = (z * jax.nn.sigmoid(z)).astype(o_ref.dtype)
        return 0

    lax.fori_loop(0, ts // CONV_NORM_TILE, norm_body, 0)


def _conv(u3, conv_w, conv_b, ln_g, ln_b):
    b, s, c = u3.shape
    ts = min(CONV_SEQ_TILE, s)
    hist_per_tile = ts // CONV_PAD
    fixed = lambda i, j: (0, 0)
    return pl.pallas_call(
        _conv_kernel,
        grid=(b, s // ts),
        in_specs=[pl.BlockSpec((1, ts, c), lambda i, j: (i, j, 0)),
                  pl.BlockSpec((1, CONV_PAD, c),
                               lambda i, j: (i, jnp.maximum(j * hist_per_tile - 1, 0), 0)),
                  pl.BlockSpec((CONV_WIDTH, c), fixed),
                  pl.BlockSpec((1, c), fixed),
                  pl.BlockSpec((1, c), fixed),
                  pl.BlockSpec((1, c), fixed)],
        out_specs=pl.BlockSpec((1, ts, c), lambda i, j: (i, j, 0)),
        out_shape=jax.ShapeDtypeStruct((b, s, c), BF16),
        scratch_shapes=[pltpu.VMEM((SUBLANES, ts + CONV_PAD + SUBLANES, c), F32),
                        pltpu.VMEM((ts, c), F32)],
        compiler_params=_cparams(("parallel", "parallel")),
        name="conv_mixer",
    )(u3, u3, conv_w, conv_b, ln_g, ln_b)


ATT_TILE = 256
FAR_BUCKET = REL_BUCKETS // 2 - 1


def _t5_bucket(rel):
    nb = REL_BUCKETS // 2
    max_exact = nb // 2
    ret = jnp.where(rel > 0, nb, 0)
    n = jnp.abs(rel)
    nf = jnp.maximum(n, 1).astype(jnp.float32)
    large = max_exact + (jnp.log(nf / max_exact) / math.log(REL_MAX_DIST / max_exact)
                         * (nb - max_exact)).astype(jnp.int32)
    large = jnp.minimum(large, nb - 1)
    return ret + jnp.where(n < max_exact, n, large)


def _near_buckets():
    assert ATT_TILE >= REL_MAX_DIST and ATT_TILE % CHUNK == 0
    qpos = jnp.arange(ATT_TILE, dtype=I32)[:, None]
    kpos = jnp.arange(ATT_TILE, dtype=I32)[None, :]
    prev = _t5_bucket(kpos - ATT_TILE - qpos)
    diag = _t5_bucket(kpos - qpos)
    diag = jnp.where(kpos // CHUNK <= qpos // CHUNK, diag, -1)
    return jnp.stack([prev, diag]).astype(I32)


def _attn_kernel(tab_ref, q_ref, k_ref, v_ref, bkt_ref, bd_ref, qg_ref, kg_ref, lqk_ref, sg_ref, o_ref,
                 qz_ref, kn_ref, v1_ref, bias_ref, *, seq, lambda_init):
    h = pl.program_id(0)
    t = ATT_TILE
    n_tiles = seq // t
    lane = lax.broadcasted_iota(I32, (1, VALUE_DIM), 1)
    first = lane < HEAD_DIM

    @pl.when(pl.program_id(1) == 0)
    def _():
        far = tab_ref[FAR_BUCKET, h]
        for d in range(2):
            bkt = bkt_ref[d]
            tile = jnp.full((t, t), NEG_INF, F32)
            for b in range(REL_BUCKETS):
                tile = jnp.where(bkt == b, (tab_ref[b, h] - far) * LOG2E, tile)
            bias_ref[d] = tile

    def half_norm(x, g):
        x2 = x * x
        hi = x2.astype(BF16)
        lo = (x2 - hi.astype(F32)).astype(BF16)
        ms = (_dot(hi, bd_ref[...]) + _dot(lo, bd_ref[...])) * (1.0 / HEAD_DIM)
        return x * lax.rsqrt(ms + EPS) * g

    q_scale = HEAD_DIM ** -0.5 * LOG2E

    def prep(i, _):
        r0 = pl.multiple_of(i * t, t)
        qn = half_norm(q_ref[0, pl.ds(r0, t), :], qg_ref[...]) * q_scale
        qz_ref[0, pl.ds(r0, t), :] = jnp.where(first, qn, 0.0).astype(BF16)
        qz_ref[1, pl.ds(r0, t), :] = jnp.where(first, 0.0, qn).astype(BF16)
        kn_ref[pl.ds(r0, t), :] = half_norm(k_ref[0, pl.ds(r0, t), :], kg_ref[...]).astype(BF16)
        ones_col = jnp.broadcast_to(jnp.where(lane == 0, 1.0, 0.0).astype(BF16), (t, VALUE_DIM))
        v1_ref[pl.ds(r0, t), :] = jnp.concatenate([v_ref[0, pl.ds(r0, t), :], ones_col], axis=1)
        return 0

    lax.fori_loop(0, n_tiles, prep, 0)

    lqk = lqk_ref[...]
    lam = (jnp.exp(jnp.sum(lqk[0:1] * lqk[1:2], axis=-1, keepdims=True))
           - jnp.exp(jnp.sum(lqk[2:3] * lqk[3:4], axis=-1, keepdims=True)) + lambda_init)

    for i in range(n_tiles):
        q0, kend = i * t, (i + 1) * t
        keys = kn_ref[0:kend, :]
        vals = v1_ref[0:kend, :]
        maps = []
        for m in range(2):
            s = _dot_nt(qz_ref[m, q0:q0 + t, :], keys)
            parts = [s[:, kend - t:] + bias_ref[1]]
            if i >= 1:
                parts.insert(0, s[:, kend - 2 * t:kend - t] + bias_ref[0])
            if i >= 2:
                parts.insert(0, s[:, :kend - 2 * t])
            s = jnp.concatenate(parts, axis=1) if len(parts) > 1 else parts[0]
            p = jnp.exp2(s - jnp.max(s, axis=-1, keepdims=True))
            pv = _dot(p.astype(BF16), vals)
            maps.append(pv[:, :VALUE_DIM] / pv[:, VALUE_DIM:VALUE_DIM + 1])
        o = maps[0] - lam * maps[1]
        ms = jnp.mean(o * o, axis=-1, keepdims=True)
        o = o * lax.rsqrt(ms + EPS) * sg_ref[...] * (1.0 - lambda_init)
        o_ref[0, q0:q0 + t, :] = o.astype(o_ref.dtype)


def _attention(q3, k3, v3, rel_bias, q_g, k_g, lam_qk, subln_g, lambda_init):
    b, s, _ = q3.shape
    t = ATT_TILE
    head = lambda j, i: (i, 0, j)
    fixed2 = lambda j, i: (0, 0)
    fixed3 = lambda j, i: (0, 0, 0)
    half = np.arange(VALUE_DIM) // HEAD_DIM
    blockdiag = jnp.asarray(half[:, None] == half[None, :], dtype=BF16)
    return pl.pallas_call(
        functools.partial(_attn_kernel, seq=s, lambda_init=lambda_init),
        grid=(N_HEADS, b),
        in_specs=[pl.BlockSpec(memory_space=pltpu.SMEM),
                  pl.BlockSpec((1, s, VALUE_DIM), head),
                  pl.BlockSpec((1, s, VALUE_DIM), head),
                  pl.BlockSpec((1, s, VALUE_DIM), head),
                  pl.BlockSpec((2, t, t), fixed3),
                  pl.BlockSpec((VALUE_DIM, VALUE_DIM), fixed2),
                  pl.BlockSpec((1, VALUE_DIM), fixed2),
                  pl.BlockSpec((1, VALUE_DIM), fixed2),
                  pl.BlockSpec((4, HEAD_DIM), fixed2),
                  pl.BlockSpec((1, VALUE_DIM), fixed2)],
        out_specs=pl.BlockSpec((1, s, VALUE_DIM), head),
        out_shape=jax.ShapeDtypeStruct((b, s, N_HEADS * VALUE_DIM), BF16),
        scratch_shapes=[pltpu.VMEM((2, s, VALUE_DIM), BF16),
                        pltpu.VMEM((s, VALUE_DIM), BF16),
                        pltpu.VMEM((s, 2 * VALUE_DIM), BF16),
                        pltpu.VMEM((2, t, t), F32)],
        compiler_params=_cparams(("arbitrary", "arbitrary")),
        name="diff_attn",
    )(rel_bias, q3, k3, v3, _near_buckets(), blockdiag, q_g, k_g, lam_qk, subln_g)


def _out_proj_kernel(x_ref, c_ref, a_ref, wc_ref, wa_ref, g_ref, rwh_ref, rwl_ref, rb_ref, tri_ref,
                     h_ref, hn_ref, idx_ref, gate_ref, rank_ref, cnt_ref, carry_ref):
    @pl.when(pl.program_id(0) == 0)
    def _():
        carry_ref[...] = jnp.zeros_like(carry_ref)

    h = x_ref[...] + _dot(c_ref[...], wc_ref[...]) + _dot(a_ref[...], wa_ref[...])
    h_ref[...] = h
    ms = jnp.mean(h * h, axis=-1, keepdims=True)
    hn = h * lax.rsqrt(ms + EPS) * g_ref[...]
    _store_packed_rows(hn_ref, (), hn)

    hi = hn.astype(BF16)
    lo = (hn - hi.astype(F32)).astype(BF16)
    logits = (_dot(hi, rwh_ref[...]) + (_dot(hi, rwl_ref[...]) + _dot(lo, rwh_ref[...]))) + rb_ref[...]

    tm = logits.shape[0]
    lane = lax.broadcasted_iota(I32, (tm, LANES), 1).astype(F32)
    work = logits
    vals, idxs = [], []
    for _ in range(TOP_K):
        mx = jnp.max(work, axis=-1, keepdims=True)
        ix = jnp.min(jnp.where(work == mx, lane, float(LANES)), axis=-1, keepdims=True)
        vals.append(mx)
        idxs.append(ix)
        work = jnp.where(lane == ix, NEG_INF, work)
    exps = [jnp.exp(v - vals[0]) for v in vals]
    denom = exps[0]
    for e in exps[1:]:
        denom = denom + e

    sel = jnp.zeros((tm, LANES), F32)
    for ix in idxs:
        sel = sel + jnp.where(lane == ix, 1.0, 0.0)
    rank = _dot(tri_ref[...], sel.astype(BF16)) + carry_ref[...]
    carry_ref[...] = carry_ref[...] + jnp.sum(sel, axis=0, keepdims=True)
    cnt_ref[...] = carry_ref[...]

    idx_out = jnp.zeros((tm, LANES), F32)
    gate_out = jnp.zeros((tm, LANES), F32)
    rank_out = jnp.zeros((tm, LANES), F32)
    for k in range(TOP_K):
        rk = jnp.sum(jnp.where(lane == idxs[k], rank, 0.0), axis=-1, keepdims=True)
        idx_out = jnp.where(lane == k, idxs[k], idx_out)
        gate_out = jnp.where(lane == k, exps[k] / denom, gate_out)
        rank_out = jnp.where(lane == k, rk, rank_out)
    idx_ref[...] = jnp.transpose(idx_out)[:SUBLANES].astype(I32)
    rank_ref[...] = jnp.transpose(rank_out)[:SUBLANES].astype(I32)
    gate_ref[...] = gate_out


def _out_proj(x2, conv_o, attn_o, wc, wa, g, rwh, rwl, rb, tm):
    n, d = x2.shape
    dc, da = conv_o.shape[1], attn_o.shape[1]
    row = lambda i: (i, 0)
    fixed = lambda i: (0, 0)
    tri = jnp.tril(jnp.ones((tm, tm), F32), -1).astype(BF16)
    return pl.pallas_call(
        _out_proj_kernel,
        grid=(n // tm,),
        in_specs=[pl.BlockSpec((tm, d), row),
                  pl.BlockSpec((tm, dc), row),
                  pl.BlockSpec((tm, da), row),
                  pl.BlockSpec((dc, d), fixed),
                  pl.BlockSpec((da, d), fixed),
                  pl.BlockSpec((1, d), fixed),
                  pl.BlockSpec((d, LANES), fixed),
                  pl.BlockSpec((d, LANES), fixed),
                  pl.BlockSpec((1, LANES), fixed),
                  pl.BlockSpec((tm, tm), fixed)],
        out_specs=[pl.BlockSpec((tm, d), row),
                   pl.BlockSpec((tm * d // (2 * LANES), LANES), row),
                   pl.BlockSpec((SUBLANES, tm), lambda i: (0, i)),
                   pl.BlockSpec((tm, LANES), row),
                   pl.BlockSpec((SUBLANES, tm), lambda i: (0, i)),
                   pl.BlockSpec((1, LANES), fixed)],
        out_shape=[jax.ShapeDtypeStruct((n, d), F32),
                   jax.ShapeDtypeStruct((n * d // (2 * LANES), LANES), U32),
                   jax.ShapeDtypeStruct((SUBLANES, n), I32),
                   jax.ShapeDtypeStruct((n, LANES), F32),
                   jax.ShapeDtypeStruct((SUBLANES, n), I32),
                   jax.ShapeDtypeStruct((1, LANES), F32)],
        scratch_shapes=[pltpu.VMEM((1, LANES), F32)],
        compiler_params=_cparams(("arbitrary",)),
        name="out_proj_router",
    )(x2, conv_o, attn_o, wc, wa, g, rwh, rwl, rb, tri)


MOE_BLOCK = 512


def _moe_kernel(be_ref, first_ref, wslot_ref, nxt_ref, nused_ref,
                x_ref, wgu_hbm, wd_hbm, bgu_ref, bd_ref, perm_ref,
                y_ref, wgu_f32, wd_f32, wsem, wgu_bf, wd_bf, x_bf, act_bf):
    i = pl.program_id(0)
    bm = MOE_BLOCK
    n_used = nused_ref[0]
    d = wd_bf.shape[1]
    d_ff = wd_bf.shape[0]
    n_groups = d_ff // LANES

    def weight_copies(e, slot):
        return (pltpu.make_async_copy(wgu_hbm.at[e], wgu_f32.at[slot], wsem.at[0, slot]),
                pltpu.make_async_copy(wd_hbm.at[e], wd_f32.at[slot], wsem.at[1, slot]))

    @pl.when(i == 0)
    def _():
        for c in weight_copies(be_ref[0], 0):
            c.start()

    active = i < n_used

    @pl.when(jnp.logical_and(active, first_ref[i] == 1))
    def _():
        ws = wslot_ref[i]
        for c in weight_copies(be_ref[i], ws):
            c.wait()
        for g in range(n_groups):
            cols = pl.ds(g * MXU_DIM, MXU_DIM)
            wgu_bf[:, cols] = _dot(wgu_f32[ws, :, cols].astype(BF16), perm_ref[...]).astype(BF16)
        wd_bf[...] = wd_f32[ws].astype(BF16)

        @pl.when(nxt_ref[i] >= 0)
        def _():
            for c in weight_copies(nxt_ref[i], 1 - ws):
                c.start()

    def activate(g, gu):
        gu = gu + bgu_ref[0, :, g * MXU_DIM:(g + 1) * MXU_DIM]
        gate = jnp.minimum(gu[:, :LANES], SWIGLU_LIMIT)
        lin = jnp.clip(gu[:, LANES:], -SWIGLU_LIMIT, SWIGLU_LIMIT)
        act_bf[:, g * LANES:(g + 1) * LANES] = (
            gate * jax.nn.sigmoid(SWIGLU_ALPHA * gate) * (lin + 1.0)).astype(BF16)

    @pl.when(active)
    def _():
        k = d // (2 * LANES)
        for j in range(k):
            lo, hi = _load_packed_tile(x_ref, (), bm, k, j)
            x_bf[:, j * LANES:(j + 1) * LANES] = lo.astype(BF16)
            x_bf[:, (k + j) * LANES:(k + j + 1) * LANES] = hi.astype(BF16)
        for g in range(n_groups):
            activate(g, _dot(x_bf[...], wgu_bf[:, g * MXU_DIM:(g + 1) * MXU_DIM]))
        _store_packed_rows(y_ref, (), _dot(act_bf[...], wd_bf[...]) + bd_ref[0])

    @pl.when(i >= n_used)
    def _():
        y_ref[...] = jnp.zeros_like(y_ref)


def _deinterleave_perm():
    src = np.arange(MXU_DIM)
    dst = np.where(src % 2 == 0, src // 2, LANES + src // 2)
    p = np.zeros((MXU_DIM, MXU_DIM), np.float32)
    p[src, dst] = 1.0
    return jnp.asarray(p, dtype=BF16)


def _moe(plan, xs, w_gate_up, bgu_p, w_down, b_down):
    n_exp, d, d_gu = w_gate_up.shape
    d_ff = w_down.shape[1]
    bm = MOE_BLOCK
    nblk = plan[0].shape[0]
    slab = bm * d // (2 * LANES)
    by_expert = lambda i, be, *_: (be[i], 0, 0)
    grid_spec = pltpu.PrefetchScalarGridSpec(
        num_scalar_prefetch=5,
        grid=(nblk,),
        in_specs=[
            pl.BlockSpec((slab, LANES), lambda i, *_: (i, 0)),
            pl.BlockSpec(memory_space=pl.ANY),
            pl.BlockSpec(memory_space=pl.ANY),
            pl.BlockSpec((1, 1, d_gu), by_expert),
            pl.BlockSpec((1, 1, d), by_expert),
            pl.BlockSpec((MXU_DIM, MXU_DIM), lambda i, *_: (0, 0)),
        ],
        out_specs=pl.BlockSpec((slab, LANES), lambda i, *_: (i, 0)),
        scratch_shapes=[pltpu.VMEM((2, d, d_gu), F32),
                        pltpu.VMEM((2, d_ff, d), F32),
                        pltpu.SemaphoreType.DMA((2, 2)),
                        pltpu.VMEM((d, d_gu), BF16),
                        pltpu.VMEM((d_ff, d), BF16),
                        pltpu.VMEM((bm, d), BF16),
                        pltpu.VMEM((bm, d_ff), BF16)],
    )
    return pl.pallas_call(
        _moe_kernel,
        grid_spec=grid_spec,
        out_shape=jax.ShapeDtypeStruct((nblk * slab, LANES), U32),
        compiler_params=_cparams(("arbitrary",)),
        name="moe_experts",
    )(*plan, xs, w_gate_up, w_down, bgu_p, b_down, _deinterleave_perm())


SC_CORES = 2
SC_SUBCORES = 16
SC_CHUNK = 64


def _sc_mesh():
    return plsc.VectorSubcoreMesh(core_axis_name="c", subcore_axis_name="s")


def _sc_dispatch(rows3, pos_chunks, n_slots):
    n, r, _ = rows3.shape
    workers = SC_CORES * SC_SUBCORES
    per_worker = n // SC_CHUNK // workers

    def body(rows_hbm, pos_hbm, out_hbm, idx_v, rows_v, sem):
        wid = lax.axis_index("s") * SC_CORES + lax.axis_index("c")

        @pl.loop(0, per_worker)
        def _(c):
            chunk = wid * per_worker + c
            pltpu.sync_copy(pos_hbm.at[chunk], idx_v)
            pltpu.sync_copy(rows_hbm.at[pl.ds(chunk * SC_CHUNK, SC_CHUNK)], rows_v)
            copies = [pltpu.async_copy(rows_v, out_hbm.at[idx_v.at[k]], sem) for k in range(TOP_K)]
            for cp in copies:
                cp.wait()

    return pl.kernel(
        body,
        out_type=jax.ShapeDtypeStruct((n_slots, r, LANES), rows3.dtype),
        mesh=_sc_mesh(),
        scratch_types=[pltpu.VMEM((TOP_K, SC_CHUNK), I32),
                       pltpu.VMEM((SC_CHUNK, r, LANES), rows3.dtype),
                       pltpu.SemaphoreType.DMA],
        name="sc_dispatch",
    )(rows3, pos_chunks)


def _sc_gather(rows3, pos_chunks, n):
    _, r, _ = rows3.shape
    workers = SC_CORES * SC_SUBCORES
    per_worker = n // SC_CHUNK // workers

    def body(rows_hbm, pos_hbm, out_hbm, idx_v, buf_a, buf_b, sem_a, sem_b):
        wid = lax.axis_index("s") * SC_CORES + lax.axis_index("c")
        bufs, sems = (buf_a, buf_b), (sem_a, sem_b)

        @pl.loop(0, per_worker)
        def _(c):
            chunk = wid * per_worker + c
            pltpu.sync_copy(pos_hbm.at[chunk], idx_v)
            gathers = [None] * TOP_K
            gathers[0] = pltpu.async_copy(rows_hbm.at[idx_v.at[0]], bufs[0], sems[0])
            for k in range(TOP_K):
                if k + 1 < TOP_K:
                    nxt = (k + 1) % 2
                    gathers[k + 1] = pltpu.async_copy(rows_hbm.at[idx_v.at[k + 1]], bufs[nxt], sems[nxt])
                gathers[k].wait()
                pltpu.sync_copy(bufs[k % 2], out_hbm.at[k, pl.ds(chunk * SC_CHUNK, SC_CHUNK)])

    return pl.kernel(
        body,
        out_type=jax.ShapeDtypeStruct((TOP_K, n, r, LANES), rows3.dtype),
        mesh=_sc_mesh(),
        scratch_types=[pltpu.VMEM((TOP_K, SC_CHUNK), I32),
                       pltpu.VMEM((SC_CHUNK, r, LANES), rows3.dtype),
                       pltpu.VMEM((SC_CHUNK, r, LANES), rows3.dtype),
                       pltpu.SemaphoreType.DMA,
                       pltpu.SemaphoreType.DMA],
        name="sc_gather",
    )(rows3, pos_chunks)


COMBINE_TILE = 512
TAIL_PIECES = 4


def _combine_kernel(h_ref, gate_ref, y_ref, o_ref):
    tc, d = h_ref.shape
    kt = d // (2 * LANES)
    gates = gate_ref[...]
    for j in range(kt):
        lo_acc = h_ref[:, j * LANES:(j + 1) * LANES]
        hi_acc = h_ref[:, (kt + j) * LANES:(kt + j + 1) * LANES]
        for k in range(TOP_K):
            lo, hi = _load_packed_tile(y_ref, (k,), tc, kt, j)
            g = gates[:, k:k + 1]
            lo_acc = lo_acc + g * lo
            hi_acc = hi_acc + g * hi
        o_ref[:, j * LANES:(j + 1) * LANES] = lo_acc
        o_ref[:, (kt + j) * LANES:(kt + j + 1) * LANES] = hi_acc


def _combine(h, gates, y4, first_tile):
    n, d = h.shape
    tc = COMBINE_TILE
    slab = tc * d // (2 * LANES)
    row = lambda i: (first_tile + i, 0)
    return pl.pallas_call(
        _combine_kernel,
        grid=(y4.shape[1] // slab,),
        in_specs=[pl.BlockSpec((tc, d), row),
                  pl.BlockSpec((tc, LANES), row),
                  pl.BlockSpec((TOP_K, slab, LANES), lambda i: (0, i, 0))],
        out_specs=pl.BlockSpec((tc, d), row),
        out_shape=jax.ShapeDtypeStruct((n, d), F32),
        input_output_aliases={0: 0},
        compiler_params=_cparams(("parallel",)),
        name="moe_combine",
    )(h, gates, y4)


def _layer(h3, layer, norm_mix_g, w_in, conv_w, conv_b, conv_ln_g, conv_ln_b, q_norm_g, k_norm_g,
           lambda_qk, subln_g, rel_bias, w_out, norm_ffn_g, router_w, router_b,
           w_gate_up, b_gate_up, w_down, b_down):
    b, s, d = h3.shape
    n = b * s
    d_conv = conv_w.shape[-1]
    d_attn = N_HEADS * VALUE_DIM
    lambda_init = 0.8 - 0.6 * math.exp(-0.3 * layer)
    x2 = h3.reshape(n, d)

    u, q, k, v = _in_proj(x2, norm_mix_g.reshape(1, d), w_in.astype(BF16), d_conv, d_attn, tm=512)
    conv_o = _conv(u.reshape(b, s, d_conv), conv_w, conv_b.reshape(1, d_conv),
                   conv_ln_g.reshape(1, d_conv), conv_ln_b.reshape(1, d_conv))
    attn_o = _attention(q.reshape(b, s, d_attn), k.reshape(b, s, d_attn), v.reshape(b, s, d_attn),
                        rel_bias, q_norm_g.reshape(1, VALUE_DIM), k_norm_g.reshape(1, VALUE_DIM),
                        lambda_qk, subln_g.reshape(1, VALUE_DIM), lambda_init)

    n_exp = router_w.shape[1]
    rw = jnp.zeros((d, LANES), F32).at[:, :n_exp].set(router_w)
    rwh = rw.astype(BF16)
    rwl = (rw - rwh.astype(F32)).astype(BF16)
    rb = jnp.full((1, LANES), NEG_INF, F32).at[0, :n_exp].set(router_b)
    w_out_bf = w_out.astype(BF16)
    hres, hn, idx, gates, rank, cnt = _out_proj(
        x2, conv_o.reshape(n, d_conv), attn_o.reshape(n, d_attn),
        w_out_bf[:d_conv], w_out_bf[d_conv:], norm_ffn_g.reshape(1, d), rwh, rwl, rb, tm=512)

    bm = MOE_BLOCK
    nblk = n * TOP_K // bm + n_exp
    counts = cnt[0, :n_exp].astype(I32)
    padded = (counts + bm - 1) // bm * bm
    eid = jnp.arange(n_exp, dtype=I32)
    pad_end = jnp.sum(jnp.where(eid[None, :] <= eid[:, None], padded[None, :], 0), axis=1)
    pad_start = pad_end - padded
    start_of = jnp.sum(jnp.where(idx[None, :TOP_K] == eid[:, None, None],
                                 pad_start[:, None, None], 0), axis=0)
    pos = start_of + rank[:TOP_K]
    pos_chunks = pos.reshape(TOP_K, n // SC_CHUNK, SC_CHUNK).transpose(1, 0, 2)
    block_start = jnp.arange(nblk, dtype=I32) * bm
    block_expert = jnp.minimum(jnp.sum(block_start[:, None] >= pad_end[None, :], axis=1),
                               n_exp - 1).astype(I32)
    n_used = (pad_end[-1:] // bm).astype(I32)
    blk = jnp.arange(nblk, dtype=I32)
    first = jnp.logical_and(
        jnp.concatenate([jnp.ones((1,), bool), block_expert[1:] != block_expert[:-1]]),
        blk < n_used[0]).astype(I32)
    opened = jnp.sum(jnp.where(blk[None, :] <= blk[:, None], first[None, :], 0), axis=1)
    wslot = ((opened - 1) % 2).astype(I32)
    later_used = jnp.logical_and(eid[None, :] > eid[:, None], counts[None, :] > 0)
    next_used = jnp.min(jnp.where(later_used, eid[None, :], n_exp), axis=1)
    next_used = jnp.where(next_used == n_exp, -1, next_used).astype(I32)
    next_of_block = jnp.sum(jnp.where(block_expert[:, None] == eid[None, :], next_used[None, :], 0),
                            axis=1).astype(I32)
    plan = (block_expert, first, wslot, next_of_block, n_used)

    d_ff = w_down.shape[1]
    bgu_p = b_gate_up.reshape(n_exp, d_ff // LANES, LANES, 2).transpose(0, 1, 3, 2).reshape(n_exp, 1, 2 * d_ff)
    r = d // (2 * LANES)
    xs = _sc_dispatch(hn.reshape(n, r, LANES), pos_chunks, nblk * bm)
    yb = _moe(plan, xs.reshape(nblk * bm * r, LANES), w_gate_up, bgu_p, w_down,
              b_down.reshape(n_exp, 1, d))
    yb3 = yb.reshape(nblk * bm, r, LANES)
    piece_quantum = max(COMBINE_TILE, SC_CHUNK * SC_CORES * SC_SUBCORES)
    pieces = TAIL_PIECES if n % (TAIL_PIECES * piece_quantum) == 0 else 1
    piece = n // pieces
    out = hres
    for c in range(pieces):
        chunks = pos_chunks[c * (piece // SC_CHUNK):(c + 1) * (piece // SC_CHUNK)]
        y4 = _sc_gather(yb3, chunks, piece)
        out = _combine(out, gates, y4.reshape(TOP_K, piece * r, LANES), c * (piece // COMBINE_TILE))
    return out.reshape(b, s, d)


def kernel(x, norm_mix_g, w_in, conv_w, conv_b, conv_ln_g, conv_ln_b, q_norm_g, k_norm_g, lambda_qk,
           subln_g, rel_bias, w_out, norm_ffn_g, router_w, router_b, w_gate_up, b_gate_up, w_down,
           b_down):
    h = x
    for layer in range(norm_mix_g.shape[0]):
        h = _layer(h, layer, norm_mix_g[layer], w_in[layer], conv_w[layer], conv_b[layer],
                   conv_ln_g[layer], conv_ln_b[layer], q_norm_g[layer], k_norm_g[layer],
                   lambda_qk[layer], subln_g[layer], rel_bias, w_out[layer], norm_ffn_g[layer],
                   router_w[layer], router_b[layer], w_gate_up[layer], b_gate_up[layer],
                   w_down[layer], b_down[layer])
    return h
```

```python
import functools
import math

import jax
import jax.numpy as jnp
import numpy as np
from jax import lax
from jax.experimental import pallas as pl
from jax.experimental.pallas import tpu as pltpu
from jax.experimental.pallas import tpu_sc as plsc

F32 = jnp.float32
BF16 = jnp.bfloat16
I32 = jnp.int32
U32 = jnp.uint32

CHUNK = 64
CONV_WIDTH = 31
N_HEADS = 4
HEAD_DIM = 64
VALUE_DIM = 2 * HEAD_DIM
REL_BUCKETS = 32
REL_MAX_DIST = 128
N_EXPERTS = 32
TOP_K = 4
SWIGLU_LIMIT = 7.0
SWIGLU_ALPHA = 1.702
EPS = 1e-5
LOG2E = 1.4426950408889634

LANES = 128
SUBLANES = 8
MXU_DIM = 256
VMEM_LIMIT = 56 * 1024 * 1024

NEG_INF = float("-inf")


def _cparams(sem, vmem=VMEM_LIMIT, flags=None):
    return pltpu.CompilerParams(dimension_semantics=sem, vmem_limit_bytes=vmem, flags=flags)


def _dot(a, b):
    return jnp.dot(a, b, preferred_element_type=F32)


def _dot_nt(a, b):
    return lax.dot_general(a, b, (((1,), (1,)), ((), ())), preferred_element_type=F32)


def _pack_pair(lo, hi):
    lo_bits = lax.bitcast_convert_type(lo.astype(BF16).astype(F32), U32)
    hi_bits = lax.bitcast_convert_type(hi.astype(BF16).astype(F32), U32)
    return (lo_bits >> 16) | hi_bits


def _unpack_pair(w):
    return (lax.bitcast_convert_type(w << 16, F32),
            lax.bitcast_convert_type(w & jnp.uint32(0xFFFF0000), F32))


def _store_packed_rows(ref, lead, x):
    rows, k = x.shape[0], x.shape[1] // (2 * LANES)
    for j in range(k):
        w = _pack_pair(x[:, j * LANES:(j + 1) * LANES], x[:, (k + j) * LANES:(k + j + 1) * LANES])
        ref[(*lead, pl.ds(j, rows, stride=k), slice(None))] = w


def _load_packed_tile(ref, lead, rows, k, j):
    return _unpack_pair(ref[(*lead, pl.ds(j, rows, stride=k), slice(None))])


def _in_proj_kernel(x_ref, g_ref, w_ref, u_ref, q_ref, k_ref, v_ref, *, d_conv, d_attn):
    x = x_ref[...]
    ms = jnp.mean(x * x, axis=-1, keepdims=True)
    y = (x * lax.rsqrt(ms + EPS) * g_ref[...]).astype(BF16)
    proj = _dot(y, w_ref[...])
    a = proj[:, :d_conv]
    g = proj[:, d_conv:2 * d_conv]
    u_ref[...] = a * jax.nn.sigmoid(g)
    o = 2 * d_conv
    q_ref[...] = proj[:, o:o + d_attn]
    k_ref[...] = proj[:, o + d_attn:o + 2 * d_attn]
    v_ref[...] = proj[:, o + 2 * d_attn:o + 3 * d_attn].astype(BF16)


def _in_proj(x2, g, w_bf, d_conv, d_attn, tm):
    n, d = x2.shape
    d_in = w_bf.shape[1]
    row = lambda i: (i, 0)
    fixed = lambda i: (0, 0)
    return pl.pallas_call(
        functools.partial(_in_proj_kernel, d_conv=d_conv, d_attn=d_attn),
        grid=(n // tm,),
        in_specs=[pl.BlockSpec((tm, d), row),
                  pl.BlockSpec((1, d), fixed),
                  pl.BlockSpec((d, d_in), fixed)],
        out_specs=[pl.BlockSpec((tm, d_conv), row),
                   pl.BlockSpec((tm, d_attn), row),
                   pl.BlockSpec((tm, d_attn), row),
                   pl.BlockSpec((tm, d_attn), row)],
        out_shape=[jax.ShapeDtypeStruct((n, d_conv), F32),
                   jax.ShapeDtypeStruct((n, d_attn), F32),
                   jax.ShapeDtypeStruct((n, d_attn), F32),
                   jax.ShapeDtypeStruct((n, d_attn), BF16)],
        compiler_params=_cparams(("parallel",)),
        name="in_proj",
    )(x2, g, w_bf)


CONV_PAD = 32
CONV_SEQ_TILE = 512
CONV_TILE = 128
CONV_NORM_TILE = 256


def _conv_kernel(u_ref, prev_ref, w_ref, cb_ref, lg_ref, lb_ref, o_ref, sh_ref, y_ref):
    ts, c = u_ref.shape[1], u_ref.shape[2]
    plen = ts + CONV_PAD
    hist = prev_ref[0]
    hist = jnp.where(pl.program_id(1) > 0, hist, jnp.zeros_like(hist))
    sh_ref[0, pl.ds(0, CONV_PAD), :] = hist
    sh_ref[0, pl.ds(CONV_PAD, ts), :] = u_ref[0]
    sh_ref[0, pl.ds(plen, SUBLANES), :] = jnp.zeros((SUBLANES, c), F32)

    bt = CONV_PAD

    def shift(i, _):
        p0 = pl.multiple_of(i * bt, bt)
        win = sh_ref[0, pl.ds(p0, bt + SUBLANES), :]
        for r in range(1, SUBLANES):
            sh_ref[r, pl.ds(p0, bt), :] = win[r:r + bt]
        return 0

    lax.fori_loop(0, plen // bt, shift, 0)

    off0 = CONV_PAD - (CONV_WIDTH - 1)

    by_shift = {}
    for j in range(CONV_WIDTH):
        a, r = divmod(off0 + j, SUBLANES)
        by_shift.setdefault(r, []).append((a, j))

    for lt in range(c // LANES):
        lanes = slice(lt * LANES, (lt + 1) * LANES)
        taps = [w_ref[pl.ds(j, 1), lanes] for j in range(CONV_WIDTH)]

        def tap_body(i, _, lanes=lanes, taps=taps):
            t0 = pl.multiple_of(i * CONV_TILE, CONV_TILE)
            acc = jnp.zeros((CONV_TILE, LANES), F32)
            for r, group in by_shift.items():
                a_lo = min(a for a, _ in group)
                a_hi = max(a for a, _ in group)
                rows = CONV_TILE + (a_hi - a_lo) * SUBLANES
                start = pl.multiple_of(t0 + a_lo * SUBLANES, SUBLANES)
                win = sh_ref[r, pl.ds(start, rows), lanes]
                for a, j in group:
                    lo = (a - a_lo) * SUBLANES
                    acc = acc + win[lo:lo + CONV_TILE] * taps[j]
            y_ref[pl.ds(t0, CONV_TILE), lanes] = acc
            return 0

        lax.fori_loop(0, ts // CONV_TILE, tap_body, 0)

    def norm_body(i, _):
        t0 = pl.multiple_of(i * CONV_NORM_TILE, CONV_NORM_TILE)
        y = y_ref[pl.ds(t0, CONV_NORM_TILE), :] + cb_ref[...]
        mu = jnp.mean(y, axis=-1, keepdims=True)
        yc = y - mu
        var = jnp.mean(yc * yc, axis=-1, keepdims=True)
        z = yc * lax.rsqrt(var + EPS) * lg_ref[...] + lb_ref[...]
        o_ref[0, pl.ds(t0, CONV_NORM_TILE), :] = (z * jax.nn.sigmoid(z)).astype(o_ref.dtype)
        return 0

    lax.fori_loop(0, ts // CONV_NORM_TILE, norm_body, 0)


def _conv(u3, conv_w, conv_b, ln_g, ln_b):
    b, s, c = u3.shape
    ts = min(CONV_SEQ_TILE, s)
    hist_per_tile = ts // CONV_PAD
    fixed = lambda i, j: (0, 0)
    return pl.pallas_call(
        _conv_kernel,
        grid=(b, s // ts),
        in_specs=[pl.BlockSpec((1, ts, c), lambda i, j: (i, j, 0)),
                  pl.BlockSpec((1, CONV_PAD, c),
                               lambda i, j: (i, jnp.maximum(j * hist_per_tile - 1, 0), 0)),
                  pl.BlockSpec((CONV_WIDTH, c), fixed),
                  pl.BlockSpec((1, c), fixed),
                  pl.BlockSpec((1, c), fixed),
                  pl.BlockSpec((1, c), fixed)],
        out_specs=pl.BlockSpec((1, ts, c), lambda i, j: (i, j, 0)),
        out_shape=jax.ShapeDtypeStruct((b, s, c), BF16),
        scratch_shapes=[pltpu.VMEM((SUBLANES, ts + CONV_PAD + SUBLANES, c), F32),
                        pltpu.VMEM((ts, c), F32)],
        compiler_params=_cparams(("parallel", "parallel")),
        name="conv_mixer",
    )(u3, u3, conv_w, conv_b, ln_g, ln_b)


ATT_TILE = 256
ATT_SEQS = 2
FAR_BUCKET = REL_BUCKETS // 2 - 1


def _t5_bucket(rel):
    nb = REL_BUCKETS // 2
    max_exact = nb // 2
    ret = jnp.where(rel > 0, nb, 0)
    n = jnp.abs(rel)
    nf = jnp.maximum(n, 1).astype(jnp.float32)
    large = max_exact + (jnp.log(nf / max_exact) / math.log(REL_MAX_DIST / max_exact)
                         * (nb - max_exact)).astype(jnp.int32)
    large = jnp.minimum(large, nb - 1)
    return ret + jnp.where(n < max_exact, n, large)


def _near_buckets():
    assert ATT_TILE >= REL_MAX_DIST and ATT_TILE % CHUNK == 0
    qpos = jnp.arange(ATT_TILE, dtype=I32)[:, None]
    kpos = jnp.arange(ATT_TILE, dtype=I32)[None, :]
    prev = _t5_bucket(kpos - ATT_TILE - qpos)
    diag = _t5_bucket(kpos - qpos)
    diag = jnp.where(kpos // CHUNK <= qpos // CHUNK, diag, -1)
    return jnp.stack([prev, diag]).astype(I32)


def _attn_kernel(tab_ref, q_ref, k_ref, v_ref, bkt_ref, bd_ref, qg_ref, kg_ref, lqk_ref, sg_ref, o_ref,
                 qz_ref, kn_ref, v1_ref, bias_ref, *, seq, lambda_init):
    h = pl.program_id(0)
    t = ATT_TILE
    n_tiles = seq // t
    lane = lax.broadcasted_iota(I32, (1, VALUE_DIM), 1)
    first = lane < HEAD_DIM

    @pl.when(pl.program_id(1) == 0)
    def _():
        far = tab_ref[FAR_BUCKET, h]
        for d in range(2):
            bkt = bkt_ref[d]
            tile = jnp.full((t, t), NEG_INF, F32)
            for b in range(REL_BUCKETS):
                tile = jnp.where(bkt == b, (tab_ref[b, h] - far) * LOG2E, tile)
            bias_ref[d] = tile

    def half_norm(x, g):
        x2 = x * x
        hi = x2.astype(BF16)
        lo = (x2 - hi.astype(F32)).astype(BF16)
        ms = (_dot(hi, bd_ref[...]) + _dot(lo, bd_ref[...])) * (1.0 / HEAD_DIM)
        return x * lax.rsqrt(ms + EPS) * g

    q_scale = HEAD_DIM ** -0.5 * LOG2E

    n_seq = q_ref.shape[0]

    def prep(i, _):
        r0 = pl.multiple_of(i * t, t)
        ones_col = jnp.broadcast_to(jnp.where(lane == 0, 1.0, 0.0).astype(BF16), (t, VALUE_DIM))
        for b in range(n_seq):
            qn = half_norm(q_ref[b, pl.ds(r0, t), :], qg_ref[...]) * q_scale
            qz_ref[b, 0, pl.ds(r0, t), :] = jnp.where(first, qn, 0.0).astype(BF16)
            qz_ref[b, 1, pl.ds(r0, t), :] = jnp.where(first, 0.0, qn).astype(BF16)
            kn_ref[b, pl.ds(r0, t), :] = half_norm(k_ref[b, pl.ds(r0, t), :], kg_ref[...]).astype(BF16)
            v1_ref[b, pl.ds(r0, t), :] = jnp.concatenate([v_ref[b, pl.ds(r0, t), :], ones_col], axis=1)
        return 0

    lax.fori_loop(0, n_tiles, prep, 0)

    lqk = lqk_ref[...]
    lam = (jnp.exp(jnp.sum(lqk[0:1] * lqk[1:2], axis=-1, keepdims=True))
           - jnp.exp(jnp.sum(lqk[2:3] * lqk[3:4], axis=-1, keepdims=True)) + lambda_init)

    for i in range(n_tiles):
        q0, kend = i * t, (i + 1) * t
        for b in range(n_seq):
            keys = kn_ref[b, 0:kend, :]
            vals = v1_ref[b, 0:kend, :]
            maps = []
            for m in range(2):
                s = _dot_nt(qz_ref[b, m, q0:q0 + t, :], keys)
                parts = [s[:, kend - t:] + bias_ref[1]]
                if i >= 1:
                    parts.insert(0, s[:, kend - 2 * t:kend - t] + bias_ref[0])
                if i >= 2:
                    parts.insert(0, s[:, :kend - 2 * t])
                s = jnp.concatenate(parts, axis=1) if len(parts) > 1 else parts[0]
                p = jnp.exp2(s - jnp.max(s, axis=-1, keepdims=True))
                pv = _dot(p.astype(BF16), vals)
                maps.append(pv[:, :VALUE_DIM] / pv[:, VALUE_DIM:VALUE_DIM + 1])
            o = maps[0] - lam * maps[1]
            ms = jnp.mean(o * o, axis=-1, keepdims=True)
            o = o * lax.rsqrt(ms + EPS) * sg_ref[...] * (1.0 - lambda_init)
            o_ref[b, q0:q0 + t, :] = o.astype(o_ref.dtype)


def _attention(q3, k3, v3, rel_bias, q_g, k_g, lam_qk, subln_g, lambda_init):
    b, s, _ = q3.shape
    t = ATT_TILE
    nb = ATT_SEQS if b % ATT_SEQS == 0 else 1
    head = lambda j, i: (i, 0, j)
    fixed2 = lambda j, i: (0, 0)
    fixed3 = lambda j, i: (0, 0, 0)
    half = np.arange(VALUE_DIM) // HEAD_DIM
    blockdiag = jnp.asarray(half[:, None] == half[None, :], dtype=BF16)
    return pl.pallas_call(
        functools.partial(_attn_kernel, seq=s, lambda_init=lambda_init),
        grid=(N_HEADS, b // nb),
        in_specs=[pl.BlockSpec(memory_space=pltpu.SMEM),
                  pl.BlockSpec((nb, s, VALUE_DIM), head),
                  pl.BlockSpec((nb, s, VALUE_DIM), head),
                  pl.BlockSpec((nb, s, VALUE_DIM), head),
                  pl.BlockSpec((2, t, t), fixed3),
                  pl.BlockSpec((VALUE_DIM, VALUE_DIM), fixed2),
                  pl.BlockSpec((1, VALUE_DIM), fixed2),
                  pl.BlockSpec((1, VALUE_DIM), fixed2),
                  pl.BlockSpec((4, HEAD_DIM), fixed2),
                  pl.BlockSpec((1, VALUE_DIM), fixed2)],
        out_specs=pl.BlockSpec((nb, s, VALUE_DIM), head),
        out_shape=jax.ShapeDtypeStruct((b, s, N_HEADS * VALUE_DIM), BF16),
        scratch_shapes=[pltpu.VMEM((nb, 2, s, VALUE_DIM), BF16),
                        pltpu.VMEM((nb, s, VALUE_DIM), BF16),
                        pltpu.VMEM((nb, s, 2 * VALUE_DIM), BF16),
                        pltpu.VMEM((2, t, t), F32)],
        compiler_params=_cparams(("arbitrary", "arbitrary")),
        name="diff_attn",
    )(rel_bias, q3, k3, v3, _near_buckets(), blockdiag, q_g, k_g, lam_qk, subln_g)


def _out_proj_kernel(x_ref, c_ref, a_ref, wc_ref, wa_ref, g_ref, rwh_ref, rwl_ref, rb_ref, tri_ref,
                     h_ref, hn_ref, idx_ref, gate_ref, rank_ref, cnt_ref, carry_ref):
    @pl.when(pl.program_id(0) == 0)
    def _():
        carry_ref[...] = jnp.zeros_like(carry_ref)

    h = x_ref[...] + _dot(c_ref[...], wc_ref[...]) + _dot(a_ref[...], wa_ref[...])
    h_ref[...] = h
    ms = jnp.mean(h * h, axis=-1, keepdims=True)
    hn = h * lax.rsqrt(ms + EPS) * g_ref[...]
    _store_packed_rows(hn_ref, (), hn)

    hi = hn.astype(BF16)
    lo = (hn - hi.astype(F32)).astype(BF16)
    logits = (_dot(hi, rwh_ref[...]) + (_dot(hi, rwl_ref[...]) + _dot(lo, rwh_ref[...]))) + rb_ref[...]

    tm = logits.shape[0]
    lane = lax.broadcasted_iota(I32, (tm, LANES), 1).astype(F32)
    work = logits
    vals, idxs = [], []
    for _ in range(TOP_K):
        mx = jnp.max(work, axis=-1, keepdims=True)
        ix = jnp.min(jnp.where(work == mx, lane, float(LANES)), axis=-1, keepdims=True)
        vals.append(mx)
        idxs.append(ix)
        work = jnp.where(lane == ix, NEG_INF, work)
    exps = [jnp.exp(v - vals[0]) for v in vals]
    denom = exps[0]
    for e in exps[1:]:
        denom = denom + e

    sel = jnp.zeros((tm, LANES), F32)
    for ix in idxs:
        sel = sel + jnp.where(lane == ix, 1.0, 0.0)
    rank = _dot(tri_ref[...], sel.astype(BF16)) + carry_ref[...]
    carry_ref[...] = carry_ref[...] + jnp.sum(sel, axis=0, keepdims=True)
    cnt_ref[...] = carry_ref[...]

    idx_out = jnp.zeros((tm, LANES), F32)
    gate_out = jnp.zeros((tm, LANES), F32)
    rank_out = jnp.zeros((tm, LANES), F32)
    for k in range(TOP_K):
        rk = jnp.sum(jnp.where(lane == idxs[k], rank, 0.0), axis=-1, keepdims=True)
        idx_out = jnp.where(lane == k, idxs[k], idx_out)
        gate_out = jnp.where(lane == k, exps[k] / denom, gate_out)
        rank_out = jnp.where(lane == k, rk, rank_out)
    idx_ref[...] = jnp.transpose(idx_out)[:SUBLANES].astype(I32)
    rank_ref[...] = jnp.transpose(rank_out)[:SUBLANES].astype(I32)
    gate_ref[...] = gate_out


def _out_proj(x2, conv_o, attn_o, wc, wa, g, rwh, rwl, rb, tm):
    n, d = x2.shape
    dc, da = conv_o.shape[1], attn_o.shape[1]
    row = lambda i: (i, 0)
    fixed = lambda i: (0, 0)
    tri = jnp.tril(jnp.ones((tm, tm), F32), -1).astype(BF16)
    return pl.pallas_call(
        _out_proj_kernel,
        grid=(n // tm,),
        in_specs=[pl.BlockSpec((tm, d), row),
                  pl.BlockSpec((tm, dc), row),
                  pl.BlockSpec((tm, da), row),
                  pl.BlockSpec((dc, d), fixed),
                  pl.BlockSpec((da, d), fixed),
                  pl.BlockSpec((1, d), fixed),
                  pl.BlockSpec((d, LANES), fixed),
                  pl.BlockSpec((d, LANES), fixed),
                  pl.BlockSpec((1, LANES), fixed),
                  pl.BlockSpec((tm, tm), fixed)],
        out_specs=[pl.BlockSpec((tm, d), row),
                   pl.BlockSpec((tm * d // (2 * LANES), LANES), row),
                   pl.BlockSpec((SUBLANES, tm), lambda i: (0, i)),
                   pl.BlockSpec((tm, LANES), row),
                   pl.BlockSpec((SUBLANES, tm), lambda i: (0, i)),
                   pl.BlockSpec((1, LANES), fixed)],
        out_shape=[jax.ShapeDtypeStruct((n, d), F32),
                   jax.ShapeDtypeStruct((n * d // (2 * LANES), LANES), U32),
                   jax.ShapeDtypeStruct((SUBLANES, n), I32),
                   jax.ShapeDtypeStruct((n, LANES), F32),
                   jax.ShapeDtypeStruct((SUBLANES, n), I32),
                   jax.ShapeDtypeStruct((1, LANES), F32)],
        scratch_shapes=[pltpu.VMEM((1, LANES), F32)],
        compiler_params=_cparams(("arbitrary",)),
        name="out_proj_router",
    )(x2, conv_o, attn_o, wc, wa, g, rwh, rwl, rb, tri)


MOE_BLOCK = 512


def _moe_kernel(be_ref, first_ref, wslot_ref, nxt_ref, nused_ref,
                x_ref, wgu_hbm, wd_hbm, bgu_ref, bd_ref, perm_ref,
                y_ref, wgu_f32, wd_f32, wsem, wgu_bf, wd_bf, x_bf, act_bf):
    i = pl.program_id(0)
    bm = MOE_BLOCK
    n_used = nused_ref[0]
    d = wd_bf.shape[1]
    d_ff = wd_bf.shape[0]
    n_groups = d_ff // LANES

    def weight_copies(e, slot):
        return (pltpu.make_async_copy(wgu_hbm.at[e], wgu_f32.at[slot], wsem.at[0, slot]),
                pltpu.make_async_copy(wd_hbm.at[e], wd_f32.at[slot], wsem.at[1, slot]))

    @pl.when(i == 0)
    def _():
        for c in weight_copies(be_ref[0], 0):
            c.start()

    active = i < n_used

    @pl.when(jnp.logical_and(active, first_ref[i] == 1))
    def _():
        ws = wslot_ref[i]
        for c in weight_copies(be_ref[i], ws):
            c.wait()
        for g in range(n_groups):
            cols = pl.ds(g * MXU_DIM, MXU_DIM)
            wgu_bf[:, cols] = _dot(wgu_f32[ws, :, cols].astype(BF16), perm_ref[...]).astype(BF16)
        wd_bf[...] = wd_f32[ws].astype(BF16)

        @pl.when(nxt_ref[i] >= 0)
        def _():
            for c in weight_copies(nxt_ref[i], 1 - ws):
                c.start()

    def activate(g, gu):
        gu = gu + bgu_ref[0, :, g * MXU_DIM:(g + 1) * MXU_DIM]
        gate = jnp.minimum(gu[:, :LANES], SWIGLU_LIMIT)
        lin = jnp.clip(gu[:, LANES:], -SWIGLU_LIMIT, SWIGLU_LIMIT)
        act_bf[:, g * LANES:(g + 1) * LANES] = (
            gate * jax.nn.sigmoid(SWIGLU_ALPHA * gate) * (lin + 1.0)).astype(BF16)

    @pl.when(active)
    def _():
        k = d // (2 * LANES)
        for j in range(k):
            lo, hi = _load_packed_tile(x_ref, (), bm, k, j)
            x_bf[:, j * LANES:(j + 1) * LANES] = lo.astype(BF16)
            x_bf[:, (k + j) * LANES:(k + j + 1) * LANES] = hi.astype(BF16)
        for g in range(n_groups):
            activate(g, _dot(x_bf[...], wgu_bf[:, g * MXU_DIM:(g + 1) * MXU_DIM]))
        _store_packed_rows(y_ref, (), _dot(act_bf[...], wd_bf[...]) + bd_ref[0])

    @pl.when(i >= n_used)
    def _():
        y_ref[...] = jnp.zeros_like(y_ref)


def _deinterleave_perm():
    src = np.arange(MXU_DIM)
    dst = np.where(src % 2 == 0, src // 2, LANES + src // 2)
    p = np.zeros((MXU_DIM, MXU_DIM), np.float32)
    p[src, dst] = 1.0
    return jnp.asarray(p, dtype=BF16)


def _moe(plan, xs, w_gate_up, bgu_p, w_down, b_down):
    n_exp, d, d_gu = w_gate_up.shape
    d_ff = w_down.shape[1]
    bm = MOE_BLOCK
    nblk = plan[0].shape[0]
    slab = bm * d // (2 * LANES)
    by_expert = lambda i, be, *_: (be[i], 0, 0)
    grid_spec = pltpu.PrefetchScalarGridSpec(
        num_scalar_prefetch=5,
        grid=(nblk,),
        in_specs=[
            pl.BlockSpec((slab, LANES), lambda i, *_: (i, 0)),
            pl.BlockSpec(memory_space=pl.ANY),
            pl.BlockSpec(memory_space=pl.ANY),
            pl.BlockSpec((1, 1, d_gu), by_expert),
            pl.BlockSpec((1, 1, d), by_expert),
            pl.BlockSpec((MXU_DIM, MXU_DIM), lambda i, *_: (0, 0)),
        ],
        out_specs=pl.BlockSpec((slab, LANES), lambda i, *_: (i, 0)),
        scratch_shapes=[pltpu.VMEM((2, d, d_gu), F32),
                        pltpu.VMEM((2, d_ff, d), F32),
                        pltpu.SemaphoreType.DMA((2, 2)),
                        pltpu.VMEM((d, d_gu), BF16),
                        pltpu.VMEM((d_ff, d), BF16),
                        pltpu.VMEM((bm, d), BF16),
                        pltpu.VMEM((bm, d_ff), BF16)],
    )
    return pl.pallas_call(
        _moe_kernel,
        grid_spec=grid_spec,
        out_shape=jax.ShapeDtypeStruct((nblk * slab, LANES), U32),
        compiler_params=_cparams(("arbitrary",)),
        name="moe_experts",
    )(*plan, xs, w_gate_up, w_down, bgu_p, b_down, _deinterleave_perm())


SC_CORES = 2
SC_SUBCORES = 16
SC_CHUNK = 64


def _sc_mesh():
    return plsc.VectorSubcoreMesh(core_axis_name="c", subcore_axis_name="s")


def _sc_dispatch(rows3, pos_chunks, n_slots):
    n, r, _ = rows3.shape
    workers = SC_CORES * SC_SUBCORES
    per_worker = n // SC_CHUNK // workers

    def body(rows_hbm, pos_hbm, out_hbm, idx_v, rows_v, sem):
        wid = lax.axis_index("s") * SC_CORES + lax.axis_index("c")

        @pl.loop(0, per_worker)
        def _(c):
            chunk = wid * per_worker + c
            pltpu.sync_copy(pos_hbm.at[chunk], idx_v)
            pltpu.sync_copy(rows_hbm.at[pl.ds(chunk * SC_CHUNK, SC_CHUNK)], rows_v)
            copies = [pltpu.async_copy(rows_v, out_hbm.at[idx_v.at[k]], sem) for k in range(TOP_K)]
            for cp in copies:
                cp.wait()

    return pl.kernel(
        body,
        out_type=jax.ShapeDtypeStruct((n_slots, r, LANES), rows3.dtype),
        mesh=_sc_mesh(),
        scratch_types=[pltpu.VMEM((TOP_K, SC_CHUNK), I32),
                       pltpu.VMEM((SC_CHUNK, r, LANES), rows3.dtype),
                       pltpu.SemaphoreType.DMA],
        name="sc_dispatch",
    )(rows3, pos_chunks)


def _sc_gather(rows3, pos_chunks, n):
    _, r, _ = rows3.shape
    workers = SC_CORES * SC_SUBCORES
    per_worker = n // SC_CHUNK // workers

    def body(rows_hbm, pos_hbm, out_hbm, idx_v, buf_a, buf_b, sem_a, sem_b):
        wid = lax.axis_index("s") * SC_CORES + lax.axis_index("c")
        bufs, sems = (buf_a, buf_b), (sem_a, sem_b)

        @pl.loop(0, per_worker)
        def _(c):
            chunk = wid * per_worker + c
            pltpu.sync_copy(pos_hbm.at[chunk], idx_v)
            gathers = [None] * TOP_K
            gathers[0] = pltpu.async_copy(rows_hbm.at[idx_v.at[0]], bufs[0], sems[0])
            for k in range(TOP_K):
                if k + 1 < TOP_K:
                    nxt = (k + 1) % 2
                    gathers[k + 1] = pltpu.async_copy(rows_hbm.at[idx_v.at[k + 1]], bufs[nxt], sems[nxt])
                gathers[k].wait()
                pltpu.sync_copy(bufs[k % 2], out_hbm.at[k, pl.ds(chunk * SC_CHUNK, SC_CHUNK)])

    return pl.kernel(
        body,
        out_type=jax.ShapeDtypeStruct((TOP_K, n, r, LANES), rows3.dtype),
        mesh=_sc_mesh(),
        scratch_types=[pltpu.VMEM((TOP_K, SC_CHUNK), I32),
                       pltpu.VMEM((SC_CHUNK, r, LANES), rows3.dtype),
                       pltpu.VMEM((SC_CHUNK, r, LANES), rows3.dtype),
                       pltpu.SemaphoreType.DMA,
                       pltpu.SemaphoreType.DMA],
        name="sc_gather",
    )(rows3, pos_chunks)


COMBINE_TILE = 512


def _combine_kernel(h_ref, gate_ref, y_ref, o_ref):
    tc, d = h_ref.shape
    kt = d // (2 * LANES)
    gates = gate_ref[...]
    for j in range(kt):
        lo_acc = h_ref[:, j * LANES:(j + 1) * LANES]
        hi_acc = h_ref[:, (kt + j) * LANES:(kt + j + 1) * LANES]
        for k in range(TOP_K):
            lo, hi = _load_packed_tile(y_ref, (k,), tc, kt, j)
            g = gates[:, k:k + 1]
            lo_acc = lo_acc + g * lo
            hi_acc = hi_acc + g * hi
        o_ref[:, j * LANES:(j + 1) * LANES] = lo_acc
        o_ref[:, (kt + j) * LANES:(kt + j + 1) * LANES] = hi_acc


def _combine(h, gates, y4):
    n, d = h.shape
    tc = COMBINE_TILE
    slab = tc * d // (2 * LANES)
    row = lambda i: (i, 0)
    return pl.pallas_call(
        _combine_kernel,
        grid=(n // tc,),
        in_specs=[pl.BlockSpec((tc, d), row),
                  pl.BlockSpec((tc, LANES), row),
                  pl.BlockSpec((TOP_K, slab, LANES), lambda i: (0, i, 0))],
        out_specs=pl.BlockSpec((tc, d), row),
        out_shape=jax.ShapeDtypeStruct((n, d), F32),
        input_output_aliases={0: 0},
        compiler_params=_cparams(("parallel",)),
        name="moe_combine",
    )(h, gates, y4)


def _layer(h3, layer, norm_mix_g, w_in, conv_w, conv_b, conv_ln_g, conv_ln_b, q_norm_g, k_norm_g,
           lambda_qk, subln_g, rel_bias, w_out, norm_ffn_g, router_w, router_b,
           w_gate_up, b_gate_up, w_down, b_down):
    b, s, d = h3.shape
    n = b * s
    d_conv = conv_w.shape[-1]
    d_attn = N_HEADS * VALUE_DIM
    lambda_init = 0.8 - 0.6 * math.exp(-0.3 * layer)
    x2 = h3.reshape(n, d)

    u, q, k, v = _in_proj(x2, norm_mix_g.reshape(1, d), w_in.astype(BF16), d_conv, d_attn, tm=512)
    conv_o = _conv(u.reshape(b, s, d_conv), conv_w, conv_b.reshape(1, d_conv),
                   conv_ln_g.reshape(1, d_conv), conv_ln_b.reshape(1, d_conv))
    attn_o = _attention(q.reshape(b, s, d_attn), k.reshape(b, s, d_attn), v.reshape(b, s, d_attn),
                        rel_bias, q_norm_g.reshape(1, VALUE_DIM), k_norm_g.reshape(1, VALUE_DIM),
                        lambda_qk, subln_g.reshape(1, VALUE_DIM), lambda_init)

    n_exp = router_w.shape[1]
    rw = jnp.zeros((d, LANES), F32).at[:, :n_exp].set(router_w)
    rwh = rw.astype(BF16)
    rwl = (rw - rwh.astype(F32)).astype(BF16)
    rb = jnp.full((1, LANES), NEG_INF, F32).at[0, :n_exp].set(router_b)
    w_out_bf = w_out.astype(BF16)
    hres, hn, idx, gates, rank, cnt = _out_proj(
        x2, conv_o.reshape(n, d_conv), attn_o.reshape(n, d_attn),
        w_out_bf[:d_conv], w_out_bf[d_conv:], norm_ffn_g.reshape(1, d), rwh, rwl, rb, tm=512)

    bm = MOE_BLOCK
    nblk = n * TOP_K // bm + n_exp
    counts = cnt[0, :n_exp].astype(I32)
    padded = (counts + bm - 1) // bm * bm
    eid = jnp.arange(n_exp, dtype=I32)
    pad_end = jnp.sum(jnp.where(eid[None, :] <= eid[:, None], padded[None, :], 0), axis=1)
    pad_start = pad_end - padded
    start_of = jnp.sum(jnp.where(idx[None, :TOP_K] == eid[:, None, None],
                                 pad_start[:, None, None], 0), axis=0)
    pos = start_of + rank[:TOP_K]
    pos_chunks = pos.reshape(TOP_K, n // SC_CHUNK, SC_CHUNK).transpose(1, 0, 2)
    block_start = jnp.arange(nblk, dtype=I32) * bm
    block_expert = jnp.minimum(jnp.sum(block_start[:, None] >= pad_end[None, :], axis=1),
                               n_exp - 1).astype(I32)
    n_used = (pad_end[-1:] // bm).astype(I32)
    blk = jnp.arange(nblk, dtype=I32)
    first = jnp.logical_and(
        jnp.concatenate([jnp.ones((1,), bool), block_expert[1:] != block_expert[:-1]]),
        blk < n_used[0]).astype(I32)
    opened = jnp.sum(jnp.where(blk[None, :] <= blk[:, None], first[None, :], 0), axis=1)
    wslot = ((opened - 1) % 2).astype(I32)
    later_used = jnp.logical_and(eid[None, :] > eid[:, None], counts[None, :] > 0)
    next_used = jnp.min(jnp.where(later_used, eid[None, :], n_exp), axis=1)
    next_used = jnp.where(next_used == n_exp, -1, next_used).astype(I32)
    next_of_block = jnp.sum(jnp.where(block_expert[:, None] == eid[None, :], next_used[None, :], 0),
                            axis=1).astype(I32)
    plan = (block_expert, first, wslot, next_of_block, n_used)

    d_ff = w_down.shape[1]
    bgu_p = b_gate_up.reshape(n_exp, d_ff // LANES, LANES, 2).transpose(0, 1, 3, 2).reshape(n_exp, 1, 2 * d_ff)
    r = d // (2 * LANES)
    xs = _sc_dispatch(hn.reshape(n, r, LANES), pos_chunks, nblk * bm)
    yb = _moe(plan, xs.reshape(nblk * bm * r, LANES), w_gate_up, bgu_p, w_down,
              b_down.reshape(n_exp, 1, d))
    y4 = _sc_gather(yb.reshape(nblk * bm, r, LANES), pos_chunks, n)
    out = _combine(hres, gates, y4.reshape(TOP_K, n * r, LANES))
    return out.reshape(b, s, d)


def kernel(x, norm_mix_g, w_in, conv_w, conv_b, conv_ln_g, conv_ln_b, q_norm_g, k_norm_g, lambda_qk,
           subln_g, rel_bias, w_out, norm_ffn_g, router_w, router_b, w_gate_up, b_gate_up, w_down,
           b_down):
    h = x
    for layer in range(norm_mix_g.shape[0]):
        h = _layer(h, layer, norm_mix_g[layer], w_in[layer], conv_w[layer], conv_b[layer],
                   conv_ln_g[layer], conv_ln_b[layer], q_norm_g[layer], k_norm_g[layer],
                   lambda_qk[layer], subln_g[layer], rel_bias, w_out[layer], norm_ffn_g[layer],
                   router_w[layer], router_b[layer], w_gate_up[layer], b_gate_up[layer],
                   w_down[layer], b_down[layer])
    return h
```

```python
import functools
import math

import jax
import jax.numpy as jnp
import numpy as np
from jax import lax
from jax.experimental import pallas as pl
from jax.experimental.pallas import tpu as pltpu
from jax.experimental.pallas import tpu_sc as plsc

F32 = jnp.float32
BF16 = jnp.bfloat16
I32 = jnp.int32
U32 = jnp.uint32

CHUNK = 64
CONV_WIDTH = 31
N_HEADS = 4
HEAD_DIM = 64
VALUE_DIM = 2 * HEAD_DIM
REL_BUCKETS = 32
REL_MAX_DIST = 128
N_EXPERTS = 32
TOP_K = 4
SWIGLU_LIMIT = 7.0
SWIGLU_ALPHA = 1.702
EPS = 1e-5
LOG2E = 1.4426950408889634

LANES = 128
SUBLANES = 8
MXU_DIM = 256
VMEM_LIMIT = 56 * 1024 * 1024

NEG_INF = float("-inf")


def _cparams(sem, vmem=VMEM_LIMIT, flags=None):
    return pltpu.CompilerParams(dimension_semantics=sem, vmem_limit_bytes=vmem, flags=flags)


def _dot(a, b):
    return jnp.dot(a, b, preferred_element_type=F32)


def _dot_nt(a, b):
    return lax.dot_general(a, b, (((1,), (1,)), ((), ())), preferred_element_type=F32)


def _pack_pair(lo, hi):
    lo_bits = lax.bitcast_convert_type(lo.astype(BF16).astype(F32), U32)
    hi_bits = lax.bitcast_convert_type(hi.astype(BF16).astype(F32), U32)
    return (lo_bits >> 16) | hi_bits


def _unpack_pair(w):
    return (lax.bitcast_convert_type(w << 16, F32),
            lax.bitcast_convert_type(w & jnp.uint32(0xFFFF0000), F32))


def _store_packed_rows(ref, lead, x):
    rows, k = x.shape[0], x.shape[1] // (2 * LANES)
    for j in range(k):
        w = _pack_pair(x[:, j * LANES:(j + 1) * LANES], x[:, (k + j) * LANES:(k + j + 1) * LANES])
        ref[(*lead, pl.ds(j, rows, stride=k), slice(None))] = w


def _load_packed_tile(ref, lead, rows, k, j):
    return _unpack_pair(ref[(*lead, pl.ds(j, rows, stride=k), slice(None))])


def _in_proj_kernel(x_ref, g_ref, w_ref, u_ref, q_ref, k_ref, v_ref, *, d_conv, d_attn):
    x = x_ref[...]
    ms = jnp.mean(x * x, axis=-1, keepdims=True)
    y = (x * lax.rsqrt(ms + EPS) * g_ref[...]).astype(BF16)
    proj = _dot(y, w_ref[...])
    a = proj[:, :d_conv]
    g = proj[:, d_conv:2 * d_conv]
    u_ref[...] = a * jax.nn.sigmoid(g)
    o = 2 * d_conv
    q_ref[...] = proj[:, o:o + d_attn]
    k_ref[...] = proj[:, o + d_attn:o + 2 * d_attn]
    v_ref[...] = proj[:, o + 2 * d_attn:o + 3 * d_attn].astype(BF16)


def _in_proj(x2, g, w_bf, d_conv, d_attn, tm):
    n, d = x2.shape
    d_in = w_bf.shape[1]
    row = lambda i: (i, 0)
    fixed = lambda i: (0, 0)
    return pl.pallas_call(
        functools.partial(_in_proj_kernel, d_conv=d_conv, d_attn=d_attn),
        grid=(n // tm,),
        in_specs=[pl.BlockSpec((tm, d), row),
                  pl.BlockSpec((1, d), fixed),
                  pl.BlockSpec((d, d_in), fixed)],
        out_specs=[pl.BlockSpec((tm, d_conv), row),
                   pl.BlockSpec((tm, d_attn), row),
                   pl.BlockSpec((tm, d_attn), row),
                   pl.BlockSpec((tm, d_attn), row)],
        out_shape=[jax.ShapeDtypeStruct((n, d_conv), F32),
                   jax.ShapeDtypeStruct((n, d_attn), F32),
                   jax.ShapeDtypeStruct((n, d_attn), F32),
                   jax.ShapeDtypeStruct((n, d_attn), BF16)],
        compiler_params=_cparams(("parallel",)),
        name="in_proj",
    )(x2, g, w_bf)


CONV_PAD = 32
CONV_SEQ_TILE = 512
CONV_TILE = 128
CONV_NORM_TILE = 256


def _conv_kernel(u_ref, prev_ref, w_ref, cb_ref, lg_ref, lb_ref, o_ref, sh_ref, y_ref):
    ts, c = u_ref.shape[1], u_ref.shape[2]
    plen = ts + CONV_PAD
    hist = prev_ref[0]
    hist = jnp.where(pl.program_id(1) > 0, hist, jnp.zeros_like(hist))
    sh_ref[0, pl.ds(0, CONV_PAD), :] = hist
    sh_ref[0, pl.ds(CONV_PAD, ts), :] = u_ref[0]
    sh_ref[0, pl.ds(plen, SUBLANES), :] = jnp.zeros((SUBLANES, c), F32)

    bt = CONV_PAD

    def shift(i, _):
        p0 = pl.multiple_of(i * bt, bt)
        win = sh_ref[0, pl.ds(p0, bt + SUBLANES), :]
        for r in range(1, SUBLANES):
            sh_ref[r, pl.ds(p0, bt), :] = win[r:r + bt]
        return 0

    lax.fori_loop(0, plen // bt, shift, 0)

    off0 = CONV_PAD - (CONV_WIDTH - 1)

    by_shift = {}
    for j in range(CONV_WIDTH):
        a, r = divmod(off0 + j, SUBLANES)
        by_shift.setdefault(r, []).append((a, j))

    for lt in range(c // LANES):
        lanes = slice(lt * LANES, (lt + 1) * LANES)
        taps = [w_ref[pl.ds(j, 1), lanes] for j in range(CONV_WIDTH)]

        def tap_body(i, _, lanes=lanes, taps=taps):
            t0 = pl.multiple_of(i * CONV_TILE, CONV_TILE)
            acc = jnp.zeros((CONV_TILE, LANES), F32)
            for r, group in by_shift.items():
                a_lo = min(a for a, _ in group)
                a_hi = max(a for a, _ in group)
                rows = CONV_TILE + (a_hi - a_lo) * SUBLANES
                start = pl.multiple_of(t0 + a_lo * SUBLANES, SUBLANES)
                win = sh_ref[r, pl.ds(start, rows), lanes]
                for a, j in group:
                    lo = (a - a_lo) * SUBLANES
                    acc = acc + win[lo:lo + CONV_TILE] * taps[j]
            y_ref[pl.ds(t0, CONV_TILE), lanes] = acc
            return 0

        lax.fori_loop(0, ts // CONV_TILE, tap_body, 0)

    def norm_body(i, _):
        t0 = pl.multiple_of(i * CONV_NORM_TILE, CONV_NORM_TILE)
        y = y_ref[pl.ds(t0, CONV_NORM_TILE), :] + cb_ref[...]
        mu = jnp.mean(y, axis=-1, keepdims=True)
        yc = y - mu
        var = jnp.mean(yc * yc, axis=-1, keepdims=True)
        z = yc * lax.rsqrt(var + EPS) * lg_ref[...] + lb_ref[...]
        o_ref[0, pl.ds(t0, CONV_NORM_TILE), :] = (z * jax.nn.sigmoid(z)).astype(o_ref.dtype)
        return 0

    lax.fori_loop(0, ts // CONV_NORM_TILE, norm_body, 0)


def _conv(u3, conv_w, conv_b, ln_g, ln_b):
    b, s, c = u3.shape
    ts = min(CONV_SEQ_TILE, s)
    hist_per_tile = ts // CONV_PAD
    fixed = lambda i, j: (0, 0)
    return pl.pallas_call(
        _conv_kernel,
        grid=(b, s // ts),
        in_specs=[pl.BlockSpec((1, ts, c), lambda i, j: (i, j, 0)),
                  pl.BlockSpec((1, CONV_PAD, c),
                               lambda i, j: (i, jnp.maximum(j * hist_per_tile - 1, 0), 0)),
                  pl.BlockSpec((CONV_WIDTH, c), fixed),
                  pl.BlockSpec((1, c), fixed),
                  pl.BlockSpec((1, c), fixed),
                  pl.BlockSpec((1, c), fixed)],
        out_specs=pl.BlockSpec((1, ts, c), lambda i, j: (i, j, 0)),
        out_shape=jax.ShapeDtypeStruct((b, s, c), BF16),
        scratch_shapes=[pltpu.VMEM((SUBLANES, ts + CONV_PAD + SUBLANES, c), F32),
                        pltpu.VMEM((ts, c), F32)],
        compiler_params=_cparams(("parallel", "parallel")),
        name="conv_mixer",
    )(u3, u3, conv_w, conv_b, ln_g, ln_b)


ATT_TILE = 256
ATT_SEQS = 2
FAR_BUCKET = REL_BUCKETS // 2 - 1


def _t5_bucket(rel):
    nb = REL_BUCKETS // 2
    max_exact = nb // 2
    ret = jnp.where(rel > 0, nb, 0)
    n = jnp.abs(rel)
    nf = jnp.maximum(n, 1).astype(jnp.float32)
    large = max_exact + (jnp.log(nf / max_exact) / math.log(REL_MAX_DIST / max_exact)
                         * (nb - max_exact)).astype(jnp.int32)
    large = jnp.minimum(large, nb - 1)
    return ret + jnp.where(n < max_exact, n, large)


def _near_buckets():
    assert ATT_TILE >= REL_MAX_DIST and ATT_TILE % CHUNK == 0
    qpos = jnp.arange(ATT_TILE, dtype=I32)[:, None]
    kpos = jnp.arange(ATT_TILE, dtype=I32)[None, :]
    prev = _t5_bucket(kpos - ATT_TILE - qpos)
    diag = _t5_bucket(kpos - qpos)
    diag = jnp.where(kpos // CHUNK <= qpos // CHUNK, diag, -1)
    return jnp.stack([prev, diag]).astype(I32)


def _attn_kernel(tab_ref, q_ref, k_ref, v_ref, bkt_ref, bd_ref, qg_ref, kg_ref, lqk_ref, sg_ref, o_ref,
                 qz_ref, kn_ref, v1_ref, bias_ref, *, seq, lambda_init):
    h = pl.program_id(0)
    t = ATT_TILE
    n_tiles = seq // t
    lane = lax.broadcasted_iota(I32, (1, VALUE_DIM), 1)
    first = lane < HEAD_DIM

    @pl.when(pl.program_id(1) == 0)
    def _():
        far = tab_ref[FAR_BUCKET, h]
        for d in range(2):
            bkt = bkt_ref[d]
            tile = jnp.full((t, t), NEG_INF, F32)
            for b in range(REL_BUCKETS):
                tile = jnp.where(bkt == b, (tab_ref[b, h] - far) * LOG2E, tile)
            bias_ref[d] = tile

    def half_norm(x, g):
        x2 = x * x
        hi = x2.astype(BF16)
        lo = (x2 - hi.astype(F32)).astype(BF16)
        ms = (_dot(hi, bd_ref[...]) + _dot(lo, bd_ref[...])) * (1.0 / HEAD_DIM)
        return x * lax.rsqrt(ms + EPS) * g

    q_scale = HEAD_DIM ** -0.5 * LOG2E

    n_seq = q_ref.shape[0]

    def prep(i, _):
        r0 = pl.multiple_of(i * t, t)
        ones_col = jnp.broadcast_to(jnp.where(lane == 0, 1.0, 0.0).astype(BF16), (t, VALUE_DIM))
        for b in range(n_seq):
            qn = half_norm(q_ref[b, pl.ds(r0, t), :], qg_ref[...]) * q_scale
            qz_ref[b, 0, pl.ds(r0, t), :] = jnp.where(first, qn, 0.0).astype(BF16)
            qz_ref[b, 1, pl.ds(r0, t), :] = jnp.where(first, 0.0, qn).astype(BF16)
            kn_ref[b, pl.ds(r0, t), :] = half_norm(k_ref[b, pl.ds(r0, t), :], kg_ref[...]).astype(BF16)
            v1_ref[b, pl.ds(r0, t), :] = jnp.concatenate([v_ref[b, pl.ds(r0, t), :], ones_col], axis=1)
        return 0

    lax.fori_loop(0, n_tiles, prep, 0)

    lqk = lqk_ref[...]
    lam = (jnp.exp(jnp.sum(lqk[0:1] * lqk[1:2], axis=-1, keepdims=True))
           - jnp.exp(jnp.sum(lqk[2:3] * lqk[3:4], axis=-1, keepdims=True)) + lambda_init)

    for i in range(n_tiles):
        q0, kend = i * t, (i + 1) * t
        for b in range(n_seq):
            keys = kn_ref[b, 0:kend, :]
            vals = v1_ref[b, 0:kend, :]
            maps = []
            for m in range(2):
                s = _dot_nt(qz_ref[b, m, q0:q0 + t, :], keys)
                parts = [s[:, kend - t:] + bias_ref[1]]
                if i >= 1:
                    parts.insert(0, s[:, kend - 2 * t:kend - t] + bias_ref[0])
                if i >= 2:
                    parts.insert(0, s[:, :kend - 2 * t])
                s = jnp.concatenate(parts, axis=1) if len(parts) > 1 else parts[0]
                p = jnp.exp2(s - jnp.max(s, axis=-1, keepdims=True))
                pv = _dot(p.astype(BF16), vals)
                maps.append(pv[:, :VALUE_DIM] / pv[:, VALUE_DIM:VALUE_DIM + 1])
            o = maps[0] - lam * maps[1]
            ms = jnp.mean(o * o, axis=-1, keepdims=True)
            o = o * lax.rsqrt(ms + EPS) * sg_ref[...] * (1.0 - lambda_init)
            o_ref[b, q0:q0 + t, :] = o.astype(o_ref.dtype)


def _attention(q3, k3, v3, rel_bias, q_g, k_g, lam_qk, subln_g, lambda_init):
    b, s, _ = q3.shape
    t = ATT_TILE
    nb = ATT_SEQS if b % ATT_SEQS == 0 else 1
    head = lambda j, i: (i, 0, j)
    fixed2 = lambda j, i: (0, 0)
    fixed3 = lambda j, i: (0, 0, 0)
    half = np.arange(VALUE_DIM) // HEAD_DIM
    blockdiag = jnp.asarray(half[:, None] == half[None, :], dtype=BF16)
    return pl.pallas_call(
        functools.partial(_attn_kernel, seq=s, lambda_init=lambda_init),
        grid=(N_HEADS, b // nb),
        in_specs=[pl.BlockSpec(memory_space=pltpu.SMEM),
                  pl.BlockSpec((nb, s, VALUE_DIM), head),
                  pl.BlockSpec((nb, s, VALUE_DIM), head),
                  pl.BlockSpec((nb, s, VALUE_DIM), head),
                  pl.BlockSpec((2, t, t), fixed3),
                  pl.BlockSpec((VALUE_DIM, VALUE_DIM), fixed2),
                  pl.BlockSpec((1, VALUE_DIM), fixed2),
                  pl.BlockSpec((1, VALUE_DIM), fixed2),
                  pl.BlockSpec((4, HEAD_DIM), fixed2),
                  pl.BlockSpec((1, VALUE_DIM), fixed2)],
        out_specs=pl.BlockSpec((nb, s, VALUE_DIM), head),
        out_shape=jax.ShapeDtypeStruct((b, s, N_HEADS * VALUE_DIM), BF16),
        scratch_shapes=[pltpu.VMEM((nb, 2, s, VALUE_DIM), BF16),
                        pltpu.VMEM((nb, s, VALUE_DIM), BF16),
                        pltpu.VMEM((nb, s, 2 * VALUE_DIM), BF16),
                        pltpu.VMEM((2, t, t), F32)],
        compiler_params=_cparams(("arbitrary", "arbitrary")),
        name="diff_attn",
    )(rel_bias, q3, k3, v3, _near_buckets(), blockdiag, q_g, k_g, lam_qk, subln_g)


def _out_proj_kernel(x_ref, c_ref, a_ref, wc_ref, wa_ref, g_ref, rw_ref, rb_ref, tri_ref,
                     h_ref, hn_ref, idx_ref, gate_ref, rank_ref, cnt_ref, carry_ref):
    @pl.when(pl.program_id(0) == 0)
    def _():
        carry_ref[...] = jnp.zeros_like(carry_ref)

    h = x_ref[...] + _dot(c_ref[...], wc_ref[...]) + _dot(a_ref[...], wa_ref[...])
    h_ref[...] = h
    ms = jnp.mean(h * h, axis=-1, keepdims=True)
    hn = h * lax.rsqrt(ms + EPS) * g_ref[...]
    _store_packed_rows(hn_ref, (), hn)

    logits = _dot(hn.astype(BF16), rw_ref[...]) + rb_ref[...]

    tm = logits.shape[0]
    lane = lax.broadcasted_iota(I32, (tm, LANES), 1).astype(F32)
    work = logits
    vals, idxs = [], []
    for _ in range(TOP_K):
        mx = jnp.max(work, axis=-1, keepdims=True)
        ix = jnp.min(jnp.where(work == mx, lane, float(LANES)), axis=-1, keepdims=True)
        vals.append(mx)
        idxs.append(ix)
        work = jnp.where(lane == ix, NEG_INF, work)
    exps = [jnp.exp(v - vals[0]) for v in vals]
    denom = exps[0]
    for e in exps[1:]:
        denom = denom + e

    sel = jnp.zeros((tm, LANES), F32)
    for ix in idxs:
        sel = sel + jnp.where(lane == ix, 1.0, 0.0)
    rank = _dot(tri_ref[...], sel.astype(BF16)) + carry_ref[...]
    carry_ref[...] = carry_ref[...] + jnp.sum(sel, axis=0, keepdims=True)
    cnt_ref[...] = carry_ref[...]

    idx_out = jnp.zeros((tm, LANES), F32)
    gate_out = jnp.zeros((tm, LANES), F32)
    rank_out = jnp.zeros((tm, LANES), F32)
    for k in range(TOP_K):
        rk = jnp.sum(jnp.where(lane == idxs[k], rank, 0.0), axis=-1, keepdims=True)
        idx_out = jnp.where(lane == k, idxs[k], idx_out)
        gate_out = jnp.where(lane == k, exps[k] / denom, gate_out)
        rank_out = jnp.where(lane == k, rk, rank_out)
    idx_ref[...] = jnp.transpose(idx_out)[:SUBLANES].astype(I32)
    rank_ref[...] = jnp.transpose(rank_out)[:SUBLANES].astype(I32)
    gate_ref[...] = gate_out


def _out_proj(x2, conv_o, attn_o, wc, wa, g, rw, rb, tm):
    n, d = x2.shape
    dc, da = conv_o.shape[1], attn_o.shape[1]
    row = lambda i: (i, 0)
    fixed = lambda i: (0, 0)
    tri = jnp.tril(jnp.ones((tm, tm), F32), -1).astype(BF16)
    return pl.pallas_call(
        _out_proj_kernel,
        grid=(n // tm,),
        in_specs=[pl.BlockSpec((tm, d), row),
                  pl.BlockSpec((tm, dc), row),
                  pl.BlockSpec((tm, da), row),
                  pl.BlockSpec((dc, d), fixed),
                  pl.BlockSpec((da, d), fixed),
                  pl.BlockSpec((1, d), fixed),
                  pl.BlockSpec((d, LANES), fixed),
                  pl.BlockSpec((1, LANES), fixed),
                  pl.BlockSpec((tm, tm), fixed)],
        out_specs=[pl.BlockSpec((tm, d), row),
                   pl.BlockSpec((tm * d // (2 * LANES), LANES), row),
                   pl.BlockSpec((SUBLANES, tm), lambda i: (0, i)),
                   pl.BlockSpec((tm, LANES), row),
                   pl.BlockSpec((SUBLANES, tm), lambda i: (0, i)),
                   pl.BlockSpec((1, LANES), fixed)],
        out_shape=[jax.ShapeDtypeStruct((n, d), F32),
                   jax.ShapeDtypeStruct((n * d // (2 * LANES), LANES), U32),
                   jax.ShapeDtypeStruct((SUBLANES, n), I32),
                   jax.ShapeDtypeStruct((n, LANES), F32),
                   jax.ShapeDtypeStruct((SUBLANES, n), I32),
                   jax.ShapeDtypeStruct((1, LANES), F32)],
        scratch_shapes=[pltpu.VMEM((1, LANES), F32)],
        compiler_params=_cparams(("arbitrary",)),
        name="out_proj_router",
    )(x2, conv_o, attn_o, wc, wa, g, rw, rb, tri)


MOE_BLOCK = 512
MOE_QUARTERS = 4


def _moe_kernel(be_ref, first_ref, wslot_ref, nxt_ref, quarters_ref, nused_ref,
                x_ref, wgu_hbm, wd_hbm, bgu_ref, bd_ref, perm_ref,
                y_ref, wgu_f32, wd_f32, wsem, wgu_bf, wd_bf, x_bf, act_bf):
    i = pl.program_id(0)
    bm = MOE_BLOCK
    n_used = nused_ref[0]
    d = wd_bf.shape[1]
    d_ff = wd_bf.shape[0]
    n_groups = d_ff // LANES

    def weight_copies(e, slot):
        return (pltpu.make_async_copy(wgu_hbm.at[e], wgu_f32.at[slot], wsem.at[0, slot]),
                pltpu.make_async_copy(wd_hbm.at[e], wd_f32.at[slot], wsem.at[1, slot]))

    @pl.when(i == 0)
    def _():
        for c in weight_copies(be_ref[0], 0):
            c.start()

    active = i < n_used

    @pl.when(jnp.logical_and(active, first_ref[i] == 1))
    def _():
        ws = wslot_ref[i]
        for c in weight_copies(be_ref[i], ws):
            c.wait()
        for g in range(n_groups):
            cols = pl.ds(g * MXU_DIM, MXU_DIM)
            wgu_bf[:, cols] = _dot(wgu_f32[ws, :, cols].astype(BF16), perm_ref[...]).astype(BF16)
        wd_bf[...] = wd_f32[ws].astype(BF16)

        @pl.when(nxt_ref[i] >= 0)
        def _():
            for c in weight_copies(nxt_ref[i], 1 - ws):
                c.start()

    def block(m):
        k = d // (2 * LANES)
        for j in range(k):
            lo, hi = _load_packed_tile(x_ref, (), m, k, j)
            x_bf[0:m, j * LANES:(j + 1) * LANES] = lo.astype(BF16)
            x_bf[0:m, (k + j) * LANES:(k + j + 1) * LANES] = hi.astype(BF16)
        for g in range(n_groups):
            gu = (_dot(x_bf[0:m, :], wgu_bf[:, g * MXU_DIM:(g + 1) * MXU_DIM])
                  + bgu_ref[0, :, g * MXU_DIM:(g + 1) * MXU_DIM])
            gate = jnp.minimum(gu[:, :LANES], SWIGLU_LIMIT)
            lin = jnp.clip(gu[:, LANES:], -SWIGLU_LIMIT, SWIGLU_LIMIT)
            act_bf[0:m, g * LANES:(g + 1) * LANES] = (
                gate * jax.nn.sigmoid(SWIGLU_ALPHA * gate) * (lin + 1.0)).astype(BF16)
        _store_packed_rows(y_ref, (), _dot(act_bf[0:m, :], wd_bf[...]) + bd_ref[0])
        if m < bm:
            y_ref[pl.ds(m * k, (bm - m) * k), :] = jnp.zeros(((bm - m) * k, LANES), y_ref.dtype)

    quarter = bm // MOE_QUARTERS
    for nq in range(1, MOE_QUARTERS + 1):
        @pl.when(jnp.logical_and(active, quarters_ref[i] == nq))
        def _(nq=nq):
            block(nq * quarter)

    @pl.when(i >= n_used)
    def _():
        y_ref[...] = jnp.zeros_like(y_ref)


def _deinterleave_perm():
    src = np.arange(MXU_DIM)
    dst = np.where(src % 2 == 0, src // 2, LANES + src // 2)
    p = np.zeros((MXU_DIM, MXU_DIM), np.float32)
    p[src, dst] = 1.0
    return jnp.asarray(p, dtype=BF16)


def _moe(plan, xs, w_gate_up, bgu_p, w_down, b_down):
    n_exp, d, d_gu = w_gate_up.shape
    d_ff = w_down.shape[1]
    bm = MOE_BLOCK
    nblk = plan[0].shape[0]
    slab = bm * d // (2 * LANES)
    by_expert = lambda i, be, *_: (be[i], 0, 0)
    grid_spec = pltpu.PrefetchScalarGridSpec(
        num_scalar_prefetch=6,
        grid=(nblk,),
        in_specs=[
            pl.BlockSpec((slab, LANES), lambda i, *_: (i, 0)),
            pl.BlockSpec(memory_space=pl.ANY),
            pl.BlockSpec(memory_space=pl.ANY),
            pl.BlockSpec((1, 1, d_gu), by_expert),
            pl.BlockSpec((1, 1, d), by_expert),
            pl.BlockSpec((MXU_DIM, MXU_DIM), lambda i, *_: (0, 0)),
        ],
        out_specs=pl.BlockSpec((slab, LANES), lambda i, *_: (i, 0)),
        scratch_shapes=[pltpu.VMEM((2, d, d_gu), F32),
                        pltpu.VMEM((2, d_ff, d), F32),
                        pltpu.SemaphoreType.DMA((2, 2)),
                        pltpu.VMEM((d, d_gu), BF16),
                        pltpu.VMEM((d_ff, d), BF16),
                        pltpu.VMEM((bm, d), BF16),
                        pltpu.VMEM((bm, d_ff), BF16)],
    )
    return pl.pallas_call(
        _moe_kernel,
        grid_spec=grid_spec,
        out_shape=jax.ShapeDtypeStruct((nblk * slab, LANES), U32),
        compiler_params=_cparams(("arbitrary",)),
        name="moe_experts",
    )(*plan, xs, w_gate_up, w_down, bgu_p, b_down, _deinterleave_perm())


SC_CORES = 2
SC_SUBCORES = 16
SC_CHUNK = 64


def _sc_mesh():
    return plsc.VectorSubcoreMesh(core_axis_name="c", subcore_axis_name="s")


def _sc_dispatch(rows3, pos_chunks, n_slots):
    n, r, _ = rows3.shape
    workers = SC_CORES * SC_SUBCORES
    per_worker = n // SC_CHUNK // workers

    def body(rows_hbm, pos_hbm, out_hbm, idx_v, rows_v, sem):
        wid = lax.axis_index("s") * SC_CORES + lax.axis_index("c")

        @pl.loop(0, per_worker)
        def _(c):
            chunk = wid * per_worker + c
            pltpu.sync_copy(pos_hbm.at[chunk], idx_v)
            pltpu.sync_copy(rows_hbm.at[pl.ds(chunk * SC_CHUNK, SC_CHUNK)], rows_v)
            copies = [pltpu.async_copy(rows_v, out_hbm.at[idx_v.at[k]], sem) for k in range(TOP_K)]
            for cp in copies:
                cp.wait()

    return pl.kernel(
        body,
        out_type=jax.ShapeDtypeStruct((n_slots, r, LANES), rows3.dtype),
        mesh=_sc_mesh(),
        scratch_types=[pltpu.VMEM((TOP_K, SC_CHUNK), I32),
                       pltpu.VMEM((SC_CHUNK, r, LANES), rows3.dtype),
                       pltpu.SemaphoreType.DMA],
        name="sc_dispatch",
    )(rows3, pos_chunks)


def _sc_gather(rows3, pos_chunks, n):
    _, r, _ = rows3.shape
    workers = SC_CORES * SC_SUBCORES
    per_worker = n // SC_CHUNK // workers

    def body(rows_hbm, pos_hbm, out_hbm, idx_v, buf_a, buf_b, sem_a, sem_b):
        wid = lax.axis_index("s") * SC_CORES + lax.axis_index("c")
        bufs, sems = (buf_a, buf_b), (sem_a, sem_b)

        @pl.loop(0, per_worker)
        def _(c):
            chunk = wid * per_worker + c
            pltpu.sync_copy(pos_hbm.at[chunk], idx_v)
            gathers = [None] * TOP_K
            gathers[0] = pltpu.async_copy(rows_hbm.at[idx_v.at[0]], bufs[0], sems[0])
            for k in range(TOP_K):
                if k + 1 < TOP_K:
                    nxt = (k + 1) % 2
                    gathers[k + 1] = pltpu.async_copy(rows_hbm.at[idx_v.at[k + 1]], bufs[nxt], sems[nxt])
                gathers[k].wait()
                pltpu.sync_copy(bufs[k % 2], out_hbm.at[k, pl.ds(chunk * SC_CHUNK, SC_CHUNK)])

    return pl.kernel(
        body,
        out_type=jax.ShapeDtypeStruct((TOP_K, n, r, LANES), rows3.dtype),
        mesh=_sc_mesh(),
        scratch_types=[pltpu.VMEM((TOP_K, SC_CHUNK), I32),
                       pltpu.VMEM((SC_CHUNK, r, LANES), rows3.dtype),
                       pltpu.VMEM((SC_CHUNK, r, LANES), rows3.dtype),
                       pltpu.SemaphoreType.DMA,
                       pltpu.SemaphoreType.DMA],
        name="sc_gather",
    )(rows3, pos_chunks)


COMBINE_TILE = 512


def _combine_kernel(h_ref, gate_ref, y_ref, o_ref):
    tc, d = h_ref.shape
    kt = d // (2 * LANES)
    gates = gate_ref[...]
    for j in range(kt):
        lo_acc = h_ref[:, j * LANES:(j + 1) * LANES]
        hi_acc = h_ref[:, (kt + j) * LANES:(kt + j + 1) * LANES]
        for k in range(TOP_K):
            lo, hi = _load_packed_tile(y_ref, (k,), tc, kt, j)
            g = gates[:, k:k + 1]
            lo_acc = lo_acc + g * lo
            hi_acc = hi_acc + g * hi
        o_ref[:, j * LANES:(j + 1) * LANES] = lo_acc
        o_ref[:, (kt + j) * LANES:(kt + j + 1) * LANES] = hi_acc


def _combine(h, gates, y4):
    n, d = h.shape
    tc = COMBINE_TILE
    slab = tc * d // (2 * LANES)
    row = lambda i: (i, 0)
    return pl.pallas_call(
        _combine_kernel,
        grid=(n // tc,),
        in_specs=[pl.BlockSpec((tc, d), row),
                  pl.BlockSpec((tc, LANES), row),
                  pl.BlockSpec((TOP_K, slab, LANES), lambda i: (0, i, 0))],
        out_specs=pl.BlockSpec((tc, d), row),
        out_shape=jax.ShapeDtypeStruct((n, d), F32),
        input_output_aliases={0: 0},
        compiler_params=_cparams(("parallel",)),
        name="moe_combine",
    )(h, gates, y4)


def _layer(h3, layer, norm_mix_g, w_in, conv_w, conv_b, conv_ln_g, conv_ln_b, q_norm_g, k_norm_g,
           lambda_qk, subln_g, rel_bias, w_out, norm_ffn_g, router_w, router_b,
           w_gate_up, b_gate_up, w_down, b_down):
    b, s, d = h3.shape
    n = b * s
    d_conv = conv_w.shape[-1]
    d_attn = N_HEADS * VALUE_DIM
    lambda_init = 0.8 - 0.6 * math.exp(-0.3 * layer)
    x2 = h3.reshape(n, d)

    u, q, k, v = _in_proj(x2, norm_mix_g.reshape(1, d), w_in.astype(BF16), d_conv, d_attn, tm=512)
    conv_o = _conv(u.reshape(b, s, d_conv), conv_w, conv_b.reshape(1, d_conv),
                   conv_ln_g.reshape(1, d_conv), conv_ln_b.reshape(1, d_conv))
    attn_o = _attention(q.reshape(b, s, d_attn), k.reshape(b, s, d_attn), v.reshape(b, s, d_attn),
                        rel_bias, q_norm_g.reshape(1, VALUE_DIM), k_norm_g.reshape(1, VALUE_DIM),
                        lambda_qk, subln_g.reshape(1, VALUE_DIM), lambda_init)

    n_exp = router_w.shape[1]
    rw = jnp.zeros((d, LANES), BF16).at[:, :n_exp].set(router_w.astype(BF16))
    rb = jnp.full((1, LANES), NEG_INF, F32).at[0, :n_exp].set(router_b)
    w_out_bf = w_out.astype(BF16)
    hres, hn, idx, gates, rank, cnt = _out_proj(
        x2, conv_o.reshape(n, d_conv), attn_o.reshape(n, d_attn),
        w_out_bf[:d_conv], w_out_bf[d_conv:], norm_ffn_g.reshape(1, d), rw, rb, tm=512)

    bm = MOE_BLOCK
    nblk = n * TOP_K // bm + n_exp
    counts = cnt[0, :n_exp].astype(I32)
    padded = (counts + bm - 1) // bm * bm
    eid = jnp.arange(n_exp, dtype=I32)
    pad_end = jnp.sum(jnp.where(eid[None, :] <= eid[:, None], padded[None, :], 0), axis=1)
    pad_start = pad_end - padded
    start_of = jnp.sum(jnp.where(idx[None, :TOP_K] == eid[:, None, None],
                                 pad_start[:, None, None], 0), axis=0)
    pos = start_of + rank[:TOP_K]
    pos_chunks = pos.reshape(TOP_K, n // SC_CHUNK, SC_CHUNK).transpose(1, 0, 2)
    block_start = jnp.arange(nblk, dtype=I32) * bm
    block_expert = jnp.minimum(jnp.sum(block_start[:, None] >= pad_end[None, :], axis=1),
                               n_exp - 1).astype(I32)
    n_used = (pad_end[-1:] // bm).astype(I32)
    blk = jnp.arange(nblk, dtype=I32)
    first = jnp.logical_and(
        jnp.concatenate([jnp.ones((1,), bool), block_expert[1:] != block_expert[:-1]]),
        blk < n_used[0]).astype(I32)
    opened = jnp.sum(jnp.where(blk[None, :] <= blk[:, None], first[None, :], 0), axis=1)
    wslot = ((opened - 1) % 2).astype(I32)
    later_used = jnp.logical_and(eid[None, :] > eid[:, None], counts[None, :] > 0)
    next_used = jnp.min(jnp.where(later_used, eid[None, :], n_exp), axis=1)
    next_used = jnp.where(next_used == n_exp, -1, next_used).astype(I32)
    next_of_block = jnp.sum(jnp.where(block_expert[:, None] == eid[None, :], next_used[None, :], 0),
                            axis=1).astype(I32)
    on_expert = block_expert[:, None] == eid[None, :]
    rows_in_block = jnp.clip(
        jnp.sum(jnp.where(on_expert, (counts + pad_start)[None, :], 0), axis=1) - block_start, 0, bm)
    quarter = bm // MOE_QUARTERS
    quarters = ((rows_in_block + quarter - 1) // quarter).astype(I32)
    plan = (block_expert, first, wslot, next_of_block, quarters, n_used)

    d_ff = w_down.shape[1]
    bgu_p = b_gate_up.reshape(n_exp, d_ff // LANES, LANES, 2).transpose(0, 1, 3, 2).reshape(n_exp, 1, 2 * d_ff)
    r = d // (2 * LANES)
    xs = _sc_dispatch(hn.reshape(n, r, LANES), pos_chunks, nblk * bm)
    yb = _moe(plan, xs.reshape(nblk * bm * r, LANES), w_gate_up, bgu_p, w_down,
              b_down.reshape(n_exp, 1, d))
    y4 = _sc_gather(yb.reshape(nblk * bm, r, LANES), pos_chunks, n)
    out = _combine(hres, gates, y4.reshape(TOP_K, n * r, LANES))
    return out.reshape(b, s, d)


def kernel(x, norm_mix_g, w_in, conv_w, conv_b, conv_ln_g, conv_ln_b, q_norm_g, k_norm_g, lambda_qk,
           subln_g, rel_bias, w_out, norm_ffn_g, router_w, router_b, w_gate_up, b_gate_up, w_down,
           b_down):
    h = x
    for layer in range(norm_mix_g.shape[0]):
        h = _layer(h, layer, norm_mix_g[layer], w_in[layer], conv_w[layer], conv_b[layer],
                   conv_ln_g[layer], conv_ln_b[layer], q_norm_g[layer], k_norm_g[layer],
                   lambda_qk[layer], subln_g[layer], rel_bias, w_out[layer], norm_ffn_g[layer],
                   router_w[layer], router_b[layer], w_gate_up[layer], b_gate_up[layer],
                   w_down[layer], b_down[layer])
    return h
```

```python
import functools
import math

import jax
import jax.numpy as jnp
import numpy as np
from jax import lax
from jax.experimental import pallas as pl
from jax.experimental.pallas import tpu as pltpu
from jax.experimental.pallas import tpu_sc as plsc

F32 = jnp.float32
BF16 = jnp.bfloat16
I32 = jnp.int32
U32 = jnp.uint32

CHUNK = 64
CONV_WIDTH = 31
N_HEADS = 4
HEAD_DIM = 64
VALUE_DIM = 2 * HEAD_DIM
REL_BUCKETS = 32
REL_MAX_DIST = 128
N_EXPERTS = 32
TOP_K = 4
SWIGLU_LIMIT = 7.0
SWIGLU_ALPHA = 1.702
EPS = 1e-5
LOG2E = 1.4426950408889634

LANES = 128
SUBLANES = 8
MXU_DIM = 256
VMEM_LIMIT = 56 * 1024 * 1024

NEG_INF = float("-inf")


def _cparams(sem, vmem=VMEM_LIMIT, flags=None):
    return pltpu.CompilerParams(dimension_semantics=sem, vmem_limit_bytes=vmem, flags=flags)


def _dot(a, b):
    return jnp.dot(a, b, preferred_element_type=F32)


def _dot_nt(a, b):
    return lax.dot_general(a, b, (((1,), (1,)), ((), ())), preferred_element_type=F32)


def _pack_pair(lo, hi):
    lo_bits = lax.bitcast_convert_type(lo.astype(BF16).astype(F32), U32)
    hi_bits = lax.bitcast_convert_type(hi.astype(BF16).astype(F32), U32)
    return (lo_bits >> 16) | hi_bits


def _unpack_pair(w):
    return (lax.bitcast_convert_type(w << 16, F32),
            lax.bitcast_convert_type(w & jnp.uint32(0xFFFF0000), F32))


def _store_packed_rows(ref, lead, x):
    rows, k = x.shape[0], x.shape[1] // (2 * LANES)
    for j in range(k):
        w = _pack_pair(x[:, j * LANES:(j + 1) * LANES], x[:, (k + j) * LANES:(k + j + 1) * LANES])
        ref[(*lead, pl.ds(j, rows, stride=k), slice(None))] = w


def _load_packed_tile(ref, lead, rows, k, j):
    return _unpack_pair(ref[(*lead, pl.ds(j, rows, stride=k), slice(None))])


def _in_proj_kernel(x_ref, g_ref, w_ref, u_ref, q_ref, k_ref, v_ref, *, d_conv, d_attn):
    x = x_ref[...]
    ms = jnp.mean(x * x, axis=-1, keepdims=True)
    y = (x * lax.rsqrt(ms + EPS) * g_ref[...]).astype(BF16)
    proj = _dot(y, w_ref[...])
    a = proj[:, :d_conv]
    g = proj[:, d_conv:2 * d_conv]
    u_ref[...] = a * jax.nn.sigmoid(g)
    o = 2 * d_conv
    q_ref[...] = proj[:, o:o + d_attn]
    k_ref[...] = proj[:, o + d_attn:o + 2 * d_attn]
    v_ref[...] = proj[:, o + 2 * d_attn:o + 3 * d_attn].astype(BF16)


def _in_proj(x2, g, w_bf, d_conv, d_attn, tm):
    n, d = x2.shape
    d_in = w_bf.shape[1]
    row = lambda i: (i, 0)
    fixed = lambda i: (0, 0)
    return pl.pallas_call(
        functools.partial(_in_proj_kernel, d_conv=d_conv, d_attn=d_attn),
        grid=(n // tm,),
        in_specs=[pl.BlockSpec((tm, d), row),
                  pl.BlockSpec((1, d), fixed),
                  pl.BlockSpec((d, d_in), fixed)],
        out_specs=[pl.BlockSpec((tm, d_conv), row),
                   pl.BlockSpec((tm, d_attn), row),
                   pl.BlockSpec((tm, d_attn), row),
                   pl.BlockSpec((tm, d_attn), row)],
        out_shape=[jax.ShapeDtypeStruct((n, d_conv), F32),
                   jax.ShapeDtypeStruct((n, d_attn), F32),
                   jax.ShapeDtypeStruct((n, d_attn), F32),
                   jax.ShapeDtypeStruct((n, d_attn), BF16)],
        compiler_params=_cparams(("parallel",)),
        name="in_proj",
    )(x2, g, w_bf)


CONV_PAD = 32
CONV_SEQ_TILE = 1024
CONV_TILE = 128
CONV_NORM_TILE = 256


def _conv_kernel(u_ref, prev_ref, w_ref, cb_ref, lg_ref, lb_ref, o_ref, sh_ref, y_ref):
    ts, c = u_ref.shape[1], u_ref.shape[2]
    plen = ts + CONV_PAD
    hist = prev_ref[0]
    hist = jnp.where(pl.program_id(1) > 0, hist, jnp.zeros_like(hist))
    sh_ref[0, pl.ds(0, CONV_PAD), :] = hist
    sh_ref[0, pl.ds(CONV_PAD, ts), :] = u_ref[0]
    sh_ref[0, pl.ds(plen, SUBLANES), :] = jnp.zeros((SUBLANES, c), F32)

    bt = CONV_PAD

    def shift(i, _):
        p0 = pl.multiple_of(i * bt, bt)
        win = sh_ref[0, pl.ds(p0, bt + SUBLANES), :]
        for r in range(1, SUBLANES):
            sh_ref[r, pl.ds(p0, bt), :] = win[r:r + bt]
        return 0

    lax.fori_loop(0, plen // bt, shift, 0)

    off0 = CONV_PAD - (CONV_WIDTH - 1)

    by_shift = {}
    for j in range(CONV_WIDTH):
        a, r = divmod(off0 + j, SUBLANES)
        by_shift.setdefault(r, []).append((a, j))

    for lt in range(c // LANES):
        lanes = slice(lt * LANES, (lt + 1) * LANES)
        taps = [w_ref[pl.ds(j, 1), lanes] for j in range(CONV_WIDTH)]

        def tap_body(i, _, lanes=lanes, taps=taps):
            t0 = pl.multiple_of(i * CONV_TILE, CONV_TILE)
            acc = jnp.zeros((CONV_TILE, LANES), F32)
            for r, group in by_shift.items():
                a_lo = min(a for a, _ in group)
                a_hi = max(a for a, _ in group)
                rows = CONV_TILE + (a_hi - a_lo) * SUBLANES
                start = pl.multiple_of(t0 + a_lo * SUBLANES, SUBLANES)
                win = sh_ref[r, pl.ds(start, rows), lanes]
                for a, j in group:
                    lo = (a - a_lo) * SUBLANES
                    acc = acc + win[lo:lo + CONV_TILE] * taps[j]
            y_ref[pl.ds(t0, CONV_TILE), lanes] = acc
            return 0

        lax.fori_loop(0, ts // CONV_TILE, tap_body, 0)

    def norm_body(i, _):
        t0 = pl.multiple_of(i * CONV_NORM_TILE, CONV_NORM_TILE)
        y = y_ref[pl.ds(t0, CONV_NORM_TILE), :] + cb_ref[...]
        mu = jnp.mean(y, axis=-1, keepdims=True)
        yc = y - mu
        var = jnp.mean(yc * yc, axis=-1, keepdims=True)
        z = yc * lax.rsqrt(var + EPS) * lg_ref[...] + lb_ref[...]
        o_ref[0, pl.ds(t0, CONV_NORM_TILE), :] = (z * jax.nn.sigmoid(z)).astype(o_ref.dtype)
        return 0

    lax.fori_loop(0, ts // CONV_NORM_TILE, norm_body, 0)


def _conv(u3, conv_w, conv_b, ln_g, ln_b):
    b, s, c = u3.shape
    ts = min(CONV_SEQ_TILE, s)
    hist_per_tile = ts // CONV_PAD
    fixed = lambda i, j: (0, 0)
    return pl.pallas_call(
        _conv_kernel,
        grid=(b, s // ts),
        in_specs=[pl.BlockSpec((1, ts, c), lambda i, j: (i, j, 0)),
                  pl.BlockSpec((1, CONV_PAD, c),
                               lambda i, j: (i, jnp.maximum(j * hist_per_tile - 1, 0), 0)),
                  pl.BlockSpec((CONV_WIDTH, c), fixed),
                  pl.BlockSpec((1, c), fixed),
                  pl.BlockSpec((1, c), fixed),
                  pl.BlockSpec((1, c), fixed)],
        out_specs=pl.BlockSpec((1, ts, c), lambda i, j: (i, j, 0)),
        out_shape=jax.ShapeDtypeStruct((b, s, c), BF16),
        scratch_shapes=[pltpu.VMEM((SUBLANES, ts + CONV_PAD + SUBLANES, c), F32),
                        pltpu.VMEM((ts, c), F32)],
        compiler_params=_cparams(("parallel", "parallel")),
        name="conv_mixer",
    )(u3, u3, conv_w, conv_b, ln_g, ln_b)


ATT_TILE = 256
ATT_SEQS = 2
FAR_BUCKET = REL_BUCKETS // 2 - 1


def _t5_bucket(rel):
    nb = REL_BUCKETS // 2
    max_exact = nb // 2
    ret = jnp.where(rel > 0, nb, 0)
    n = jnp.abs(rel)
    nf = jnp.maximum(n, 1).astype(jnp.float32)
    large = max_exact + (jnp.log(nf / max_exact) / math.log(REL_MAX_DIST / max_exact)
                         * (nb - max_exact)).astype(jnp.int32)
    large = jnp.minimum(large, nb - 1)
    return ret + jnp.where(n < max_exact, n, large)


def _near_buckets():
    assert ATT_TILE >= REL_MAX_DIST and ATT_TILE % CHUNK == 0
    qpos = jnp.arange(ATT_TILE, dtype=I32)[:, None]
    kpos = jnp.arange(ATT_TILE, dtype=I32)[None, :]
    prev = _t5_bucket(kpos - ATT_TILE - qpos)
    diag = _t5_bucket(kpos - qpos)
    diag = jnp.where(kpos // CHUNK <= qpos // CHUNK, diag, -1)
    return jnp.stack([prev, diag]).astype(I32)


def _attn_kernel(tab_ref, q_ref, k_ref, v_ref, bkt_ref, bd_ref, qg_ref, kg_ref, lqk_ref, sg_ref, o_ref,
                 qz_ref, kn_ref, v1_ref, bias_ref, *, seq, lambda_init):
    h = pl.program_id(0)
    t = ATT_TILE
    n_tiles = seq // t
    lane = lax.broadcasted_iota(I32, (1, VALUE_DIM), 1)
    first = lane < HEAD_DIM

    @pl.when(pl.program_id(1) == 0)
    def _():
        far = tab_ref[FAR_BUCKET, h]
        for d in range(2):
            bkt = bkt_ref[d]
            tile = jnp.full((t, t), NEG_INF, F32)
            for b in range(REL_BUCKETS):
                tile = jnp.where(bkt == b, (tab_ref[b, h] - far) * LOG2E, tile)
            bias_ref[d] = tile

    def half_norm(x, g):
        x2 = x * x
        hi = x2.astype(BF16)
        lo = (x2 - hi.astype(F32)).astype(BF16)
        ms = (_dot(hi, bd_ref[...]) + _dot(lo, bd_ref[...])) * (1.0 / HEAD_DIM)
        return x * lax.rsqrt(ms + EPS) * g

    q_gain = qg_ref[...] * (HEAD_DIM ** -0.5 * LOG2E)
    qg_maps = (jnp.where(first, q_gain, 0.0), jnp.where(first, 0.0, q_gain))

    n_seq = q_ref.shape[0]

    ones_col = jnp.broadcast_to(jnp.where(lane == 0, 1.0, 0.0).astype(BF16), (t, VALUE_DIM))
    for i in range(n_tiles):
        r0 = i * t
        for b in range(n_seq):
            qn = half_norm(q_ref[b, pl.ds(r0, t), :], 1.0)
            qz_ref[b, 0, pl.ds(r0, t), :] = (qn * qg_maps[0]).astype(BF16)
            qz_ref[b, 1, pl.ds(r0, t), :] = (qn * qg_maps[1]).astype(BF16)
            kn_ref[b, pl.ds(r0, t), :] = half_norm(k_ref[b, pl.ds(r0, t), :], kg_ref[...]).astype(BF16)
            v1_ref[b, pl.ds(r0, t), :] = jnp.concatenate([v_ref[b, pl.ds(r0, t), :], ones_col], axis=1)

    lqk = lqk_ref[...]
    lam = (jnp.exp(jnp.sum(lqk[0:1] * lqk[1:2], axis=-1, keepdims=True))
           - jnp.exp(jnp.sum(lqk[2:3] * lqk[3:4], axis=-1, keepdims=True)) + lambda_init)

    for i in range(n_tiles):
        q0, kend = i * t, (i + 1) * t
        for b in range(n_seq):
            keys = kn_ref[b, 0:kend, :]
            vals = v1_ref[b, 0:kend, :]
            maps = []
            for m in range(2):
                s = _dot_nt(qz_ref[b, m, q0:q0 + t, :], keys)
                parts = [s[:, kend - t:] + bias_ref[1]]
                if i >= 1:
                    parts.insert(0, s[:, kend - 2 * t:kend - t] + bias_ref[0])
                if i >= 2:
                    parts.insert(0, s[:, :kend - 2 * t])
                s = jnp.concatenate(parts, axis=1) if len(parts) > 1 else parts[0]
                p = jnp.exp2(s - jnp.max(s, axis=-1, keepdims=True))
                pv = _dot(p.astype(BF16), vals)
                maps.append(pv[:, :VALUE_DIM] / pv[:, VALUE_DIM:VALUE_DIM + 1])
            o = maps[0] - lam * maps[1]
            ms = jnp.mean(o * o, axis=-1, keepdims=True)
            o = o * lax.rsqrt(ms + EPS) * sg_ref[...] * (1.0 - lambda_init)
            o_ref[b, q0:q0 + t, :] = o.astype(o_ref.dtype)


def _attention(q3, k3, v3, rel_bias, q_g, k_g, lam_qk, subln_g, lambda_init):
    b, s, _ = q3.shape
    t = ATT_TILE
    nb = ATT_SEQS if b % ATT_SEQS == 0 else 1
    head = lambda j, i: (i, 0, j)
    fixed2 = lambda j, i: (0, 0)
    fixed3 = lambda j, i: (0, 0, 0)
    half = np.arange(VALUE_DIM) // HEAD_DIM
    blockdiag = jnp.asarray(half[:, None] == half[None, :], dtype=BF16)
    return pl.pallas_call(
        functools.partial(_attn_kernel, seq=s, lambda_init=lambda_init),
        grid=(N_HEADS, b // nb),
        in_specs=[pl.BlockSpec(memory_space=pltpu.SMEM),
                  pl.BlockSpec((nb, s, VALUE_DIM), head),
                  pl.BlockSpec((nb, s, VALUE_DIM), head),
                  pl.BlockSpec((nb, s, VALUE_DIM), head),
                  pl.BlockSpec((2, t, t), fixed3),
                  pl.BlockSpec((VALUE_DIM, VALUE_DIM), fixed2),
                  pl.BlockSpec((1, VALUE_DIM), fixed2),
                  pl.BlockSpec((1, VALUE_DIM), fixed2),
                  pl.BlockSpec((4, HEAD_DIM), fixed2),
                  pl.BlockSpec((1, VALUE_DIM), fixed2)],
        out_specs=pl.BlockSpec((nb, s, VALUE_DIM), head),
        out_shape=jax.ShapeDtypeStruct((b, s, N_HEADS * VALUE_DIM), BF16),
        scratch_shapes=[pltpu.VMEM((nb, 2, s, VALUE_DIM), BF16),
                        pltpu.VMEM((nb, s, VALUE_DIM), BF16),
                        pltpu.VMEM((nb, s, 2 * VALUE_DIM), BF16),
                        pltpu.VMEM((2, t, t), F32)],
        compiler_params=_cparams(("arbitrary", "arbitrary")),
        name="diff_attn",
    )(rel_bias, q3, k3, v3, _near_buckets(), blockdiag, q_g, k_g, lam_qk, subln_g)


def _out_proj_kernel(x_ref, c_ref, a_ref, wc_ref, wa_ref, g_ref, rw_ref, rb_ref, tri_ref,
                     h_ref, hn_ref, idx_ref, gate_ref, rank_ref, cnt_ref, carry_ref):
    @pl.when(pl.program_id(0) == 0)
    def _():
        carry_ref[...] = jnp.zeros_like(carry_ref)

    h = x_ref[...] + _dot(c_ref[...], wc_ref[...]) + _dot(a_ref[...], wa_ref[...])
    h_ref[...] = h
    ms = jnp.mean(h * h, axis=-1, keepdims=True)
    hn = h * lax.rsqrt(ms + EPS) * g_ref[...]
    _store_packed_rows(hn_ref, (), hn)

    logits = _dot(hn.astype(BF16), rw_ref[...]) + rb_ref[...]

    tm = logits.shape[0]
    lane = lax.broadcasted_iota(I32, (tm, LANES), 1).astype(F32)
    work = logits
    vals, idxs = [], []
    for _ in range(TOP_K):
        mx = jnp.max(work, axis=-1, keepdims=True)
        ix = jnp.min(jnp.where(work == mx, lane, float(LANES)), axis=-1, keepdims=True)
        vals.append(mx)
        idxs.append(ix)
        work = jnp.where(lane == ix, NEG_INF, work)
    exps = [jnp.exp(v - vals[0]) for v in vals]
    denom = exps[0]
    for e in exps[1:]:
        denom = denom + e

    sel = jnp.zeros((tm, LANES), F32)
    for ix in idxs:
        sel = sel + jnp.where(lane == ix, 1.0, 0.0)
    rank = _dot(tri_ref[...], sel.astype(BF16)) + carry_ref[...]
    carry_ref[...] = carry_ref[...] + jnp.sum(sel, axis=0, keepdims=True)
    cnt_ref[...] = carry_ref[...]

    idx_out = jnp.zeros((tm, LANES), F32)
    gate_out = jnp.zeros((tm, LANES), F32)
    rank_out = jnp.zeros((tm, LANES), F32)
    for k in range(TOP_K):
        rk = jnp.sum(jnp.where(lane == idxs[k], rank, 0.0), axis=-1, keepdims=True)
        idx_out = jnp.where(lane == k, idxs[k], idx_out)
        gate_out = jnp.where(lane == k, exps[k] / denom, gate_out)
        rank_out = jnp.where(lane == k, rk, rank_out)
    idx_ref[...] = jnp.transpose(idx_out)[:SUBLANES].astype(I32)
    rank_ref[...] = jnp.transpose(rank_out)[:SUBLANES].astype(I32)
    gate_ref[...] = gate_out


def _out_proj(x2, conv_o, attn_o, wc, wa, g, rw, rb, tm):
    n, d = x2.shape
    dc, da = conv_o.shape[1], attn_o.shape[1]
    row = lambda i: (i, 0)
    fixed = lambda i: (0, 0)
    tri = jnp.tril(jnp.ones((tm, tm), F32), -1).astype(BF16)
    return pl.pallas_call(
        _out_proj_kernel,
        grid=(n // tm,),
        in_specs=[pl.BlockSpec((tm, d), row),
                  pl.BlockSpec((tm, dc), row),
                  pl.BlockSpec((tm, da), row),
                  pl.BlockSpec((dc, d), fixed),
                  pl.BlockSpec((da, d), fixed),
                  pl.BlockSpec((1, d), fixed),
                  pl.BlockSpec((d, LANES), fixed),
                  pl.BlockSpec((1, LANES), fixed),
                  pl.BlockSpec((tm, tm), fixed)],
        out_specs=[pl.BlockSpec((tm, d), row),
                   pl.BlockSpec((tm * d // (2 * LANES), LANES), row),
                   pl.BlockSpec((SUBLANES, tm), lambda i: (0, i)),
                   pl.BlockSpec((tm, LANES), row),
                   pl.BlockSpec((SUBLANES, tm), lambda i: (0, i)),
                   pl.BlockSpec((1, LANES), fixed)],
        out_shape=[jax.ShapeDtypeStruct((n, d), F32),
                   jax.ShapeDtypeStruct((n * d // (2 * LANES), LANES), U32),
                   jax.ShapeDtypeStruct((SUBLANES, n), I32),
                   jax.ShapeDtypeStruct((n, LANES), F32),
                   jax.ShapeDtypeStruct((SUBLANES, n), I32),
                   jax.ShapeDtypeStruct((1, LANES), F32)],
        scratch_shapes=[pltpu.VMEM((1, LANES), F32)],
        compiler_params=_cparams(("arbitrary",)),
        name="out_proj_router",
    )(x2, conv_o, attn_o, wc, wa, g, rw, rb, tri)


MOE_BLOCK = 512
MOE_QUARTERS = 4


def _moe_kernel(be_ref, first_ref, wslot_ref, nxt_ref, quarters_ref, nused_ref,
                x_ref, wgu_hbm, wd_hbm, bgu_ref, bd_ref, perm_ref,
                y_ref, wgu_f32, wd_f32, wsem, wgu_bf, wd_bf, x_bf, act_bf):
    i = pl.program_id(0)
    bm = MOE_BLOCK
    n_used = nused_ref[0]
    d = wd_bf.shape[1]
    d_ff = wd_bf.shape[0]
    n_groups = d_ff // LANES

    def weight_copies(e, slot):
        return (pltpu.make_async_copy(wgu_hbm.at[e], wgu_f32.at[slot], wsem.at[0, slot]),
                pltpu.make_async_copy(wd_hbm.at[e], wd_f32.at[slot], wsem.at[1, slot]))

    @pl.when(i == 0)
    def _():
        for c in weight_copies(be_ref[0], 0):
            c.start()

    active = i < n_used

    @pl.when(jnp.logical_and(active, first_ref[i] == 1))
    def _():
        ws = wslot_ref[i]
        for c in weight_copies(be_ref[i], ws):
            c.wait()
        for g in range(n_groups):
            cols = pl.ds(g * MXU_DIM, MXU_DIM)
            wgu_bf[:, cols] = _dot(wgu_f32[ws, :, cols].astype(BF16), perm_ref[...]).astype(BF16)
        wd_bf[...] = wd_f32[ws].astype(BF16)

        @pl.when(nxt_ref[i] >= 0)
        def _():
            for c in weight_copies(nxt_ref[i], 1 - ws):
                c.start()

    def block(m):
        k = d // (2 * LANES)
        for j in range(k):
            lo, hi = _load_packed_tile(x_ref, (), m, k, j)
            x_bf[0:m, j * LANES:(j + 1) * LANES] = lo.astype(BF16)
            x_bf[0:m, (k + j) * LANES:(k + j + 1) * LANES] = hi.astype(BF16)
        for g in range(n_groups):
            gu = (_dot(x_bf[0:m, :], wgu_bf[:, g * MXU_DIM:(g + 1) * MXU_DIM])
                  + bgu_ref[0, :, g * MXU_DIM:(g + 1) * MXU_DIM])
            gate = jnp.minimum(gu[:, :LANES], SWIGLU_LIMIT)
            lin = jnp.clip(gu[:, LANES:], -SWIGLU_LIMIT, SWIGLU_LIMIT)
            act_bf[0:m, g * LANES:(g + 1) * LANES] = (
                gate * jax.nn.sigmoid(SWIGLU_ALPHA * gate) * (lin + 1.0)).astype(BF16)
        _store_packed_rows(y_ref, (), _dot(act_bf[0:m, :], wd_bf[...]) + bd_ref[0])
        if m < bm:
            y_ref[pl.ds(m * k, (bm - m) * k), :] = jnp.zeros(((bm - m) * k, LANES), y_ref.dtype)

    quarter = bm // MOE_QUARTERS
    for nq in range(1, MOE_QUARTERS + 1):
        @pl.when(jnp.logical_and(active, quarters_ref[i] == nq))
        def _(nq=nq):
            block(nq * quarter)

    @pl.when(i >= n_used)
    def _():
        y_ref[...] = jnp.zeros_like(y_ref)


def _deinterleave_perm():
    src = np.arange(MXU_DIM)
    dst = np.where(src % 2 == 0, src // 2, LANES + src // 2)
    p = np.zeros((MXU_DIM, MXU_DIM), np.float32)
    p[src, dst] = 1.0
    return jnp.asarray(p, dtype=BF16)


def _moe(plan, xs, w_gate_up, bgu_p, w_down, b_down):
    n_exp, d, d_gu = w_gate_up.shape
    d_ff = w_down.shape[1]
    bm = MOE_BLOCK
    nblk = plan[0].shape[0]
    slab = bm * d // (2 * LANES)
    by_expert = lambda i, be, *_: (be[i], 0, 0)
    grid_spec = pltpu.PrefetchScalarGridSpec(
        num_scalar_prefetch=6,
        grid=(nblk,),
        in_specs=[
            pl.BlockSpec((slab, LANES), lambda i, *_: (i, 0)),
            pl.BlockSpec(memory_space=pl.ANY),
            pl.BlockSpec(memory_space=pl.ANY),
            pl.BlockSpec((1, 1, d_gu), by_expert),
            pl.BlockSpec((1, 1, d), by_expert),
            pl.BlockSpec((MXU_DIM, MXU_DIM), lambda i, *_: (0, 0)),
        ],
        out_specs=pl.BlockSpec((slab, LANES), lambda i, *_: (i, 0)),
        scratch_shapes=[pltpu.VMEM((2, d, d_gu), F32),
                        pltpu.VMEM((2, d_ff, d), F32),
                        pltpu.SemaphoreType.DMA((2, 2)),
                        pltpu.VMEM((d, d_gu), BF16),
                        pltpu.VMEM((d_ff, d), BF16),
                        pltpu.VMEM((bm, d), BF16),
                        pltpu.VMEM((bm, d_ff), BF16)],
    )
    return pl.pallas_call(
        _moe_kernel,
        grid_spec=grid_spec,
        out_shape=jax.ShapeDtypeStruct((nblk * slab, LANES), U32),
        compiler_params=_cparams(("arbitrary",)),
        name="moe_experts",
    )(*plan, xs, w_gate_up, w_down, bgu_p, b_down, _deinterleave_perm())


SC_CORES = 2
SC_SUBCORES = 16
SC_CHUNK = 64


def _sc_mesh():
    return plsc.VectorSubcoreMesh(core_axis_name="c", subcore_axis_name="s")


def _sc_dispatch(rows3, pos_chunks, n_slots):
    n, r, _ = rows3.shape
    workers = SC_CORES * SC_SUBCORES
    per_worker = n // SC_CHUNK // workers

    def body(rows_hbm, pos_hbm, out_hbm, idx_v, rows_v, sem):
        wid = lax.axis_index("s") * SC_CORES + lax.axis_index("c")

        @pl.loop(0, per_worker)
        def _(c):
            chunk = wid * per_worker + c
            pltpu.sync_copy(pos_hbm.at[chunk], idx_v)
            pltpu.sync_copy(rows_hbm.at[pl.ds(chunk * SC_CHUNK, SC_CHUNK)], rows_v)
            copies = [pltpu.async_copy(rows_v, out_hbm.at[idx_v.at[k]], sem) for k in range(TOP_K)]
            for cp in copies:
                cp.wait()

    return pl.kernel(
        body,
        out_type=jax.ShapeDtypeStruct((n_slots, r, LANES), rows3.dtype),
        mesh=_sc_mesh(),
        scratch_types=[pltpu.VMEM((TOP_K, SC_CHUNK), I32),
                       pltpu.VMEM((SC_CHUNK, r, LANES), rows3.dtype),
                       pltpu.SemaphoreType.DMA],
        name="sc_dispatch",
    )(rows3, pos_chunks)


def _sc_gather(rows3, pos_chunks, n):
    _, r, _ = rows3.shape
    workers = SC_CORES * SC_SUBCORES
    per_worker = n // SC_CHUNK // workers

    def body(rows_hbm, pos_hbm, out_hbm, idx_v, buf_a, buf_b, sem_a, sem_b):
        wid = lax.axis_index("s") * SC_CORES + lax.axis_index("c")
        bufs, sems = (buf_a, buf_b), (sem_a, sem_b)

        @pl.loop(0, per_worker)
        def _(c):
            chunk = wid * per_worker + c
            pltpu.sync_copy(pos_hbm.at[chunk], idx_v)
            gathers = [None] * TOP_K
            gathers[0] = pltpu.async_copy(rows_hbm.at[idx_v.at[0]], bufs[0], sems[0])
            for k in range(TOP_K):
                if k + 1 < TOP_K:
                    nxt = (k + 1) % 2
                    gathers[k + 1] = pltpu.async_copy(rows_hbm.at[idx_v.at[k + 1]], bufs[nxt], sems[nxt])
                gathers[k].wait()
                pltpu.sync_copy(bufs[k % 2], out_hbm.at[k, pl.ds(chunk * SC_CHUNK, SC_CHUNK)])

    return pl.kernel(
        body,
        out_type=jax.ShapeDtypeStruct((TOP_K, n, r, LANES), rows3.dtype),
        mesh=_sc_mesh(),
        scratch_types=[pltpu.VMEM((TOP_K, SC_CHUNK), I32),
                       pltpu.VMEM((SC_CHUNK, r, LANES), rows3.dtype),
                       pltpu.VMEM((SC_CHUNK, r, LANES), rows3.dtype),
                       pltpu.SemaphoreType.DMA,
                       pltpu.SemaphoreType.DMA],
        name="sc_gather",
    )(rows3, pos_chunks)


COMBINE_TILE = 512


def _combine_kernel(h_ref, gate_ref, y_ref, o_ref):
    tc, d = h_ref.shape
    kt = d // (2 * LANES)
    gates = gate_ref[...]
    for j in range(kt):
        lo_acc = h_ref[:, j * LANES:(j + 1) * LANES]
        hi_acc = h_ref[:, (kt + j) * LANES:(kt + j + 1) * LANES]
        for k in range(TOP_K):
            lo, hi = _load_packed_tile(y_ref, (k,), tc, kt, j)
            g = gates[:, k:k + 1]
            lo_acc = lo_acc + g * lo
            hi_acc = hi_acc + g * hi
        o_ref[:, j * LANES:(j + 1) * LANES] = lo_acc
        o_ref[:, (kt + j) * LANES:(kt + j + 1) * LANES] = hi_acc


def _combine(h, gates, y4):
    n, d = h.shape
    tc = COMBINE_TILE
    slab = tc * d // (2 * LANES)
    row = lambda i: (i, 0)
    return pl.pallas_call(
        _combine_kernel,
        grid=(n // tc,),
        in_specs=[pl.BlockSpec((tc, d), row),
                  pl.BlockSpec((tc, LANES), row),
                  pl.BlockSpec((TOP_K, slab, LANES), lambda i: (0, i, 0))],
        out_specs=pl.BlockSpec((tc, d), row),
        out_shape=jax.ShapeDtypeStruct((n, d), F32),
        input_output_aliases={0: 0},
        compiler_params=_cparams(("parallel",)),
        name="moe_combine",
    )(h, gates, y4)


def _layer(h3, layer, norm_mix_g, w_in, conv_w, conv_b, conv_ln_g, conv_ln_b, q_norm_g, k_norm_g,
           lambda_qk, subln_g, rel_bias, w_out, norm_ffn_g, router_w, router_b,
           w_gate_up, b_gate_up, w_down, b_down):
    b, s, d = h3.shape
    n = b * s
    d_conv = conv_w.shape[-1]
    d_attn = N_HEADS * VALUE_DIM
    lambda_init = 0.8 - 0.6 * math.exp(-0.3 * layer)
    x2 = h3.reshape(n, d)

    u, q, k, v = _in_proj(x2, norm_mix_g.reshape(1, d), w_in.astype(BF16), d_conv, d_attn, tm=512)
    conv_o = _conv(u.reshape(b, s, d_conv), conv_w, conv_b.reshape(1, d_conv),
                   conv_ln_g.reshape(1, d_conv), conv_ln_b.reshape(1, d_conv))
    attn_o = _attention(q.reshape(b, s, d_attn), k.reshape(b, s, d_attn), v.reshape(b, s, d_attn),
                        rel_bias, q_norm_g.reshape(1, VALUE_DIM), k_norm_g.reshape(1, VALUE_DIM),
                        lambda_qk, subln_g.reshape(1, VALUE_DIM), lambda_init)

    n_exp = router_w.shape[1]
    rw = jnp.zeros((d, LANES), BF16).at[:, :n_exp].set(router_w.astype(BF16))
    rb = jnp.full((1, LANES), NEG_INF, F32).at[0, :n_exp].set(router_b)
    w_out_bf = w_out.astype(BF16)
    hres, hn, idx, gates, rank, cnt = _out_proj(
        x2, conv_o.reshape(n, d_conv), attn_o.reshape(n, d_attn),
        w_out_bf[:d_conv], w_out_bf[d_conv:], norm_ffn_g.reshape(1, d), rw, rb, tm=512)

    bm = MOE_BLOCK
    nblk = n * TOP_K // bm + n_exp
    counts = cnt[0, :n_exp].astype(I32)
    padded = (counts + bm - 1) // bm * bm
    eid = jnp.arange(n_exp, dtype=I32)
    pad_end = jnp.sum(jnp.where(eid[None, :] <= eid[:, None], padded[None, :], 0), axis=1)
    pad_start = pad_end - padded
    start_of = jnp.sum(jnp.where(idx[None, :TOP_K] == eid[:, None, None],
                                 pad_start[:, None, None], 0), axis=0)
    pos = start_of + rank[:TOP_K]
    pos_chunks = pos.reshape(TOP_K, n // SC_CHUNK, SC_CHUNK).transpose(1, 0, 2)
    block_start = jnp.arange(nblk, dtype=I32) * bm
    block_expert = jnp.minimum(jnp.sum(block_start[:, None] >= pad_end[None, :], axis=1),
                               n_exp - 1).astype(I32)
    n_used = (pad_end[-1:] // bm).astype(I32)
    blk = jnp.arange(nblk, dtype=I32)
    first = jnp.logical_and(
        jnp.concatenate([jnp.ones((1,), bool), block_expert[1:] != block_expert[:-1]]),
        blk < n_used[0]).astype(I32)
    opened = jnp.sum(jnp.where(blk[None, :] <= blk[:, None], first[None, :], 0), axis=1)
    wslot = ((opened - 1) % 2).astype(I32)
    later_used = jnp.logical_and(eid[None, :] > eid[:, None], counts[None, :] > 0)
    next_used = jnp.min(jnp.where(later_used, eid[None, :], n_exp), axis=1)
    next_used = jnp.where(next_used == n_exp, -1, next_used).astype(I32)
    next_of_block = jnp.sum(jnp.where(block_expert[:, None] == eid[None, :], next_used[None, :], 0),
                            axis=1).astype(I32)
    on_expert = block_expert[:, None] == eid[None, :]
    rows_in_block = jnp.clip(
        jnp.sum(jnp.where(on_expert, (counts + pad_start)[None, :], 0), axis=1) - block_start, 0, bm)
    quarter = bm // MOE_QUARTERS
    quarters = ((rows_in_block + quarter - 1) // quarter).astype(I32)
    plan = (block_expert, first, wslot, next_of_block, quarters, n_used)

    d_ff = w_down.shape[1]
    bgu_p = b_gate_up.reshape(n_exp, d_ff // LANES, LANES, 2).transpose(0, 1, 3, 2).reshape(n_exp, 1, 2 * d_ff)
    r = d // (2 * LANES)
    xs = _sc_dispatch(hn.reshape(n, r, LANES), pos_chunks, nblk * bm)
    yb = _moe(plan, xs.reshape(nblk * bm * r, LANES), w_gate_up, bgu_p, w_down,
              b_down.reshape(n_exp, 1, d))
    y4 = _sc_gather(yb.reshape(nblk * bm, r, LANES), pos_chunks, n)
    out = _combine(hres, gates, y4.reshape(TOP_K, n * r, LANES))
    return out.reshape(b, s, d)


def kernel(x, norm_mix_g, w_in, conv_w, conv_b, conv_ln_g, conv_ln_b, q_norm_g, k_norm_g, lambda_qk,
           subln_g, rel_bias, w_out, norm_ffn_g, router_w, router_b, w_gate_up, b_gate_up, w_down,
           b_down):
    h = x
    for layer in range(norm_mix_g.shape[0]):
        h = _layer(h, layer, norm_mix_g[layer], w_in[layer], conv_w[layer], conv_b[layer],
                   conv_ln_g[layer], conv_ln_b[layer], q_norm_g[layer], k_norm_g[layer],
                   lambda_qk[layer], subln_g[layer], rel_bias, w_out[layer], norm_ffn_g[layer],
                   router_w[layer], router_b[layer], w_gate_up[layer], b_gate_up[layer],
                   w_down[layer], b_down[layer])
    return h
```

```python
import functools
import math

import jax
import jax.numpy as jnp
import numpy as np
from jax import lax
from jax.experimental import pallas as pl
from jax.experimental.pallas import tpu as pltpu
from jax.experimental.pallas import tpu_sc as plsc

F32 = jnp.float32
BF16 = jnp.bfloat16
I32 = jnp.int32
U32 = jnp.uint32

CHUNK = 64
CONV_WIDTH = 31
N_HEADS = 4
HEAD_DIM = 64
VALUE_DIM = 2 * HEAD_DIM
REL_BUCKETS = 32
REL_MAX_DIST = 128
N_EXPERTS = 32
TOP_K = 4
SWIGLU_LIMIT = 7.0
SWIGLU_ALPHA = 1.702
EPS = 1e-5
LOG2E = 1.4426950408889634

LANES = 128
SUBLANES = 8
MXU_DIM = 256
VMEM_LIMIT = 56 * 1024 * 1024

NEG_INF = float("-inf")


def _cparams(sem, vmem=VMEM_LIMIT, flags=None):
    return pltpu.CompilerParams(dimension_semantics=sem, vmem_limit_bytes=vmem, flags=flags)


def _dot(a, b):
    return jnp.dot(a, b, preferred_element_type=F32)


def _dot_nt(a, b):
    return lax.dot_general(a, b, (((1,), (1,)), ((), ())), preferred_element_type=F32)


def _pack_pair(lo, hi):
    lo_bits = lax.bitcast_convert_type(lo.astype(BF16).astype(F32), U32)
    hi_bits = lax.bitcast_convert_type(hi.astype(BF16).astype(F32), U32)
    return (lo_bits >> 16) | hi_bits


def _unpack_pair(w):
    return (lax.bitcast_convert_type(w << 16, F32),
            lax.bitcast_convert_type(w & jnp.uint32(0xFFFF0000), F32))


def _store_packed_rows(ref, lead, x):
    rows, k = x.shape[0], x.shape[1] // (2 * LANES)
    for j in range(k):
        w = _pack_pair(x[:, j * LANES:(j + 1) * LANES], x[:, (k + j) * LANES:(k + j + 1) * LANES])
        ref[(*lead, pl.ds(j, rows, stride=k), slice(None))] = w


def _load_packed_tile(ref, lead, rows, k, j):
    return _unpack_pair(ref[(*lead, pl.ds(j, rows, stride=k), slice(None))])


def _in_proj_kernel(x_ref, g_ref, w_ref, u_ref, q_ref, k_ref, v_ref, *, d_conv, d_attn):
    x = x_ref[...]
    ms = jnp.mean(x * x, axis=-1, keepdims=True)
    y = (x * lax.rsqrt(ms + EPS) * g_ref[...]).astype(BF16)
    proj = _dot(y, w_ref[...])
    a = proj[:, :d_conv]
    g = proj[:, d_conv:2 * d_conv]
    u_ref[...] = a * jax.nn.sigmoid(g)
    o = 2 * d_conv
    q_ref[...] = proj[:, o:o + d_attn]
    k_ref[...] = proj[:, o + d_attn:o + 2 * d_attn]
    v_ref[...] = proj[:, o + 2 * d_attn:o + 3 * d_attn].astype(BF16)


def _in_proj(x2, g, w_bf, d_conv, d_attn, tm):
    n, d = x2.shape
    d_in = w_bf.shape[1]
    row = lambda i: (i, 0)
    fixed = lambda i: (0, 0)
    return pl.pallas_call(
        functools.partial(_in_proj_kernel, d_conv=d_conv, d_attn=d_attn),
        grid=(n // tm,),
        in_specs=[pl.BlockSpec((tm, d), row),
                  pl.BlockSpec((1, d), fixed),
                  pl.BlockSpec((d, d_in), fixed)],
        out_specs=[pl.BlockSpec((tm, d_conv), row),
                   pl.BlockSpec((tm, d_attn), row),
                   pl.BlockSpec((tm, d_attn), row),
                   pl.BlockSpec((tm, d_attn), row)],
        out_shape=[jax.ShapeDtypeStruct((n, d_conv), F32),
                   jax.ShapeDtypeStruct((n, d_attn), F32),
                   jax.ShapeDtypeStruct((n, d_attn), F32),
                   jax.ShapeDtypeStruct((n, d_attn), BF16)],
        compiler_params=_cparams(("parallel",)),
        name="in_proj",
    )(x2, g, w_bf)


CONV_PAD = 32
CONV_SEQ_TILE = 1024
CONV_TILE = 128
CONV_NORM_TILE = 256


def _conv_kernel(u_ref, prev_ref, w_ref, cb_ref, lg_ref, lb_ref, o_ref, sh_ref, y_ref):
    ts, c = u_ref.shape[1], u_ref.shape[2]
    plen = ts + CONV_PAD
    hist = prev_ref[0]
    hist = jnp.where(pl.program_id(1) > 0, hist, jnp.zeros_like(hist))
    sh_ref[0, pl.ds(0, CONV_PAD), :] = hist
    sh_ref[0, pl.ds(CONV_PAD, ts), :] = u_ref[0]
    sh_ref[0, pl.ds(plen, SUBLANES), :] = jnp.zeros((SUBLANES, c), F32)

    bt = CONV_PAD

    def shift(i, _):
        p0 = pl.multiple_of(i * bt, bt)
        win = sh_ref[0, pl.ds(p0, bt + SUBLANES), :]
        for r in range(1, SUBLANES):
            sh_ref[r, pl.ds(p0, bt), :] = win[r:r + bt]
        return 0

    lax.fori_loop(0, plen // bt, shift, 0)

    off0 = CONV_PAD - (CONV_WIDTH - 1)

    by_shift = {}
    for j in range(CONV_WIDTH):
        a, r = divmod(off0 + j, SUBLANES)
        by_shift.setdefault(r, []).append((a, j))

    for lt in range(c // LANES):
        lanes = slice(lt * LANES, (lt + 1) * LANES)
        taps = [w_ref[pl.ds(j, 1), lanes] for j in range(CONV_WIDTH)]

        def tap_body(i, _, lanes=lanes, taps=taps):
            t0 = pl.multiple_of(i * CONV_TILE, CONV_TILE)
            acc = jnp.zeros((CONV_TILE, LANES), F32)
            for r, group in by_shift.items():
                a_lo = min(a for a, _ in group)
                a_hi = max(a for a, _ in group)
                rows = CONV_TILE + (a_hi - a_lo) * SUBLANES
                start = pl.multiple_of(t0 + a_lo * SUBLANES, SUBLANES)
                win = sh_ref[r, pl.ds(start, rows), lanes]
                for a, j in group:
                    lo = (a - a_lo) * SUBLANES
                    acc = acc + win[lo:lo + CONV_TILE] * taps[j]
            y_ref[pl.ds(t0, CONV_TILE), lanes] = acc
            return 0

        lax.fori_loop(0, ts // CONV_TILE, tap_body, 0)

    def norm_body(i, _):
        t0 = pl.multiple_of(i * CONV_NORM_TILE, CONV_NORM_TILE)
        y = y_ref[pl.ds(t0, CONV_NORM_TILE), :] + cb_ref[...]
        mu = jnp.mean(y, axis=-1, keepdims=True)
        yc = y - mu
        var = jnp.mean(yc * yc, axis=-1, keepdims=True)
        z = yc * lax.rsqrt(var + EPS) * lg_ref[...] + lb_ref[...]
        o_ref[0, pl.ds(t0, CONV_NORM_TILE), :] = (z * jax.nn.sigmoid(z)).astype(o_ref.dtype)
        return 0

    lax.fori_loop(0, ts // CONV_NORM_TILE, norm_body, 0)


def _conv(u3, conv_w, conv_b, ln_g, ln_b):
    b, s, c = u3.shape
    ts = min(CONV_SEQ_TILE, s)
    hist_per_tile = ts // CONV_PAD
    fixed = lambda i, j: (0, 0)
    return pl.pallas_call(
        _conv_kernel,
        grid=(b, s // ts),
        in_specs=[pl.BlockSpec((1, ts, c), lambda i, j: (i, j, 0)),
                  pl.BlockSpec((1, CONV_PAD, c),
                               lambda i, j: (i, jnp.maximum(j * hist_per_tile - 1, 0), 0)),
                  pl.BlockSpec((CONV_WIDTH, c), fixed),
                  pl.BlockSpec((1, c), fixed),
                  pl.BlockSpec((1, c), fixed),
                  pl.BlockSpec((1, c), fixed)],
        out_specs=pl.BlockSpec((1, ts, c), lambda i, j: (i, j, 0)),
        out_shape=jax.ShapeDtypeStruct((b, s, c), BF16),
        scratch_shapes=[pltpu.VMEM((SUBLANES, ts + CONV_PAD + SUBLANES, c), F32),
                        pltpu.VMEM((ts, c), F32)],
        compiler_params=_cparams(("parallel", "parallel")),
        name="conv_mixer",
    )(u3, u3, conv_w, conv_b, ln_g, ln_b)


ATT_TILE = 256
ATT_SEQS = 2
FAR_BUCKET = REL_BUCKETS // 2 - 1


def _t5_bucket(rel):
    nb = REL_BUCKETS // 2
    max_exact = nb // 2
    ret = jnp.where(rel > 0, nb, 0)
    n = jnp.abs(rel)
    nf = jnp.maximum(n, 1).astype(jnp.float32)
    large = max_exact + (jnp.log(nf / max_exact) / math.log(REL_MAX_DIST / max_exact)
                         * (nb - max_exact)).astype(jnp.int32)
    large = jnp.minimum(large, nb - 1)
    return ret + jnp.where(n < max_exact, n, large)


def _near_buckets():
    assert ATT_TILE >= REL_MAX_DIST and ATT_TILE % CHUNK == 0
    qpos = jnp.arange(ATT_TILE, dtype=I32)[:, None]
    kpos = jnp.arange(ATT_TILE, dtype=I32)[None, :]
    prev = _t5_bucket(kpos - ATT_TILE - qpos)
    diag = _t5_bucket(kpos - qpos)
    diag = jnp.where(kpos // CHUNK <= qpos // CHUNK, diag, -1)
    return jnp.stack([prev, diag]).astype(I32)


def _attn_kernel(tab_ref, q_ref, k_ref, v_ref, bkt_ref, bd_ref, qg_ref, kg_ref, lqk_ref, sg_ref, o_ref,
                 qz_ref, kn_ref, v1_ref, bias_ref, *, seq, lambda_init):
    h = pl.program_id(0)
    t = ATT_TILE
    n_tiles = seq // t
    lane = lax.broadcasted_iota(I32, (1, VALUE_DIM), 1)
    first = lane < HEAD_DIM

    @pl.when(pl.program_id(1) == 0)
    def _():
        far = tab_ref[FAR_BUCKET, h]
        for d in range(2):
            bkt = bkt_ref[d]
            tile = jnp.full((t, t), NEG_INF, F32)
            for b in range(REL_BUCKETS):
                tile = jnp.where(bkt == b, (tab_ref[b, h] - far) * LOG2E, tile)
            bias_ref[d] = tile

    def half_norm(x, g):
        x2 = x * x
        hi = x2.astype(BF16)
        lo = (x2 - hi.astype(F32)).astype(BF16)
        ms = (_dot(hi, bd_ref[...]) + _dot(lo, bd_ref[...])) * (1.0 / HEAD_DIM)
        return x * lax.rsqrt(ms + EPS) * g

    q_gain = qg_ref[...] * (HEAD_DIM ** -0.5 * LOG2E)
    qg_maps = (jnp.where(first, q_gain, 0.0), jnp.where(first, 0.0, q_gain))

    n_seq = q_ref.shape[0]

    ones_col = jnp.broadcast_to(jnp.where(lane == 0, 1.0, 0.0).astype(BF16), (t, VALUE_DIM))
    for i in range(n_tiles):
        r0 = i * t
        for b in range(n_seq):
            qn = half_norm(q_ref[b, pl.ds(r0, t), :], 1.0)
            qz_ref[b, 0, pl.ds(r0, t), :] = (qn * qg_maps[0]).astype(BF16)
            qz_ref[b, 1, pl.ds(r0, t), :] = (qn * qg_maps[1]).astype(BF16)
            kn_ref[b, pl.ds(r0, t), :] = half_norm(k_ref[b, pl.ds(r0, t), :], kg_ref[...]).astype(BF16)
            v1_ref[b, pl.ds(r0, t), :] = jnp.concatenate([v_ref[b, pl.ds(r0, t), :], ones_col], axis=1)

    lqk = lqk_ref[...]
    lam = (jnp.exp(jnp.sum(lqk[0:1] * lqk[1:2], axis=-1, keepdims=True))
           - jnp.exp(jnp.sum(lqk[2:3] * lqk[3:4], axis=-1, keepdims=True)) + lambda_init)

    for i in range(n_tiles):
        q0, kend = i * t, (i + 1) * t
        for b in range(n_seq):
            keys = kn_ref[b, 0:kend, :]
            vals = v1_ref[b, 0:kend, :]
            maps = []
            for m in range(2):
                s = _dot_nt(qz_ref[b, m, q0:q0 + t, :], keys)
                parts = [s[:, kend - t:] + bias_ref[1]]
                if i >= 1:
                    parts.insert(0, s[:, kend - 2 * t:kend - t] + bias_ref[0])
                if i >= 2:
                    parts.insert(0, s[:, :kend - 2 * t])
                s = jnp.concatenate(parts, axis=1) if len(parts) > 1 else parts[0]
                p = jnp.exp2(s - jnp.max(s, axis=-1, keepdims=True))
                pv = _dot(p.astype(BF16), vals)
                maps.append(pv[:, :VALUE_DIM] / pv[:, VALUE_DIM:VALUE_DIM + 1])
            o = maps[0] - lam * maps[1]
            ms = jnp.mean(o * o, axis=-1, keepdims=True)
            o = o * lax.rsqrt(ms + EPS) * sg_ref[...] * (1.0 - lambda_init)
            o_ref[b, q0:q0 + t, :] = o.astype(o_ref.dtype)


def _attention(q3, k3, v3, rel_bias, q_g, k_g, lam_qk, subln_g, lambda_init):
    b, s, _ = q3.shape
    t = ATT_TILE
    nb = ATT_SEQS if b % ATT_SEQS == 0 else 1
    head = lambda j, i: (i, 0, j)
    fixed2 = lambda j, i: (0, 0)
    fixed3 = lambda j, i: (0, 0, 0)
    half = np.arange(VALUE_DIM) // HEAD_DIM
    blockdiag = jnp.asarray(half[:, None] == half[None, :], dtype=BF16)
    return pl.pallas_call(
        functools.partial(_attn_kernel, seq=s, lambda_init=lambda_init),
        grid=(N_HEADS, b // nb),
        in_specs=[pl.BlockSpec(memory_space=pltpu.SMEM),
                  pl.BlockSpec((nb, s, VALUE_DIM), head),
                  pl.BlockSpec((nb, s, VALUE_DIM), head),
                  pl.BlockSpec((nb, s, VALUE_DIM), head),
                  pl.BlockSpec((2, t, t), fixed3),
                  pl.BlockSpec((VALUE_DIM, VALUE_DIM), fixed2),
                  pl.BlockSpec((1, VALUE_DIM), fixed2),
                  pl.BlockSpec((1, VALUE_DIM), fixed2),
                  pl.BlockSpec((4, HEAD_DIM), fixed2),
                  pl.BlockSpec((1, VALUE_DIM), fixed2)],
        out_specs=pl.BlockSpec((nb, s, VALUE_DIM), head),
        out_shape=jax.ShapeDtypeStruct((b, s, N_HEADS * VALUE_DIM), BF16),
        scratch_shapes=[pltpu.VMEM((nb, 2, s, VALUE_DIM), BF16),
                        pltpu.VMEM((nb, s, VALUE_DIM), BF16),
                        pltpu.VMEM((nb, s, 2 * VALUE_DIM), BF16),
                        pltpu.VMEM((2, t, t), F32)],
        compiler_params=_cparams(("arbitrary", "arbitrary")),
        name="diff_attn",
    )(rel_bias, q3, k3, v3, _near_buckets(), blockdiag, q_g, k_g, lam_qk, subln_g)


def _out_proj_kernel(x_ref, c_ref, a_ref, wc_ref, wa_ref, g_ref, rw_ref, rb_ref, tri_ref,
                     h_ref, hn_ref, idx_ref, gate_ref, rank_ref, cnt_ref, carry_ref):
    @pl.when(pl.program_id(0) == 0)
    def _():
        carry_ref[...] = jnp.zeros_like(carry_ref)

    h = x_ref[...] + _dot(c_ref[...], wc_ref[...]) + _dot(a_ref[...], wa_ref[...])
    h_ref[...] = h
    ms = jnp.mean(h * h, axis=-1, keepdims=True)
    hn = h * lax.rsqrt(ms + EPS) * g_ref[...]
    _store_packed_rows(hn_ref, (), hn)

    logits = _dot(hn.astype(BF16), rw_ref[...]) + rb_ref[...]

    tm = logits.shape[0]
    lane = lax.broadcasted_iota(I32, (tm, LANES), 1).astype(F32)
    work = logits
    vals, idxs = [], []
    for _ in range(TOP_K):
        mx = jnp.max(work, axis=-1, keepdims=True)
        ix = jnp.min(jnp.where(work == mx, lane, float(LANES)), axis=-1, keepdims=True)
        vals.append(mx)
        idxs.append(ix)
        work = jnp.where(lane == ix, NEG_INF, work)
    exps = [jnp.exp(v - vals[0]) for v in vals]
    denom = exps[0]
    for e in exps[1:]:
        denom = denom + e

    sel = jnp.zeros((tm, LANES), F32)
    for ix in idxs:
        sel = sel + jnp.where(lane == ix, 1.0, 0.0)
    rank = _dot(tri_ref[...], sel.astype(BF16)) + carry_ref[...]
    carry_ref[...] = carry_ref[...] + jnp.sum(sel, axis=0, keepdims=True)
    cnt_ref[...] = carry_ref[...]

    idx_out = jnp.zeros((tm, LANES), F32)
    gate_out = jnp.zeros((tm, LANES), F32)
    rank_out = jnp.zeros((tm, LANES), F32)
    for k in range(TOP_K):
        rk = jnp.sum(jnp.where(lane == idxs[k], rank, 0.0), axis=-1, keepdims=True)
        idx_out = jnp.where(lane == k, idxs[k], idx_out)
        gate_out = jnp.where(lane == k, exps[k] / denom, gate_out)
        rank_out = jnp.where(lane == k, rk, rank_out)
    idx_ref[...] = jnp.transpose(idx_out)[:SUBLANES].astype(I32)
    rank_ref[...] = jnp.transpose(rank_out)[:SUBLANES].astype(I32)
    gate_ref[...] = gate_out


def _out_proj(x2, conv_o, attn_o, wc, wa, g, rw, rb, tm):
    n, d = x2.shape
    dc, da = conv_o.shape[1], attn_o.shape[1]
    row = lambda i: (i, 0)
    fixed = lambda i: (0, 0)
    tri = jnp.tril(jnp.ones((tm, tm), F32), -1).astype(BF16)
    return pl.pallas_call(
        _out_proj_kernel,
        grid=(n // tm,),
        in_specs=[pl.BlockSpec((tm, d), row),
                  pl.BlockSpec((tm, dc), row),
                  pl.BlockSpec((tm, da), row),
                  pl.BlockSpec((dc, d), fixed),
                  pl.BlockSpec((da, d), fixed),
                  pl.BlockSpec((1, d), fixed),
                  pl.BlockSpec((d, LANES), fixed),
                  pl.BlockSpec((1, LANES), fixed),
                  pl.BlockSpec((tm, tm), fixed)],
        out_specs=[pl.BlockSpec((tm, d), row),
                   pl.BlockSpec((tm * d // (2 * LANES), LANES), row),
                   pl.BlockSpec((SUBLANES, tm), lambda i: (0, i)),
                   pl.BlockSpec((tm, LANES), row),
                   pl.BlockSpec((SUBLANES, tm), lambda i: (0, i)),
                   pl.BlockSpec((1, LANES), fixed)],
        out_shape=[jax.ShapeDtypeStruct((n, d), F32),
                   jax.ShapeDtypeStruct((n * d // (2 * LANES), LANES), U32),
                   jax.ShapeDtypeStruct((SUBLANES, n), I32),
                   jax.ShapeDtypeStruct((n, LANES), F32),
                   jax.ShapeDtypeStruct((SUBLANES, n), I32),
                   jax.ShapeDtypeStruct((1, LANES), F32)],
        scratch_shapes=[pltpu.VMEM((1, LANES), F32)],
        compiler_params=_cparams(("arbitrary",)),
        name="out_proj_router",
    )(x2, conv_o, attn_o, wc, wa, g, rw, rb, tri)


MOE_BLOCK = 512
MOE_QUARTERS = 4


def _moe_kernel(be_ref, first_ref, wslot_ref, nxt_ref, quarters_ref, nused_ref,
                x_ref, wgu_hbm, wd_hbm, bgu_ref, bd_ref, perm_ref,
                y_ref, wgu_f32, wd_f32, wsem, wgu_bf, wd_bf, x_bf, act_bf):
    i = pl.program_id(0)
    bm = MOE_BLOCK
    n_used = nused_ref[0]
    d = wd_bf.shape[1]
    d_ff = wd_bf.shape[0]
    n_groups = d_ff // LANES

    def weight_copies(e, slot):
        return (pltpu.make_async_copy(wgu_hbm.at[e], wgu_f32.at[slot], wsem.at[0, slot]),
                pltpu.make_async_copy(wd_hbm.at[e], wd_f32.at[slot], wsem.at[1, slot]))

    @pl.when(i == 0)
    def _():
        for c in weight_copies(be_ref[0], 0):
            c.start()

    active = i < n_used

    @pl.when(jnp.logical_and(active, first_ref[i] == 1))
    def _():
        ws = wslot_ref[i]
        for c in weight_copies(be_ref[i], ws):
            c.wait()
        for g in range(n_groups):
            cols = pl.ds(g * MXU_DIM, MXU_DIM)
            wgu_bf[:, cols] = _dot(wgu_f32[ws, :, cols].astype(BF16), perm_ref[...]).astype(BF16)
        wd_bf[...] = wd_f32[ws].astype(BF16)

        @pl.when(nxt_ref[i] >= 0)
        def _():
            for c in weight_copies(nxt_ref[i], 1 - ws):
                c.start()

    def block(m):
        k = d // (2 * LANES)
        for j in range(k):
            lo, hi = _load_packed_tile(x_ref, (), m, k, j)
            x_bf[0:m, j * LANES:(j + 1) * LANES] = lo.astype(BF16)
            x_bf[0:m, (k + j) * LANES:(k + j + 1) * LANES] = hi.astype(BF16)
        for g in range(n_groups):
            gu = (_dot(x_bf[0:m, :], wgu_bf[:, g * MXU_DIM:(g + 1) * MXU_DIM])
                  + bgu_ref[0, :, g * MXU_DIM:(g + 1) * MXU_DIM])
            gate = jnp.minimum(gu[:, :LANES], SWIGLU_LIMIT)
            lin = jnp.clip(gu[:, LANES:], -SWIGLU_LIMIT, SWIGLU_LIMIT)
            act_bf[0:m, g * LANES:(g + 1) * LANES] = (
                gate * jax.nn.sigmoid(SWIGLU_ALPHA * gate) * (lin + 1.0)).astype(BF16)
        _store_packed_rows(y_ref, (), _dot(act_bf[0:m, :], wd_bf[...]) + bd_ref[0])
        if m < bm:
            y_ref[pl.ds(m * k, (bm - m) * k), :] = jnp.zeros(((bm - m) * k, LANES), y_ref.dtype)

    quarter = bm // MOE_QUARTERS
    for nq in range(1, MOE_QUARTERS + 1):
        @pl.when(jnp.logical_and(active, quarters_ref[i] == nq))
        def _(nq=nq):
            block(nq * quarter)

    @pl.when(i >= n_used)
    def _():
        y_ref[...] = jnp.zeros_like(y_ref)


def _deinterleave_perm():
    src = np.arange(MXU_DIM)
    dst = np.where(src % 2 == 0, src // 2, LANES + src // 2)
    p = np.zeros((MXU_DIM, MXU_DIM), np.float32)
    p[src, dst] = 1.0
    return jnp.asarray(p, dtype=BF16)


def _moe(plan, xs, w_gate_up, bgu_p, w_down, b_down):
    n_exp, d, d_gu = w_gate_up.shape
    d_ff = w_down.shape[1]
    bm = MOE_BLOCK
    nblk = plan[0].shape[0]
    slab = bm * d // (2 * LANES)
    by_expert = lambda i, be, *_: (be[i], 0, 0)
    grid_spec = pltpu.PrefetchScalarGridSpec(
        num_scalar_prefetch=6,
        grid=(nblk,),
        in_specs=[
            pl.BlockSpec((slab, LANES), lambda i, *_: (i, 0)),
            pl.BlockSpec(memory_space=pl.ANY),
            pl.BlockSpec(memory_space=pl.ANY),
            pl.BlockSpec((1, 1, d_gu), by_expert),
            pl.BlockSpec((1, 1, d), by_expert),
            pl.BlockSpec((MXU_DIM, MXU_DIM), lambda i, *_: (0, 0)),
        ],
        out_specs=pl.BlockSpec((slab, LANES), lambda i, *_: (i, 0)),
        scratch_shapes=[pltpu.VMEM((2, d, d_gu), F32),
                        pltpu.VMEM((2, d_ff, d), F32),
                        pltpu.SemaphoreType.DMA((2, 2)),
                        pltpu.VMEM((d, d_gu), BF16),
                        pltpu.VMEM((d_ff, d), BF16),
                        pltpu.VMEM((bm, d), BF16),
                        pltpu.VMEM((bm, d_ff), BF16)],
    )
    return pl.pallas_call(
        _moe_kernel,
        grid_spec=grid_spec,
        out_shape=jax.ShapeDtypeStruct((nblk * slab, LANES), U32),
        compiler_params=_cparams(("arbitrary",)),
        name="moe_experts",
    )(*plan, xs, w_gate_up, w_down, bgu_p, b_down, _deinterleave_perm())


SC_CORES = 2
SC_SUBCORES = 16
SC_CHUNK = 64


def _sc_mesh():
    return plsc.VectorSubcoreMesh(core_axis_name="c", subcore_axis_name="s")


def _sc_dispatch(rows3, pos_chunks, n_slots):
    n, r, _ = rows3.shape
    workers = SC_CORES * SC_SUBCORES
    per_worker = n // SC_CHUNK // workers

    assert per_worker % 2 == 0

    def body(rows_hbm, pos_hbm, out_hbm, idx_v, buf_a, buf_b, rsem_a, rsem_b, wsem):
        wid = lax.axis_index("s") * SC_CORES + lax.axis_index("c")
        base = wid * per_worker
        bufs, rsems = (buf_a, buf_b), (rsem_a, rsem_b)

        def read(chunk, s):
            return pltpu.make_async_copy(rows_hbm.at[pl.ds(chunk * SC_CHUNK, SC_CHUNK)], bufs[s], rsems[s])

        read(base, 0).start()

        @pl.loop(0, per_worker, step=2)
        def _(c):
            for s in range(2):
                chunk = base + c + s
                read(chunk, s).wait()

                @pl.when(c + s + 1 < per_worker)
                def _():
                    read(chunk + 1, 1 - s).start()

                pltpu.sync_copy(pos_hbm.at[chunk], idx_v)
                copies = [pltpu.async_copy(bufs[s], out_hbm.at[idx_v.at[k]], wsem) for k in range(TOP_K)]
                for cp in copies:
                    cp.wait()

    return pl.kernel(
        body,
        out_type=jax.ShapeDtypeStruct((n_slots, r, LANES), rows3.dtype),
        mesh=_sc_mesh(),
        scratch_types=[pltpu.VMEM((TOP_K, SC_CHUNK), I32),
                       pltpu.VMEM((SC_CHUNK, r, LANES), rows3.dtype),
                       pltpu.VMEM((SC_CHUNK, r, LANES), rows3.dtype),
                       pltpu.SemaphoreType.DMA,
                       pltpu.SemaphoreType.DMA,
                       pltpu.SemaphoreType.DMA],
        name="sc_dispatch",
    )(rows3, pos_chunks)


def _sc_gather(rows3, pos_chunks, n):
    _, r, _ = rows3.shape
    workers = SC_CORES * SC_SUBCORES
    per_worker = n // SC_CHUNK // workers

    def body(rows_hbm, pos_hbm, out_hbm, idx_v, buf_a, buf_b, sem_a, sem_b):
        wid = lax.axis_index("s") * SC_CORES + lax.axis_index("c")
        bufs, sems = (buf_a, buf_b), (sem_a, sem_b)

        @pl.loop(0, per_worker)
        def _(c):
            chunk = wid * per_worker + c
            pltpu.sync_copy(pos_hbm.at[chunk], idx_v)
            gathers = [None] * TOP_K
            gathers[0] = pltpu.async_copy(rows_hbm.at[idx_v.at[0]], bufs[0], sems[0])
            for k in range(TOP_K):
                if k + 1 < TOP_K:
                    nxt = (k + 1) % 2
                    gathers[k + 1] = pltpu.async_copy(rows_hbm.at[idx_v.at[k + 1]], bufs[nxt], sems[nxt])
                gathers[k].wait()
                pltpu.sync_copy(bufs[k % 2], out_hbm.at[k, pl.ds(chunk * SC_CHUNK, SC_CHUNK)])

    return pl.kernel(
        body,
        out_type=jax.ShapeDtypeStruct((TOP_K, n, r, LANES), rows3.dtype),
        mesh=_sc_mesh(),
        scratch_types=[pltpu.VMEM((TOP_K, SC_CHUNK), I32),
                       pltpu.VMEM((SC_CHUNK, r, LANES), rows3.dtype),
                       pltpu.VMEM((SC_CHUNK, r, LANES), rows3.dtype),
                       pltpu.SemaphoreType.DMA,
                       pltpu.SemaphoreType.DMA],
        name="sc_gather",
    )(rows3, pos_chunks)


COMBINE_TILE = 512


def _combine_kernel(h_ref, gate_ref, y_ref, o_ref):
    tc, d = h_ref.shape
    kt = d // (2 * LANES)
    gates = gate_ref[...]
    for j in range(kt):
        lo_acc = h_ref[:, j * LANES:(j + 1) * LANES]
        hi_acc = h_ref[:, (kt + j) * LANES:(kt + j + 1) * LANES]
        for k in range(TOP_K):
            lo, hi = _load_packed_tile(y_ref, (k,), tc, kt, j)
            g = gates[:, k:k + 1]
            lo_acc = lo_acc + g * lo
            hi_acc = hi_acc + g * hi
        o_ref[:, j * LANES:(j + 1) * LANES] = lo_acc
        o_ref[:, (kt + j) * LANES:(kt + j + 1) * LANES] = hi_acc


def _combine(h, gates, y4):
    n, d = h.shape
    tc = COMBINE_TILE
    slab = tc * d // (2 * LANES)
    row = lambda i: (i, 0)
    return pl.pallas_call(
        _combine_kernel,
        grid=(n // tc,),
        in_specs=[pl.BlockSpec((tc, d), row),
                  pl.BlockSpec((tc, LANES), row),
                  pl.BlockSpec((TOP_K, slab, LANES), lambda i: (0, i, 0))],
        out_specs=pl.BlockSpec((tc, d), row),
        out_shape=jax.ShapeDtypeStruct((n, d), F32),
        input_output_aliases={0: 0},
        compiler_params=_cparams(("parallel",)),
        name="moe_combine",
    )(h, gates, y4)


def _layer(h3, layer, norm_mix_g, w_in, conv_w, conv_b, conv_ln_g, conv_ln_b, q_norm_g, k_norm_g,
           lambda_qk, subln_g, rel_bias, w_out, norm_ffn_g, router_w, router_b,
           w_gate_up, b_gate_up, w_down, b_down):
    b, s, d = h3.shape
    n = b * s
    d_conv = conv_w.shape[-1]
    d_attn = N_HEADS * VALUE_DIM
    lambda_init = 0.8 - 0.6 * math.exp(-0.3 * layer)
    x2 = h3.reshape(n, d)

    u, q, k, v = _in_proj(x2, norm_mix_g.reshape(1, d), w_in.astype(BF16), d_conv, d_attn, tm=512)
    conv_o = _conv(u.reshape(b, s, d_conv), conv_w, conv_b.reshape(1, d_conv),
                   conv_ln_g.reshape(1, d_conv), conv_ln_b.reshape(1, d_conv))
    attn_o = _attention(q.reshape(b, s, d_attn), k.reshape(b, s, d_attn), v.reshape(b, s, d_attn),
                        rel_bias, q_norm_g.reshape(1, VALUE_DIM), k_norm_g.reshape(1, VALUE_DIM),
                        lambda_qk, subln_g.reshape(1, VALUE_DIM), lambda_init)

    n_exp = router_w.shape[1]
    rw = jnp.zeros((d, LANES), BF16).at[:, :n_exp].set(router_w.astype(BF16))
    rb = jnp.full((1, LANES), NEG_INF, F32).at[0, :n_exp].set(router_b)
    w_out_bf = w_out.astype(BF16)
    hres, hn, idx, gates, rank, cnt = _out_proj(
        x2, conv_o.reshape(n, d_conv), attn_o.reshape(n, d_attn),
        w_out_bf[:d_conv], w_out_bf[d_conv:], norm_ffn_g.reshape(1, d), rw, rb, tm=512)

    bm = MOE_BLOCK
    nblk = n * TOP_K // bm + n_exp
    counts = cnt[0, :n_exp].astype(I32)
    padded = (counts + bm - 1) // bm * bm
    eid = jnp.arange(n_exp, dtype=I32)
    pad_end = jnp.sum(jnp.where(eid[None, :] <= eid[:, None], padded[None, :], 0), axis=1)
    pad_start = pad_end - padded
    start_of = jnp.sum(jnp.where(idx[None, :TOP_K] == eid[:, None, None],
                                 pad_start[:, None, None], 0), axis=0)
    pos = start_of + rank[:TOP_K]
    pos_chunks = pos.reshape(TOP_K, n // SC_CHUNK, SC_CHUNK).transpose(1, 0, 2)
    block_start = jnp.arange(nblk, dtype=I32) * bm
    block_expert = jnp.minimum(jnp.sum(block_start[:, None] >= pad_end[None, :], axis=1),
                               n_exp - 1).astype(I32)
    n_used = (pad_end[-1:] // bm).astype(I32)
    blk = jnp.arange(nblk, dtype=I32)
    first = jnp.logical_and(
        jnp.concatenate([jnp.ones((1,), bool), block_expert[1:] != block_expert[:-1]]),
        blk < n_used[0]).astype(I32)
    opened = jnp.sum(jnp.where(blk[None, :] <= blk[:, None], first[None, :], 0), axis=1)
    wslot = ((opened - 1) % 2).astype(I32)
    later_used = jnp.logical_and(eid[None, :] > eid[:, None], counts[None, :] > 0)
    next_used = jnp.min(jnp.where(later_used, eid[None, :], n_exp), axis=1)
    next_used = jnp.where(next_used == n_exp, -1, next_used).astype(I32)
    next_of_block = jnp.sum(jnp.where(block_expert[:, None] == eid[None, :], next_used[None, :], 0),
                            axis=1).astype(I32)
    on_expert = block_expert[:, None] == eid[None, :]
    rows_in_block = jnp.clip(
        jnp.sum(jnp.where(on_expert, (counts + pad_start)[None, :], 0), axis=1) - block_start, 0, bm)
    quarter = bm // MOE_QUARTERS
    quarters = ((rows_in_block + quarter - 1) // quarter).astype(I32)
    plan = (block_expert, first, wslot, next_of_block, quarters, n_used)

    d_ff = w_down.shape[1]
    bgu_p = b_gate_up.reshape(n_exp, d_ff // LANES, LANES, 2).transpose(0, 1, 3, 2).reshape(n_exp, 1, 2 * d_ff)
    r = d // (2 * LANES)
    xs = _sc_dispatch(hn.reshape(n, r, LANES), pos_chunks, nblk * bm)
    yb = _moe(plan, xs.reshape(nblk * bm * r, LANES), w_gate_up, bgu_p, w_down,
              b_down.reshape(n_exp, 1, d))
    y4 = _sc_gather(yb.reshape(nblk * bm, r, LANES), pos_chunks, n)
    out = _combine(hres, gates, y4.reshape(TOP_K, n * r, LANES))
    return out.reshape(b, s, d)


def kernel(x, norm_mix_g, w_in, conv_w, conv_b, conv_ln_g, conv_ln_b, q_norm_g, k_norm_g, lambda_qk,
           subln_g, rel_bias, w_out, norm_ffn_g, router_w, router_b, w_gate_up, b_gate_up, w_down,
           b_down):
    h = x
    for layer in range(norm_mix_g.shape[0]):
        h = _layer(h, layer, norm_mix_g[layer], w_in[layer], conv_w[layer], conv_b[layer],
                   conv_ln_g[layer], conv_ln_b[layer], q_norm_g[layer], k_norm_g[layer],
                   lambda_qk[layer], subln_g[layer], rel_bias, w_out[layer], norm_ffn_g[layer],
                   router_w[layer], router_b[layer], w_gate_up[layer], b_gate_up[layer],
                   w_down[layer], b_down[layer])
    return h
```

```python
import functools
import math

import jax
import jax.numpy as jnp
import numpy as np
from jax import lax
from jax.experimental import pallas as pl
from jax.experimental.pallas import tpu as pltpu
from jax.experimental.pallas import tpu_sc as plsc

F32 = jnp.float32
BF16 = jnp.bfloat16
I32 = jnp.int32
U32 = jnp.uint32

CHUNK = 64
CONV_WIDTH = 31
N_HEADS = 4
HEAD_DIM = 64
VALUE_DIM = 2 * HEAD_DIM
REL_BUCKETS = 32
REL_MAX_DIST = 128
N_EXPERTS = 32
TOP_K = 4
SWIGLU_LIMIT = 7.0
SWIGLU_ALPHA = 1.702
EPS = 1e-5
LOG2E = 1.4426950408889634

LANES = 128
SUBLANES = 8
MXU_DIM = 256
VMEM_LIMIT = 56 * 1024 * 1024

NEG_INF = float("-inf")


def _cparams(sem, vmem=VMEM_LIMIT, flags=None):
    return pltpu.CompilerParams(dimension_semantics=sem, vmem_limit_bytes=vmem, flags=flags)


def _dot(a, b):
    return jnp.dot(a, b, preferred_element_type=F32)


def _dot_nt(a, b):
    return lax.dot_general(a, b, (((1,), (1,)), ((), ())), preferred_element_type=F32)


def _pack_pair(lo, hi):
    lo_bits = lax.bitcast_convert_type(lo.astype(BF16).astype(F32), U32)
    hi_bits = lax.bitcast_convert_type(hi.astype(BF16).astype(F32), U32)
    return (lo_bits >> 16) | hi_bits


def _unpack_pair(w):
    return (lax.bitcast_convert_type(w << 16, F32),
            lax.bitcast_convert_type(w & jnp.uint32(0xFFFF0000), F32))


def _store_packed_rows(ref, lead, x):
    rows, k = x.shape[0], x.shape[1] // (2 * LANES)
    for j in range(k):
        w = _pack_pair(x[:, j * LANES:(j + 1) * LANES], x[:, (k + j) * LANES:(k + j + 1) * LANES])
        ref[(*lead, pl.ds(j, rows, stride=k), slice(None))] = w


def _load_packed_tile(ref, lead, rows, k, j):
    return _unpack_pair(ref[(*lead, pl.ds(j, rows, stride=k), slice(None))])


def _in_proj_kernel(x_ref, g_ref, w_ref, u_ref, q_ref, k_ref, v_ref, *, d_conv, d_attn):
    x = x_ref[...]
    ms = jnp.mean(x * x, axis=-1, keepdims=True)
    y = (x * lax.rsqrt(ms + EPS) * g_ref[...]).astype(BF16)
    proj = _dot(y, w_ref[...])
    a = proj[:, :d_conv]
    g = proj[:, d_conv:2 * d_conv]
    u_ref[...] = a * jax.nn.sigmoid(g)
    o = 2 * d_conv
    q_ref[...] = proj[:, o:o + d_attn]
    k_ref[...] = proj[:, o + d_attn:o + 2 * d_attn]
    v_ref[...] = proj[:, o + 2 * d_attn:o + 3 * d_attn].astype(BF16)


def _in_proj(x2, g, w_bf, d_conv, d_attn, tm):
    n, d = x2.shape
    d_in = w_bf.shape[1]
    row = lambda i: (i, 0)
    fixed = lambda i: (0, 0)
    return pl.pallas_call(
        functools.partial(_in_proj_kernel, d_conv=d_conv, d_attn=d_attn),
        grid=(n // tm,),
        in_specs=[pl.BlockSpec((tm, d), row),
                  pl.BlockSpec((1, d), fixed),
                  pl.BlockSpec((d, d_in), fixed)],
        out_specs=[pl.BlockSpec((tm, d_conv), row),
                   pl.BlockSpec((tm, d_attn), row),
                   pl.BlockSpec((tm, d_attn), row),
                   pl.BlockSpec((tm, d_attn), row)],
        out_shape=[jax.ShapeDtypeStruct((n, d_conv), F32),
                   jax.ShapeDtypeStruct((n, d_attn), F32),
                   jax.ShapeDtypeStruct((n, d_attn), F32),
                   jax.ShapeDtypeStruct((n, d_attn), BF16)],
        compiler_params=_cparams(("parallel",)),
        name="in_proj",
    )(x2, g, w_bf)


CONV_PAD = 32
CONV_SEQ_TILE = 1024
CONV_TILE = 128
CONV_NORM_TILE = 256


def _conv_kernel(u_ref, prev_ref, w_ref, cb_ref, lg_ref, lb_ref, o_ref, sh_ref, y_ref):
    ts, c = u_ref.shape[1], u_ref.shape[2]
    plen = ts + CONV_PAD
    hist = prev_ref[0]
    hist = jnp.where(pl.program_id(1) > 0, hist, jnp.zeros_like(hist))
    sh_ref[0, pl.ds(0, CONV_PAD), :] = hist
    sh_ref[0, pl.ds(CONV_PAD, ts), :] = u_ref[0]
    sh_ref[0, pl.ds(plen, SUBLANES), :] = jnp.zeros((SUBLANES, c), F32)

    bt = CONV_PAD

    def shift(i, _):
        p0 = pl.multiple_of(i * bt, bt)
        win = sh_ref[0, pl.ds(p0, bt + SUBLANES), :]
        for r in range(1, SUBLANES):
            sh_ref[r, pl.ds(p0, bt), :] = win[r:r + bt]
        return 0

    lax.fori_loop(0, plen // bt, shift, 0)

    off0 = CONV_PAD - (CONV_WIDTH - 1)

    by_shift = {}
    for j in range(CONV_WIDTH):
        a, r = divmod(off0 + j, SUBLANES)
        by_shift.setdefault(r, []).append((a, j))

    for lt in range(c // LANES):
        lanes = slice(lt * LANES, (lt + 1) * LANES)
        taps = [w_ref[pl.ds(j, 1), lanes] for j in range(CONV_WIDTH)]

        def tap_body(i, _, lanes=lanes, taps=taps):
            t0 = pl.multiple_of(i * CONV_TILE, CONV_TILE)
            acc = jnp.zeros((CONV_TILE, LANES), F32)
            for r, group in by_shift.items():
                a_lo = min(a for a, _ in group)
                a_hi = max(a for a, _ in group)
                rows = CONV_TILE + (a_hi - a_lo) * SUBLANES
                start = pl.multiple_of(t0 + a_lo * SUBLANES, SUBLANES)
                win = sh_ref[r, pl.ds(start, rows), lanes]
                for a, j in group:
                    lo = (a - a_lo) * SUBLANES
                    acc = acc + win[lo:lo + CONV_TILE] * taps[j]
            y_ref[pl.ds(t0, CONV_TILE), lanes] = acc
            return 0

        lax.fori_loop(0, ts // CONV_TILE, tap_body, 0)

    def norm_body(i, _):
        t0 = pl.multiple_of(i * CONV_NORM_TILE, CONV_NORM_TILE)
        y = y_ref[pl.ds(t0, CONV_NORM_TILE), :] + cb_ref[...]
        mu = jnp.mean(y, axis=-1, keepdims=True)
        yc = y - mu
        var = jnp.mean(yc * yc, axis=-1, keepdims=True)
        z = yc * lax.rsqrt(var + EPS) * lg_ref[...] + lb_ref[...]
        o_ref[0, pl.ds(t0, CONV_NORM_TILE), :] = (z * jax.nn.sigmoid(z)).astype(o_ref.dtype)
        return 0

    lax.fori_loop(0, ts // CONV_NORM_TILE, norm_body, 0)


def _conv(u3, conv_w, conv_b, ln_g, ln_b):
    b, s, c = u3.shape
    ts = min(CONV_SEQ_TILE, s)
    hist_per_tile = ts // CONV_PAD
    fixed = lambda i, j: (0, 0)
    return pl.pallas_call(
        _conv_kernel,
        grid=(b, s // ts),
        in_specs=[pl.BlockSpec((1, ts, c), lambda i, j: (i, j, 0)),
                  pl.BlockSpec((1, CONV_PAD, c),
                               lambda i, j: (i, jnp.maximum(j * hist_per_tile - 1, 0), 0)),
                  pl.BlockSpec((CONV_WIDTH, c), fixed),
                  pl.BlockSpec((1, c), fixed),
                  pl.BlockSpec((1, c), fixed),
                  pl.BlockSpec((1, c), fixed)],
        out_specs=pl.BlockSpec((1, ts, c), lambda i, j: (i, j, 0)),
        out_shape=jax.ShapeDtypeStruct((b, s, c), BF16),
        scratch_shapes=[pltpu.VMEM((SUBLANES, ts + CONV_PAD + SUBLANES, c), F32),
                        pltpu.VMEM((ts, c), F32)],
        compiler_params=_cparams(("parallel", "parallel")),
        name="conv_mixer",
    )(u3, u3, conv_w, conv_b, ln_g, ln_b)


ATT_TILE = 256
ATT_SEQS = 2
FAR_BUCKET = REL_BUCKETS // 2 - 1


def _t5_bucket(rel):
    nb = REL_BUCKETS // 2
    max_exact = nb // 2
    ret = jnp.where(rel > 0, nb, 0)
    n = jnp.abs(rel)
    nf = jnp.maximum(n, 1).astype(jnp.float32)
    large = max_exact + (jnp.log(nf / max_exact) / math.log(REL_MAX_DIST / max_exact)
                         * (nb - max_exact)).astype(jnp.int32)
    large = jnp.minimum(large, nb - 1)
    return ret + jnp.where(n < max_exact, n, large)


def _near_buckets():
    assert ATT_TILE >= REL_MAX_DIST and ATT_TILE % CHUNK == 0
    qpos = jnp.arange(ATT_TILE, dtype=I32)[:, None]
    kpos = jnp.arange(ATT_TILE, dtype=I32)[None, :]
    prev = _t5_bucket(kpos - ATT_TILE - qpos)
    diag = _t5_bucket(kpos - qpos)
    diag = jnp.where(kpos // CHUNK <= qpos // CHUNK, diag, -1)
    return jnp.stack([prev, diag]).astype(I32)


def _attn_kernel(tab_ref, q_ref, k_ref, v_ref, bkt_ref, bd_ref, qg_ref, kg_ref, lqk_ref, sg_ref, o_ref,
                 qz_ref, kn_ref, v1_ref, bias_ref, *, seq, lambda_init):
    h = pl.program_id(0)
    t = ATT_TILE
    n_tiles = seq // t
    lane = lax.broadcasted_iota(I32, (1, VALUE_DIM), 1)
    first = lane < HEAD_DIM

    @pl.when(pl.program_id(1) == 0)
    def _():
        far = tab_ref[FAR_BUCKET, h]
        for d in range(2):
            bkt = bkt_ref[d]
            tile = jnp.full((t, t), NEG_INF, F32)
            for b in range(REL_BUCKETS):
                tile = jnp.where(bkt == b, (tab_ref[b, h] - far) * LOG2E, tile)
            bias_ref[d] = tile

    def half_norm(x, g):
        x2 = x * x
        hi = x2.astype(BF16)
        lo = (x2 - hi.astype(F32)).astype(BF16)
        ms = (_dot(hi, bd_ref[...]) + _dot(lo, bd_ref[...])) * (1.0 / HEAD_DIM)
        return x * lax.rsqrt(ms + EPS) * g

    q_gain = qg_ref[...] * (HEAD_DIM ** -0.5 * LOG2E)
    qg_maps = (jnp.where(first, q_gain, 0.0), jnp.where(first, 0.0, q_gain))

    n_seq = q_ref.shape[0]

    ones_col = jnp.broadcast_to(jnp.where(lane == 0, 1.0, 0.0).astype(BF16), (t, VALUE_DIM))
    for i in range(n_tiles):
        r0 = i * t
        for b in range(n_seq):
            qn = half_norm(q_ref[b, pl.ds(r0, t), :], 1.0)
            qz_ref[b, 0, pl.ds(r0, t), :] = (qn * qg_maps[0]).astype(BF16)
            qz_ref[b, 1, pl.ds(r0, t), :] = (qn * qg_maps[1]).astype(BF16)
            kn_ref[b, pl.ds(r0, t), :] = half_norm(k_ref[b, pl.ds(r0, t), :], kg_ref[...]).astype(BF16)
            v1_ref[b, pl.ds(r0, t), :] = jnp.concatenate([v_ref[b, pl.ds(r0, t), :], ones_col], axis=1)

    lqk = lqk_ref[...]
    lam = (jnp.exp(jnp.sum(lqk[0:1] * lqk[1:2], axis=-1, keepdims=True))
           - jnp.exp(jnp.sum(lqk[2:3] * lqk[3:4], axis=-1, keepdims=True)) + lambda_init)

    for i in range(n_tiles):
        q0, kend = i * t, (i + 1) * t
        for b in range(n_seq):
            keys = kn_ref[b, 0:kend, :]
            vals = v1_ref[b, 0:kend, :]
            maps = []
            for m in range(2):
                s = _dot_nt(qz_ref[b, m, q0:q0 + t, :], keys)
                parts = [s[:, kend - t:] + bias_ref[1]]
                if i >= 1:
                    parts.insert(0, s[:, kend - 2 * t:kend - t] + bias_ref[0])
                if i >= 2:
                    parts.insert(0, s[:, :kend - 2 * t])
                s = jnp.concatenate(parts, axis=1) if len(parts) > 1 else parts[0]
                p = jnp.exp2(s - jnp.max(s, axis=-1, keepdims=True))
                pv = _dot(p.astype(BF16), vals)
                maps.append(pv[:, :VALUE_DIM] / pv[:, VALUE_DIM:VALUE_DIM + 1])
            o = maps[0] - lam * maps[1]
            ms = jnp.mean(o * o, axis=-1, keepdims=True)
            o = o * lax.rsqrt(ms + EPS) * sg_ref[...] * (1.0 - lambda_init)
            o_ref[b, q0:q0 + t, :] = o.astype(o_ref.dtype)


def _attention(q3, k3, v3, rel_bias, q_g, k_g, lam_qk, subln_g, lambda_init):
    b, s, _ = q3.shape
    t = ATT_TILE
    nb = ATT_SEQS if b % ATT_SEQS == 0 else 1
    head = lambda j, i: (i, 0, j)
    fixed2 = lambda j, i: (0, 0)
    fixed3 = lambda j, i: (0, 0, 0)
    half = np.arange(VALUE_DIM) // HEAD_DIM
    blockdiag = jnp.asarray(half[:, None] == half[None, :], dtype=BF16)
    return pl.pallas_call(
        functools.partial(_attn_kernel, seq=s, lambda_init=lambda_init),
        grid=(N_HEADS, b // nb),
        in_specs=[pl.BlockSpec(memory_space=pltpu.SMEM),
                  pl.BlockSpec((nb, s, VALUE_DIM), head),
                  pl.BlockSpec((nb, s, VALUE_DIM), head),
                  pl.BlockSpec((nb, s, VALUE_DIM), head),
                  pl.BlockSpec((2, t, t), fixed3),
                  pl.BlockSpec((VALUE_DIM, VALUE_DIM), fixed2),
                  pl.BlockSpec((1, VALUE_DIM), fixed2),
                  pl.BlockSpec((1, VALUE_DIM), fixed2),
                  pl.BlockSpec((4, HEAD_DIM), fixed2),
                  pl.BlockSpec((1, VALUE_DIM), fixed2)],
        out_specs=pl.BlockSpec((nb, s, VALUE_DIM), head),
        out_shape=jax.ShapeDtypeStruct((b, s, N_HEADS * VALUE_DIM), BF16),
        scratch_shapes=[pltpu.VMEM((nb, 2, s, VALUE_DIM), BF16),
                        pltpu.VMEM((nb, s, VALUE_DIM), BF16),
                        pltpu.VMEM((nb, s, 2 * VALUE_DIM), BF16),
                        pltpu.VMEM((2, t, t), F32)],
        compiler_params=_cparams(("arbitrary", "arbitrary")),
        name="diff_attn",
    )(rel_bias, q3, k3, v3, _near_buckets(), blockdiag, q_g, k_g, lam_qk, subln_g)


def _out_proj_kernel(x_ref, c_ref, a_ref, wc_ref, wa_ref, g_ref, rw_ref, rb_ref, tri_ref,
                     h_ref, hn_ref, idx_ref, gate_ref, rank_ref, cnt_ref, carry_ref):
    @pl.when(pl.program_id(0) == 0)
    def _():
        carry_ref[...] = jnp.zeros_like(carry_ref)

    h = x_ref[...] + _dot(c_ref[...], wc_ref[...]) + _dot(a_ref[...], wa_ref[...])
    h_ref[...] = h
    ms = jnp.mean(h * h, axis=-1, keepdims=True)
    hn = h * lax.rsqrt(ms + EPS) * g_ref[...]
    _store_packed_rows(hn_ref, (), hn)

    logits = _dot(hn.astype(BF16), rw_ref[...]) + rb_ref[...]

    tm = logits.shape[0]
    lane = lax.broadcasted_iota(I32, (tm, LANES), 1).astype(F32)
    work = logits
    vals, idxs = [], []
    for _ in range(TOP_K):
        mx = jnp.max(work, axis=-1, keepdims=True)
        ix = jnp.min(jnp.where(work == mx, lane, float(LANES)), axis=-1, keepdims=True)
        vals.append(mx)
        idxs.append(ix)
        work = jnp.where(lane == ix, NEG_INF, work)
    exps = [jnp.exp(v - vals[0]) for v in vals]
    denom = exps[0]
    for e in exps[1:]:
        denom = denom + e

    sel = jnp.zeros((tm, LANES), F32)
    for ix in idxs:
        sel = sel + jnp.where(lane == ix, 1.0, 0.0)
    rank = _dot(tri_ref[...], sel.astype(BF16)) + carry_ref[...]
    carry_ref[...] = carry_ref[...] + jnp.sum(sel, axis=0, keepdims=True)
    cnt_ref[...] = carry_ref[...]

    idx_out = jnp.zeros((tm, LANES), F32)
    gate_out = jnp.zeros((tm, LANES), F32)
    rank_out = jnp.zeros((tm, LANES), F32)
    for k in range(TOP_K):
        rk = jnp.sum(jnp.where(lane == idxs[k], rank, 0.0), axis=-1, keepdims=True)
        idx_out = jnp.where(lane == k, idxs[k], idx_out)
        gate_out = jnp.where(lane == k, exps[k] / denom, gate_out)
        rank_out = jnp.where(lane == k, rk, rank_out)
    idx_ref[...] = jnp.transpose(idx_out)[:SUBLANES].astype(I32)
    rank_ref[...] = jnp.transpose(rank_out)[:SUBLANES].astype(I32)
    gate_ref[...] = gate_out


def _out_proj(x2, conv_o, attn_o, wc, wa, g, rw, rb, tm):
    n, d = x2.shape
    dc, da = conv_o.shape[1], attn_o.shape[1]
    row = lambda i: (i, 0)
    fixed = lambda i: (0, 0)
    tri = jnp.tril(jnp.ones((tm, tm), F32), -1).astype(BF16)
    return pl.pallas_call(
        _out_proj_kernel,
        grid=(n // tm,),
        in_specs=[pl.BlockSpec((tm, d), row),
                  pl.BlockSpec((tm, dc), row),
                  pl.BlockSpec((tm, da), row),
                  pl.BlockSpec((dc, d), fixed),
                  pl.BlockSpec((da, d), fixed),
                  pl.BlockSpec((1, d), fixed),
                  pl.BlockSpec((d, LANES), fixed),
                  pl.BlockSpec((1, LANES), fixed),
                  pl.BlockSpec((tm, tm), fixed)],
        out_specs=[pl.BlockSpec((tm, d), row),
                   pl.BlockSpec((tm * d // (2 * LANES), LANES), row),
                   pl.BlockSpec((SUBLANES, tm), lambda i: (0, i)),
                   pl.BlockSpec((tm, LANES), row),
                   pl.BlockSpec((SUBLANES, tm), lambda i: (0, i)),
                   pl.BlockSpec((1, LANES), fixed)],
        out_shape=[jax.ShapeDtypeStruct((n, d), F32),
                   jax.ShapeDtypeStruct((n * d // (2 * LANES), LANES), U32),
                   jax.ShapeDtypeStruct((SUBLANES, n), I32),
                   jax.ShapeDtypeStruct((n, LANES), F32),
                   jax.ShapeDtypeStruct((SUBLANES, n), I32),
                   jax.ShapeDtypeStruct((1, LANES), F32)],
        scratch_shapes=[pltpu.VMEM((1, LANES), F32)],
        compiler_params=_cparams(("arbitrary",)),
        name="out_proj_router",
    )(x2, conv_o, attn_o, wc, wa, g, rw, rb, tri)


MOE_BLOCK = 1024
MOE_QUARTERS = 4


def _moe_kernel(be_ref, first_ref, wslot_ref, nxt_ref, quarters_ref, nused_ref,
                x_ref, wgu_hbm, wd_hbm, bgu_ref, bd_ref, perm_ref,
                y_ref, wgu_f32, wd_f32, wsem, wgu_bf, wd_bf, x_bf, act_bf):
    i = pl.program_id(0)
    bm = MOE_BLOCK
    n_used = nused_ref[0]
    d = wd_bf.shape[1]
    d_ff = wd_bf.shape[0]
    n_groups = d_ff // LANES

    def weight_copies(e, slot):
        return (pltpu.make_async_copy(wgu_hbm.at[e], wgu_f32.at[slot], wsem.at[0, slot]),
                pltpu.make_async_copy(wd_hbm.at[e], wd_f32.at[slot], wsem.at[1, slot]))

    @pl.when(i == 0)
    def _():
        for c in weight_copies(be_ref[0], 0):
            c.start()

    active = i < n_used

    @pl.when(jnp.logical_and(active, first_ref[i] == 1))
    def _():
        ws = wslot_ref[i]
        for c in weight_copies(be_ref[i], ws):
            c.wait()
        for g in range(n_groups):
            cols = pl.ds(g * MXU_DIM, MXU_DIM)
            wgu_bf[:, cols] = _dot(wgu_f32[ws, :, cols].astype(BF16), perm_ref[...]).astype(BF16)
        wd_bf[...] = wd_f32[ws].astype(BF16)

        @pl.when(nxt_ref[i] >= 0)
        def _():
            for c in weight_copies(nxt_ref[i], 1 - ws):
                c.start()

    def block(m):
        k = d // (2 * LANES)
        for j in range(k):
            lo, hi = _load_packed_tile(x_ref, (), m, k, j)
            x_bf[0:m, j * LANES:(j + 1) * LANES] = lo.astype(BF16)
            x_bf[0:m, (k + j) * LANES:(k + j + 1) * LANES] = hi.astype(BF16)
        for g in range(n_groups):
            gu = (_dot(x_bf[0:m, :], wgu_bf[:, g * MXU_DIM:(g + 1) * MXU_DIM])
                  + bgu_ref[0, :, g * MXU_DIM:(g + 1) * MXU_DIM])
            gate = jnp.minimum(gu[:, :LANES], SWIGLU_LIMIT)
            lin = jnp.clip(gu[:, LANES:], -SWIGLU_LIMIT, SWIGLU_LIMIT)
            act_bf[0:m, g * LANES:(g + 1) * LANES] = (
                gate * jax.nn.sigmoid(SWIGLU_ALPHA * gate) * (lin + 1.0)).astype(BF16)
        _store_packed_rows(y_ref, (), _dot(act_bf[0:m, :], wd_bf[...]) + bd_ref[0])
        if m < bm:
            y_ref[pl.ds(m * k, (bm - m) * k), :] = jnp.zeros(((bm - m) * k, LANES), y_ref.dtype)

    quarter = bm // MOE_QUARTERS
    for nq in range(1, MOE_QUARTERS + 1):
        @pl.when(jnp.logical_and(active, quarters_ref[i] == nq))
        def _(nq=nq):
            block(nq * quarter)

    @pl.when(i >= n_used)
    def _():
        y_ref[...] = jnp.zeros_like(y_ref)


def _deinterleave_perm():
    src = np.arange(MXU_DIM)
    dst = np.where(src % 2 == 0, src // 2, LANES + src // 2)
    p = np.zeros((MXU_DIM, MXU_DIM), np.float32)
    p[src, dst] = 1.0
    return jnp.asarray(p, dtype=BF16)


def _moe(plan, xs, w_gate_up, bgu_p, w_down, b_down):
    n_exp, d, d_gu = w_gate_up.shape
    d_ff = w_down.shape[1]
    bm = MOE_BLOCK
    nblk = plan[0].shape[0]
    slab = bm * d // (2 * LANES)
    by_expert = lambda i, be, *_: (be[i], 0, 0)
    grid_spec = pltpu.PrefetchScalarGridSpec(
        num_scalar_prefetch=6,
        grid=(nblk,),
        in_specs=[
            pl.BlockSpec((slab, LANES), lambda i, *_: (i, 0)),
            pl.BlockSpec(memory_space=pl.ANY),
            pl.BlockSpec(memory_space=pl.ANY),
            pl.BlockSpec((1, 1, d_gu), by_expert),
            pl.BlockSpec((1, 1, d), by_expert),
            pl.BlockSpec((MXU_DIM, MXU_DIM), lambda i, *_: (0, 0)),
        ],
        out_specs=pl.BlockSpec((slab, LANES), lambda i, *_: (i, 0)),
        scratch_shapes=[pltpu.VMEM((2, d, d_gu), F32),
                        pltpu.VMEM((2, d_ff, d), F32),
                        pltpu.SemaphoreType.DMA((2, 2)),
                        pltpu.VMEM((d, d_gu), BF16),
                        pltpu.VMEM((d_ff, d), BF16),
                        pltpu.VMEM((bm, d), BF16),
                        pltpu.VMEM((bm, d_ff), BF16)],
    )
    return pl.pallas_call(
        _moe_kernel,
        grid_spec=grid_spec,
        out_shape=jax.ShapeDtypeStruct((nblk * slab, LANES), U32),
        compiler_params=_cparams(("arbitrary",)),
        name="moe_experts",
    )(*plan, xs, w_gate_up, w_down, bgu_p, b_down, _deinterleave_perm())


SC_CORES = 2
SC_SUBCORES = 16
SC_CHUNK = 64


def _sc_mesh():
    return plsc.VectorSubcoreMesh(core_axis_name="c", subcore_axis_name="s")


def _sc_dispatch(rows3, pos_chunks, n_slots):
    n, r, _ = rows3.shape
    workers = SC_CORES * SC_SUBCORES
    per_worker = n // SC_CHUNK // workers

    assert per_worker % 2 == 0

    def body(rows_hbm, pos_hbm, out_hbm, idx_v, buf_a, buf_b, rsem_a, rsem_b, wsem):
        wid = lax.axis_index("s") * SC_CORES + lax.axis_index("c")
        base = wid * per_worker
        bufs, rsems = (buf_a, buf_b), (rsem_a, rsem_b)

        def read(chunk, s):
            return pltpu.make_async_copy(rows_hbm.at[pl.ds(chunk * SC_CHUNK, SC_CHUNK)], bufs[s], rsems[s])

        read(base, 0).start()

        @pl.loop(0, per_worker, step=2)
        def _(c):
            for s in range(2):
                chunk = base + c + s
                read(chunk, s).wait()

                @pl.when(c + s + 1 < per_worker)
                def _():
                    read(chunk + 1, 1 - s).start()

                pltpu.sync_copy(pos_hbm.at[chunk], idx_v)
                copies = [pltpu.async_copy(bufs[s], out_hbm.at[idx_v.at[k]], wsem) for k in range(TOP_K)]
                for cp in copies:
                    cp.wait()

    return pl.kernel(
        body,
        out_type=jax.ShapeDtypeStruct((n_slots, r, LANES), rows3.dtype),
        mesh=_sc_mesh(),
        scratch_types=[pltpu.VMEM((TOP_K, SC_CHUNK), I32),
                       pltpu.VMEM((SC_CHUNK, r, LANES), rows3.dtype),
                       pltpu.VMEM((SC_CHUNK, r, LANES), rows3.dtype),
                       pltpu.SemaphoreType.DMA,
                       pltpu.SemaphoreType.DMA,
                       pltpu.SemaphoreType.DMA],
        name="sc_dispatch",
    )(rows3, pos_chunks)


def _sc_gather(rows3, pos_chunks, n):
    _, r, _ = rows3.shape
    workers = SC_CORES * SC_SUBCORES
    per_worker = n // SC_CHUNK // workers

    def body(rows_hbm, pos_hbm, out_hbm, idx_v, buf_a, buf_b, sem_a, sem_b):
        wid = lax.axis_index("s") * SC_CORES + lax.axis_index("c")
        bufs, sems = (buf_a, buf_b), (sem_a, sem_b)

        @pl.loop(0, per_worker)
        def _(c):
            chunk = wid * per_worker + c
            pltpu.sync_copy(pos_hbm.at[chunk], idx_v)
            gathers = [None] * TOP_K
            gathers[0] = pltpu.async_copy(rows_hbm.at[idx_v.at[0]], bufs[0], sems[0])
            for k in range(TOP_K):
                if k + 1 < TOP_K:
                    nxt = (k + 1) % 2
                    gathers[k + 1] = pltpu.async_copy(rows_hbm.at[idx_v.at[k + 1]], bufs[nxt], sems[nxt])
                gathers[k].wait()
                pltpu.sync_copy(bufs[k % 2], out_hbm.at[k, pl.ds(chunk * SC_CHUNK, SC_CHUNK)])

    return pl.kernel(
        body,
        out_type=jax.ShapeDtypeStruct((TOP_K, n, r, LANES), rows3.dtype),
        mesh=_sc_mesh(),
        scratch_types=[pltpu.VMEM((TOP_K, SC_CHUNK), I32),
                       pltpu.VMEM((SC_CHUNK, r, LANES), rows3.dtype),
                       pltpu.VMEM((SC_CHUNK, r, LANES), rows3.dtype),
                       pltpu.SemaphoreType.DMA,
                       pltpu.SemaphoreType.DMA],
        name="sc_gather",
    )(rows3, pos_chunks)


COMBINE_TILE = 512


def _combine_kernel(h_ref, gate_ref, y_ref, o_ref):
    tc, d = h_ref.shape
    kt = d // (2 * LANES)
    gates = gate_ref[...]
    for j in range(kt):
        lo_acc = h_ref[:, j * LANES:(j + 1) * LANES]
        hi_acc = h_ref[:, (kt + j) * LANES:(kt + j + 1) * LANES]
        for k in range(TOP_K):
            lo, hi = _load_packed_tile(y_ref, (k,), tc, kt, j)
            g = gates[:, k:k + 1]
            lo_acc = lo_acc + g * lo
            hi_acc = hi_acc + g * hi
        o_ref[:, j * LANES:(j + 1) * LANES] = lo_acc
        o_ref[:, (kt + j) * LANES:(kt + j + 1) * LANES] = hi_acc


def _combine(h, gates, y4):
    n, d = h.shape
    tc = COMBINE_TILE
    slab = tc * d // (2 * LANES)
    row = lambda i: (i, 0)
    return pl.pallas_call(
        _combine_kernel,
        grid=(n // tc,),
        in_specs=[pl.BlockSpec((tc, d), row),
                  pl.BlockSpec((tc, LANES), row),
                  pl.BlockSpec((TOP_K, slab, LANES), lambda i: (0, i, 0))],
        out_specs=pl.BlockSpec((tc, d), row),
        out_shape=jax.ShapeDtypeStruct((n, d), F32),
        input_output_aliases={0: 0},
        compiler_params=_cparams(("parallel",)),
        name="moe_combine",
    )(h, gates, y4)


def _layer(h3, layer, norm_mix_g, w_in, conv_w, conv_b, conv_ln_g, conv_ln_b, q_norm_g, k_norm_g,
           lambda_qk, subln_g, rel_bias, w_out, norm_ffn_g, router_w, router_b,
           w_gate_up, b_gate_up, w_down, b_down):
    b, s, d = h3.shape
    n = b * s
    d_conv = conv_w.shape[-1]
    d_attn = N_HEADS * VALUE_DIM
    lambda_init = 0.8 - 0.6 * math.exp(-0.3 * layer)
    x2 = h3.reshape(n, d)

    u, q, k, v = _in_proj(x2, norm_mix_g.reshape(1, d), w_in.astype(BF16), d_conv, d_attn, tm=512)
    conv_o = _conv(u.reshape(b, s, d_conv), conv_w, conv_b.reshape(1, d_conv),
                   conv_ln_g.reshape(1, d_conv), conv_ln_b.reshape(1, d_conv))
    attn_o = _attention(q.reshape(b, s, d_attn), k.reshape(b, s, d_attn), v.reshape(b, s, d_attn),
                        rel_bias, q_norm_g.reshape(1, VALUE_DIM), k_norm_g.reshape(1, VALUE_DIM),
                        lambda_qk, subln_g.reshape(1, VALUE_DIM), lambda_init)

    n_exp = router_w.shape[1]
    rw = jnp.zeros((d, LANES), BF16).at[:, :n_exp].set(router_w.astype(BF16))
    rb = jnp.full((1, LANES), NEG_INF, F32).at[0, :n_exp].set(router_b)
    w_out_bf = w_out.astype(BF16)
    hres, hn, idx, gates, rank, cnt = _out_proj(
        x2, conv_o.reshape(n, d_conv), attn_o.reshape(n, d_attn),
        w_out_bf[:d_conv], w_out_bf[d_conv:], norm_ffn_g.reshape(1, d), rw, rb, tm=512)

    bm = MOE_BLOCK
    nblk = n * TOP_K // bm + n_exp
    counts = cnt[0, :n_exp].astype(I32)
    padded = (counts + bm - 1) // bm * bm
    eid = jnp.arange(n_exp, dtype=I32)
    pad_end = jnp.sum(jnp.where(eid[None, :] <= eid[:, None], padded[None, :], 0), axis=1)
    pad_start = pad_end - padded
    start_of = jnp.sum(jnp.where(idx[None, :TOP_K] == eid[:, None, None],
                                 pad_start[:, None, None], 0), axis=0)
    pos = start_of + rank[:TOP_K]
    pos_chunks = pos.reshape(TOP_K, n // SC_CHUNK, SC_CHUNK).transpose(1, 0, 2)
    block_start = jnp.arange(nblk, dtype=I32) * bm
    block_expert = jnp.minimum(jnp.sum(block_start[:, None] >= pad_end[None, :], axis=1),
                               n_exp - 1).astype(I32)
    n_used = (pad_end[-1:] // bm).astype(I32)
    blk = jnp.arange(nblk, dtype=I32)
    first = jnp.logical_and(
        jnp.concatenate([jnp.ones((1,), bool), block_expert[1:] != block_expert[:-1]]),
        blk < n_used[0]).astype(I32)
    opened = jnp.sum(jnp.where(blk[None, :] <= blk[:, None], first[None, :], 0), axis=1)
    wslot = ((opened - 1) % 2).astype(I32)
    later_used = jnp.logical_and(eid[None, :] > eid[:, None], counts[None, :] > 0)
    next_used = jnp.min(jnp.where(later_used, eid[None, :], n_exp), axis=1)
    next_used = jnp.where(next_used == n_exp, -1, next_used).astype(I32)
    next_of_block = jnp.sum(jnp.where(block_expert[:, None] == eid[None, :], next_used[None, :], 0),
                            axis=1).astype(I32)
    on_expert = block_expert[:, None] == eid[None, :]
    rows_in_block = jnp.clip(
        jnp.sum(jnp.where(on_expert, (counts + pad_start)[None, :], 0), axis=1) - block_start, 0, bm)
    quarter = bm // MOE_QUARTERS
    quarters = ((rows_in_block + quarter - 1) // quarter).astype(I32)
    plan = (block_expert, first, wslot, next_of_block, quarters, n_used)

    d_ff = w_down.shape[1]
    bgu_p = b_gate_up.reshape(n_exp, d_ff // LANES, LANES, 2).transpose(0, 1, 3, 2).reshape(n_exp, 1, 2 * d_ff)
    r = d // (2 * LANES)
    xs = _sc_dispatch(hn.reshape(n, r, LANES), pos_chunks, nblk * bm)
    yb = _moe(plan, xs.reshape(nblk * bm * r, LANES), w_gate_up, bgu_p, w_down,
              b_down.reshape(n_exp, 1, d))
    y4 = _sc_gather(yb.reshape(nblk * bm, r, LANES), pos_chunks, n)
    out = _combine(hres, gates, y4.reshape(TOP_K, n * r, LANES))
    return out.reshape(b, s, d)


def kernel(x, norm_mix_g, w_in, conv_w, conv_b, conv_ln_g, conv_ln_b, q_norm_g, k_norm_g, lambda_qk,
           subln_g, rel_bias, w_out, norm_ffn_g, router_w, router_b, w_gate_up, b_gate_up, w_down,
           b_down):
    h = x
    for layer in range(norm_mix_g.shape[0]):
        h = _layer(h, layer, norm_mix_g[layer], w_in[layer], conv_w[layer], conv_b[layer],
                   conv_ln_g[layer], conv_ln_b[layer], q_norm_g[layer], k_norm_g[layer],
                   lambda_qk[layer], subln_g[layer], rel_bias, w_out[layer], norm_ffn_g[layer],
                   router_w[layer], router_b[layer], w_gate_up[layer], b_gate_up[layer],
                   w_down[layer], b_down[layer])
    return h
```

```python
import functools
import math

import jax
import jax.numpy as jnp
import numpy as np
from jax import lax
from jax.experimental import pallas as pl
from jax.experimental.pallas import tpu as pltpu
from jax.experimental.pallas import tpu_sc as plsc

F32 = jnp.float32
BF16 = jnp.bfloat16
I32 = jnp.int32
U32 = jnp.uint32

CHUNK = 64
CONV_WIDTH = 31
N_HEADS = 4
HEAD_DIM = 64
VALUE_DIM = 2 * HEAD_DIM
REL_BUCKETS = 32
REL_MAX_DIST = 128
N_EXPERTS = 32
TOP_K = 4
SWIGLU_LIMIT = 7.0
SWIGLU_ALPHA = 1.702
EPS = 1e-5
LOG2E = 1.4426950408889634

LANES = 128
SUBLANES = 8
MXU_DIM = 256
VMEM_LIMIT = 56 * 1024 * 1024

NEG_INF = float("-inf")


def _cparams(sem, vmem=VMEM_LIMIT, flags=None):
    return pltpu.CompilerParams(dimension_semantics=sem, vmem_limit_bytes=vmem, flags=flags)


def _dot(a, b):
    return jnp.dot(a, b, preferred_element_type=F32)


def _dot_nt(a, b):
    return lax.dot_general(a, b, (((1,), (1,)), ((), ())), preferred_element_type=F32)


def _pack_pair(lo, hi):
    lo_bits = lax.bitcast_convert_type(lo.astype(BF16).astype(F32), U32)
    hi_bits = lax.bitcast_convert_type(hi.astype(BF16).astype(F32), U32)
    return (lo_bits >> 16) | hi_bits


def _unpack_pair(w):
    return (lax.bitcast_convert_type(w << 16, F32),
            lax.bitcast_convert_type(w & jnp.uint32(0xFFFF0000), F32))


def _store_packed_rows(ref, lead, x):
    rows, k = x.shape[0], x.shape[1] // (2 * LANES)
    for j in range(k):
        w = _pack_pair(x[:, j * LANES:(j + 1) * LANES], x[:, (k + j) * LANES:(k + j + 1) * LANES])
        ref[(*lead, pl.ds(j, rows, stride=k), slice(None))] = w


def _load_packed_tile(ref, lead, rows, k, j):
    return _unpack_pair(ref[(*lead, pl.ds(j, rows, stride=k), slice(None))])


PROJ_TILE = 1024
def _in_proj_kernel(x_ref, g_ref, w_ref, u_ref, q_ref, k_ref, v_ref, *, d_conv, d_attn):
    x = x_ref[...]
    ms = jnp.mean(x * x, axis=-1, keepdims=True)
    y = (x * lax.rsqrt(ms + EPS) * g_ref[...]).astype(BF16)
    proj = _dot(y, w_ref[...])
    a = proj[:, :d_conv]
    g = proj[:, d_conv:2 * d_conv]
    u_ref[...] = a * jax.nn.sigmoid(g)
    o = 2 * d_conv
    q_ref[...] = proj[:, o:o + d_attn]
    k_ref[...] = proj[:, o + d_attn:o + 2 * d_attn]
    v_ref[...] = proj[:, o + 2 * d_attn:o + 3 * d_attn].astype(BF16)


def _in_proj(x2, g, w_bf, d_conv, d_attn, tm):
    n, d = x2.shape
    d_in = w_bf.shape[1]
    row = lambda i: (i, 0)
    fixed = lambda i: (0, 0)
    return pl.pallas_call(
        functools.partial(_in_proj_kernel, d_conv=d_conv, d_attn=d_attn),
        grid=(n // tm,),
        in_specs=[pl.BlockSpec((tm, d), row),
                  pl.BlockSpec((1, d), fixed),
                  pl.BlockSpec((d, d_in), fixed)],
        out_specs=[pl.BlockSpec((tm, d_conv), row),
                   pl.BlockSpec((tm, d_attn), row),
                   pl.BlockSpec((tm, d_attn), row),
                   pl.BlockSpec((tm, d_attn), row)],
        out_shape=[jax.ShapeDtypeStruct((n, d_conv), F32),
                   jax.ShapeDtypeStruct((n, d_attn), F32),
                   jax.ShapeDtypeStruct((n, d_attn), F32),
                   jax.ShapeDtypeStruct((n, d_attn), BF16)],
        compiler_params=_cparams(("parallel",)),
        name="in_proj",
    )(x2, g, w_bf)


CONV_PAD = 32
CONV_SEQ_TILE = 1024
CONV_TILE = 128
CONV_NORM_TILE = 256


def _conv_kernel(u_ref, prev_ref, w_ref, cb_ref, lg_ref, lb_ref, o_ref, sh_ref, y_ref):
    ts, c = u_ref.shape[1], u_ref.shape[2]
    plen = ts + CONV_PAD
    hist = prev_ref[0]
    hist = jnp.where(pl.program_id(1) > 0, hist, jnp.zeros_like(hist))
    sh_ref[0, pl.ds(0, CONV_PAD), :] = hist
    sh_ref[0, pl.ds(CONV_PAD, ts), :] = u_ref[0]
    sh_ref[0, pl.ds(plen, SUBLANES), :] = jnp.zeros((SUBLANES, c), F32)

    bt = CONV_PAD

    def shift(i, _):
        p0 = pl.multiple_of(i * bt, bt)
        win = sh_ref[0, pl.ds(p0, bt + SUBLANES), :]
        for r in range(1, SUBLANES):
            sh_ref[r, pl.ds(p0, bt), :] = win[r:r + bt]
        return 0

    lax.fori_loop(0, plen // bt, shift, 0)

    off0 = CONV_PAD - (CONV_WIDTH - 1)

    by_shift = {}
    for j in range(CONV_WIDTH):
        a, r = divmod(off0 + j, SUBLANES)
        by_shift.setdefault(r, []).append((a, j))

    for lt in range(c // LANES):
        lanes = slice(lt * LANES, (lt + 1) * LANES)
        taps = [w_ref[pl.ds(j, 1), lanes] for j in range(CONV_WIDTH)]

        def tap_body(i, _, lanes=lanes, taps=taps):
            t0 = pl.multiple_of(i * CONV_TILE, CONV_TILE)
            acc = jnp.zeros((CONV_TILE, LANES), F32)
            for r, group in by_shift.items():
                a_lo = min(a for a, _ in group)
                a_hi = max(a for a, _ in group)
                rows = CONV_TILE + (a_hi - a_lo) * SUBLANES
                start = pl.multiple_of(t0 + a_lo * SUBLANES, SUBLANES)
                win = sh_ref[r, pl.ds(start, rows), lanes]
                for a, j in group:
                    lo = (a - a_lo) * SUBLANES
                    acc = acc + win[lo:lo + CONV_TILE] * taps[j]
            y_ref[pl.ds(t0, CONV_TILE), lanes] = acc
            return 0

        lax.fori_loop(0, ts // CONV_TILE, tap_body, 0)

    def norm_body(i, _):
        t0 = pl.multiple_of(i * CONV_NORM_TILE, CONV_NORM_TILE)
        y = y_ref[pl.ds(t0, CONV_NORM_TILE), :] + cb_ref[...]
        mu = jnp.mean(y, axis=-1, keepdims=True)
        yc = y - mu
        var = jnp.mean(yc * yc, axis=-1, keepdims=True)
        z = yc * lax.rsqrt(var + EPS) * lg_ref[...] + lb_ref[...]
        o_ref[0, pl.ds(t0, CONV_NORM_TILE), :] = (z * jax.nn.sigmoid(z)).astype(o_ref.dtype)
        return 0

    lax.fori_loop(0, ts // CONV_NORM_TILE, norm_body, 0)


def _conv(u3, conv_w, conv_b, ln_g, ln_b):
    b, s, c = u3.shape
    ts = min(CONV_SEQ_TILE, s)
    hist_per_tile = ts // CONV_PAD
    fixed = lambda i, j: (0, 0)
    return pl.pallas_call(
        _conv_kernel,
        grid=(b, s // ts),
        in_specs=[pl.BlockSpec((1, ts, c), lambda i, j: (i, j, 0)),
                  pl.BlockSpec((1, CONV_PAD, c),
                               lambda i, j: (i, jnp.maximum(j * hist_per_tile - 1, 0), 0)),
                  pl.BlockSpec((CONV_WIDTH, c), fixed),
                  pl.BlockSpec((1, c), fixed),
                  pl.BlockSpec((1, c), fixed),
                  pl.BlockSpec((1, c), fixed)],
        out_specs=pl.BlockSpec((1, ts, c), lambda i, j: (i, j, 0)),
        out_shape=jax.ShapeDtypeStruct((b, s, c), BF16),
        scratch_shapes=[pltpu.VMEM((SUBLANES, ts + CONV_PAD + SUBLANES, c), F32),
                        pltpu.VMEM((ts, c), F32)],
        compiler_params=_cparams(("parallel", "parallel")),
        name="conv_mixer",
    )(u3, u3, conv_w, conv_b, ln_g, ln_b)


ATT_TILE = 256
ATT_SEQS = 2
FAR_BUCKET = REL_BUCKETS // 2 - 1


def _t5_bucket(rel):
    nb = REL_BUCKETS // 2
    max_exact = nb // 2
    ret = jnp.where(rel > 0, nb, 0)
    n = jnp.abs(rel)
    nf = jnp.maximum(n, 1).astype(jnp.float32)
    large = max_exact + (jnp.log(nf / max_exact) / math.log(REL_MAX_DIST / max_exact)
                         * (nb - max_exact)).astype(jnp.int32)
    large = jnp.minimum(large, nb - 1)
    return ret + jnp.where(n < max_exact, n, large)


def _near_buckets():
    assert ATT_TILE >= REL_MAX_DIST and ATT_TILE % CHUNK == 0
    qpos = jnp.arange(ATT_TILE, dtype=I32)[:, None]
    kpos = jnp.arange(ATT_TILE, dtype=I32)[None, :]
    prev = _t5_bucket(kpos - ATT_TILE - qpos)
    diag = _t5_bucket(kpos - qpos)
    diag = jnp.where(kpos // CHUNK <= qpos // CHUNK, diag, -1)
    return jnp.stack([prev, diag]).astype(I32)


def _attn_kernel(tab_ref, q_ref, k_ref, v_ref, bkt_ref, bd_ref, qg_ref, kg_ref, lqk_ref, sg_ref, o_ref,
                 qz_ref, kn_ref, v1_ref, bias_ref, *, seq, lambda_init):
    h = pl.program_id(0)
    t = ATT_TILE
    n_tiles = seq // t
    lane = lax.broadcasted_iota(I32, (1, VALUE_DIM), 1)
    first = lane < HEAD_DIM

    @pl.when(pl.program_id(1) == 0)
    def _():
        far = tab_ref[FAR_BUCKET, h]
        for d in range(2):
            bkt = bkt_ref[d]
            tile = jnp.full((t, t), NEG_INF, F32)
            for b in range(REL_BUCKETS):
                tile = jnp.where(bkt == b, (tab_ref[b, h] - far) * LOG2E, tile)
            bias_ref[d] = tile

    def half_norm(x, g):
        x2 = x * x
        hi = x2.astype(BF16)
        lo = (x2 - hi.astype(F32)).astype(BF16)
        ms = (_dot(hi, bd_ref[...]) + _dot(lo, bd_ref[...])) * (1.0 / HEAD_DIM)
        return x * lax.rsqrt(ms + EPS) * g

    q_gain = qg_ref[...] * (HEAD_DIM ** -0.5 * LOG2E)
    qg_maps = (jnp.where(first, q_gain, 0.0), jnp.where(first, 0.0, q_gain))

    n_seq = q_ref.shape[0]

    ones_col = jnp.broadcast_to(jnp.where(lane == 0, 1.0, 0.0).astype(BF16), (t, VALUE_DIM))
    for i in range(n_tiles):
        r0 = i * t
        for b in range(n_seq):
            qn = half_norm(q_ref[b, pl.ds(r0, t), :], 1.0)
            qz_ref[b, 0, pl.ds(r0, t), :] = (qn * qg_maps[0]).astype(BF16)
            qz_ref[b, 1, pl.ds(r0, t), :] = (qn * qg_maps[1]).astype(BF16)
            kn_ref[b, pl.ds(r0, t), :] = half_norm(k_ref[b, pl.ds(r0, t), :], kg_ref[...]).astype(BF16)
            v1_ref[b, pl.ds(r0, t), :] = jnp.concatenate([v_ref[b, pl.ds(r0, t), :], ones_col], axis=1)

    lqk = lqk_ref[...]
    lam = (jnp.exp(jnp.sum(lqk[0:1] * lqk[1:2], axis=-1, keepdims=True))
           - jnp.exp(jnp.sum(lqk[2:3] * lqk[3:4], axis=-1, keepdims=True)) + lambda_init)

    for i in range(n_tiles):
        q0, kend = i * t, (i + 1) * t
        for b in range(n_seq):
            keys = kn_ref[b, 0:kend, :]
            vals = v1_ref[b, 0:kend, :]
            maps = []
            for m in range(2):
                s = _dot_nt(qz_ref[b, m, q0:q0 + t, :], keys)
                parts = [s[:, kend - t:] + bias_ref[1]]
                if i >= 1:
                    parts.insert(0, s[:, kend - 2 * t:kend - t] + bias_ref[0])
                if i >= 2:
                    parts.insert(0, s[:, :kend - 2 * t])
                s = jnp.concatenate(parts, axis=1) if len(parts) > 1 else parts[0]
                p = jnp.exp2(s - jnp.max(s, axis=-1, keepdims=True))
                pv = _dot(p.astype(BF16), vals)
                maps.append(pv[:, :VALUE_DIM] / pv[:, VALUE_DIM:VALUE_DIM + 1])
            o = maps[0] - lam * maps[1]
            ms = jnp.mean(o * o, axis=-1, keepdims=True)
            o = o * lax.rsqrt(ms + EPS) * sg_ref[...] * (1.0 - lambda_init)
            o_ref[b, q0:q0 + t, :] = o.astype(o_ref.dtype)


def _attention(q3, k3, v3, rel_bias, q_g, k_g, lam_qk, subln_g, lambda_init):
    b, s, _ = q3.shape
    t = ATT_TILE
    nb = ATT_SEQS if b % ATT_SEQS == 0 else 1
    head = lambda j, i: (i, 0, j)
    fixed2 = lambda j, i: (0, 0)
    fixed3 = lambda j, i: (0, 0, 0)
    half = np.arange(VALUE_DIM) // HEAD_DIM
    blockdiag = jnp.asarray(half[:, None] == half[None, :], dtype=BF16)
    return pl.pallas_call(
        functools.partial(_attn_kernel, seq=s, lambda_init=lambda_init),
        grid=(N_HEADS, b // nb),
        in_specs=[pl.BlockSpec(memory_space=pltpu.SMEM),
                  pl.BlockSpec((nb, s, VALUE_DIM), head),
                  pl.BlockSpec((nb, s, VALUE_DIM), head),
                  pl.BlockSpec((nb, s, VALUE_DIM), head),
                  pl.BlockSpec((2, t, t), fixed3),
                  pl.BlockSpec((VALUE_DIM, VALUE_DIM), fixed2),
                  pl.BlockSpec((1, VALUE_DIM), fixed2),
                  pl.BlockSpec((1, VALUE_DIM), fixed2),
                  pl.BlockSpec((4, HEAD_DIM), fixed2),
                  pl.BlockSpec((1, VALUE_DIM), fixed2)],
        out_specs=pl.BlockSpec((nb, s, VALUE_DIM), head),
        out_shape=jax.ShapeDtypeStruct((b, s, N_HEADS * VALUE_DIM), BF16),
        scratch_shapes=[pltpu.VMEM((nb, 2, s, VALUE_DIM), BF16),
                        pltpu.VMEM((nb, s, VALUE_DIM), BF16),
                        pltpu.VMEM((nb, s, 2 * VALUE_DIM), BF16),
                        pltpu.VMEM((2, t, t), F32)],
        compiler_params=_cparams(("arbitrary", "arbitrary")),
        name="diff_attn",
    )(rel_bias, q3, k3, v3, _near_buckets(), blockdiag, q_g, k_g, lam_qk, subln_g)


def _out_proj_kernel(x_ref, c_ref, a_ref, wc_ref, wa_ref, g_ref, rw_ref, rb_ref, tri_ref,
                     h_ref, hn_ref, idx_ref, gate_ref, rank_ref, cnt_ref, carry_ref):
    @pl.when(pl.program_id(0) == 0)
    def _():
        carry_ref[...] = jnp.zeros_like(carry_ref)

    h = x_ref[...] + _dot(c_ref[...], wc_ref[...]) + _dot(a_ref[...], wa_ref[...])
    h_ref[...] = h
    ms = jnp.mean(h * h, axis=-1, keepdims=True)
    hn = h * lax.rsqrt(ms + EPS) * g_ref[...]
    _store_packed_rows(hn_ref, (), hn)

    logits = _dot(hn.astype(BF16), rw_ref[...]) + rb_ref[...]

    tm = logits.shape[0]
    lane = lax.broadcasted_iota(I32, (tm, LANES), 1).astype(F32)
    work = logits
    vals, idxs = [], []
    for _ in range(TOP_K):
        mx = jnp.max(work, axis=-1, keepdims=True)
        ix = jnp.min(jnp.where(work == mx, lane, float(LANES)), axis=-1, keepdims=True)
        vals.append(mx)
        idxs.append(ix)
        work = jnp.where(lane == ix, NEG_INF, work)
    exps = [jnp.exp(v - vals[0]) for v in vals]
    denom = exps[0]
    for e in exps[1:]:
        denom = denom + e

    sel = jnp.zeros((tm, LANES), F32)
    for ix in idxs:
        sel = sel + jnp.where(lane == ix, 1.0, 0.0)
    rank = _dot(tri_ref[...], sel.astype(BF16)) + carry_ref[...]
    carry_ref[...] = carry_ref[...] + jnp.sum(sel, axis=0, keepdims=True)
    cnt_ref[...] = carry_ref[...]

    idx_out = jnp.zeros((tm, LANES), F32)
    gate_out = jnp.zeros((tm, LANES), F32)
    rank_out = jnp.zeros((tm, LANES), F32)
    for k in range(TOP_K):
        rk = jnp.sum(jnp.where(lane == idxs[k], rank, 0.0), axis=-1, keepdims=True)
        idx_out = jnp.where(lane == k, idxs[k], idx_out)
        gate_out = jnp.where(lane == k, exps[k] / denom, gate_out)
        rank_out = jnp.where(lane == k, rk, rank_out)
    idx_ref[...] = jnp.transpose(idx_out)[:SUBLANES].astype(I32)
    rank_ref[...] = jnp.transpose(rank_out)[:SUBLANES].astype(I32)
    gate_ref[...] = gate_out


def _out_proj(x2, conv_o, attn_o, wc, wa, g, rw, rb, tm):
    n, d = x2.shape
    dc, da = conv_o.shape[1], attn_o.shape[1]
    row = lambda i: (i, 0)
    fixed = lambda i: (0, 0)
    tri = jnp.tril(jnp.ones((tm, tm), F32), -1).astype(BF16)
    return pl.pallas_call(
        _out_proj_kernel,
        grid=(n // tm,),
        in_specs=[pl.BlockSpec((tm, d), row),
                  pl.BlockSpec((tm, dc), row),
                  pl.BlockSpec((tm, da), row),
                  pl.BlockSpec((dc, d), fixed),
                  pl.BlockSpec((da, d), fixed),
                  pl.BlockSpec((1, d), fixed),
                  pl.BlockSpec((d, LANES), fixed),
                  pl.BlockSpec((1, LANES), fixed),
                  pl.BlockSpec((tm, tm), fixed)],
        out_specs=[pl.BlockSpec((tm, d), row),
                   pl.BlockSpec((tm * d // (2 * LANES), LANES), row),
                   pl.BlockSpec((SUBLANES, tm), lambda i: (0, i)),
                   pl.BlockSpec((tm, LANES), row),
                   pl.BlockSpec((SUBLANES, tm), lambda i: (0, i)),
                   pl.BlockSpec((1, LANES), fixed)],
        out_shape=[jax.ShapeDtypeStruct((n, d), F32),
                   jax.ShapeDtypeStruct((n * d // (2 * LANES), LANES), U32),
                   jax.ShapeDtypeStruct((SUBLANES, n), I32),
                   jax.ShapeDtypeStruct((n, LANES), F32),
                   jax.ShapeDtypeStruct((SUBLANES, n), I32),
                   jax.ShapeDtypeStruct((1, LANES), F32)],
        scratch_shapes=[pltpu.VMEM((1, LANES), F32)],
        compiler_params=_cparams(("arbitrary",)),
        name="out_proj_router",
    )(x2, conv_o, attn_o, wc, wa, g, rw, rb, tri)


MOE_BLOCK = 1024
MOE_QUARTERS = 4


def _moe_kernel(be_ref, first_ref, wslot_ref, nxt_ref, quarters_ref, nused_ref,
                x_ref, wgu_hbm, wd_hbm, bgu_ref, bd_ref, perm_ref,
                y_ref, wgu_f32, wd_f32, wsem, wgu_bf, wd_bf, x_bf, act_bf):
    i = pl.program_id(0)
    bm = MOE_BLOCK
    n_used = nused_ref[0]
    d = wd_bf.shape[1]
    d_ff = wd_bf.shape[0]
    n_groups = d_ff // LANES

    def weight_copies(e, slot):
        return (pltpu.make_async_copy(wgu_hbm.at[e], wgu_f32.at[slot], wsem.at[0, slot]),
                pltpu.make_async_copy(wd_hbm.at[e], wd_f32.at[slot], wsem.at[1, slot]))

    @pl.when(i == 0)
    def _():
        for c in weight_copies(be_ref[0], 0):
            c.start()

    active = i < n_used

    @pl.when(jnp.logical_and(active, first_ref[i] == 1))
    def _():
        ws = wslot_ref[i]
        for c in weight_copies(be_ref[i], ws):
            c.wait()
        for g in range(n_groups):
            cols = pl.ds(g * MXU_DIM, MXU_DIM)
            wgu_bf[:, cols] = _dot(wgu_f32[ws, :, cols].astype(BF16), perm_ref[...]).astype(BF16)
        wd_bf[...] = wd_f32[ws].astype(BF16)

        @pl.when(nxt_ref[i] >= 0)
        def _():
            for c in weight_copies(nxt_ref[i], 1 - ws):
                c.start()

    def block(m):
        k = d // (2 * LANES)
        for j in range(k):
            lo, hi = _load_packed_tile(x_ref, (), m, k, j)
            x_bf[0:m, j * LANES:(j + 1) * LANES] = lo.astype(BF16)
            x_bf[0:m, (k + j) * LANES:(k + j + 1) * LANES] = hi.astype(BF16)
        for g in range(n_groups):
            gu = (_dot(x_bf[0:m, :], wgu_bf[:, g * MXU_DIM:(g + 1) * MXU_DIM])
                  + bgu_ref[0, :, g * MXU_DIM:(g + 1) * MXU_DIM])
            gate = jnp.minimum(gu[:, :LANES], SWIGLU_LIMIT)
            lin = jnp.clip(gu[:, LANES:], -SWIGLU_LIMIT, SWIGLU_LIMIT)
            act_bf[0:m, g * LANES:(g + 1) * LANES] = (
                gate * jax.nn.sigmoid(SWIGLU_ALPHA * gate) * (lin + 1.0)).astype(BF16)
        _store_packed_rows(y_ref, (), _dot(act_bf[0:m, :], wd_bf[...]) + bd_ref[0])
        if m < bm:
            y_ref[pl.ds(m * k, (bm - m) * k), :] = jnp.zeros(((bm - m) * k, LANES), y_ref.dtype)

    quarter = bm // MOE_QUARTERS
    for nq in range(1, MOE_QUARTERS + 1):
        @pl.when(jnp.logical_and(active, quarters_ref[i] == nq))
        def _(nq=nq):
            block(nq * quarter)

    @pl.when(i >= n_used)
    def _():
        y_ref[...] = jnp.zeros_like(y_ref)


def _deinterleave_perm():
    src = np.arange(MXU_DIM)
    dst = np.where(src % 2 == 0, src // 2, LANES + src // 2)
    p = np.zeros((MXU_DIM, MXU_DIM), np.float32)
    p[src, dst] = 1.0
    return jnp.asarray(p, dtype=BF16)


def _moe(plan, xs, w_gate_up, bgu_p, w_down, b_down):
    n_exp, d, d_gu = w_gate_up.shape
    d_ff = w_down.shape[1]
    bm = MOE_BLOCK
    nblk = plan[0].shape[0]
    slab = bm * d // (2 * LANES)
    by_expert = lambda i, be, *_: (be[i], 0, 0)
    grid_spec = pltpu.PrefetchScalarGridSpec(
        num_scalar_prefetch=6,
        grid=(nblk,),
        in_specs=[
            pl.BlockSpec((slab, LANES), lambda i, *_: (i, 0)),
            pl.BlockSpec(memory_space=pl.ANY),
            pl.BlockSpec(memory_space=pl.ANY),
            pl.BlockSpec((1, 1, d_gu), by_expert),
            pl.BlockSpec((1, 1, d), by_expert),
            pl.BlockSpec((MXU_DIM, MXU_DIM), lambda i, *_: (0, 0)),
        ],
        out_specs=pl.BlockSpec((slab, LANES), lambda i, *_: (i, 0)),
        scratch_shapes=[pltpu.VMEM((2, d, d_gu), F32),
                        pltpu.VMEM((2, d_ff, d), F32),
                        pltpu.SemaphoreType.DMA((2, 2)),
                        pltpu.VMEM((d, d_gu), BF16),
                        pltpu.VMEM((d_ff, d), BF16),
                        pltpu.VMEM((bm, d), BF16),
                        pltpu.VMEM((bm, d_ff), BF16)],
    )
    return pl.pallas_call(
        _moe_kernel,
        grid_spec=grid_spec,
        out_shape=jax.ShapeDtypeStruct((nblk * slab, LANES), U32),
        compiler_params=_cparams(("arbitrary",)),
        name="moe_experts",
    )(*plan, xs, w_gate_up, w_down, bgu_p, b_down, _deinterleave_perm())


SC_CORES = 2
SC_SUBCORES = 16
SC_CHUNK = 64


def _sc_mesh():
    return plsc.VectorSubcoreMesh(core_axis_name="c", subcore_axis_name="s")


def _sc_dispatch(rows3, pos_chunks, n_slots):
    n, r, _ = rows3.shape
    workers = SC_CORES * SC_SUBCORES
    per_worker = n // SC_CHUNK // workers

    assert per_worker % 2 == 0

    def body(rows_hbm, pos_hbm, out_hbm, idx_v, buf_a, buf_b, rsem_a, rsem_b, wsem):
        wid = lax.axis_index("s") * SC_CORES + lax.axis_index("c")
        base = wid * per_worker
        bufs, rsems = (buf_a, buf_b), (rsem_a, rsem_b)

        def read(chunk, s):
            return pltpu.make_async_copy(rows_hbm.at[pl.ds(chunk * SC_CHUNK, SC_CHUNK)], bufs[s], rsems[s])

        read(base, 0).start()

        @pl.loop(0, per_worker, step=2)
        def _(c):
            for s in range(2):
                chunk = base + c + s
                read(chunk, s).wait()

                @pl.when(c + s + 1 < per_worker)
                def _():
                    read(chunk + 1, 1 - s).start()

                pltpu.sync_copy(pos_hbm.at[chunk], idx_v)
                copies = [pltpu.async_copy(bufs[s], out_hbm.at[idx_v.at[k]], wsem) for k in range(TOP_K)]
                for cp in copies:
                    cp.wait()

    return pl.kernel(
        body,
        out_type=jax.ShapeDtypeStruct((n_slots, r, LANES), rows3.dtype),
        mesh=_sc_mesh(),
        scratch_types=[pltpu.VMEM((TOP_K, SC_CHUNK), I32),
                       pltpu.VMEM((SC_CHUNK, r, LANES), rows3.dtype),
                       pltpu.VMEM((SC_CHUNK, r, LANES), rows3.dtype),
                       pltpu.SemaphoreType.DMA,
                       pltpu.SemaphoreType.DMA,
                       pltpu.SemaphoreType.DMA],
        name="sc_dispatch",
    )(rows3, pos_chunks)


def _sc_gather(rows3, pos_chunks, n):
    _, r, _ = rows3.shape
    workers = SC_CORES * SC_SUBCORES
    per_worker = n // SC_CHUNK // workers

    def body(rows_hbm, pos_hbm, out_hbm, idx_v, buf_a, buf_b, sem_a, sem_b):
        wid = lax.axis_index("s") * SC_CORES + lax.axis_index("c")
        bufs, sems = (buf_a, buf_b), (sem_a, sem_b)

        @pl.loop(0, per_worker)
        def _(c):
            chunk = wid * per_worker + c
            pltpu.sync_copy(pos_hbm.at[chunk], idx_v)
            gathers = [None] * TOP_K
            gathers[0] = pltpu.async_copy(rows_hbm.at[idx_v.at[0]], bufs[0], sems[0])
            for k in range(TOP_K):
                if k + 1 < TOP_K:
                    nxt = (k + 1) % 2
                    gathers[k + 1] = pltpu.async_copy(rows_hbm.at[idx_v.at[k + 1]], bufs[nxt], sems[nxt])
                gathers[k].wait()
                pltpu.sync_copy(bufs[k % 2], out_hbm.at[k, pl.ds(chunk * SC_CHUNK, SC_CHUNK)])

    return pl.kernel(
        body,
        out_type=jax.ShapeDtypeStruct((TOP_K, n, r, LANES), rows3.dtype),
        mesh=_sc_mesh(),
        scratch_types=[pltpu.VMEM((TOP_K, SC_CHUNK), I32),
                       pltpu.VMEM((SC_CHUNK, r, LANES), rows3.dtype),
                       pltpu.VMEM((SC_CHUNK, r, LANES), rows3.dtype),
                       pltpu.SemaphoreType.DMA,
                       pltpu.SemaphoreType.DMA],
        name="sc_gather",
    )(rows3, pos_chunks)


COMBINE_TILE = 512


def _combine_kernel(h_ref, gate_ref, y_ref, o_ref):
    tc, d = h_ref.shape
    kt = d // (2 * LANES)
    gates = gate_ref[...]
    for j in range(kt):
        lo_acc = h_ref[:, j * LANES:(j + 1) * LANES]
        hi_acc = h_ref[:, (kt + j) * LANES:(kt + j + 1) * LANES]
        for k in range(TOP_K):
            lo, hi = _load_packed_tile(y_ref, (k,), tc, kt, j)
            g = gates[:, k:k + 1]
            lo_acc = lo_acc + g * lo
            hi_acc = hi_acc + g * hi
        o_ref[:, j * LANES:(j + 1) * LANES] = lo_acc
        o_ref[:, (kt + j) * LANES:(kt + j + 1) * LANES] = hi_acc


def _combine(h, gates, y4):
    n, d = h.shape
    tc = COMBINE_TILE
    slab = tc * d // (2 * LANES)
    row = lambda i: (i, 0)
    return pl.pallas_call(
        _combine_kernel,
        grid=(n // tc,),
        in_specs=[pl.BlockSpec((tc, d), row),
                  pl.BlockSpec((tc, LANES), row),
                  pl.BlockSpec((TOP_K, slab, LANES), lambda i: (0, i, 0))],
        out_specs=pl.BlockSpec((tc, d), row),
        out_shape=jax.ShapeDtypeStruct((n, d), F32),
        input_output_aliases={0: 0},
        compiler_params=_cparams(("parallel",)),
        name="moe_combine",
    )(h, gates, y4)


def _layer(h3, layer, norm_mix_g, w_in, conv_w, conv_b, conv_ln_g, conv_ln_b, q_norm_g, k_norm_g,
           lambda_qk, subln_g, rel_bias, w_out, norm_ffn_g, router_w, router_b,
           w_gate_up, b_gate_up, w_down, b_down):
    b, s, d = h3.shape
    n = b * s
    d_conv = conv_w.shape[-1]
    d_attn = N_HEADS * VALUE_DIM
    lambda_init = 0.8 - 0.6 * math.exp(-0.3 * layer)
    x2 = h3.reshape(n, d)

    u, q, k, v = _in_proj(x2, norm_mix_g.reshape(1, d), w_in.astype(BF16), d_conv, d_attn,
                          tm=min(PROJ_TILE, n))
    conv_o = _conv(u.reshape(b, s, d_conv), conv_w, conv_b.reshape(1, d_conv),
                   conv_ln_g.reshape(1, d_conv), conv_ln_b.reshape(1, d_conv))
    attn_o = _attention(q.reshape(b, s, d_attn), k.reshape(b, s, d_attn), v.reshape(b, s, d_attn),
                        rel_bias, q_norm_g.reshape(1, VALUE_DIM), k_norm_g.reshape(1, VALUE_DIM),
                        lambda_qk, subln_g.reshape(1, VALUE_DIM), lambda_init)

    n_exp = router_w.shape[1]
    rw = jnp.zeros((d, LANES), BF16).at[:, :n_exp].set(router_w.astype(BF16))
    rb = jnp.full((1, LANES), NEG_INF, F32).at[0, :n_exp].set(router_b)
    w_out_bf = w_out.astype(BF16)
    hres, hn, idx, gates, rank, cnt = _out_proj(
        x2, conv_o.reshape(n, d_conv), attn_o.reshape(n, d_attn),
        w_out_bf[:d_conv], w_out_bf[d_conv:], norm_ffn_g.reshape(1, d), rw, rb,
        tm=min(PROJ_TILE, n))

    bm = MOE_BLOCK
    nblk = n * TOP_K // bm + n_exp
    counts = cnt[0, :n_exp].astype(I32)
    padded = (counts + bm - 1) // bm * bm
    eid = jnp.arange(n_exp, dtype=I32)
    pad_end = jnp.sum(jnp.where(eid[None, :] <= eid[:, None], padded[None, :], 0), axis=1)
    pad_start = pad_end - padded
    start_of = jnp.sum(jnp.where(idx[None, :TOP_K] == eid[:, None, None],
                                 pad_start[:, None, None], 0), axis=0)
    pos = start_of + rank[:TOP_K]
    pos_chunks = pos.reshape(TOP_K, n // SC_CHUNK, SC_CHUNK).transpose(1, 0, 2)
    block_start = jnp.arange(nblk, dtype=I32) * bm
    block_expert = jnp.minimum(jnp.sum(block_start[:, None] >= pad_end[None, :], axis=1),
                               n_exp - 1).astype(I32)
    n_used = (pad_end[-1:] // bm).astype(I32)
    blk = jnp.arange(nblk, dtype=I32)
    first = jnp.logical_and(
        jnp.concatenate([jnp.ones((1,), bool), block_expert[1:] != block_expert[:-1]]),
        blk < n_used[0]).astype(I32)
    opened = jnp.sum(jnp.where(blk[None, :] <= blk[:, None], first[None, :], 0), axis=1)
    wslot = ((opened - 1) % 2).astype(I32)
    later_used = jnp.logical_and(eid[None, :] > eid[:, None], counts[None, :] > 0)
    next_used = jnp.min(jnp.where(later_used, eid[None, :], n_exp), axis=1)
    next_used = jnp.where(next_used == n_exp, -1, next_used).astype(I32)
    next_of_block = jnp.sum(jnp.where(block_expert[:, None] == eid[None, :], next_used[None, :], 0),
                            axis=1).astype(I32)
    on_expert = block_expert[:, None] == eid[None, :]
    rows_in_block = jnp.clip(
        jnp.sum(jnp.where(on_expert, (counts + pad_start)[None, :], 0), axis=1) - block_start, 0, bm)
    quarter = bm // MOE_QUARTERS
    quarters = ((rows_in_block + quarter - 1) // quarter).astype(I32)
    plan = (block_expert, first, wslot, next_of_block, quarters, n_used)

    d_ff = w_down.shape[1]
    bgu_p = b_gate_up.reshape(n_exp, d_ff // LANES, LANES, 2).transpose(0, 1, 3, 2).reshape(n_exp, 1, 2 * d_ff)
    r = d // (2 * LANES)
    xs = _sc_dispatch(hn.reshape(n, r, LANES), pos_chunks, nblk * bm)
    yb = _moe(plan, xs.reshape(nblk * bm * r, LANES), w_gate_up, bgu_p, w_down,
              b_down.reshape(n_exp, 1, d))
    y4 = _sc_gather(yb.reshape(nblk * bm, r, LANES), pos_chunks, n)
    out = _combine(hres, gates, y4.reshape(TOP_K, n * r, LANES))
    return out.reshape(b, s, d)


def kernel(x, norm_mix_g, w_in, conv_w, conv_b, conv_ln_g, conv_ln_b, q_norm_g, k_norm_g, lambda_qk,
           subln_g, rel_bias, w_out, norm_ffn_g, router_w, router_b, w_gate_up, b_gate_up, w_down,
           b_down):
    h = x
    for layer in range(norm_mix_g.shape[0]):
        h = _layer(h, layer, norm_mix_g[layer], w_in[layer], conv_w[layer], conv_b[layer],
                   conv_ln_g[layer], conv_ln_b[layer], q_norm_g[layer], k_norm_g[layer],
                   lambda_qk[layer], subln_g[layer], rel_bias, w_out[layer], norm_ffn_g[layer],
                   router_w[layer], router_b[layer], w_gate_up[layer], b_gate_up[layer],
                   w_down[layer], b_down[layer])
    return h
```

```python
import functools
import math

import jax
import jax.numpy as jnp
import numpy as np
from jax import lax
from jax.experimental import pallas as pl
from jax.experimental.pallas import tpu as pltpu
from jax.experimental.pallas import tpu_sc as plsc

F32 = jnp.float32
BF16 = jnp.bfloat16
I32 = jnp.int32
U32 = jnp.uint32

CHUNK = 64
CONV_WIDTH = 31
N_HEADS = 4
HEAD_DIM = 64
VALUE_DIM = 2 * HEAD_DIM
REL_BUCKETS = 32
REL_MAX_DIST = 128
N_EXPERTS = 32
TOP_K = 4
SWIGLU_LIMIT = 7.0
SWIGLU_ALPHA = 1.702
EPS = 1e-5
LOG2E = 1.4426950408889634

LANES = 128
SUBLANES = 8
MXU_DIM = 256
VMEM_LIMIT = 56 * 1024 * 1024

NEG_INF = float("-inf")


def _cparams(sem, vmem=VMEM_LIMIT, flags=None):
    return pltpu.CompilerParams(dimension_semantics=sem, vmem_limit_bytes=vmem, flags=flags)


def _dot(a, b):
    return jnp.dot(a, b, preferred_element_type=F32)


def _dot_nt(a, b):
    return lax.dot_general(a, b, (((1,), (1,)), ((), ())), preferred_element_type=F32)


def _pack_pair(lo, hi):
    lo_bits = lax.bitcast_convert_type(lo.astype(BF16).astype(F32), U32)
    hi_bits = lax.bitcast_convert_type(hi.astype(BF16).astype(F32), U32)
    return (lo_bits >> 16) | hi_bits


def _unpack_pair(w):
    return (lax.bitcast_convert_type(w << 16, F32),
            lax.bitcast_convert_type(w & jnp.uint32(0xFFFF0000), F32))


def _store_packed_rows(ref, lead, x):
    rows, k = x.shape[0], x.shape[1] // (2 * LANES)
    for j in range(k):
        w = _pack_pair(x[:, j * LANES:(j + 1) * LANES], x[:, (k + j) * LANES:(k + j + 1) * LANES])
        ref[(*lead, pl.ds(j, rows, stride=k), slice(None))] = w


def _load_packed_tile(ref, lead, rows, k, j):
    return _unpack_pair(ref[(*lead, pl.ds(j, rows, stride=k), slice(None))])


PROJ_TILE = 1024
def _in_proj_kernel(x_ref, g_ref, w_ref, u_ref, q_ref, k_ref, v_ref, *, d_conv, d_attn):
    x = x_ref[...]
    ms = jnp.mean(x * x, axis=-1, keepdims=True)
    y = (x * lax.rsqrt(ms + EPS) * g_ref[...]).astype(BF16)
    proj = _dot(y, w_ref[...])
    a = proj[:, :d_conv]
    g = proj[:, d_conv:2 * d_conv]
    u_ref[...] = a * jax.nn.sigmoid(g)
    o = 2 * d_conv
    q_ref[...] = proj[:, o:o + d_attn]
    k_ref[...] = proj[:, o + d_attn:o + 2 * d_attn]
    v_ref[...] = proj[:, o + 2 * d_attn:o + 3 * d_attn].astype(BF16)


def _in_proj(x2, g, w_bf, d_conv, d_attn, tm):
    n, d = x2.shape
    d_in = w_bf.shape[1]
    row = lambda i: (i, 0)
    fixed = lambda i: (0, 0)
    return pl.pallas_call(
        functools.partial(_in_proj_kernel, d_conv=d_conv, d_attn=d_attn),
        grid=(n // tm,),
        in_specs=[pl.BlockSpec((tm, d), row),
                  pl.BlockSpec((1, d), fixed),
                  pl.BlockSpec((d, d_in), fixed)],
        out_specs=[pl.BlockSpec((tm, d_conv), row),
                   pl.BlockSpec((tm, d_attn), row),
                   pl.BlockSpec((tm, d_attn), row),
                   pl.BlockSpec((tm, d_attn), row)],
        out_shape=[jax.ShapeDtypeStruct((n, d_conv), F32),
                   jax.ShapeDtypeStruct((n, d_attn), F32),
                   jax.ShapeDtypeStruct((n, d_attn), F32),
                   jax.ShapeDtypeStruct((n, d_attn), BF16)],
        compiler_params=_cparams(("parallel",)),
        name="in_proj",
    )(x2, g, w_bf)


CONV_PAD = 32
CONV_SEQ_TILE = 1024
CONV_TILE = 128
CONV_NORM_TILE = 256


def _conv_kernel(u_ref, prev_ref, w_ref, cb_ref, lg_ref, lb_ref, o_ref, sh_ref, y_ref):
    ts, c = u_ref.shape[1], u_ref.shape[2]
    plen = ts + CONV_PAD
    hist = prev_ref[0]
    hist = jnp.where(pl.program_id(1) > 0, hist, jnp.zeros_like(hist))
    sh_ref[0, pl.ds(0, CONV_PAD), :] = hist
    sh_ref[0, pl.ds(CONV_PAD, ts), :] = u_ref[0]
    sh_ref[0, pl.ds(plen, SUBLANES), :] = jnp.zeros((SUBLANES, c), F32)

    bt = CONV_PAD

    def shift(i, _):
        p0 = pl.multiple_of(i * bt, bt)
        win = sh_ref[0, pl.ds(p0, bt + SUBLANES), :]
        for r in range(1, SUBLANES):
            sh_ref[r, pl.ds(p0, bt), :] = win[r:r + bt]
        return 0

    lax.fori_loop(0, plen // bt, shift, 0)

    off0 = CONV_PAD - (CONV_WIDTH - 1)

    by_shift = {}
    for j in range(CONV_WIDTH):
        a, r = divmod(off0 + j, SUBLANES)
        by_shift.setdefault(r, []).append((a, j))

    for lt in range(c // LANES):
        lanes = slice(lt * LANES, (lt + 1) * LANES)
        taps = [w_ref[pl.ds(j, 1), lanes] for j in range(CONV_WIDTH)]

        def tap_body(i, _, lanes=lanes, taps=taps):
            t0 = pl.multiple_of(i * CONV_TILE, CONV_TILE)
            acc = jnp.zeros((CONV_TILE, LANES), F32)
            for r, group in by_shift.items():
                a_lo = min(a for a, _ in group)
                a_hi = max(a for a, _ in group)
                rows = CONV_TILE + (a_hi - a_lo) * SUBLANES
                start = pl.multiple_of(t0 + a_lo * SUBLANES, SUBLANES)
                win = sh_ref[r, pl.ds(start, rows), lanes]
                for a, j in group:
                    lo = (a - a_lo) * SUBLANES
                    acc = acc + win[lo:lo + CONV_TILE] * taps[j]
            y_ref[pl.ds(t0, CONV_TILE), lanes] = acc
            return 0

        lax.fori_loop(0, ts // CONV_TILE, tap_body, 0)

    def norm_body(i, _):
        t0 = pl.multiple_of(i * CONV_NORM_TILE, CONV_NORM_TILE)
        y = y_ref[pl.ds(t0, CONV_NORM_TILE), :] + cb_ref[...]
        mu = jnp.mean(y, axis=-1, keepdims=True)
        yc = y - mu
        var = jnp.mean(yc * yc, axis=-1, keepdims=True)
        z = yc * lax.rsqrt(var + EPS) * lg_ref[...] + lb_ref[...]
        o_ref[0, pl.ds(t0, CONV_NORM_TILE), :] = (z * jax.nn.sigmoid(z)).astype(o_ref.dtype)
        return 0

    lax.fori_loop(0, ts // CONV_NORM_TILE, norm_body, 0)


def _conv(u3, conv_w, conv_b, ln_g, ln_b):
    b, s, c = u3.shape
    ts = min(CONV_SEQ_TILE, s)
    hist_per_tile = ts // CONV_PAD
    fixed = lambda i, j: (0, 0)
    return pl.pallas_call(
        _conv_kernel,
        grid=(b, s // ts),
        in_specs=[pl.BlockSpec((1, ts, c), lambda i, j: (i, j, 0)),
                  pl.BlockSpec((1, CONV_PAD, c),
                               lambda i, j: (i, jnp.maximum(j * hist_per_tile - 1, 0), 0)),
                  pl.BlockSpec((CONV_WIDTH, c), fixed),
                  pl.BlockSpec((1, c), fixed),
                  pl.BlockSpec((1, c), fixed),
                  pl.BlockSpec((1, c), fixed)],
        out_specs=pl.BlockSpec((1, ts, c), lambda i, j: (i, j, 0)),
        out_shape=jax.ShapeDtypeStruct((b, s, c), BF16),
        scratch_shapes=[pltpu.VMEM((SUBLANES, ts + CONV_PAD + SUBLANES, c), F32),
                        pltpu.VMEM((ts, c), F32)],
        compiler_params=_cparams(("parallel", "parallel")),
        name="conv_mixer",
    )(u3, u3, conv_w, conv_b, ln_g, ln_b)


ATT_TILE = 256
ATT_SEQS = 2
FAR_BUCKET = REL_BUCKETS // 2 - 1


def _t5_bucket(rel):
    nb = REL_BUCKETS // 2
    max_exact = nb // 2
    ret = jnp.where(rel > 0, nb, 0)
    n = jnp.abs(rel)
    nf = jnp.maximum(n, 1).astype(jnp.float32)
    large = max_exact + (jnp.log(nf / max_exact) / math.log(REL_MAX_DIST / max_exact)
                         * (nb - max_exact)).astype(jnp.int32)
    large = jnp.minimum(large, nb - 1)
    return ret + jnp.where(n < max_exact, n, large)


def _near_buckets():
    assert ATT_TILE >= REL_MAX_DIST and ATT_TILE % CHUNK == 0
    qpos = jnp.arange(ATT_TILE, dtype=I32)[:, None]
    kpos = jnp.arange(ATT_TILE, dtype=I32)[None, :]
    prev = _t5_bucket(kpos - ATT_TILE - qpos)
    diag = _t5_bucket(kpos - qpos)
    diag = jnp.where(kpos // CHUNK <= qpos // CHUNK, diag, -1)
    return jnp.stack([prev, diag]).astype(I32)


def _attn_kernel(tab_ref, q_ref, k_ref, v_ref, bkt_ref, bd_ref, qg_ref, kg_ref, lqk_ref, sg_ref, o_ref,
                 qz_ref, kn_ref, v1_ref, bias_ref, *, seq, lambda_init):
    h = pl.program_id(0)
    t = ATT_TILE
    n_tiles = seq // t
    lane = lax.broadcasted_iota(I32, (1, VALUE_DIM), 1)
    first = lane < HEAD_DIM

    @pl.when(pl.program_id(1) == 0)
    def _():
        far = tab_ref[FAR_BUCKET, h]
        for d in range(2):
            bkt = bkt_ref[d]
            tile = jnp.full((t, t), NEG_INF, F32)
            for b in range(REL_BUCKETS):
                tile = jnp.where(bkt == b, (tab_ref[b, h] - far) * LOG2E, tile)
            bias_ref[d] = tile

    def half_norm(x, g):
        x2 = x * x
        hi = x2.astype(BF16)
        lo = (x2 - hi.astype(F32)).astype(BF16)
        ms = (_dot(hi, bd_ref[...]) + _dot(lo, bd_ref[...])) * (1.0 / HEAD_DIM)
        return x * lax.rsqrt(ms + EPS) * g

    q_gain = qg_ref[...] * (HEAD_DIM ** -0.5 * LOG2E)
    qg_maps = (jnp.where(first, q_gain, 0.0), jnp.where(first, 0.0, q_gain))

    n_seq = q_ref.shape[0]

    ones_col = jnp.broadcast_to(jnp.where(lane == 0, 1.0, 0.0).astype(BF16), (t, VALUE_DIM))
    for i in range(n_tiles):
        r0 = i * t
        for b in range(n_seq):
            qn = half_norm(q_ref[b, pl.ds(r0, t), :], 1.0)
            qz_ref[b, 0, pl.ds(r0, t), :] = (qn * qg_maps[0]).astype(BF16)
            qz_ref[b, 1, pl.ds(r0, t), :] = (qn * qg_maps[1]).astype(BF16)
            kn = half_norm(k_ref[b, pl.ds(r0, t), :], kg_ref[...])
            kn_ref[b, :, pl.ds(r0, t)] = jnp.transpose(kn).astype(BF16)
            v1_ref[b, pl.ds(r0, t), :] = jnp.concatenate([v_ref[b, pl.ds(r0, t), :], ones_col], axis=1)

    lqk = lqk_ref[...]
    lam = (jnp.exp(jnp.sum(lqk[0:1] * lqk[1:2], axis=-1, keepdims=True))
           - jnp.exp(jnp.sum(lqk[2:3] * lqk[3:4], axis=-1, keepdims=True)) + lambda_init)

    for i in range(n_tiles):
        q0, kend = i * t, (i + 1) * t
        for b in range(n_seq):
            keys = kn_ref[b, :, 0:kend]
            vals = v1_ref[b, 0:kend, :]
            maps = []
            for m in range(2):
                s = _dot(qz_ref[b, m, q0:q0 + t, :], keys)
                parts = [s[:, kend - t:] + bias_ref[1]]
                if i >= 1:
                    parts.insert(0, s[:, kend - 2 * t:kend - t] + bias_ref[0])
                if i >= 2:
                    parts.insert(0, s[:, :kend - 2 * t])
                s = jnp.concatenate(parts, axis=1) if len(parts) > 1 else parts[0]
                p = jnp.exp2(s - jnp.max(s, axis=-1, keepdims=True))
                pv = _dot(p.astype(BF16), vals)
                maps.append(pv[:, :VALUE_DIM] / pv[:, VALUE_DIM:VALUE_DIM + 1])
            o = maps[0] - lam * maps[1]
            ms = jnp.mean(o * o, axis=-1, keepdims=True)
            o = o * lax.rsqrt(ms + EPS) * sg_ref[...] * (1.0 - lambda_init)
            o_ref[b, q0:q0 + t, :] = o.astype(o_ref.dtype)


def _attention(q3, k3, v3, rel_bias, q_g, k_g, lam_qk, subln_g, lambda_init):
    b, s, _ = q3.shape
    t = ATT_TILE
    nb = ATT_SEQS if b % ATT_SEQS == 0 else 1
    head = lambda j, i: (i, 0, j)
    fixed2 = lambda j, i: (0, 0)
    fixed3 = lambda j, i: (0, 0, 0)
    half = np.arange(VALUE_DIM) // HEAD_DIM
    blockdiag = jnp.asarray(half[:, None] == half[None, :], dtype=BF16)
    return pl.pallas_call(
        functools.partial(_attn_kernel, seq=s, lambda_init=lambda_init),
        grid=(N_HEADS, b // nb),
        in_specs=[pl.BlockSpec(memory_space=pltpu.SMEM),
                  pl.BlockSpec((nb, s, VALUE_DIM), head),
                  pl.BlockSpec((nb, s, VALUE_DIM), head),
                  pl.BlockSpec((nb, s, VALUE_DIM), head),
                  pl.BlockSpec((2, t, t), fixed3),
                  pl.BlockSpec((VALUE_DIM, VALUE_DIM), fixed2),
                  pl.BlockSpec((1, VALUE_DIM), fixed2),
                  pl.BlockSpec((1, VALUE_DIM), fixed2),
                  pl.BlockSpec((4, HEAD_DIM), fixed2),
                  pl.BlockSpec((1, VALUE_DIM), fixed2)],
        out_specs=pl.BlockSpec((nb, s, VALUE_DIM), head),
        out_shape=jax.ShapeDtypeStruct((b, s, N_HEADS * VALUE_DIM), BF16),
        scratch_shapes=[pltpu.VMEM((nb, 2, s, VALUE_DIM), BF16),
                        pltpu.VMEM((nb, VALUE_DIM, s), BF16),
                        pltpu.VMEM((nb, s, 2 * VALUE_DIM), BF16),
                        pltpu.VMEM((2, t, t), F32)],
        compiler_params=_cparams(("arbitrary", "arbitrary")),
        name="diff_attn",
    )(rel_bias, q3, k3, v3, _near_buckets(), blockdiag, q_g, k_g, lam_qk, subln_g)


def _out_proj_kernel(x_ref, c_ref, a_ref, wc_ref, wa_ref, g_ref, rw_ref, rb_ref, tri_ref,
                     h_ref, hn_ref, idx_ref, gate_ref, rank_ref, cnt_ref, carry_ref):
    @pl.when(pl.program_id(0) == 0)
    def _():
        carry_ref[...] = jnp.zeros_like(carry_ref)

    h = x_ref[...] + _dot(c_ref[...], wc_ref[...]) + _dot(a_ref[...], wa_ref[...])
    h_ref[...] = h
    ms = jnp.mean(h * h, axis=-1, keepdims=True)
    hn = h * lax.rsqrt(ms + EPS) * g_ref[...]
    _store_packed_rows(hn_ref, (), hn)

    logits = _dot(hn.astype(BF16), rw_ref[...]) + rb_ref[...]

    tm = logits.shape[0]
    lane = lax.broadcasted_iota(I32, (tm, LANES), 1).astype(F32)
    work = logits
    vals, idxs = [], []
    for _ in range(TOP_K):
        mx = jnp.max(work, axis=-1, keepdims=True)
        ix = jnp.min(jnp.where(work == mx, lane, float(LANES)), axis=-1, keepdims=True)
        vals.append(mx)
        idxs.append(ix)
        work = jnp.where(lane == ix, NEG_INF, work)
    exps = [jnp.exp(v - vals[0]) for v in vals]
    denom = exps[0]
    for e in exps[1:]:
        denom = denom + e

    sel = jnp.zeros((tm, LANES), F32)
    for ix in idxs:
        sel = sel + jnp.where(lane == ix, 1.0, 0.0)
    rank = _dot(tri_ref[...], sel.astype(BF16)) + carry_ref[...]
    carry_ref[...] = carry_ref[...] + jnp.sum(sel, axis=0, keepdims=True)
    cnt_ref[...] = carry_ref[...]

    idx_out = jnp.zeros((tm, LANES), F32)
    gate_out = jnp.zeros((tm, LANES), F32)
    rank_out = jnp.zeros((tm, LANES), F32)
    for k in range(TOP_K):
        rk = jnp.sum(jnp.where(lane == idxs[k], rank, 0.0), axis=-1, keepdims=True)
        idx_out = jnp.where(lane == k, idxs[k], idx_out)
        gate_out = jnp.where(lane == k, exps[k] / denom, gate_out)
        rank_out = jnp.where(lane == k, rk, rank_out)
    idx_ref[...] = jnp.transpose(idx_out)[:SUBLANES].astype(I32)
    rank_ref[...] = jnp.transpose(rank_out)[:SUBLANES].astype(I32)
    gate_ref[...] = gate_out


def _out_proj(x2, conv_o, attn_o, wc, wa, g, rw, rb, tm):
    n, d = x2.shape
    dc, da = conv_o.shape[1], attn_o.shape[1]
    row = lambda i: (i, 0)
    fixed = lambda i: (0, 0)
    tri = jnp.tril(jnp.ones((tm, tm), F32), -1).astype(BF16)
    return pl.pallas_call(
        _out_proj_kernel,
        grid=(n // tm,),
        in_specs=[pl.BlockSpec((tm, d), row),
                  pl.BlockSpec((tm, dc), row),
                  pl.BlockSpec((tm, da), row),
                  pl.BlockSpec((dc, d), fixed),
                  pl.BlockSpec((da, d), fixed),
                  pl.BlockSpec((1, d), fixed),
                  pl.BlockSpec((d, LANES), fixed),
                  pl.BlockSpec((1, LANES), fixed),
                  pl.BlockSpec((tm, tm), fixed)],
        out_specs=[pl.BlockSpec((tm, d), row),
                   pl.BlockSpec((tm * d // (2 * LANES), LANES), row),
                   pl.BlockSpec((SUBLANES, tm), lambda i: (0, i)),
                   pl.BlockSpec((tm, LANES), row),
                   pl.BlockSpec((SUBLANES, tm), lambda i: (0, i)),
                   pl.BlockSpec((1, LANES), fixed)],
        out_shape=[jax.ShapeDtypeStruct((n, d), F32),
                   jax.ShapeDtypeStruct((n * d // (2 * LANES), LANES), U32),
                   jax.ShapeDtypeStruct((SUBLANES, n), I32),
                   jax.ShapeDtypeStruct((n, LANES), F32),
                   jax.ShapeDtypeStruct((SUBLANES, n), I32),
                   jax.ShapeDtypeStruct((1, LANES), F32)],
        scratch_shapes=[pltpu.VMEM((1, LANES), F32)],
        compiler_params=_cparams(("arbitrary",)),
        name="out_proj_router",
    )(x2, conv_o, attn_o, wc, wa, g, rw, rb, tri)


MOE_BLOCK = 1024
MOE_QUARTERS = 4


def _moe_kernel(be_ref, first_ref, wslot_ref, nxt_ref, quarters_ref, nused_ref,
                x_ref, wgu_hbm, wd_hbm, bgu_ref, bd_ref, perm_ref,
                y_ref, wgu_f32, wd_f32, wsem, wgu_bf, wd_bf, x_bf, act_bf):
    i = pl.program_id(0)
    bm = MOE_BLOCK
    n_used = nused_ref[0]
    d = wd_bf.shape[1]
    d_ff = wd_bf.shape[0]
    n_groups = d_ff // LANES

    def weight_copies(e, slot):
        return (pltpu.make_async_copy(wgu_hbm.at[e], wgu_f32.at[slot], wsem.at[0, slot]),
                pltpu.make_async_copy(wd_hbm.at[e], wd_f32.at[slot], wsem.at[1, slot]))

    @pl.when(i == 0)
    def _():
        for c in weight_copies(be_ref[0], 0):
            c.start()

    active = i < n_used

    @pl.when(jnp.logical_and(active, first_ref[i] == 1))
    def _():
        ws = wslot_ref[i]
        for c in weight_copies(be_ref[i], ws):
            c.wait()
        for g in range(n_groups):
            cols = pl.ds(g * MXU_DIM, MXU_DIM)
            wgu_bf[:, cols] = _dot(wgu_f32[ws, :, cols].astype(BF16), perm_ref[...]).astype(BF16)
        wd_bf[...] = wd_f32[ws].astype(BF16)

        @pl.when(nxt_ref[i] >= 0)
        def _():
            for c in weight_copies(nxt_ref[i], 1 - ws):
                c.start()

    def block(m):
        k = d // (2 * LANES)
        for j in range(k):
            lo, hi = _load_packed_tile(x_ref, (), m, k, j)
            x_bf[0:m, j * LANES:(j + 1) * LANES] = lo.astype(BF16)
            x_bf[0:m, (k + j) * LANES:(k + j + 1) * LANES] = hi.astype(BF16)
        for g in range(n_groups):
            gu = (_dot(x_bf[0:m, :], wgu_bf[:, g * MXU_DIM:(g + 1) * MXU_DIM])
                  + bgu_ref[0, :, g * MXU_DIM:(g + 1) * MXU_DIM])
            gate = jnp.minimum(gu[:, :LANES], SWIGLU_LIMIT)
            lin = jnp.clip(gu[:, LANES:], -SWIGLU_LIMIT, SWIGLU_LIMIT)
            act_bf[0:m, g * LANES:(g + 1) * LANES] = (
                gate * jax.nn.sigmoid(SWIGLU_ALPHA * gate) * (lin + 1.0)).astype(BF16)
        _store_packed_rows(y_ref, (), _dot(act_bf[0:m, :], wd_bf[...]) + bd_ref[0])
        if m < bm:
            y_ref[pl.ds(m * k, (bm - m) * k), :] = jnp.zeros(((bm - m) * k, LANES), y_ref.dtype)

    quarter = bm // MOE_QUARTERS
    for nq in range(1, MOE_QUARTERS + 1):
        @pl.when(jnp.logical_and(active, quarters_ref[i] == nq))
        def _(nq=nq):
            block(nq * quarter)

    @pl.when(i >= n_used)
    def _():
        y_ref[...] = jnp.zeros_like(y_ref)


def _deinterleave_perm():
    src = np.arange(MXU_DIM)
    dst = np.where(src % 2 == 0, src // 2, LANES + src // 2)
    p = np.zeros((MXU_DIM, MXU_DIM), np.float32)
    p[src, dst] = 1.0
    return jnp.asarray(p, dtype=BF16)


def _moe(plan, xs, w_gate_up, bgu_p, w_down, b_down):
    n_exp, d, d_gu = w_gate_up.shape
    d_ff = w_down.shape[1]
    bm = MOE_BLOCK
    nblk = plan[0].shape[0]
    slab = bm * d // (2 * LANES)
    by_expert = lambda i, be, *_: (be[i], 0, 0)
    grid_spec = pltpu.PrefetchScalarGridSpec(
        num_scalar_prefetch=6,
        grid=(nblk,),
        in_specs=[
            pl.BlockSpec((slab, LANES), lambda i, *_: (i, 0)),
            pl.BlockSpec(memory_space=pl.ANY),
            pl.BlockSpec(memory_space=pl.ANY),
            pl.BlockSpec((1, 1, d_gu), by_expert),
            pl.BlockSpec((1, 1, d), by_expert),
            pl.BlockSpec((MXU_DIM, MXU_DIM), lambda i, *_: (0, 0)),
        ],
        out_specs=pl.BlockSpec((slab, LANES), lambda i, *_: (i, 0)),
        scratch_shapes=[pltpu.VMEM((2, d, d_gu), F32),
                        pltpu.VMEM((2, d_ff, d), F32),
                        pltpu.SemaphoreType.DMA((2, 2)),
                        pltpu.VMEM((d, d_gu), BF16),
                        pltpu.VMEM((d_ff, d), BF16),
                        pltpu.VMEM((bm, d), BF16),
                        pltpu.VMEM((bm, d_ff), BF16)],
    )
    return pl.pallas_call(
        _moe_kernel,
        grid_spec=grid_spec,
        out_shape=jax.ShapeDtypeStruct((nblk * slab, LANES), U32),
        compiler_params=_cparams(("arbitrary",)),
        name="moe_experts",
    )(*plan, xs, w_gate_up, w_down, bgu_p, b_down, _deinterleave_perm())


SC_CORES = 2
SC_SUBCORES = 16
SC_CHUNK = 64


def _sc_mesh():
    return plsc.VectorSubcoreMesh(core_axis_name="c", subcore_axis_name="s")


def _sc_dispatch(rows3, pos_chunks, n_slots):
    n, r, _ = rows3.shape
    workers = SC_CORES * SC_SUBCORES
    per_worker = n // SC_CHUNK // workers

    assert per_worker % 2 == 0

    def body(rows_hbm, pos_hbm, out_hbm, idx_v, buf_a, buf_b, rsem_a, rsem_b, wsem):
        wid = lax.axis_index("s") * SC_CORES + lax.axis_index("c")
        base = wid * per_worker
        bufs, rsems = (buf_a, buf_b), (rsem_a, rsem_b)

        def read(chunk, s):
            return pltpu.make_async_copy(rows_hbm.at[pl.ds(chunk * SC_CHUNK, SC_CHUNK)], bufs[s], rsems[s])

        read(base, 0).start()

        @pl.loop(0, per_worker, step=2)
        def _(c):
            for s in range(2):
                chunk = base + c + s
                read(chunk, s).wait()

                @pl.when(c + s + 1 < per_worker)
                def _():
                    read(chunk + 1, 1 - s).start()

                pltpu.sync_copy(pos_hbm.at[chunk], idx_v)
                copies = [pltpu.async_copy(bufs[s], out_hbm.at[idx_v.at[k]], wsem) for k in range(TOP_K)]
                for cp in copies:
                    cp.wait()

    return pl.kernel(
        body,
        out_type=jax.ShapeDtypeStruct((n_slots, r, LANES), rows3.dtype),
        mesh=_sc_mesh(),
        scratch_types=[pltpu.VMEM((TOP_K, SC_CHUNK), I32),
                       pltpu.VMEM((SC_CHUNK, r, LANES), rows3.dtype),
                       pltpu.VMEM((SC_CHUNK, r, LANES), rows3.dtype),
                       pltpu.SemaphoreType.DMA,
                       pltpu.SemaphoreType.DMA,
                       pltpu.SemaphoreType.DMA],
        name="sc_dispatch",
    )(rows3, pos_chunks)


def _sc_gather(rows3, pos_chunks, n):
    _, r, _ = rows3.shape
    workers = SC_CORES * SC_SUBCORES
    per_worker = n // SC_CHUNK // workers

    def body(rows_hbm, pos_hbm, out_hbm, idx_v, buf_a, buf_b, sem_a, sem_b):
        wid = lax.axis_index("s") * SC_CORES + lax.axis_index("c")
        bufs, sems = (buf_a, buf_b), (sem_a, sem_b)

        @pl.loop(0, per_worker)
        def _(c):
            chunk = wid * per_worker + c
            pltpu.sync_copy(pos_hbm.at[chunk], idx_v)
            gathers = [None] * TOP_K
            gathers[0] = pltpu.async_copy(rows_hbm.at[idx_v.at[0]], bufs[0], sems[0])
            for k in range(TOP_K):
                if k + 1 < TOP_K:
                    nxt = (k + 1) % 2
                    gathers[k + 1] = pltpu.async_copy(rows_hbm.at[idx_v.at[k + 1]], bufs[nxt], sems[nxt])
                gathers[k].wait()
                pltpu.sync_copy(bufs[k % 2], out_hbm.at[k, pl.ds(chunk * SC_CHUNK, SC_CHUNK)])

    return pl.kernel(
        body,
        out_type=jax.ShapeDtypeStruct((TOP_K, n, r, LANES), rows3.dtype),
        mesh=_sc_mesh(),
        scratch_types=[pltpu.VMEM((TOP_K, SC_CHUNK), I32),
                       pltpu.VMEM((SC_CHUNK, r, LANES), rows3.dtype),
                       pltpu.VMEM((SC_CHUNK, r, LANES), rows3.dtype),
                       pltpu.SemaphoreType.DMA,
                       pltpu.SemaphoreType.DMA],
        name="sc_gather",
    )(rows3, pos_chunks)


COMBINE_TILE = 512


def _combine_kernel(h_ref, gate_ref, y_ref, o_ref):
    tc, d = h_ref.shape
    kt = d // (2 * LANES)
    gates = gate_ref[...]
    for j in range(kt):
        lo_acc = h_ref[:, j * LANES:(j + 1) * LANES]
        hi_acc = h_ref[:, (kt + j) * LANES:(kt + j + 1) * LANES]
        for k in range(TOP_K):
            lo, hi = _load_packed_tile(y_ref, (k,), tc, kt, j)
            g = gates[:, k:k + 1]
            lo_acc = lo_acc + g * lo
            hi_acc = hi_acc + g * hi
        o_ref[:, j * LANES:(j + 1) * LANES] = lo_acc
        o_ref[:, (kt + j) * LANES:(kt + j + 1) * LANES] = hi_acc


def _combine(h, gates, y4):
    n, d = h.shape
    tc = COMBINE_TILE
    slab = tc * d // (2 * LANES)
    row = lambda i: (i, 0)
    return pl.pallas_call(
        _combine_kernel,
        grid=(n // tc,),
        in_specs=[pl.BlockSpec((tc, d), row),
                  pl.BlockSpec((tc, LANES), row),
                  pl.BlockSpec((TOP_K, slab, LANES), lambda i: (0, i, 0))],
        out_specs=pl.BlockSpec((tc, d), row),
        out_shape=jax.ShapeDtypeStruct((n, d), F32),
        input_output_aliases={0: 0},
        compiler_params=_cparams(("parallel",)),
        name="moe_combine",
    )(h, gates, y4)


def _layer(h3, layer, norm_mix_g, w_in, conv_w, conv_b, conv_ln_g, conv_ln_b, q_norm_g, k_norm_g,
           lambda_qk, subln_g, rel_bias, w_out, norm_ffn_g, router_w, router_b,
           w_gate_up, b_gate_up, w_down, b_down):
    b, s, d = h3.shape
    n = b * s
    d_conv = conv_w.shape[-1]
    d_attn = N_HEADS * VALUE_DIM
    lambda_init = 0.8 - 0.6 * math.exp(-0.3 * layer)
    x2 = h3.reshape(n, d)

    u, q, k, v = _in_proj(x2, norm_mix_g.reshape(1, d), w_in.astype(BF16), d_conv, d_attn,
                          tm=min(PROJ_TILE, n))
    conv_o = _conv(u.reshape(b, s, d_conv), conv_w, conv_b.reshape(1, d_conv),
                   conv_ln_g.reshape(1, d_conv), conv_ln_b.reshape(1, d_conv))
    attn_o = _attention(q.reshape(b, s, d_attn), k.reshape(b, s, d_attn), v.reshape(b, s, d_attn),
                        rel_bias, q_norm_g.reshape(1, VALUE_DIM), k_norm_g.reshape(1, VALUE_DIM),
                        lambda_qk, subln_g.reshape(1, VALUE_DIM), lambda_init)

    n_exp = router_w.shape[1]
    rw = jnp.zeros((d, LANES), BF16).at[:, :n_exp].set(router_w.astype(BF16))
    rb = jnp.full((1, LANES), NEG_INF, F32).at[0, :n_exp].set(router_b)
    w_out_bf = w_out.astype(BF16)
    hres, hn, idx, gates, rank, cnt = _out_proj(
        x2, conv_o.reshape(n, d_conv), attn_o.reshape(n, d_attn),
        w_out_bf[:d_conv], w_out_bf[d_conv:], norm_ffn_g.reshape(1, d), rw, rb,
        tm=min(PROJ_TILE, n))

    bm = MOE_BLOCK
    nblk = n * TOP_K // bm + n_exp
    counts = cnt[0, :n_exp].astype(I32)
    padded = (counts + bm - 1) // bm * bm
    eid = jnp.arange(n_exp, dtype=I32)
    pad_end = jnp.sum(jnp.where(eid[None, :] <= eid[:, None], padded[None, :], 0), axis=1)
    pad_start = pad_end - padded
    start_of = jnp.sum(jnp.where(idx[None, :TOP_K] == eid[:, None, None],
                                 pad_start[:, None, None], 0), axis=0)
    pos = start_of + rank[:TOP_K]
    pos_chunks = pos.reshape(TOP_K, n // SC_CHUNK, SC_CHUNK).transpose(1, 0, 2)
    block_start = jnp.arange(nblk, dtype=I32) * bm
    block_expert = jnp.minimum(jnp.sum(block_start[:, None] >= pad_end[None, :], axis=1),
                               n_exp - 1).astype(I32)
    n_used = (pad_end[-1:] // bm).astype(I32)
    blk = jnp.arange(nblk, dtype=I32)
    first = jnp.logical_and(
        jnp.concatenate([jnp.ones((1,), bool), block_expert[1:] != block_expert[:-1]]),
        blk < n_used[0]).astype(I32)
    opened = jnp.sum(jnp.where(blk[None, :] <= blk[:, None], first[None, :], 0), axis=1)
    wslot = ((opened - 1) % 2).astype(I32)
    later_used = jnp.logical_and(eid[None, :] > eid[:, None], counts[None, :] > 0)
    next_used = jnp.min(jnp.where(later_used, eid[None, :], n_exp), axis=1)
    next_used = jnp.where(next_used == n_exp, -1, next_used).astype(I32)
    next_of_block = jnp.sum(jnp.where(block_expert[:, None] == eid[None, :], next_used[None, :], 0),
                            axis=1).astype(I32)
    on_expert = block_expert[:, None] == eid[None, :]
    rows_in_block = jnp.clip(
        jnp.sum(jnp.where(on_expert, (counts + pad_start)[None, :], 0), axis=1) - block_start, 0, bm)
    quarter = bm // MOE_QUARTERS
    quarters = ((rows_in_block + quarter - 1) // quarter).astype(I32)
    plan = (block_expert, first, wslot, next_of_block, quarters, n_used)

    d_ff = w_down.shape[1]
    bgu_p = b_gate_up.reshape(n_exp, d_ff // LANES, LANES, 2).transpose(0, 1, 3, 2).reshape(n_exp, 1, 2 * d_ff)
    r = d // (2 * LANES)
    xs = _sc_dispatch(hn.reshape(n, r, LANES), pos_chunks, nblk * bm)
    yb = _moe(plan, xs.reshape(nblk * bm * r, LANES), w_gate_up, bgu_p, w_down,
              b_down.reshape(n_exp, 1, d))
    y4 = _sc_gather(yb.reshape(nblk * bm, r, LANES), pos_chunks, n)
    out = _combine(hres, gates, y4.reshape(TOP_K, n * r, LANES))
    return out.reshape(b, s, d)


def kernel(x, norm_mix_g, w_in, conv_w, conv_b, conv_ln_g, conv_ln_b, q_norm_g, k_norm_g, lambda_qk,
           subln_g, rel_bias, w_out, norm_ffn_g, router_w, router_b, w_gate_up, b_gate_up, w_down,
           b_down):
    h = x
    for layer in range(norm_mix_g.shape[0]):
        h = _layer(h, layer, norm_mix_g[layer], w_in[layer], conv_w[layer], conv_b[layer],
                   conv_ln_g[layer], conv_ln_b[layer], q_norm_g[layer], k_norm_g[layer],
                   lambda_qk[layer], subln_g[layer], rel_bias, w_out[layer], norm_ffn_g[layer],
                   router_w[layer], router_b[layer], w_gate_up[layer], b_gate_up[layer],
                   w_down[layer], b_down[layer])
    return h
```

```python
import functools
import math

import jax
import jax.numpy as jnp
import numpy as np
from jax import lax
from jax.experimental import pallas as pl
from jax.experimental.pallas import tpu as pltpu
from jax.experimental.pallas import tpu_sc as plsc

F32 = jnp.float32
BF16 = jnp.bfloat16
I32 = jnp.int32
U32 = jnp.uint32

CHUNK = 64
CONV_WIDTH = 31
N_HEADS = 4
HEAD_DIM = 64
VALUE_DIM = 2 * HEAD_DIM
REL_BUCKETS = 32
REL_MAX_DIST = 128
N_EXPERTS = 32
TOP_K = 4
SWIGLU_LIMIT = 7.0
SWIGLU_ALPHA = 1.702
EPS = 1e-5
LOG2E = 1.4426950408889634

LANES = 128
SUBLANES = 8
MXU_DIM = 256
VMEM_LIMIT = 56 * 1024 * 1024

NEG_INF = float("-inf")


def _cparams(sem, vmem=VMEM_LIMIT, flags=None):
    return pltpu.CompilerParams(dimension_semantics=sem, vmem_limit_bytes=vmem, flags=flags)


def _dot(a, b):
    return jnp.dot(a, b, preferred_element_type=F32)


def _dot_nt(a, b):
    return lax.dot_general(a, b, (((1,), (1,)), ((), ())), preferred_element_type=F32)


def _pack_pair(lo, hi):
    lo_bits = lax.bitcast_convert_type(lo.astype(BF16).astype(F32), U32)
    hi_bits = lax.bitcast_convert_type(hi.astype(BF16).astype(F32), U32)
    return (lo_bits >> 16) | hi_bits


def _unpack_pair(w):
    return (lax.bitcast_convert_type(w << 16, F32),
            lax.bitcast_convert_type(w & jnp.uint32(0xFFFF0000), F32))


def _store_packed_rows(ref, lead, x):
    rows, k = x.shape[0], x.shape[1] // (2 * LANES)
    for j in range(k):
        w = _pack_pair(x[:, j * LANES:(j + 1) * LANES], x[:, (k + j) * LANES:(k + j + 1) * LANES])
        ref[(*lead, pl.ds(j, rows, stride=k), slice(None))] = w


def _load_packed_tile(ref, lead, rows, k, j):
    return _unpack_pair(ref[(*lead, pl.ds(j, rows, stride=k), slice(None))])


PROJ_TILE = 1024
def _in_proj_kernel(x_ref, g_ref, w_ref, u_ref, q_ref, k_ref, v_ref, *, d_conv, d_attn):
    x = x_ref[...]
    ms = jnp.mean(x * x, axis=-1, keepdims=True)
    y = (x * lax.rsqrt(ms + EPS) * g_ref[...]).astype(BF16)
    proj = _dot(y, w_ref[...])
    a = proj[:, :d_conv]
    g = proj[:, d_conv:2 * d_conv]
    u_ref[...] = a * jax.nn.sigmoid(g)
    o = 2 * d_conv
    q_ref[...] = proj[:, o:o + d_attn]
    k_ref[...] = proj[:, o + d_attn:o + 2 * d_attn]
    v_ref[...] = proj[:, o + 2 * d_attn:o + 3 * d_attn].astype(BF16)


def _in_proj(x2, g, w_bf, d_conv, d_attn, tm):
    n, d = x2.shape
    d_in = w_bf.shape[1]
    row = lambda i: (i, 0)
    fixed = lambda i: (0, 0)
    return pl.pallas_call(
        functools.partial(_in_proj_kernel, d_conv=d_conv, d_attn=d_attn),
        grid=(n // tm,),
        in_specs=[pl.BlockSpec((tm, d), row),
                  pl.BlockSpec((1, d), fixed),
                  pl.BlockSpec((d, d_in), fixed)],
        out_specs=[pl.BlockSpec((tm, d_conv), row),
                   pl.BlockSpec((tm, d_attn), row),
                   pl.BlockSpec((tm, d_attn), row),
                   pl.BlockSpec((tm, d_attn), row)],
        out_shape=[jax.ShapeDtypeStruct((n, d_conv), F32),
                   jax.ShapeDtypeStruct((n, d_attn), F32),
                   jax.ShapeDtypeStruct((n, d_attn), F32),
                   jax.ShapeDtypeStruct((n, d_attn), BF16)],
        compiler_params=_cparams(("parallel",)),
        name="in_proj",
    )(x2, g, w_bf)


CONV_PAD = 32
CONV_SEQ_TILE = 1024
CONV_TILE = 128
CONV_NORM_TILE = 256


def _conv_kernel(u_ref, prev_ref, w_ref, cb_ref, lg_ref, lb_ref, o_ref, sh_ref, y_ref):
    ts, c = u_ref.shape[1], u_ref.shape[2]
    plen = ts + CONV_PAD
    hist = prev_ref[0]
    hist = jnp.where(pl.program_id(1) > 0, hist, jnp.zeros_like(hist))
    sh_ref[0, pl.ds(0, CONV_PAD), :] = hist
    sh_ref[0, pl.ds(CONV_PAD, ts), :] = u_ref[0]
    sh_ref[0, pl.ds(plen, SUBLANES), :] = jnp.zeros((SUBLANES, c), F32)

    bt = CONV_PAD

    def shift(i, _):
        p0 = pl.multiple_of(i * bt, bt)
        win = sh_ref[0, pl.ds(p0, bt + SUBLANES), :]
        for r in range(1, SUBLANES):
            sh_ref[r, pl.ds(p0, bt), :] = win[r:r + bt]
        return 0

    lax.fori_loop(0, plen // bt, shift, 0)

    off0 = CONV_PAD - (CONV_WIDTH - 1)

    by_shift = {}
    for j in range(CONV_WIDTH):
        a, r = divmod(off0 + j, SUBLANES)
        by_shift.setdefault(r, []).append((a, j))

    for lt in range(c // LANES):
        lanes = slice(lt * LANES, (lt + 1) * LANES)
        taps = [w_ref[pl.ds(j, 1), lanes] for j in range(CONV_WIDTH)]

        def tap_body(i, _, lanes=lanes, taps=taps):
            t0 = pl.multiple_of(i * CONV_TILE, CONV_TILE)
            acc = jnp.zeros((CONV_TILE, LANES), F32)
            for r, group in by_shift.items():
                a_lo = min(a for a, _ in group)
                a_hi = max(a for a, _ in group)
                rows = CONV_TILE + (a_hi - a_lo) * SUBLANES
                start = pl.multiple_of(t0 + a_lo * SUBLANES, SUBLANES)
                win = sh_ref[r, pl.ds(start, rows), lanes]
                for a, j in group:
                    lo = (a - a_lo) * SUBLANES
                    acc = acc + win[lo:lo + CONV_TILE] * taps[j]
            y_ref[pl.ds(t0, CONV_TILE), lanes] = acc
            return 0

        lax.fori_loop(0, ts // CONV_TILE, tap_body, 0)

    def norm_body(i, _):
        t0 = pl.multiple_of(i * CONV_NORM_TILE, CONV_NORM_TILE)
        y = y_ref[pl.ds(t0, CONV_NORM_TILE), :] + cb_ref[...]
        mu = jnp.mean(y, axis=-1, keepdims=True)
        yc = y - mu
        var = jnp.mean(yc * yc, axis=-1, keepdims=True)
        z = yc * lax.rsqrt(var + EPS) * lg_ref[...] + lb_ref[...]
        o_ref[0, pl.ds(t0, CONV_NORM_TILE), :] = (z * jax.nn.sigmoid(z)).astype(o_ref.dtype)
        return 0

    lax.fori_loop(0, ts // CONV_NORM_TILE, norm_body, 0)


def _conv(u3, conv_w, conv_b, ln_g, ln_b):
    b, s, c = u3.shape
    ts = min(CONV_SEQ_TILE, s)
    hist_per_tile = ts // CONV_PAD
    fixed = lambda i, j: (0, 0)
    return pl.pallas_call(
        _conv_kernel,
        grid=(b, s // ts),
        in_specs=[pl.BlockSpec((1, ts, c), lambda i, j: (i, j, 0)),
                  pl.BlockSpec((1, CONV_PAD, c),
                               lambda i, j: (i, jnp.maximum(j * hist_per_tile - 1, 0), 0)),
                  pl.BlockSpec((CONV_WIDTH, c), fixed),
                  pl.BlockSpec((1, c), fixed),
                  pl.BlockSpec((1, c), fixed),
                  pl.BlockSpec((1, c), fixed)],
        out_specs=pl.BlockSpec((1, ts, c), lambda i, j: (i, j, 0)),
        out_shape=jax.ShapeDtypeStruct((b, s, c), BF16),
        scratch_shapes=[pltpu.VMEM((SUBLANES, ts + CONV_PAD + SUBLANES, c), F32),
                        pltpu.VMEM((ts, c), F32)],
        compiler_params=_cparams(("parallel", "parallel")),
        name="conv_mixer",
    )(u3, u3, conv_w, conv_b, ln_g, ln_b)


ATT_TILE = 256
ATT_SEQS = 4
FAR_BUCKET = REL_BUCKETS // 2 - 1


def _t5_bucket(rel):
    nb = REL_BUCKETS // 2
    max_exact = nb // 2
    ret = jnp.where(rel > 0, nb, 0)
    n = jnp.abs(rel)
    nf = jnp.maximum(n, 1).astype(jnp.float32)
    large = max_exact + (jnp.log(nf / max_exact) / math.log(REL_MAX_DIST / max_exact)
                         * (nb - max_exact)).astype(jnp.int32)
    large = jnp.minimum(large, nb - 1)
    return ret + jnp.where(n < max_exact, n, large)


def _near_buckets():
    assert ATT_TILE >= REL_MAX_DIST and ATT_TILE % CHUNK == 0
    qpos = jnp.arange(ATT_TILE, dtype=I32)[:, None]
    kpos = jnp.arange(ATT_TILE, dtype=I32)[None, :]
    prev = _t5_bucket(kpos - ATT_TILE - qpos)
    diag = _t5_bucket(kpos - qpos)
    diag = jnp.where(kpos // CHUNK <= qpos // CHUNK, diag, -1)
    return jnp.stack([prev, diag]).astype(I32)


def _attn_kernel(tab_ref, q_ref, k_ref, v_ref, bkt_ref, bd_ref, qg_ref, kg_ref, lqk_ref, sg_ref, o_ref,
                 qz_ref, kn_ref, v1_ref, bias_ref, *, seq, lambda_init):
    h = pl.program_id(0)
    t = ATT_TILE
    n_tiles = seq // t
    lane = lax.broadcasted_iota(I32, (1, VALUE_DIM), 1)
    first = lane < HEAD_DIM

    @pl.when(pl.program_id(1) == 0)
    def _():
        far = tab_ref[FAR_BUCKET, h]
        for d in range(2):
            bkt = bkt_ref[d]
            tile = jnp.full((t, t), NEG_INF, F32)
            for b in range(REL_BUCKETS):
                tile = jnp.where(bkt == b, (tab_ref[b, h] - far) * LOG2E, tile)
            bias_ref[d] = tile

    def half_norm(x, g):
        x2 = x * x
        hi = x2.astype(BF16)
        lo = (x2 - hi.astype(F32)).astype(BF16)
        ms = (_dot(hi, bd_ref[...]) + _dot(lo, bd_ref[...])) * (1.0 / HEAD_DIM)
        return x * lax.rsqrt(ms + EPS) * g

    q_gain = qg_ref[...] * (HEAD_DIM ** -0.5 * LOG2E)
    qg_maps = (jnp.where(first, q_gain, 0.0), jnp.where(first, 0.0, q_gain))

    n_seq = q_ref.shape[0]

    ones_col = jnp.broadcast_to(jnp.where(lane == 0, 1.0, 0.0).astype(BF16), (t, VALUE_DIM))
    for i in range(n_tiles):
        r0 = i * t
        for b in range(n_seq):
            qn = half_norm(q_ref[b, pl.ds(r0, t), :], 1.0)
            qz_ref[b, 0, pl.ds(r0, t), :] = (qn * qg_maps[0]).astype(BF16)
            qz_ref[b, 1, pl.ds(r0, t), :] = (qn * qg_maps[1]).astype(BF16)
            kn_ref[b, pl.ds(r0, t), :] = half_norm(k_ref[b, pl.ds(r0, t), :], kg_ref[...]).astype(BF16)
            v1_ref[b, pl.ds(r0, t), :] = jnp.concatenate([v_ref[b, pl.ds(r0, t), :], ones_col], axis=1)

    lqk = lqk_ref[...]
    lam = (jnp.exp(jnp.sum(lqk[0:1] * lqk[1:2], axis=-1, keepdims=True))
           - jnp.exp(jnp.sum(lqk[2:3] * lqk[3:4], axis=-1, keepdims=True)) + lambda_init)

    for i in range(n_tiles):
        q0, kend = i * t, (i + 1) * t
        for b in range(n_seq):
            keys = kn_ref[b, 0:kend, :]
            vals = v1_ref[b, 0:kend, :]
            maps = []
            for m in range(2):
                s = _dot_nt(qz_ref[b, m, q0:q0 + t, :], keys)
                parts = [s[:, kend - t:] + bias_ref[1]]
                if i >= 1:
                    parts.insert(0, s[:, kend - 2 * t:kend - t] + bias_ref[0])
                if i >= 2:
                    parts.insert(0, s[:, :kend - 2 * t])
                s = jnp.concatenate(parts, axis=1) if len(parts) > 1 else parts[0]
                p = jnp.exp2(s - jnp.max(s, axis=-1, keepdims=True))
                pv = _dot(p.astype(BF16), vals)
                maps.append(pv[:, :VALUE_DIM] / pv[:, VALUE_DIM:VALUE_DIM + 1])
            o = maps[0] - lam * maps[1]
            ms = jnp.mean(o * o, axis=-1, keepdims=True)
            o = o * lax.rsqrt(ms + EPS) * sg_ref[...] * (1.0 - lambda_init)
            o_ref[b, q0:q0 + t, :] = o.astype(o_ref.dtype)


def _attention(q3, k3, v3, rel_bias, q_g, k_g, lam_qk, subln_g, lambda_init):
    b, s, _ = q3.shape
    t = ATT_TILE
    nb = ATT_SEQS if b % ATT_SEQS == 0 else 1
    head = lambda j, i: (i, 0, j)
    fixed2 = lambda j, i: (0, 0)
    fixed3 = lambda j, i: (0, 0, 0)
    half = np.arange(VALUE_DIM) // HEAD_DIM
    blockdiag = jnp.asarray(half[:, None] == half[None, :], dtype=BF16)
    return pl.pallas_call(
        functools.partial(_attn_kernel, seq=s, lambda_init=lambda_init),
        grid=(N_HEADS, b // nb),
        in_specs=[pl.BlockSpec(memory_space=pltpu.SMEM),
                  pl.BlockSpec((nb, s, VALUE_DIM), head),
                  pl.BlockSpec((nb, s, VALUE_DIM), head),
                  pl.BlockSpec((nb, s, VALUE_DIM), head),
                  pl.BlockSpec((2, t, t), fixed3),
                  pl.BlockSpec((VALUE_DIM, VALUE_DIM), fixed2),
                  pl.BlockSpec((1, VALUE_DIM), fixed2),
                  pl.BlockSpec((1, VALUE_DIM), fixed2),
                  pl.BlockSpec((4, HEAD_DIM), fixed2),
                  pl.BlockSpec((1, VALUE_DIM), fixed2)],
        out_specs=pl.BlockSpec((nb, s, VALUE_DIM), head),
        out_shape=jax.ShapeDtypeStruct((b, s, N_HEADS * VALUE_DIM), BF16),
        scratch_shapes=[pltpu.VMEM((nb, 2, s, VALUE_DIM), BF16),
                        pltpu.VMEM((nb, s, VALUE_DIM), BF16),
                        pltpu.VMEM((nb, s, 2 * VALUE_DIM), BF16),
                        pltpu.VMEM((2, t, t), F32)],
        compiler_params=_cparams(("arbitrary", "arbitrary")),
        name="diff_attn",
    )(rel_bias, q3, k3, v3, _near_buckets(), blockdiag, q_g, k_g, lam_qk, subln_g)


def _out_proj_kernel(x_ref, c_ref, a_ref, wc_ref, wa_ref, g_ref, rw_ref, rb_ref, tri_ref,
                     h_ref, hn_ref, idx_ref, gate_ref, rank_ref, cnt_ref, carry_ref):
    @pl.when(pl.program_id(0) == 0)
    def _():
        carry_ref[...] = jnp.zeros_like(carry_ref)

    h = x_ref[...] + _dot(c_ref[...], wc_ref[...]) + _dot(a_ref[...], wa_ref[...])
    h_ref[...] = h
    ms = jnp.mean(h * h, axis=-1, keepdims=True)
    hn = h * lax.rsqrt(ms + EPS) * g_ref[...]
    _store_packed_rows(hn_ref, (), hn)

    logits = _dot(hn.astype(BF16), rw_ref[...]) + rb_ref[...]

    tm = logits.shape[0]
    lane = lax.broadcasted_iota(I32, (tm, LANES), 1).astype(F32)
    work = logits
    vals, idxs = [], []
    for _ in range(TOP_K):
        mx = jnp.max(work, axis=-1, keepdims=True)
        ix = jnp.min(jnp.where(work == mx, lane, float(LANES)), axis=-1, keepdims=True)
        vals.append(mx)
        idxs.append(ix)
        work = jnp.where(lane == ix, NEG_INF, work)
    exps = [jnp.exp(v - vals[0]) for v in vals]
    denom = exps[0]
    for e in exps[1:]:
        denom = denom + e

    sel = jnp.zeros((tm, LANES), F32)
    for ix in idxs:
        sel = sel + jnp.where(lane == ix, 1.0, 0.0)
    rank = _dot(tri_ref[...], sel.astype(BF16)) + carry_ref[...]
    carry_ref[...] = carry_ref[...] + jnp.sum(sel, axis=0, keepdims=True)
    cnt_ref[...] = carry_ref[...]

    idx_out = jnp.zeros((tm, LANES), F32)
    gate_out = jnp.zeros((tm, LANES), F32)
    rank_out = jnp.zeros((tm, LANES), F32)
    for k in range(TOP_K):
        rk = jnp.sum(jnp.where(lane == idxs[k], rank, 0.0), axis=-1, keepdims=True)
        idx_out = jnp.where(lane == k, idxs[k], idx_out)
        gate_out = jnp.where(lane == k, exps[k] / denom, gate_out)
        rank_out = jnp.where(lane == k, rk, rank_out)
    idx_ref[...] = jnp.transpose(idx_out)[:SUBLANES].astype(I32)
    rank_ref[...] = jnp.transpose(rank_out)[:SUBLANES].astype(I32)
    gate_ref[...] = gate_out


def _out_proj(x2, conv_o, attn_o, wc, wa, g, rw, rb, tm):
    n, d = x2.shape
    dc, da = conv_o.shape[1], attn_o.shape[1]
    row = lambda i: (i, 0)
    fixed = lambda i: (0, 0)
    tri = jnp.tril(jnp.ones((tm, tm), F32), -1).astype(BF16)
    return pl.pallas_call(
        _out_proj_kernel,
        grid=(n // tm,),
        in_specs=[pl.BlockSpec((tm, d), row),
                  pl.BlockSpec((tm, dc), row),
                  pl.BlockSpec((tm, da), row),
                  pl.BlockSpec((dc, d), fixed),
                  pl.BlockSpec((da, d), fixed),
                  pl.BlockSpec((1, d), fixed),
                  pl.BlockSpec((d, LANES), fixed),
                  pl.BlockSpec((1, LANES), fixed),
                  pl.BlockSpec((tm, tm), fixed)],
        out_specs=[pl.BlockSpec((tm, d), row),
                   pl.BlockSpec((tm * d // (2 * LANES), LANES), row),
                   pl.BlockSpec((SUBLANES, tm), lambda i: (0, i)),
                   pl.BlockSpec((tm, LANES), row),
                   pl.BlockSpec((SUBLANES, tm), lambda i: (0, i)),
                   pl.BlockSpec((1, LANES), fixed)],
        out_shape=[jax.ShapeDtypeStruct((n, d), F32),
                   jax.ShapeDtypeStruct((n * d // (2 * LANES), LANES), U32),
                   jax.ShapeDtypeStruct((SUBLANES, n), I32),
                   jax.ShapeDtypeStruct((n, LANES), F32),
                   jax.ShapeDtypeStruct((SUBLANES, n), I32),
                   jax.ShapeDtypeStruct((1, LANES), F32)],
        scratch_shapes=[pltpu.VMEM((1, LANES), F32)],
        compiler_params=_cparams(("arbitrary",)),
        name="out_proj_router",
    )(x2, conv_o, attn_o, wc, wa, g, rw, rb, tri)


MOE_BLOCK = 1024
MOE_QUARTERS = 4


def _moe_kernel(be_ref, first_ref, wslot_ref, nxt_ref, quarters_ref, nused_ref,
                x_ref, wgu_hbm, wd_hbm, bgu_ref, bd_ref, perm_ref,
                y_ref, wgu_f32, wd_f32, wsem, wgu_bf, wd_bf, x_bf, act_bf):
    i = pl.program_id(0)
    bm = MOE_BLOCK
    n_used = nused_ref[0]
    d = wd_bf.shape[1]
    d_ff = wd_bf.shape[0]
    n_groups = d_ff // LANES

    def weight_copies(e, slot):
        return (pltpu.make_async_copy(wgu_hbm.at[e], wgu_f32.at[slot], wsem.at[0, slot]),
                pltpu.make_async_copy(wd_hbm.at[e], wd_f32.at[slot], wsem.at[1, slot]))

    @pl.when(i == 0)
    def _():
        for c in weight_copies(be_ref[0], 0):
            c.start()

    active = i < n_used

    @pl.when(jnp.logical_and(active, first_ref[i] == 1))
    def _():
        ws = wslot_ref[i]
        for c in weight_copies(be_ref[i], ws):
            c.wait()
        for g in range(n_groups):
            cols = pl.ds(g * MXU_DIM, MXU_DIM)
            wgu_bf[:, cols] = _dot(wgu_f32[ws, :, cols].astype(BF16), perm_ref[...]).astype(BF16)
        wd_bf[...] = wd_f32[ws].astype(BF16)

        @pl.when(nxt_ref[i] >= 0)
        def _():
            for c in weight_copies(nxt_ref[i], 1 - ws):
                c.start()

    def block(m):
        k = d // (2 * LANES)
        for j in range(k):
            lo, hi = _load_packed_tile(x_ref, (), m, k, j)
            x_bf[0:m, j * LANES:(j + 1) * LANES] = lo.astype(BF16)
            x_bf[0:m, (k + j) * LANES:(k + j + 1) * LANES] = hi.astype(BF16)
        for g in range(n_groups):
            gu = (_dot(x_bf[0:m, :], wgu_bf[:, g * MXU_DIM:(g + 1) * MXU_DIM])
                  + bgu_ref[0, :, g * MXU_DIM:(g + 1) * MXU_DIM])
            gate = jnp.minimum(gu[:, :LANES], SWIGLU_LIMIT)
            lin = jnp.clip(gu[:, LANES:], -SWIGLU_LIMIT, SWIGLU_LIMIT)
            act_bf[0:m, g * LANES:(g + 1) * LANES] = (
                gate * jax.nn.sigmoid(SWIGLU_ALPHA * gate) * (lin + 1.0)).astype(BF16)
        _store_packed_rows(y_ref, (), _dot(act_bf[0:m, :], wd_bf[...]) + bd_ref[0])
        if m < bm:
            y_ref[pl.ds(m * k, (bm - m) * k), :] = jnp.zeros(((bm - m) * k, LANES), y_ref.dtype)

    quarter = bm // MOE_QUARTERS
    for nq in range(1, MOE_QUARTERS + 1):
        @pl.when(jnp.logical_and(active, quarters_ref[i] == nq))
        def _(nq=nq):
            block(nq * quarter)

    @pl.when(i >= n_used)
    def _():
        y_ref[...] = jnp.zeros_like(y_ref)


def _deinterleave_perm():
    src = np.arange(MXU_DIM)
    dst = np.where(src % 2 == 0, src // 2, LANES + src // 2)
    p = np.zeros((MXU_DIM, MXU_DIM), np.float32)
    p[src, dst] = 1.0
    return jnp.asarray(p, dtype=BF16)


def _moe(plan, xs, w_gate_up, bgu_p, w_down, b_down):
    n_exp, d, d_gu = w_gate_up.shape
    d_ff = w_down.shape[1]
    bm = MOE_BLOCK
    nblk = plan[0].shape[0]
    slab = bm * d // (2 * LANES)
    by_expert = lambda i, be, *_: (be[i], 0, 0)
    grid_spec = pltpu.PrefetchScalarGridSpec(
        num_scalar_prefetch=6,
        grid=(nblk,),
        in_specs=[
            pl.BlockSpec((slab, LANES), lambda i, *_: (i, 0)),
            pl.BlockSpec(memory_space=pl.ANY),
            pl.BlockSpec(memory_space=pl.ANY),
            pl.BlockSpec((1, 1, d_gu), by_expert),
            pl.BlockSpec((1, 1, d), by_expert),
            pl.BlockSpec((MXU_DIM, MXU_DIM), lambda i, *_: (0, 0)),
        ],
        out_specs=pl.BlockSpec((slab, LANES), lambda i, *_: (i, 0)),
        scratch_shapes=[pltpu.VMEM((2, d, d_gu), F32),
                        pltpu.VMEM((2, d_ff, d), F32),
                        pltpu.SemaphoreType.DMA((2, 2)),
                        pltpu.VMEM((d, d_gu), BF16),
                        pltpu.VMEM((d_ff, d), BF16),
                        pltpu.VMEM((bm, d), BF16),
                        pltpu.VMEM((bm, d_ff), BF16)],
    )
    return pl.pallas_call(
        _moe_kernel,
        grid_spec=grid_spec,
        out_shape=jax.ShapeDtypeStruct((nblk * slab, LANES), U32),
        compiler_params=_cparams(("arbitrary",)),
        name="moe_experts",
    )(*plan, xs, w_gate_up, w_down, bgu_p, b_down, _deinterleave_perm())


SC_CORES = 2
SC_SUBCORES = 16
SC_CHUNK = 64


def _sc_mesh():
    return plsc.VectorSubcoreMesh(core_axis_name="c", subcore_axis_name="s")


def _sc_dispatch(rows3, pos_chunks, n_slots):
    n, r, _ = rows3.shape
    workers = SC_CORES * SC_SUBCORES
    per_worker = n // SC_CHUNK // workers

    assert per_worker % 2 == 0

    def body(rows_hbm, pos_hbm, out_hbm, idx_v, buf_a, buf_b, rsem_a, rsem_b, wsem):
        wid = lax.axis_index("s") * SC_CORES + lax.axis_index("c")
        base = wid * per_worker
        bufs, rsems = (buf_a, buf_b), (rsem_a, rsem_b)

        def read(chunk, s):
            return pltpu.make_async_copy(rows_hbm.at[pl.ds(chunk * SC_CHUNK, SC_CHUNK)], bufs[s], rsems[s])

        read(base, 0).start()

        @pl.loop(0, per_worker, step=2)
        def _(c):
            for s in range(2):
                chunk = base + c + s
                read(chunk, s).wait()

                @pl.when(c + s + 1 < per_worker)
                def _():
                    read(chunk + 1, 1 - s).start()

                pltpu.sync_copy(pos_hbm.at[chunk], idx_v)
                copies = [pltpu.async_copy(bufs[s], out_hbm.at[idx_v.at[k]], wsem) for k in range(TOP_K)]
                for cp in copies:
                    cp.wait()

    return pl.kernel(
        body,
        out_type=jax.ShapeDtypeStruct((n_slots, r, LANES), rows3.dtype),
        mesh=_sc_mesh(),
        scratch_types=[pltpu.VMEM((TOP_K, SC_CHUNK), I32),
                       pltpu.VMEM((SC_CHUNK, r, LANES), rows3.dtype),
                       pltpu.VMEM((SC_CHUNK, r, LANES), rows3.dtype),
                       pltpu.SemaphoreType.DMA,
                       pltpu.SemaphoreType.DMA,
                       pltpu.SemaphoreType.DMA],
        name="sc_dispatch",
    )(rows3, pos_chunks)


def _sc_gather(rows3, pos_chunks, n):
    _, r, _ = rows3.shape
    workers = SC_CORES * SC_SUBCORES
    per_worker = n // SC_CHUNK // workers

    def body(rows_hbm, pos_hbm, out_hbm, idx_v, buf_a, buf_b, sem_a, sem_b):
        wid = lax.axis_index("s") * SC_CORES + lax.axis_index("c")
        bufs, sems = (buf_a, buf_b), (sem_a, sem_b)

        @pl.loop(0, per_worker)
        def _(c):
            chunk = wid * per_worker + c
            pltpu.sync_copy(pos_hbm.at[chunk], idx_v)
            gathers = [None] * TOP_K
            gathers[0] = pltpu.async_copy(rows_hbm.at[idx_v.at[0]], bufs[0], sems[0])
            for k in range(TOP_K):
                if k + 1 < TOP_K:
                    nxt = (k + 1) % 2
                    gathers[k + 1] = pltpu.async_copy(rows_hbm.at[idx_v.at[k + 1]], bufs[nxt], sems[nxt])
                gathers[k].wait()
                pltpu.sync_copy(bufs[k % 2], out_hbm.at[k, pl.ds(chunk * SC_CHUNK, SC_CHUNK)])

    return pl.kernel(
        body,
        out_type=jax.ShapeDtypeStruct((TOP_K, n, r, LANES), rows3.dtype),
        mesh=_sc_mesh(),
        scratch_types=[pltpu.VMEM((TOP_K, SC_CHUNK), I32),
                       pltpu.VMEM((SC_CHUNK, r, LANES), rows3.dtype),
                       pltpu.VMEM((SC_CHUNK, r, LANES), rows3.dtype),
                       pltpu.SemaphoreType.DMA,
                       pltpu.SemaphoreType.DMA],
        name="sc_gather",
    )(rows3, pos_chunks)


COMBINE_TILE = 512


def _combine_kernel(h_ref, gate_ref, y_ref, o_ref):
    tc, d = h_ref.shape
    kt = d // (2 * LANES)
    gates = gate_ref[...]
    for j in range(kt):
        lo_acc = h_ref[:, j * LANES:(j + 1) * LANES]
        hi_acc = h_ref[:, (kt + j) * LANES:(kt + j + 1) * LANES]
        for k in range(TOP_K):
            lo, hi = _load_packed_tile(y_ref, (k,), tc, kt, j)
            g = gates[:, k:k + 1]
            lo_acc = lo_acc + g * lo
            hi_acc = hi_acc + g * hi
        o_ref[:, j * LANES:(j + 1) * LANES] = lo_acc
        o_ref[:, (kt + j) * LANES:(kt + j + 1) * LANES] = hi_acc


def _combine(h, gates, y4):
    n, d = h.shape
    tc = COMBINE_TILE
    slab = tc * d // (2 * LANES)
    row = lambda i: (i, 0)
    return pl.pallas_call(
        _combine_kernel,
        grid=(n // tc,),
        in_specs=[pl.BlockSpec((tc, d), row),
                  pl.BlockSpec((tc, LANES), row),
                  pl.BlockSpec((TOP_K, slab, LANES), lambda i: (0, i, 0))],
        out_specs=pl.BlockSpec((tc, d), row),
        out_shape=jax.ShapeDtypeStruct((n, d), F32),
        input_output_aliases={0: 0},
        compiler_params=_cparams(("parallel",)),
        name="moe_combine",
    )(h, gates, y4)


def _layer(h3, layer, norm_mix_g, w_in, conv_w, conv_b, conv_ln_g, conv_ln_b, q_norm_g, k_norm_g,
           lambda_qk, subln_g, rel_bias, w_out, norm_ffn_g, router_w, router_b,
           w_gate_up, b_gate_up, w_down, b_down):
    b, s, d = h3.shape
    n = b * s
    d_conv = conv_w.shape[-1]
    d_attn = N_HEADS * VALUE_DIM
    lambda_init = 0.8 - 0.6 * math.exp(-0.3 * layer)
    x2 = h3.reshape(n, d)

    u, q, k, v = _in_proj(x2, norm_mix_g.reshape(1, d), w_in.astype(BF16), d_conv, d_attn,
                          tm=min(PROJ_TILE, n))
    conv_o = _conv(u.reshape(b, s, d_conv), conv_w, conv_b.reshape(1, d_conv),
                   conv_ln_g.reshape(1, d_conv), conv_ln_b.reshape(1, d_conv))
    attn_o = _attention(q.reshape(b, s, d_attn), k.reshape(b, s, d_attn), v.reshape(b, s, d_attn),
                        rel_bias, q_norm_g.reshape(1, VALUE_DIM), k_norm_g.reshape(1, VALUE_DIM),
                        lambda_qk, subln_g.reshape(1, VALUE_DIM), lambda_init)

    n_exp = router_w.shape[1]
    rw = jnp.zeros((d, LANES), BF16).at[:, :n_exp].set(router_w.astype(BF16))
    rb = jnp.full((1, LANES), NEG_INF, F32).at[0, :n_exp].set(router_b)
    w_out_bf = w_out.astype(BF16)
    hres, hn, idx, gates, rank, cnt = _out_proj(
        x2, conv_o.reshape(n, d_conv), attn_o.reshape(n, d_attn),
        w_out_bf[:d_conv], w_out_bf[d_conv:], norm_ffn_g.reshape(1, d), rw, rb,
        tm=min(PROJ_TILE, n))

    bm = MOE_BLOCK
    nblk = n * TOP_K // bm + n_exp
    counts = cnt[0, :n_exp].astype(I32)
    padded = (counts + bm - 1) // bm * bm
    eid = jnp.arange(n_exp, dtype=I32)
    pad_end = jnp.sum(jnp.where(eid[None, :] <= eid[:, None], padded[None, :], 0), axis=1)
    pad_start = pad_end - padded
    start_of = jnp.sum(jnp.where(idx[None, :TOP_K] == eid[:, None, None],
                                 pad_start[:, None, None], 0), axis=0)
    pos = start_of + rank[:TOP_K]
    pos_chunks = pos.reshape(TOP_K, n // SC_CHUNK, SC_CHUNK).transpose(1, 0, 2)
    block_start = jnp.arange(nblk, dtype=I32) * bm
    block_expert = jnp.minimum(jnp.sum(block_start[:, None] >= pad_end[None, :], axis=1),
                               n_exp - 1).astype(I32)
    n_used = (pad_end[-1:] // bm).astype(I32)
    blk = jnp.arange(nblk, dtype=I32)
    first = jnp.logical_and(
        jnp.concatenate([jnp.ones((1,), bool), block_expert[1:] != block_expert[:-1]]),
        blk < n_used[0]).astype(I32)
    opened = jnp.sum(jnp.where(blk[None, :] <= blk[:, None], first[None, :], 0), axis=1)
    wslot = ((opened - 1) % 2).astype(I32)
    later_used = jnp.logical_and(eid[None, :] > eid[:, None], counts[None, :] > 0)
    next_used = jnp.min(jnp.where(later_used, eid[None, :], n_exp), axis=1)
    next_used = jnp.where(next_used == n_exp, -1, next_used).astype(I32)
    next_of_block = jnp.sum(jnp.where(block_expert[:, None] == eid[None, :], next_used[None, :], 0),
                            axis=1).astype(I32)
    on_expert = block_expert[:, None] == eid[None, :]
    rows_in_block = jnp.clip(
        jnp.sum(jnp.where(on_expert, (counts + pad_start)[None, :], 0), axis=1) - block_start, 0, bm)
    quarter = bm // MOE_QUARTERS
    quarters = ((rows_in_block + quarter - 1) // quarter).astype(I32)
    plan = (block_expert, first, wslot, next_of_block, quarters, n_used)

    d_ff = w_down.shape[1]
    bgu_p = b_gate_up.reshape(n_exp, d_ff // LANES, LANES, 2).transpose(0, 1, 3, 2).reshape(n_exp, 1, 2 * d_ff)
    r = d // (2 * LANES)
    xs = _sc_dispatch(hn.reshape(n, r, LANES), pos_chunks, nblk * bm)
    yb = _moe(plan, xs.reshape(nblk * bm * r, LANES), w_gate_up, bgu_p, w_down,
              b_down.reshape(n_exp, 1, d))
    y4 = _sc_gather(yb.reshape(nblk * bm, r, LANES), pos_chunks, n)
    out = _combine(hres, gates, y4.reshape(TOP_K, n * r, LANES))
    return out.reshape(b, s, d)


def kernel(x, norm_mix_g, w_in, conv_w, conv_b, conv_ln_g, conv_ln_b, q_norm_g, k_norm_g, lambda_qk,
           subln_g, rel_bias, w_out, norm_ffn_g, router_w, router_b, w_gate_up, b_gate_up, w_down,
           b_down):
    h = x
    for layer in range(norm_mix_g.shape[0]):
        h = _layer(h, layer, norm_mix_g[layer], w_in[layer], conv_w[layer], conv_b[layer],
                   conv_ln_g[layer], conv_ln_b[layer], q_norm_g[layer], k_norm_g[layer],
                   lambda_qk[layer], subln_g[layer], rel_bias, w_out[layer], norm_ffn_g[layer],
                   router_w[layer], router_b[layer], w_gate_up[layer], b_gate_up[layer],
                   w_down[layer], b_down[layer])
    return h
```

```python
import functools
import math

import jax
import jax.numpy as jnp
import numpy as np
from jax import lax
from jax.experimental import pallas as pl
from jax.experimental.pallas import tpu as pltpu
from jax.experimental.pallas import tpu_sc as plsc

F32 = jnp.float32
BF16 = jnp.bfloat16
I32 = jnp.int32
U32 = jnp.uint32

CHUNK = 64
CONV_WIDTH = 31
N_HEADS = 4
HEAD_DIM = 64
VALUE_DIM = 2 * HEAD_DIM
REL_BUCKETS = 32
REL_MAX_DIST = 128
N_EXPERTS = 32
TOP_K = 4
SWIGLU_LIMIT = 7.0
SWIGLU_ALPHA = 1.702
EPS = 1e-5
LOG2E = 1.4426950408889634

LANES = 128
SUBLANES = 8
MXU_DIM = 256
VMEM_LIMIT = 56 * 1024 * 1024

NEG_INF = float("-inf")


def _cparams(sem, vmem=VMEM_LIMIT, flags=None):
    return pltpu.CompilerParams(dimension_semantics=sem, vmem_limit_bytes=vmem, flags=flags)


def _dot(a, b):
    return jnp.dot(a, b, preferred_element_type=F32)


def _dot_nt(a, b):
    return lax.dot_general(a, b, (((1,), (1,)), ((), ())), preferred_element_type=F32)


def _pack_pair(lo, hi):
    lo_bits = lax.bitcast_convert_type(lo.astype(BF16).astype(F32), U32)
    hi_bits = lax.bitcast_convert_type(hi.astype(BF16).astype(F32), U32)
    return (lo_bits >> 16) | hi_bits


def _unpack_pair(w):
    return (lax.bitcast_convert_type(w << 16, F32),
            lax.bitcast_convert_type(w & jnp.uint32(0xFFFF0000), F32))


def _store_packed_rows(ref, lead, x):
    rows, k = x.shape[0], x.shape[1] // (2 * LANES)
    for j in range(k):
        w = _pack_pair(x[:, j * LANES:(j + 1) * LANES], x[:, (k + j) * LANES:(k + j + 1) * LANES])
        ref[(*lead, pl.ds(j, rows, stride=k), slice(None))] = w


def _load_packed_tile(ref, lead, rows, k, j):
    return _unpack_pair(ref[(*lead, pl.ds(j, rows, stride=k), slice(None))])


PROJ_TILE = 1024
def _in_proj_kernel(x_ref, g_ref, w_ref, u_ref, q_ref, k_ref, v_ref, *, d_conv, d_attn):
    x = x_ref[...]
    ms = jnp.mean(x * x, axis=-1, keepdims=True)
    y = (x * lax.rsqrt(ms + EPS) * g_ref[...]).astype(BF16)
    proj = _dot(y, w_ref[...])
    a = proj[:, :d_conv]
    g = proj[:, d_conv:2 * d_conv]
    u_ref[...] = a * jax.nn.sigmoid(g)
    o = 2 * d_conv
    q_ref[...] = proj[:, o:o + d_attn]
    k_ref[...] = proj[:, o + d_attn:o + 2 * d_attn]
    v_ref[...] = proj[:, o + 2 * d_attn:o + 3 * d_attn].astype(BF16)


def _in_proj(x2, g, w_bf, d_conv, d_attn, tm):
    n, d = x2.shape
    d_in = w_bf.shape[1]
    row = lambda i: (i, 0)
    fixed = lambda i: (0, 0)
    return pl.pallas_call(
        functools.partial(_in_proj_kernel, d_conv=d_conv, d_attn=d_attn),
        grid=(n // tm,),
        in_specs=[pl.BlockSpec((tm, d), row),
                  pl.BlockSpec((1, d), fixed),
                  pl.BlockSpec((d, d_in), fixed)],
        out_specs=[pl.BlockSpec((tm, d_conv), row),
                   pl.BlockSpec((tm, d_attn), row),
                   pl.BlockSpec((tm, d_attn), row),
                   pl.BlockSpec((tm, d_attn), row)],
        out_shape=[jax.ShapeDtypeStruct((n, d_conv), F32),
                   jax.ShapeDtypeStruct((n, d_attn), F32),
                   jax.ShapeDtypeStruct((n, d_attn), F32),
                   jax.ShapeDtypeStruct((n, d_attn), BF16)],
        compiler_params=_cparams(("parallel",)),
        name="in_proj",
    )(x2, g, w_bf)


CONV_PAD = 32
CONV_SEQ_TILE = 1024
CONV_TILE = 128
CONV_NORM_TILE = 256


def _conv_kernel(u_ref, prev_ref, w_ref, cb_ref, lg_ref, lb_ref, o_ref, sh_ref, y_ref):
    ts, c = u_ref.shape[1], u_ref.shape[2]
    plen = ts + CONV_PAD
    hist = prev_ref[0]
    hist = jnp.where(pl.program_id(1) > 0, hist, jnp.zeros_like(hist))
    sh_ref[0, pl.ds(0, CONV_PAD), :] = hist
    sh_ref[0, pl.ds(CONV_PAD, ts), :] = u_ref[0]
    sh_ref[0, pl.ds(plen, SUBLANES), :] = jnp.zeros((SUBLANES, c), F32)

    bt = CONV_PAD

    def shift(i, _):
        p0 = pl.multiple_of(i * bt, bt)
        win = sh_ref[0, pl.ds(p0, bt + SUBLANES), :]
        for r in range(1, SUBLANES):
            sh_ref[r, pl.ds(p0, bt), :] = win[r:r + bt]
        return 0

    lax.fori_loop(0, plen // bt, shift, 0)

    off0 = CONV_PAD - (CONV_WIDTH - 1)

    by_shift = {}
    for j in range(CONV_WIDTH):
        a, r = divmod(off0 + j, SUBLANES)
        by_shift.setdefault(r, []).append((a, j))

    for lt in range(c // LANES):
        lanes = slice(lt * LANES, (lt + 1) * LANES)
        taps = [w_ref[pl.ds(j, 1), lanes] for j in range(CONV_WIDTH)]

        def tap_body(i, _, lanes=lanes, taps=taps):
            t0 = pl.multiple_of(i * CONV_TILE, CONV_TILE)
            acc = jnp.zeros((CONV_TILE, LANES), F32)
            for r, group in by_shift.items():
                a_lo = min(a for a, _ in group)
                a_hi = max(a for a, _ in group)
                rows = CONV_TILE + (a_hi - a_lo) * SUBLANES
                start = pl.multiple_of(t0 + a_lo * SUBLANES, SUBLANES)
                win = sh_ref[r, pl.ds(start, rows), lanes]
                for a, j in group:
                    lo = (a - a_lo) * SUBLANES
                    acc = acc + win[lo:lo + CONV_TILE] * taps[j]
            y_ref[pl.ds(t0, CONV_TILE), lanes] = acc
            return 0

        lax.fori_loop(0, ts // CONV_TILE, tap_body, 0)

    def norm_body(i, _):
        t0 = pl.multiple_of(i * CONV_NORM_TILE, CONV_NORM_TILE)
        y = y_ref[pl.ds(t0, CONV_NORM_TILE), :] + cb_ref[...]
        mu = jnp.mean(y, axis=-1, keepdims=True)
        yc = y - mu
        var = jnp.mean(yc * yc, axis=-1, keepdims=True)
        z = yc * lax.rsqrt(var + EPS) * lg_ref[...] + lb_ref[...]
        o_ref[0, pl.ds(t0, CONV_NORM_TILE), :] = (z * jax.nn.sigmoid(z)).astype(o_ref.dtype)
        return 0

    lax.fori_loop(0, ts // CONV_NORM_TILE, norm_body, 0)


def _conv(u3, conv_w, conv_b, ln_g, ln_b):
    b, s, c = u3.shape
    ts = min(CONV_SEQ_TILE, s)
    hist_per_tile = ts // CONV_PAD
    fixed = lambda i, j: (0, 0)
    return pl.pallas_call(
        _conv_kernel,
        grid=(b, s // ts),
        in_specs=[pl.BlockSpec((1, ts, c), lambda i, j: (i, j, 0)),
                  pl.BlockSpec((1, CONV_PAD, c),
                               lambda i, j: (i, jnp.maximum(j * hist_per_tile - 1, 0), 0)),
                  pl.BlockSpec((CONV_WIDTH, c), fixed),
                  pl.BlockSpec((1, c), fixed),
                  pl.BlockSpec((1, c), fixed),
                  pl.BlockSpec((1, c), fixed)],
        out_specs=pl.BlockSpec((1, ts, c), lambda i, j: (i, j, 0)),
        out_shape=jax.ShapeDtypeStruct((b, s, c), BF16),
        scratch_shapes=[pltpu.VMEM((SUBLANES, ts + CONV_PAD + SUBLANES, c), F32),
                        pltpu.VMEM((ts, c), F32)],
        compiler_params=_cparams(("parallel", "parallel")),
        name="conv_mixer",
    )(u3, u3, conv_w, conv_b, ln_g, ln_b)


ATT_TILE = 256
ATT_SEQS = 2
FAR_BUCKET = REL_BUCKETS // 2 - 1


def _t5_bucket(rel):
    nb = REL_BUCKETS // 2
    max_exact = nb // 2
    ret = jnp.where(rel > 0, nb, 0)
    n = jnp.abs(rel)
    nf = jnp.maximum(n, 1).astype(jnp.float32)
    large = max_exact + (jnp.log(nf / max_exact) / math.log(REL_MAX_DIST / max_exact)
                         * (nb - max_exact)).astype(jnp.int32)
    large = jnp.minimum(large, nb - 1)
    return ret + jnp.where(n < max_exact, n, large)


def _near_buckets():
    assert ATT_TILE >= REL_MAX_DIST and ATT_TILE % CHUNK == 0
    qpos = jnp.arange(ATT_TILE, dtype=I32)[:, None]
    kpos = jnp.arange(ATT_TILE, dtype=I32)[None, :]
    prev = _t5_bucket(kpos - ATT_TILE - qpos)
    diag = _t5_bucket(kpos - qpos)
    diag = jnp.where(kpos // CHUNK <= qpos // CHUNK, diag, -1)
    return jnp.stack([prev, diag]).astype(I32)


def _attn_kernel(tab_ref, q_ref, k_ref, v_ref, bkt_ref, bd_ref, qg_ref, kg_ref, lqk_ref, sg_ref, o_ref,
                 qz_ref, kn_ref, v1_ref, bias_ref, *, seq, lambda_init):
    h = pl.program_id(0)
    t = ATT_TILE
    n_tiles = seq // t
    lane = lax.broadcasted_iota(I32, (1, VALUE_DIM), 1)
    first = lane < HEAD_DIM

    @pl.when(pl.program_id(1) == 0)
    def _():
        far = tab_ref[FAR_BUCKET, h]
        for d in range(2):
            bkt = bkt_ref[d]
            tile = jnp.full((t, t), NEG_INF, F32)
            for b in range(REL_BUCKETS):
                tile = jnp.where(bkt == b, (tab_ref[b, h] - far) * LOG2E, tile)
            bias_ref[d] = tile

    def half_norm(x, g):
        x2 = x * x
        hi = x2.astype(BF16)
        lo = (x2 - hi.astype(F32)).astype(BF16)
        ms = (_dot(hi, bd_ref[...]) + _dot(lo, bd_ref[...])) * (1.0 / HEAD_DIM)
        return x * lax.rsqrt(ms + EPS) * g

    q_gain = qg_ref[...] * (HEAD_DIM ** -0.5 * LOG2E)
    qg_maps = (jnp.where(first, q_gain, 0.0), jnp.where(first, 0.0, q_gain))

    n_seq = q_ref.shape[0]

    ones_col = jnp.broadcast_to(jnp.where(lane == 0, 1.0, 0.0).astype(BF16), (t, VALUE_DIM))
    for i in range(n_tiles):
        r0 = i * t
        for b in range(n_seq):
            qn = half_norm(q_ref[b, pl.ds(r0, t), :], 1.0)
            qz_ref[b, 0, pl.ds(r0, t), :] = (qn * qg_maps[0]).astype(BF16)
            qz_ref[b, 1, pl.ds(r0, t), :] = (qn * qg_maps[1]).astype(BF16)
            kn_ref[b, pl.ds(r0, t), :] = half_norm(k_ref[b, pl.ds(r0, t), :], kg_ref[...]).astype(BF16)
            v1_ref[b, pl.ds(r0, t), :] = jnp.concatenate([v_ref[b, pl.ds(r0, t), :], ones_col], axis=1)

    lqk = lqk_ref[...]
    lam = (jnp.exp(jnp.sum(lqk[0:1] * lqk[1:2], axis=-1, keepdims=True))
           - jnp.exp(jnp.sum(lqk[2:3] * lqk[3:4], axis=-1, keepdims=True)) + lambda_init)

    for i in range(n_tiles):
        q0, kend = i * t, (i + 1) * t
        for b in range(n_seq):
            keys = kn_ref[b, 0:kend, :]
            vals = v1_ref[b, 0:kend, :]
            maps = []
            for m in range(2):
                s = _dot_nt(qz_ref[b, m, q0:q0 + t, :], keys)
                parts = [s[:, kend - t:] + bias_ref[1]]
                if i >= 1:
                    parts.insert(0, s[:, kend - 2 * t:kend - t] + bias_ref[0])
                if i >= 2:
                    parts.insert(0, s[:, :kend - 2 * t])
                s = jnp.concatenate(parts, axis=1) if len(parts) > 1 else parts[0]
                p = jnp.exp2(s - jnp.max(s, axis=-1, keepdims=True))
                pv = _dot(p.astype(BF16), vals)
                maps.append(pv[:, :VALUE_DIM] / pv[:, VALUE_DIM:VALUE_DIM + 1])
            o = maps[0] - lam * maps[1]
            ms = jnp.mean(o * o, axis=-1, keepdims=True)
            o = o * lax.rsqrt(ms + EPS) * sg_ref[...] * (1.0 - lambda_init)
            o_ref[b, q0:q0 + t, :] = o.astype(o_ref.dtype)


def _attention(q3, k3, v3, rel_bias, q_g, k_g, lam_qk, subln_g, lambda_init):
    b, s, _ = q3.shape
    t = ATT_TILE
    nb = ATT_SEQS if b % ATT_SEQS == 0 else 1
    head = lambda j, i: (i, 0, j)
    fixed2 = lambda j, i: (0, 0)
    fixed3 = lambda j, i: (0, 0, 0)
    half = np.arange(VALUE_DIM) // HEAD_DIM
    blockdiag = jnp.asarray(half[:, None] == half[None, :], dtype=BF16)
    return pl.pallas_call(
        functools.partial(_attn_kernel, seq=s, lambda_init=lambda_init),
        grid=(N_HEADS, b // nb),
        in_specs=[pl.BlockSpec(memory_space=pltpu.SMEM),
                  pl.BlockSpec((nb, s, VALUE_DIM), head),
                  pl.BlockSpec((nb, s, VALUE_DIM), head),
                  pl.BlockSpec((nb, s, VALUE_DIM), head),
                  pl.BlockSpec((2, t, t), fixed3),
                  pl.BlockSpec((VALUE_DIM, VALUE_DIM), fixed2),
                  pl.BlockSpec((1, VALUE_DIM), fixed2),
                  pl.BlockSpec((1, VALUE_DIM), fixed2),
                  pl.BlockSpec((4, HEAD_DIM), fixed2),
                  pl.BlockSpec((1, VALUE_DIM), fixed2)],
        out_specs=pl.BlockSpec((nb, s, VALUE_DIM), head),
        out_shape=jax.ShapeDtypeStruct((b, s, N_HEADS * VALUE_DIM), BF16),
        scratch_shapes=[pltpu.VMEM((nb, 2, s, VALUE_DIM), BF16),
                        pltpu.VMEM((nb, s, VALUE_DIM), BF16),
                        pltpu.VMEM((nb, s, 2 * VALUE_DIM), BF16),
                        pltpu.VMEM((2, t, t), F32)],
        compiler_params=_cparams(("arbitrary", "arbitrary")),
        name="diff_attn",
    )(rel_bias, q3, k3, v3, _near_buckets(), blockdiag, q_g, k_g, lam_qk, subln_g)


def _out_proj_kernel(x_ref, c_ref, a_ref, wc_ref, wa_ref, g_ref, rw_ref, rb_ref, tri_ref,
                     h_ref, hn_ref, idx_ref, gate_ref, rank_ref, cnt_ref, carry_ref):
    @pl.when(pl.program_id(0) == 0)
    def _():
        carry_ref[...] = jnp.zeros_like(carry_ref)

    h = x_ref[...] + _dot(c_ref[...], wc_ref[...]) + _dot(a_ref[...], wa_ref[...])
    h_ref[...] = h
    ms = jnp.mean(h * h, axis=-1, keepdims=True)
    hn = h * lax.rsqrt(ms + EPS) * g_ref[...]
    _store_packed_rows(hn_ref, (), hn)

    logits = _dot(hn.astype(BF16), rw_ref[...]) + rb_ref[...]

    tm = logits.shape[0]
    lane = lax.broadcasted_iota(I32, (tm, LANES), 1).astype(F32)
    work = logits
    vals, idxs = [], []
    for _ in range(TOP_K):
        mx = jnp.max(work, axis=-1, keepdims=True)
        ix = jnp.min(jnp.where(work == mx, lane, float(LANES)), axis=-1, keepdims=True)
        vals.append(mx)
        idxs.append(ix)
        work = jnp.where(lane == ix, NEG_INF, work)
    exps = [jnp.exp(v - vals[0]) for v in vals]
    denom = exps[0]
    for e in exps[1:]:
        denom = denom + e

    sel = jnp.zeros((tm, LANES), F32)
    for ix in idxs:
        sel = sel + jnp.where(lane == ix, 1.0, 0.0)
    rank = _dot(tri_ref[...], sel.astype(BF16)) + carry_ref[...]
    carry_ref[...] = carry_ref[...] + jnp.sum(sel, axis=0, keepdims=True)
    cnt_ref[...] = carry_ref[...]

    idx_out = jnp.zeros((tm, LANES), F32)
    gate_out = jnp.zeros((tm, LANES), F32)
    rank_out = jnp.zeros((tm, LANES), F32)
    for k in range(TOP_K):
        rk = jnp.sum(jnp.where(lane == idxs[k], rank, 0.0), axis=-1, keepdims=True)
        idx_out = jnp.where(lane == k, idxs[k], idx_out)
        gate_out = jnp.where(lane == k, exps[k] / denom, gate_out)
        rank_out = jnp.where(lane == k, rk, rank_out)
    idx_ref[...] = jnp.transpose(idx_out)[:SUBLANES].astype(I32)
    rank_ref[...] = jnp.transpose(rank_out)[:SUBLANES].astype(I32)
    gate_ref[...] = gate_out


def _out_proj(x2, conv_o, attn_o, wc, wa, g, rw, rb, tm):
    n, d = x2.shape
    dc, da = conv_o.shape[1], attn_o.shape[1]
    row = lambda i: (i, 0)
    fixed = lambda i: (0, 0)
    tri = jnp.tril(jnp.ones((tm, tm), F32), -1).astype(BF16)
    return pl.pallas_call(
        _out_proj_kernel,
        grid=(n // tm,),
        in_specs=[pl.BlockSpec((tm, d), row),
                  pl.BlockSpec((tm, dc), row),
                  pl.BlockSpec((tm, da), row),
                  pl.BlockSpec((dc, d), fixed),
                  pl.BlockSpec((da, d), fixed),
                  pl.BlockSpec((1, d), fixed),
                  pl.BlockSpec((d, LANES), fixed),
                  pl.BlockSpec((1, LANES), fixed),
                  pl.BlockSpec((tm, tm), fixed)],
        out_specs=[pl.BlockSpec((tm, d), row),
                   pl.BlockSpec((tm * d // (2 * LANES), LANES), row),
                   pl.BlockSpec((SUBLANES, tm), lambda i: (0, i)),
                   pl.BlockSpec((tm, LANES), row),
                   pl.BlockSpec((SUBLANES, tm), lambda i: (0, i)),
                   pl.BlockSpec((1, LANES), fixed)],
        out_shape=[jax.ShapeDtypeStruct((n, d), F32),
                   jax.ShapeDtypeStruct((n * d // (2 * LANES), LANES), U32),
                   jax.ShapeDtypeStruct((SUBLANES, n), I32),
                   jax.ShapeDtypeStruct((n, LANES), F32),
                   jax.ShapeDtypeStruct((SUBLANES, n), I32),
                   jax.ShapeDtypeStruct((1, LANES), F32)],
        scratch_shapes=[pltpu.VMEM((1, LANES), F32)],
        compiler_params=_cparams(("arbitrary",)),
        name="out_proj_router",
    )(x2, conv_o, attn_o, wc, wa, g, rw, rb, tri)


MOE_BLOCK = 1024
MOE_QUARTERS = 4


def _moe_kernel(be_ref, first_ref, wslot_ref, nxt_ref, quarters_ref, nused_ref,
                x_ref, wgu_hbm, wd_hbm, bgu_ref, bd_ref, perm_ref,
                y_ref, wgu_f32, wd_f32, wsem, wgu_bf, wd_bf, x_bf, act_bf):
    i = pl.program_id(0)
    bm = MOE_BLOCK
    n_used = nused_ref[0]
    d = wd_bf.shape[1]
    d_ff = wd_bf.shape[0]
    n_groups = d_ff // LANES

    def weight_copies(e, slot):
        return (pltpu.make_async_copy(wgu_hbm.at[e], wgu_f32.at[slot], wsem.at[0, slot]),
                pltpu.make_async_copy(wd_hbm.at[e], wd_f32.at[slot], wsem.at[1, slot]))

    @pl.when(i == 0)
    def _():
        for c in weight_copies(be_ref[0], 0):
            c.start()

    active = i < n_used

    @pl.when(jnp.logical_and(active, first_ref[i] == 1))
    def _():
        ws = wslot_ref[i]
        for c in weight_copies(be_ref[i], ws):
            c.wait()
        for g in range(n_groups):
            cols = pl.ds(g * MXU_DIM, MXU_DIM)
            wgu_bf[:, cols] = _dot(wgu_f32[ws, :, cols].astype(BF16), perm_ref[...]).astype(BF16)
        wd_bf[...] = wd_f32[ws].astype(BF16)

        @pl.when(nxt_ref[i] >= 0)
        def _():
            for c in weight_copies(nxt_ref[i], 1 - ws):
                c.start()

    def block(m):
        k = d // (2 * LANES)
        for j in range(k):
            lo, hi = _load_packed_tile(x_ref, (), m, k, j)
            x_bf[0:m, j * LANES:(j + 1) * LANES] = lo.astype(BF16)
            x_bf[0:m, (k + j) * LANES:(k + j + 1) * LANES] = hi.astype(BF16)
        for g in range(n_groups):
            gu = (_dot(x_bf[0:m, :], wgu_bf[:, g * MXU_DIM:(g + 1) * MXU_DIM])
                  + bgu_ref[0, :, g * MXU_DIM:(g + 1) * MXU_DIM])
            gate = jnp.minimum(gu[:, :LANES], SWIGLU_LIMIT)
            lin = jnp.clip(gu[:, LANES:], -SWIGLU_LIMIT, SWIGLU_LIMIT)
            act_bf[0:m, g * LANES:(g + 1) * LANES] = (
                gate * jax.nn.sigmoid(SWIGLU_ALPHA * gate) * (lin + 1.0)).astype(BF16)
        _store_packed_rows(y_ref, (), _dot(act_bf[0:m, :], wd_bf[...]) + bd_ref[0])
        if m < bm:
            y_ref[pl.ds(m * k, (bm - m) * k), :] = jnp.zeros(((bm - m) * k, LANES), y_ref.dtype)

    quarter = bm // MOE_QUARTERS
    for nq in range(1, MOE_QUARTERS + 1):
        @pl.when(jnp.logical_and(active, quarters_ref[i] == nq))
        def _(nq=nq):
            block(nq * quarter)

    @pl.when(i >= n_used)
    def _():
        y_ref[...] = jnp.zeros_like(y_ref)


def _deinterleave_perm():
    src = np.arange(MXU_DIM)
    dst = np.where(src % 2 == 0, src // 2, LANES + src // 2)
    p = np.zeros((MXU_DIM, MXU_DIM), np.float32)
    p[src, dst] = 1.0
    return jnp.asarray(p, dtype=BF16)


def _moe(plan, xs, w_gate_up, bgu_p, w_down, b_down):
    n_exp, d, d_gu = w_gate_up.shape
    d_ff = w_down.shape[1]
    bm = MOE_BLOCK
    nblk = plan[0].shape[0]
    slab = bm * d // (2 * LANES)
    by_expert = lambda i, be, *_: (be[i], 0, 0)
    grid_spec = pltpu.PrefetchScalarGridSpec(
        num_scalar_prefetch=6,
        grid=(nblk,),
        in_specs=[
            pl.BlockSpec((slab, LANES), lambda i, *s: (jnp.minimum(i, s[-1][0] - 1), 0)),
            pl.BlockSpec(memory_space=pl.ANY),
            pl.BlockSpec(memory_space=pl.ANY),
            pl.BlockSpec((1, 1, d_gu), by_expert),
            pl.BlockSpec((1, 1, d), by_expert),
            pl.BlockSpec((MXU_DIM, MXU_DIM), lambda i, *_: (0, 0)),
        ],
        out_specs=pl.BlockSpec((slab, LANES),
                               lambda i, *s: (jnp.where(i < s[-1][0], i, nblk - 1), 0)),
        scratch_shapes=[pltpu.VMEM((2, d, d_gu), F32),
                        pltpu.VMEM((2, d_ff, d), F32),
                        pltpu.SemaphoreType.DMA((2, 2)),
                        pltpu.VMEM((d, d_gu), BF16),
                        pltpu.VMEM((d_ff, d), BF16),
                        pltpu.VMEM((bm, d), BF16),
                        pltpu.VMEM((bm, d_ff), BF16)],
    )
    return pl.pallas_call(
        _moe_kernel,
        grid_spec=grid_spec,
        out_shape=jax.ShapeDtypeStruct((nblk * slab, LANES), U32),
        compiler_params=_cparams(("arbitrary",)),
        name="moe_experts",
    )(*plan, xs, w_gate_up, w_down, bgu_p, b_down, _deinterleave_perm())


SC_CORES = 2
SC_SUBCORES = 16
SC_CHUNK = 64


def _sc_mesh():
    return plsc.VectorSubcoreMesh(core_axis_name="c", subcore_axis_name="s")


def _sc_dispatch(rows3, pos_chunks, n_slots):
    n, r, _ = rows3.shape
    workers = SC_CORES * SC_SUBCORES
    per_worker = n // SC_CHUNK // workers

    assert per_worker % 2 == 0

    def body(rows_hbm, pos_hbm, out_hbm, idx_v, buf_a, buf_b, rsem_a, rsem_b, wsem):
        wid = lax.axis_index("s") * SC_CORES + lax.axis_index("c")
        base = wid * per_worker
        bufs, rsems = (buf_a, buf_b), (rsem_a, rsem_b)

        def read(chunk, s):
            return pltpu.make_async_copy(rows_hbm.at[pl.ds(chunk * SC_CHUNK, SC_CHUNK)], bufs[s], rsems[s])

        read(base, 0).start()

        @pl.loop(0, per_worker, step=2)
        def _(c):
            for s in range(2):
                chunk = base + c + s
                read(chunk, s).wait()

                @pl.when(c + s + 1 < per_worker)
                def _():
                    read(chunk + 1, 1 - s).start()

                pltpu.sync_copy(pos_hbm.at[chunk], idx_v)
                copies = [pltpu.async_copy(bufs[s], out_hbm.at[idx_v.at[k]], wsem) for k in range(TOP_K)]
                for cp in copies:
                    cp.wait()

    return pl.kernel(
        body,
        out_type=jax.ShapeDtypeStruct((n_slots, r, LANES), rows3.dtype),
        mesh=_sc_mesh(),
        scratch_types=[pltpu.VMEM((TOP_K, SC_CHUNK), I32),
                       pltpu.VMEM((SC_CHUNK, r, LANES), rows3.dtype),
                       pltpu.VMEM((SC_CHUNK, r, LANES), rows3.dtype),
                       pltpu.SemaphoreType.DMA,
                       pltpu.SemaphoreType.DMA,
                       pltpu.SemaphoreType.DMA],
        name="sc_dispatch",
    )(rows3, pos_chunks)


def _sc_gather(rows3, pos_chunks, n):
    _, r, _ = rows3.shape
    workers = SC_CORES * SC_SUBCORES
    per_worker = n // SC_CHUNK // workers

    def body(rows_hbm, pos_hbm, out_hbm, idx_v, buf_a, buf_b, sem_a, sem_b):
        wid = lax.axis_index("s") * SC_CORES + lax.axis_index("c")
        bufs, sems = (buf_a, buf_b), (sem_a, sem_b)

        @pl.loop(0, per_worker)
        def _(c):
            chunk = wid * per_worker + c
            pltpu.sync_copy(pos_hbm.at[chunk], idx_v)
            gathers = [None] * TOP_K
            gathers[0] = pltpu.async_copy(rows_hbm.at[idx_v.at[0]], bufs[0], sems[0])
            for k in range(TOP_K):
                if k + 1 < TOP_K:
                    nxt = (k + 1) % 2
                    gathers[k + 1] = pltpu.async_copy(rows_hbm.at[idx_v.at[k + 1]], bufs[nxt], sems[nxt])
                gathers[k].wait()
                pltpu.sync_copy(bufs[k % 2], out_hbm.at[k, pl.ds(chunk * SC_CHUNK, SC_CHUNK)])

    return pl.kernel(
        body,
        out_type=jax.ShapeDtypeStruct((TOP_K, n, r, LANES), rows3.dtype),
        mesh=_sc_mesh(),
        scratch_types=[pltpu.VMEM((TOP_K, SC_CHUNK), I32),
                       pltpu.VMEM((SC_CHUNK, r, LANES), rows3.dtype),
                       pltpu.VMEM((SC_CHUNK, r, LANES), rows3.dtype),
                       pltpu.SemaphoreType.DMA,
                       pltpu.SemaphoreType.DMA],
        name="sc_gather",
    )(rows3, pos_chunks)


COMBINE_TILE = 1024


def _combine_kernel(h_ref, gate_ref, y_ref, o_ref):
    tc, d = h_ref.shape
    kt = d // (2 * LANES)
    gates = gate_ref[...]
    for j in range(kt):
        lo_acc = h_ref[:, j * LANES:(j + 1) * LANES]
        hi_acc = h_ref[:, (kt + j) * LANES:(kt + j + 1) * LANES]
        for k in range(TOP_K):
            lo, hi = _load_packed_tile(y_ref, (k,), tc, kt, j)
            g = gates[:, k:k + 1]
            lo_acc = lo_acc + g * lo
            hi_acc = hi_acc + g * hi
        o_ref[:, j * LANES:(j + 1) * LANES] = lo_acc
        o_ref[:, (kt + j) * LANES:(kt + j + 1) * LANES] = hi_acc


def _combine(h, gates, y4):
    n, d = h.shape
    tc = COMBINE_TILE
    slab = tc * d // (2 * LANES)
    row = lambda i: (i, 0)
    return pl.pallas_call(
        _combine_kernel,
        grid=(n // tc,),
        in_specs=[pl.BlockSpec((tc, d), row),
                  pl.BlockSpec((tc, LANES), row),
                  pl.BlockSpec((TOP_K, slab, LANES), lambda i: (0, i, 0))],
        out_specs=pl.BlockSpec((tc, d), row),
        out_shape=jax.ShapeDtypeStruct((n, d), F32),
        input_output_aliases={0: 0},
        compiler_params=_cparams(("parallel",)),
        name="moe_combine",
    )(h, gates, y4)


def _layer(h3, layer, norm_mix_g, w_in, conv_w, conv_b, conv_ln_g, conv_ln_b, q_norm_g, k_norm_g,
           lambda_qk, subln_g, rel_bias, w_out, norm_ffn_g, router_w, router_b,
           w_gate_up, b_gate_up, w_down, b_down):
    b, s, d = h3.shape
    n = b * s
    d_conv = conv_w.shape[-1]
    d_attn = N_HEADS * VALUE_DIM
    lambda_init = 0.8 - 0.6 * math.exp(-0.3 * layer)
    x2 = h3.reshape(n, d)

    u, q, k, v = _in_proj(x2, norm_mix_g.reshape(1, d), w_in.astype(BF16), d_conv, d_attn,
                          tm=min(PROJ_TILE, n))
    conv_o = _conv(u.reshape(b, s, d_conv), conv_w, conv_b.reshape(1, d_conv),
                   conv_ln_g.reshape(1, d_conv), conv_ln_b.reshape(1, d_conv))
    attn_o = _attention(q.reshape(b, s, d_attn), k.reshape(b, s, d_attn), v.reshape(b, s, d_attn),
                        rel_bias, q_norm_g.reshape(1, VALUE_DIM), k_norm_g.reshape(1, VALUE_DIM),
                        lambda_qk, subln_g.reshape(1, VALUE_DIM), lambda_init)

    n_exp = router_w.shape[1]
    rw = jnp.zeros((d, LANES), BF16).at[:, :n_exp].set(router_w.astype(BF16))
    rb = jnp.full((1, LANES), NEG_INF, F32).at[0, :n_exp].set(router_b)
    w_out_bf = w_out.astype(BF16)
    hres, hn, idx, gates, rank, cnt = _out_proj(
        x2, conv_o.reshape(n, d_conv), attn_o.reshape(n, d_attn),
        w_out_bf[:d_conv], w_out_bf[d_conv:], norm_ffn_g.reshape(1, d), rw, rb,
        tm=min(PROJ_TILE, n))

    bm = MOE_BLOCK
    nblk = n * TOP_K // bm + n_exp
    counts = cnt[0, :n_exp].astype(I32)
    padded = (counts + bm - 1) // bm * bm
    eid = jnp.arange(n_exp, dtype=I32)
    pad_end = jnp.sum(jnp.where(eid[None, :] <= eid[:, None], padded[None, :], 0), axis=1)
    pad_start = pad_end - padded
    start_of = jnp.sum(jnp.where(idx[None, :TOP_K] == eid[:, None, None],
                                 pad_start[:, None, None], 0), axis=0)
    pos = start_of + rank[:TOP_K]
    pos_chunks = pos.reshape(TOP_K, n // SC_CHUNK, SC_CHUNK).transpose(1, 0, 2)
    block_start = jnp.arange(nblk, dtype=I32) * bm
    block_expert = jnp.minimum(jnp.sum(block_start[:, None] >= pad_end[None, :], axis=1),
                               n_exp - 1).astype(I32)
    n_used = (pad_end[-1:] // bm).astype(I32)
    blk = jnp.arange(nblk, dtype=I32)
    first = jnp.logical_and(
        jnp.concatenate([jnp.ones((1,), bool), block_expert[1:] != block_expert[:-1]]),
        blk < n_used[0]).astype(I32)
    opened = jnp.sum(jnp.where(blk[None, :] <= blk[:, None], first[None, :], 0), axis=1)
    wslot = ((opened - 1) % 2).astype(I32)
    later_used = jnp.logical_and(eid[None, :] > eid[:, None], counts[None, :] > 0)
    next_used = jnp.min(jnp.where(later_used, eid[None, :], n_exp), axis=1)
    next_used = jnp.where(next_used == n_exp, -1, next_used).astype(I32)
    next_of_block = jnp.sum(jnp.where(block_expert[:, None] == eid[None, :], next_used[None, :], 0),
                            axis=1).astype(I32)
    on_expert = block_expert[:, None] == eid[None, :]
    rows_in_block = jnp.clip(
        jnp.sum(jnp.where(on_expert, (counts + pad_start)[None, :], 0), axis=1) - block_start, 0, bm)
    quarter = bm // MOE_QUARTERS
    quarters = ((rows_in_block + quarter - 1) // quarter).astype(I32)
    plan = (block_expert, first, wslot, next_of_block, quarters, n_used)

    d_ff = w_down.shape[1]
    bgu_p = b_gate_up.reshape(n_exp, d_ff // LANES, LANES, 2).transpose(0, 1, 3, 2).reshape(n_exp, 1, 2 * d_ff)
    r = d // (2 * LANES)
    xs = _sc_dispatch(hn.reshape(n, r, LANES), pos_chunks, nblk * bm)
    yb = _moe(plan, xs.reshape(nblk * bm * r, LANES), w_gate_up, bgu_p, w_down,
              b_down.reshape(n_exp, 1, d))
    y4 = _sc_gather(yb.reshape(nblk * bm, r, LANES), pos_chunks, n)
    out = _combine(hres, gates, y4.reshape(TOP_K, n * r, LANES))
    return out.reshape(b, s, d)


def kernel(x, norm_mix_g, w_in, conv_w, conv_b, conv_ln_g, conv_ln_b, q_norm_g, k_norm_g, lambda_qk,
           subln_g, rel_bias, w_out, norm_ffn_g, router_w, router_b, w_gate_up, b_gate_up, w_down,
           b_down):
    h = x
    for layer in range(norm_mix_g.shape[0]):
        h = _layer(h, layer, norm_mix_g[layer], w_in[layer], conv_w[layer], conv_b[layer],
                   conv_ln_g[layer], conv_ln_b[layer], q_norm_g[layer], k_norm_g[layer],
                   lambda_qk[layer], subln_g[layer], rel_bias, w_out[layer], norm_ffn_g[layer],
                   router_w[layer], router_b[layer], w_gate_up[layer], b_gate_up[layer],
                   w_down[layer], b_down[layer])
    return h
```

```python
import functools
import math

import jax
import jax.numpy as jnp
import numpy as np
from jax import lax
from jax.experimental import pallas as pl
from jax.experimental.pallas import tpu as pltpu
from jax.experimental.pallas import tpu_sc as plsc

F32 = jnp.float32
BF16 = jnp.bfloat16
I32 = jnp.int32
U32 = jnp.uint32

CHUNK = 64
CONV_WIDTH = 31
N_HEADS = 4
HEAD_DIM = 64
VALUE_DIM = 2 * HEAD_DIM
REL_BUCKETS = 32
REL_MAX_DIST = 128
N_EXPERTS = 32
TOP_K = 4
SWIGLU_LIMIT = 7.0
SWIGLU_ALPHA = 1.702
EPS = 1e-5
LOG2E = 1.4426950408889634

LANES = 128
SUBLANES = 8
MXU_DIM = 256
VMEM_LIMIT = 56 * 1024 * 1024

NEG_INF = float("-inf")


def _cparams(sem, vmem=VMEM_LIMIT, flags=None):
    return pltpu.CompilerParams(dimension_semantics=sem, vmem_limit_bytes=vmem, flags=flags)


def _dot(a, b):
    return jnp.dot(a, b, preferred_element_type=F32)


def _dot_nt(a, b):
    return lax.dot_general(a, b, (((1,), (1,)), ((), ())), preferred_element_type=F32)


def _pack_pair(lo, hi):
    lo_bits = lax.bitcast_convert_type(lo.astype(BF16).astype(F32), U32)
    hi_bits = lax.bitcast_convert_type(hi.astype(BF16).astype(F32), U32)
    return (lo_bits >> 16) | hi_bits


def _unpack_pair(w):
    return (lax.bitcast_convert_type(w << 16, F32),
            lax.bitcast_convert_type(w & jnp.uint32(0xFFFF0000), F32))


def _store_packed_rows(ref, lead, x):
    rows, k = x.shape[0], x.shape[1] // (2 * LANES)
    for j in range(k):
        w = _pack_pair(x[:, j * LANES:(j + 1) * LANES], x[:, (k + j) * LANES:(k + j + 1) * LANES])
        ref[(*lead, pl.ds(j, rows, stride=k), slice(None))] = w


def _load_packed_tile(ref, lead, rows, k, j):
    return _unpack_pair(ref[(*lead, pl.ds(j, rows, stride=k), slice(None))])


PROJ_TILE = 1024
def _in_proj_kernel(x_ref, g_ref, w_ref, u_ref, q_ref, k_ref, v_ref, *, d_conv, d_attn):
    x = x_ref[...]
    ms = jnp.mean(x * x, axis=-1, keepdims=True)
    y = (x * lax.rsqrt(ms + EPS) * g_ref[...]).astype(BF16)
    proj = _dot(y, w_ref[...])
    a = proj[:, :d_conv]
    g = proj[:, d_conv:2 * d_conv]
    u_ref[...] = a * jax.nn.sigmoid(g)
    o = 2 * d_conv
    q_ref[...] = proj[:, o:o + d_attn]
    k_ref[...] = proj[:, o + d_attn:o + 2 * d_attn]
    v_ref[...] = proj[:, o + 2 * d_attn:o + 3 * d_attn].astype(BF16)


def _in_proj(x2, g, w_bf, d_conv, d_attn, tm):
    n, d = x2.shape
    d_in = w_bf.shape[1]
    row = lambda i: (i, 0)
    fixed = lambda i: (0, 0)
    return pl.pallas_call(
        functools.partial(_in_proj_kernel, d_conv=d_conv, d_attn=d_attn),
        grid=(n // tm,),
        in_specs=[pl.BlockSpec((tm, d), row),
                  pl.BlockSpec((1, d), fixed),
                  pl.BlockSpec((d, d_in), fixed)],
        out_specs=[pl.BlockSpec((tm, d_conv), row),
                   pl.BlockSpec((tm, d_attn), row),
                   pl.BlockSpec((tm, d_attn), row),
                   pl.BlockSpec((tm, d_attn), row)],
        out_shape=[jax.ShapeDtypeStruct((n, d_conv), F32),
                   jax.ShapeDtypeStruct((n, d_attn), F32),
                   jax.ShapeDtypeStruct((n, d_attn), F32),
                   jax.ShapeDtypeStruct((n, d_attn), BF16)],
        compiler_params=_cparams(("parallel",)),
        name="in_proj",
    )(x2, g, w_bf)


CONV_PAD = 32
CONV_SEQ_TILE = 1024
CONV_TILE = 128
CONV_NORM_TILE = 256


def _conv_kernel(u_ref, prev_ref, w_ref, cb_ref, lg_ref, lb_ref, o_ref, sh_ref, y_ref):
    ts, c = u_ref.shape[1], u_ref.shape[2]
    plen = ts + CONV_PAD
    hist = prev_ref[0]
    hist = jnp.where(pl.program_id(1) > 0, hist, jnp.zeros_like(hist))
    sh_ref[0, pl.ds(0, CONV_PAD), :] = hist
    sh_ref[0, pl.ds(CONV_PAD, ts), :] = u_ref[0]
    sh_ref[0, pl.ds(plen, SUBLANES), :] = jnp.zeros((SUBLANES, c), F32)

    bt = CONV_PAD

    def shift(i, _):
        p0 = pl.multiple_of(i * bt, bt)
        win = sh_ref[0, pl.ds(p0, bt + SUBLANES), :]
        for r in range(1, SUBLANES):
            sh_ref[r, pl.ds(p0, bt), :] = win[r:r + bt]
        return 0

    lax.fori_loop(0, plen // bt, shift, 0)

    off0 = CONV_PAD - (CONV_WIDTH - 1)

    by_shift = {}
    for j in range(CONV_WIDTH):
        a, r = divmod(off0 + j, SUBLANES)
        by_shift.setdefault(r, []).append((a, j))

    for lt in range(c // LANES):
        lanes = slice(lt * LANES, (lt + 1) * LANES)
        taps = [w_ref[pl.ds(j, 1), lanes] for j in range(CONV_WIDTH)]

        def tap_body(i, _, lanes=lanes, taps=taps):
            t0 = pl.multiple_of(i * CONV_TILE, CONV_TILE)
            acc = jnp.zeros((CONV_TILE, LANES), F32)
            for r, group in by_shift.items():
                a_lo = min(a for a, _ in group)
                a_hi = max(a for a, _ in group)
                rows = CONV_TILE + (a_hi - a_lo) * SUBLANES
                start = pl.multiple_of(t0 + a_lo * SUBLANES, SUBLANES)
                win = sh_ref[r, pl.ds(start, rows), lanes]
                for a, j in group:
                    lo = (a - a_lo) * SUBLANES
                    acc = acc + win[lo:lo + CONV_TILE] * taps[j]
            y_ref[pl.ds(t0, CONV_TILE), lanes] = acc
            return 0

        lax.fori_loop(0, ts // CONV_TILE, tap_body, 0)

    def norm_body(i, _):
        t0 = pl.multiple_of(i * CONV_NORM_TILE, CONV_NORM_TILE)
        y = y_ref[pl.ds(t0, CONV_NORM_TILE), :] + cb_ref[...]
        mu = jnp.mean(y, axis=-1, keepdims=True)
        yc = y - mu
        var = jnp.mean(yc * yc, axis=-1, keepdims=True)
        z = yc * lax.rsqrt(var + EPS) * lg_ref[...] + lb_ref[...]
        o_ref[0, pl.ds(t0, CONV_NORM_TILE), :] = (z * jax.nn.sigmoid(z)).astype(o_ref.dtype)
        return 0

    lax.fori_loop(0, ts // CONV_NORM_TILE, norm_body, 0)


def _conv(u3, conv_w, conv_b, ln_g, ln_b):
    b, s, c = u3.shape
    ts = min(CONV_SEQ_TILE, s)
    hist_per_tile = ts // CONV_PAD
    fixed = lambda i, j: (0, 0)
    return pl.pallas_call(
        _conv_kernel,
        grid=(b, s // ts),
        in_specs=[pl.BlockSpec((1, ts, c), lambda i, j: (i, j, 0)),
                  pl.BlockSpec((1, CONV_PAD, c),
                               lambda i, j: (i, jnp.maximum(j * hist_per_tile - 1, 0), 0)),
                  pl.BlockSpec((CONV_WIDTH, c), fixed),
                  pl.BlockSpec((1, c), fixed),
                  pl.BlockSpec((1, c), fixed),
                  pl.BlockSpec((1, c), fixed)],
        out_specs=pl.BlockSpec((1, ts, c), lambda i, j: (i, j, 0)),
        out_shape=jax.ShapeDtypeStruct((b, s, c), BF16),
        scratch_shapes=[pltpu.VMEM((SUBLANES, ts + CONV_PAD + SUBLANES, c), F32),
                        pltpu.VMEM((ts, c), F32)],
        compiler_params=_cparams(("parallel", "parallel")),
        name="conv_mixer",
    )(u3, u3, conv_w, conv_b, ln_g, ln_b)


ATT_TILE = 256
ATT_SEQS = 2
FAR_BUCKET = REL_BUCKETS // 2 - 1


def _t5_bucket(rel):
    nb = REL_BUCKETS // 2
    max_exact = nb // 2
    ret = jnp.where(rel > 0, nb, 0)
    n = jnp.abs(rel)
    nf = jnp.maximum(n, 1).astype(jnp.float32)
    large = max_exact + (jnp.log(nf / max_exact) / math.log(REL_MAX_DIST / max_exact)
                         * (nb - max_exact)).astype(jnp.int32)
    large = jnp.minimum(large, nb - 1)
    return ret + jnp.where(n < max_exact, n, large)


def _near_buckets():
    assert ATT_TILE >= REL_MAX_DIST and ATT_TILE % CHUNK == 0
    qpos = jnp.arange(ATT_TILE, dtype=I32)[:, None]
    kpos = jnp.arange(ATT_TILE, dtype=I32)[None, :]
    prev = _t5_bucket(kpos - ATT_TILE - qpos)
    diag = _t5_bucket(kpos - qpos)
    diag = jnp.where(kpos // CHUNK <= qpos // CHUNK, diag, -1)
    return jnp.stack([prev, diag]).astype(I32)


def _attn_kernel(tab_ref, q_ref, k_ref, v_ref, bkt_ref, bd_ref, qg_ref, kg_ref, lqk_ref, sg_ref, o_ref,
                 qz_ref, kn_ref, v1_ref, bias_ref, *, seq, lambda_init):
    h = pl.program_id(0)
    t = ATT_TILE
    n_tiles = seq // t
    lane = lax.broadcasted_iota(I32, (1, VALUE_DIM), 1)
    first = lane < HEAD_DIM

    @pl.when(pl.program_id(1) == 0)
    def _():
        far = tab_ref[FAR_BUCKET, h]
        for d in range(2):
            bkt = bkt_ref[d]
            tile = jnp.full((t, t), NEG_INF, F32)
            for b in range(REL_BUCKETS):
                tile = jnp.where(bkt == b, (tab_ref[b, h] - far) * LOG2E, tile)
            bias_ref[d] = tile

    def half_norm(x, g):
        x2 = x * x
        hi = x2.astype(BF16)
        lo = (x2 - hi.astype(F32)).astype(BF16)
        ms = (_dot(hi, bd_ref[...]) + _dot(lo, bd_ref[...])) * (1.0 / HEAD_DIM)
        return x * lax.rsqrt(ms + EPS) * g

    q_gain = qg_ref[...] * (HEAD_DIM ** -0.5 * LOG2E)
    qg_maps = (jnp.where(first, q_gain, 0.0), jnp.where(first, 0.0, q_gain))

    n_seq = q_ref.shape[0]

    ones_col = jnp.broadcast_to(jnp.where(lane == 0, 1.0, 0.0).astype(BF16), (t, VALUE_DIM))
    for i in range(n_tiles):
        r0 = i * t
        for b in range(n_seq):
            qn = half_norm(q_ref[b, pl.ds(r0, t), :], 1.0)
            qz_ref[b, 0, pl.ds(r0, t), :] = (qn * qg_maps[0]).astype(BF16)
            qz_ref[b, 1, pl.ds(r0, t), :] = (qn * qg_maps[1]).astype(BF16)
            kn_ref[b, pl.ds(r0, t), :] = half_norm(k_ref[b, pl.ds(r0, t), :], kg_ref[...]).astype(BF16)
            v1_ref[b, pl.ds(r0, t), :] = jnp.concatenate([v_ref[b, pl.ds(r0, t), :], ones_col], axis=1)

    lqk = lqk_ref[...]
    lam = (jnp.exp(jnp.sum(lqk[0:1] * lqk[1:2], axis=-1, keepdims=True))
           - jnp.exp(jnp.sum(lqk[2:3] * lqk[3:4], axis=-1, keepdims=True)) + lambda_init)

    for i in range(n_tiles):
        q0, kend = i * t, (i + 1) * t
        for b in range(n_seq):
            keys = kn_ref[b, 0:kend, :]
            vals = v1_ref[b, 0:kend, :]
            maps = []
            for m in range(2):
                s = _dot_nt(qz_ref[b, m, q0:q0 + t, :], keys)
                parts = [s[:, kend - t:] + bias_ref[1]]
                if i >= 1:
                    parts.insert(0, s[:, kend - 2 * t:kend - t] + bias_ref[0])
                if i >= 2:
                    parts.insert(0, s[:, :kend - 2 * t])
                s = jnp.concatenate(parts, axis=1) if len(parts) > 1 else parts[0]
                p = jnp.exp2(s - jnp.max(s, axis=-1, keepdims=True))
                pv = _dot(p.astype(BF16), vals)
                maps.append(pv[:, :VALUE_DIM] / pv[:, VALUE_DIM:VALUE_DIM + 1])
            o = maps[0] - lam * maps[1]
            ms = jnp.mean(o * o, axis=-1, keepdims=True)
            o = o * lax.rsqrt(ms + EPS) * sg_ref[...] * (1.0 - lambda_init)
            o_ref[b, q0:q0 + t, :] = o.astype(o_ref.dtype)


def _attention(q3, k3, v3, rel_bias, q_g, k_g, lam_qk, subln_g, lambda_init):
    b, s, _ = q3.shape
    t = ATT_TILE
    nb = ATT_SEQS if b % ATT_SEQS == 0 else 1
    head = lambda j, i: (i, 0, j)
    fixed2 = lambda j, i: (0, 0)
    fixed3 = lambda j, i: (0, 0, 0)
    half = np.arange(VALUE_DIM) // HEAD_DIM
    blockdiag = jnp.asarray(half[:, None] == half[None, :], dtype=BF16)
    return pl.pallas_call(
        functools.partial(_attn_kernel, seq=s, lambda_init=lambda_init),
        grid=(N_HEADS, b // nb),
        in_specs=[pl.BlockSpec(memory_space=pltpu.SMEM),
                  pl.BlockSpec((nb, s, VALUE_DIM), head),
                  pl.BlockSpec((nb, s, VALUE_DIM), head),
                  pl.BlockSpec((nb, s, VALUE_DIM), head),
                  pl.BlockSpec((2, t, t), fixed3),
                  pl.BlockSpec((VALUE_DIM, VALUE_DIM), fixed2),
                  pl.BlockSpec((1, VALUE_DIM), fixed2),
                  pl.BlockSpec((1, VALUE_DIM), fixed2),
                  pl.BlockSpec((4, HEAD_DIM), fixed2),
                  pl.BlockSpec((1, VALUE_DIM), fixed2)],
        out_specs=pl.BlockSpec((nb, s, VALUE_DIM), head),
        out_shape=jax.ShapeDtypeStruct((b, s, N_HEADS * VALUE_DIM), BF16),
        scratch_shapes=[pltpu.VMEM((nb, 2, s, VALUE_DIM), BF16),
                        pltpu.VMEM((nb, s, VALUE_DIM), BF16),
                        pltpu.VMEM((nb, s, 2 * VALUE_DIM), BF16),
                        pltpu.VMEM((2, t, t), F32)],
        compiler_params=_cparams(("arbitrary", "arbitrary")),
        name="diff_attn",
    )(rel_bias, q3, k3, v3, _near_buckets(), blockdiag, q_g, k_g, lam_qk, subln_g)


def _out_proj_kernel(x_ref, c_ref, a_ref, wc_ref, wa_ref, g_ref, rw_ref, rb_ref, tri_ref,
                     h_ref, hn_ref, idx_ref, gate_ref, rank_ref, cnt_ref, carry_ref):
    @pl.when(pl.program_id(0) == 0)
    def _():
        carry_ref[...] = jnp.zeros_like(carry_ref)

    h = x_ref[...] + _dot(c_ref[...], wc_ref[...]) + _dot(a_ref[...], wa_ref[...])
    h_ref[...] = h
    ms = jnp.mean(h * h, axis=-1, keepdims=True)
    hn = h * lax.rsqrt(ms + EPS) * g_ref[...]
    _store_packed_rows(hn_ref, (), hn)

    logits = _dot(hn.astype(BF16), rw_ref[...]) + rb_ref[...]

    tm = logits.shape[0]
    lane = lax.broadcasted_iota(I32, (tm, LANES), 1).astype(F32)
    work = logits
    vals, idxs = [], []
    for _ in range(TOP_K):
        mx = jnp.max(work, axis=-1, keepdims=True)
        ix = jnp.min(jnp.where(work == mx, lane, float(LANES)), axis=-1, keepdims=True)
        vals.append(mx)
        idxs.append(ix)
        work = jnp.where(lane == ix, NEG_INF, work)
    exps = [jnp.exp(v - vals[0]) for v in vals]
    denom = exps[0]
    for e in exps[1:]:
        denom = denom + e

    sel = jnp.zeros((tm, LANES), F32)
    for ix in idxs:
        sel = sel + jnp.where(lane == ix, 1.0, 0.0)
    rank = _dot(tri_ref[...], sel.astype(BF16)) + carry_ref[...]
    carry_ref[...] = carry_ref[...] + jnp.sum(sel, axis=0, keepdims=True)
    cnt_ref[...] = carry_ref[...]

    idx_out = jnp.zeros((tm, LANES), F32)
    gate_out = jnp.zeros((tm, LANES), F32)
    rank_out = jnp.zeros((tm, LANES), F32)
    for k in range(TOP_K):
        rk = jnp.sum(jnp.where(lane == idxs[k], rank, 0.0), axis=-1, keepdims=True)
        idx_out = jnp.where(lane == k, idxs[k], idx_out)
        gate_out = jnp.where(lane == k, exps[k] / denom, gate_out)
        rank_out = jnp.where(lane == k, rk, rank_out)
    idx_ref[...] = jnp.transpose(idx_out)[:SUBLANES].astype(I32)
    rank_ref[...] = jnp.transpose(rank_out)[:SUBLANES].astype(I32)
    gate_ref[...] = gate_out


def _out_proj(x2, conv_o, attn_o, wc, wa, g, rw, rb, tm):
    n, d = x2.shape
    dc, da = conv_o.shape[1], attn_o.shape[1]
    row = lambda i: (i, 0)
    fixed = lambda i: (0, 0)
    tri = jnp.tril(jnp.ones((tm, tm), F32), -1).astype(BF16)
    return pl.pallas_call(
        _out_proj_kernel,
        grid=(n // tm,),
        in_specs=[pl.BlockSpec((tm, d), row),
                  pl.BlockSpec((tm, dc), row),
                  pl.BlockSpec((tm, da), row),
                  pl.BlockSpec((dc, d), fixed),
                  pl.BlockSpec((da, d), fixed),
                  pl.BlockSpec((1, d), fixed),
                  pl.BlockSpec((d, LANES), fixed),
                  pl.BlockSpec((1, LANES), fixed),
                  pl.BlockSpec((tm, tm), fixed)],
        out_specs=[pl.BlockSpec((tm, d), row),
                   pl.BlockSpec((tm * d // (2 * LANES), LANES), row),
                   pl.BlockSpec((SUBLANES, tm), lambda i: (0, i)),
                   pl.BlockSpec((tm, LANES), row),
                   pl.BlockSpec((SUBLANES, tm), lambda i: (0, i)),
                   pl.BlockSpec((1, LANES), fixed)],
        out_shape=[jax.ShapeDtypeStruct((n, d), F32),
                   jax.ShapeDtypeStruct((n * d // (2 * LANES), LANES), U32),
                   jax.ShapeDtypeStruct((SUBLANES, n), I32),
                   jax.ShapeDtypeStruct((n, LANES), F32),
                   jax.ShapeDtypeStruct((SUBLANES, n), I32),
                   jax.ShapeDtypeStruct((1, LANES), F32)],
        scratch_shapes=[pltpu.VMEM((1, LANES), F32)],
        compiler_params=_cparams(("arbitrary",)),
        name="out_proj_router",
    )(x2, conv_o, attn_o, wc, wa, g, rw, rb, tri)


MOE_BLOCK = 1024
MOE_QUARTERS = 8


def _moe_kernel(be_ref, first_ref, wslot_ref, nxt_ref, quarters_ref, nused_ref,
                x_ref, wgu_hbm, wd_hbm, bgu_ref, bd_ref, perm_ref,
                y_ref, wgu_f32, wd_f32, wsem, wgu_bf, wd_bf, x_bf, act_bf):
    i = pl.program_id(0)
    bm = MOE_BLOCK
    n_used = nused_ref[0]
    d = wd_bf.shape[1]
    d_ff = wd_bf.shape[0]
    n_groups = d_ff // LANES

    def weight_copies(e, slot):
        return (pltpu.make_async_copy(wgu_hbm.at[e], wgu_f32.at[slot], wsem.at[0, slot]),
                pltpu.make_async_copy(wd_hbm.at[e], wd_f32.at[slot], wsem.at[1, slot]))

    @pl.when(i == 0)
    def _():
        for c in weight_copies(be_ref[0], 0):
            c.start()

    active = i < n_used

    @pl.when(jnp.logical_and(active, first_ref[i] == 1))
    def _():
        ws = wslot_ref[i]
        for c in weight_copies(be_ref[i], ws):
            c.wait()
        for g in range(n_groups):
            cols = pl.ds(g * MXU_DIM, MXU_DIM)
            wgu_bf[:, cols] = _dot(wgu_f32[ws, :, cols].astype(BF16), perm_ref[...]).astype(BF16)
        wd_bf[...] = wd_f32[ws].astype(BF16)

        @pl.when(nxt_ref[i] >= 0)
        def _():
            for c in weight_copies(nxt_ref[i], 1 - ws):
                c.start()

    def block(m):
        k = d // (2 * LANES)
        for j in range(k):
            lo, hi = _load_packed_tile(x_ref, (), m, k, j)
            x_bf[0:m, j * LANES:(j + 1) * LANES] = lo.astype(BF16)
            x_bf[0:m, (k + j) * LANES:(k + j + 1) * LANES] = hi.astype(BF16)
        for g in range(n_groups):
            gu = (_dot(x_bf[0:m, :], wgu_bf[:, g * MXU_DIM:(g + 1) * MXU_DIM])
                  + bgu_ref[0, :, g * MXU_DIM:(g + 1) * MXU_DIM])
            gate = jnp.minimum(gu[:, :LANES], SWIGLU_LIMIT)
            lin = jnp.clip(gu[:, LANES:], -SWIGLU_LIMIT, SWIGLU_LIMIT)
            act_bf[0:m, g * LANES:(g + 1) * LANES] = (
                gate * jax.nn.sigmoid(SWIGLU_ALPHA * gate) * (lin + 1.0)).astype(BF16)
        _store_packed_rows(y_ref, (), _dot(act_bf[0:m, :], wd_bf[...]) + bd_ref[0])
        if m < bm:
            y_ref[pl.ds(m * k, (bm - m) * k), :] = jnp.zeros(((bm - m) * k, LANES), y_ref.dtype)

    quarter = bm // MOE_QUARTERS
    for nq in range(1, MOE_QUARTERS + 1):
        @pl.when(jnp.logical_and(active, quarters_ref[i] == nq))
        def _(nq=nq):
            block(nq * quarter)

    @pl.when(i >= n_used)
    def _():
        y_ref[...] = jnp.zeros_like(y_ref)


def _deinterleave_perm():
    src = np.arange(MXU_DIM)
    dst = np.where(src % 2 == 0, src // 2, LANES + src // 2)
    p = np.zeros((MXU_DIM, MXU_DIM), np.float32)
    p[src, dst] = 1.0
    return jnp.asarray(p, dtype=BF16)


def _moe(plan, xs, w_gate_up, bgu_p, w_down, b_down):
    n_exp, d, d_gu = w_gate_up.shape
    d_ff = w_down.shape[1]
    bm = MOE_BLOCK
    nblk = plan[0].shape[0]
    slab = bm * d // (2 * LANES)
    by_expert = lambda i, be, *_: (be[i], 0, 0)
    grid_spec = pltpu.PrefetchScalarGridSpec(
        num_scalar_prefetch=6,
        grid=(nblk,),
        in_specs=[
            pl.BlockSpec((slab, LANES), lambda i, *s: (jnp.minimum(i, s[-1][0] - 1), 0)),
            pl.BlockSpec(memory_space=pl.ANY),
            pl.BlockSpec(memory_space=pl.ANY),
            pl.BlockSpec((1, 1, d_gu), by_expert),
            pl.BlockSpec((1, 1, d), by_expert),
            pl.BlockSpec((MXU_DIM, MXU_DIM), lambda i, *_: (0, 0)),
        ],
        out_specs=pl.BlockSpec((slab, LANES),
                               lambda i, *s: (jnp.where(i < s[-1][0], i, nblk - 1), 0)),
        scratch_shapes=[pltpu.VMEM((2, d, d_gu), F32),
                        pltpu.VMEM((2, d_ff, d), F32),
                        pltpu.SemaphoreType.DMA((2, 2)),
                        pltpu.VMEM((d, d_gu), BF16),
                        pltpu.VMEM((d_ff, d), BF16),
                        pltpu.VMEM((bm, d), BF16),
                        pltpu.VMEM((bm, d_ff), BF16)],
    )
    return pl.pallas_call(
        _moe_kernel,
        grid_spec=grid_spec,
        out_shape=jax.ShapeDtypeStruct((nblk * slab, LANES), U32),
        compiler_params=_cparams(("arbitrary",)),
        name="moe_experts",
    )(*plan, xs, w_gate_up, w_down, bgu_p, b_down, _deinterleave_perm())


SC_CORES = 2
SC_SUBCORES = 16
SC_CHUNK = 64


def _sc_mesh():
    return plsc.VectorSubcoreMesh(core_axis_name="c", subcore_axis_name="s")


def _sc_dispatch(rows3, pos_chunks, n_slots):
    n, r, _ = rows3.shape
    workers = SC_CORES * SC_SUBCORES
    per_worker = n // SC_CHUNK // workers

    assert per_worker % 2 == 0

    def body(rows_hbm, pos_hbm, out_hbm, idx_v, buf_a, buf_b, rsem_a, rsem_b, wsem):
        wid = lax.axis_index("s") * SC_CORES + lax.axis_index("c")
        base = wid * per_worker
        bufs, rsems = (buf_a, buf_b), (rsem_a, rsem_b)

        def read(chunk, s):
            return pltpu.make_async_copy(rows_hbm.at[pl.ds(chunk * SC_CHUNK, SC_CHUNK)], bufs[s], rsems[s])

        read(base, 0).start()

        @pl.loop(0, per_worker, step=2)
        def _(c):
            for s in range(2):
                chunk = base + c + s
                read(chunk, s).wait()

                @pl.when(c + s + 1 < per_worker)
                def _():
                    read(chunk + 1, 1 - s).start()

                pltpu.sync_copy(pos_hbm.at[chunk], idx_v)
                copies = [pltpu.async_copy(bufs[s], out_hbm.at[idx_v.at[k]], wsem) for k in range(TOP_K)]
                for cp in copies:
                    cp.wait()

    return pl.kernel(
        body,
        out_type=jax.ShapeDtypeStruct((n_slots, r, LANES), rows3.dtype),
        mesh=_sc_mesh(),
        scratch_types=[pltpu.VMEM((TOP_K, SC_CHUNK), I32),
                       pltpu.VMEM((SC_CHUNK, r, LANES), rows3.dtype),
                       pltpu.VMEM((SC_CHUNK, r, LANES), rows3.dtype),
                       pltpu.SemaphoreType.DMA,
                       pltpu.SemaphoreType.DMA,
                       pltpu.SemaphoreType.DMA],
        name="sc_dispatch",
    )(rows3, pos_chunks)


def _sc_gather(rows3, pos_chunks, n):
    _, r, _ = rows3.shape
    workers = SC_CORES * SC_SUBCORES
    per_worker = n // SC_CHUNK // workers

    def body(rows_hbm, pos_hbm, out_hbm, idx_v, buf_a, buf_b, sem_a, sem_b):
        wid = lax.axis_index("s") * SC_CORES + lax.axis_index("c")
        bufs, sems = (buf_a, buf_b), (sem_a, sem_b)

        @pl.loop(0, per_worker)
        def _(c):
            chunk = wid * per_worker + c
            pltpu.sync_copy(pos_hbm.at[chunk], idx_v)
            gathers = [None] * TOP_K
            gathers[0] = pltpu.async_copy(rows_hbm.at[idx_v.at[0]], bufs[0], sems[0])
            for k in range(TOP_K):
                if k + 1 < TOP_K:
                    nxt = (k + 1) % 2
                    gathers[k + 1] = pltpu.async_copy(rows_hbm.at[idx_v.at[k + 1]], bufs[nxt], sems[nxt])
                gathers[k].wait()
                pltpu.sync_copy(bufs[k % 2], out_hbm.at[k, pl.ds(chunk * SC_CHUNK, SC_CHUNK)])

    return pl.kernel(
        body,
        out_type=jax.ShapeDtypeStruct((TOP_K, n, r, LANES), rows3.dtype),
        mesh=_sc_mesh(),
        scratch_types=[pltpu.VMEM((TOP_K, SC_CHUNK), I32),
                       pltpu.VMEM((SC_CHUNK, r, LANES), rows3.dtype),
                       pltpu.VMEM((SC_CHUNK, r, LANES), rows3.dtype),
                       pltpu.SemaphoreType.DMA,
                       pltpu.SemaphoreType.DMA],
        name="sc_gather",
    )(rows3, pos_chunks)


COMBINE_TILE = 1024


def _combine_kernel(h_ref, gate_ref, y_ref, o_ref):
    tc, d = h_ref.shape
    kt = d // (2 * LANES)
    gates = gate_ref[...]
    for j in range(kt):
        lo_acc = h_ref[:, j * LANES:(j + 1) * LANES]
        hi_acc = h_ref[:, (kt + j) * LANES:(kt + j + 1) * LANES]
        for k in range(TOP_K):
            lo, hi = _load_packed_tile(y_ref, (k,), tc, kt, j)
            g = gates[:, k:k + 1]
            lo_acc = lo_acc + g * lo
            hi_acc = hi_acc + g * hi
        o_ref[:, j * LANES:(j + 1) * LANES] = lo_acc
        o_ref[:, (kt + j) * LANES:(kt + j + 1) * LANES] = hi_acc


def _combine(h, gates, y4):
    n, d = h.shape
    tc = COMBINE_TILE
    slab = tc * d // (2 * LANES)
    row = lambda i: (i, 0)
    return pl.pallas_call(
        _combine_kernel,
        grid=(n // tc,),
        in_specs=[pl.BlockSpec((tc, d), row),
                  pl.BlockSpec((tc, LANES), row),
                  pl.BlockSpec((TOP_K, slab, LANES), lambda i: (0, i, 0))],
        out_specs=pl.BlockSpec((tc, d), row),
        out_shape=jax.ShapeDtypeStruct((n, d), F32),
        input_output_aliases={0: 0},
        compiler_params=_cparams(("parallel",)),
        name="moe_combine",
    )(h, gates, y4)


def _layer(h3, layer, norm_mix_g, w_in, conv_w, conv_b, conv_ln_g, conv_ln_b, q_norm_g, k_norm_g,
           lambda_qk, subln_g, rel_bias, w_out, norm_ffn_g, router_w, router_b,
           w_gate_up, b_gate_up, w_down, b_down):
    b, s, d = h3.shape
    n = b * s
    d_conv = conv_w.shape[-1]
    d_attn = N_HEADS * VALUE_DIM
    lambda_init = 0.8 - 0.6 * math.exp(-0.3 * layer)
    x2 = h3.reshape(n, d)

    u, q, k, v = _in_proj(x2, norm_mix_g.reshape(1, d), w_in.astype(BF16), d_conv, d_attn,
                          tm=min(PROJ_TILE, n))
    conv_o = _conv(u.reshape(b, s, d_conv), conv_w, conv_b.reshape(1, d_conv),
                   conv_ln_g.reshape(1, d_conv), conv_ln_b.reshape(1, d_conv))
    attn_o = _attention(q.reshape(b, s, d_attn), k.reshape(b, s, d_attn), v.reshape(b, s, d_attn),
                        rel_bias, q_norm_g.reshape(1, VALUE_DIM), k_norm_g.reshape(1, VALUE_DIM),
                        lambda_qk, subln_g.reshape(1, VALUE_DIM), lambda_init)

    n_exp = router_w.shape[1]
    rw = jnp.zeros((d, LANES), BF16).at[:, :n_exp].set(router_w.astype(BF16))
    rb = jnp.full((1, LANES), NEG_INF, F32).at[0, :n_exp].set(router_b)
    w_out_bf = w_out.astype(BF16)
    hres, hn, idx, gates, rank, cnt = _out_proj(
        x2, conv_o.reshape(n, d_conv), attn_o.reshape(n, d_attn),
        w_out_bf[:d_conv], w_out_bf[d_conv:], norm_ffn_g.reshape(1, d), rw, rb,
        tm=min(PROJ_TILE, n))

    bm = MOE_BLOCK
    nblk = n * TOP_K // bm + n_exp
    counts = cnt[0, :n_exp].astype(I32)
    padded = (counts + bm - 1) // bm * bm
    eid = jnp.arange(n_exp, dtype=I32)
    pad_end = jnp.sum(jnp.where(eid[None, :] <= eid[:, None], padded[None, :], 0), axis=1)
    pad_start = pad_end - padded
    start_of = jnp.sum(jnp.where(idx[None, :TOP_K] == eid[:, None, None],
                                 pad_start[:, None, None], 0), axis=0)
    pos = start_of + rank[:TOP_K]
    pos_chunks = pos.reshape(TOP_K, n // SC_CHUNK, SC_CHUNK).transpose(1, 0, 2)
    block_start = jnp.arange(nblk, dtype=I32) * bm
    block_expert = jnp.minimum(jnp.sum(block_start[:, None] >= pad_end[None, :], axis=1),
                               n_exp - 1).astype(I32)
    n_used = (pad_end[-1:] // bm).astype(I32)
    blk = jnp.arange(nblk, dtype=I32)
    first = jnp.logical_and(
        jnp.concatenate([jnp.ones((1,), bool), block_expert[1:] != block_expert[:-1]]),
        blk < n_used[0]).astype(I32)
    opened = jnp.sum(jnp.where(blk[None, :] <= blk[:, None], first[None, :], 0), axis=1)
    wslot = ((opened - 1) % 2).astype(I32)
    later_used = jnp.logical_and(eid[None, :] > eid[:, None], counts[None, :] > 0)
    next_used = jnp.min(jnp.where(later_used, eid[None, :], n_exp), axis=1)
    next_used = jnp.where(next_used == n_exp, -1, next_used).astype(I32)
    next_of_block = jnp.sum(jnp.where(block_expert[:, None] == eid[None, :], next_used[None, :], 0),
                            axis=1).astype(I32)
    on_expert = block_expert[:, None] == eid[None, :]
    rows_in_block = jnp.clip(
        jnp.sum(jnp.where(on_expert, (counts + pad_start)[None, :], 0), axis=1) - block_start, 0, bm)
    quarter = bm // MOE_QUARTERS
    quarters = ((rows_in_block + quarter - 1) // quarter).astype(I32)
    plan = (block_expert, first, wslot, next_of_block, quarters, n_used)

    d_ff = w_down.shape[1]
    bgu_p = b_gate_up.reshape(n_exp, d_ff // LANES, LANES, 2).transpose(0, 1, 3, 2).reshape(n_exp, 1, 2 * d_ff)
    r = d // (2 * LANES)
    xs = _sc_dispatch(hn.reshape(n, r, LANES), pos_chunks, nblk * bm)
    yb = _moe(plan, xs.reshape(nblk * bm * r, LANES), w_gate_up, bgu_p, w_down,
              b_down.reshape(n_exp, 1, d))
    y4 = _sc_gather(yb.reshape(nblk * bm, r, LANES), pos_chunks, n)
    out = _combine(hres, gates, y4.reshape(TOP_K, n * r, LANES))
    return out.reshape(b, s, d)


def kernel(x, norm_mix_g, w_in, conv_w, conv_b, conv_ln_g, conv_ln_b, q_norm_g, k_norm_g, lambda_qk,
           subln_g, rel_bias, w_out, norm_ffn_g, router_w, router_b, w_gate_up, b_gate_up, w_down,
           b_down):
    h = x
    for layer in range(norm_mix_g.shape[0]):
        h = _layer(h, layer, norm_mix_g[layer], w_in[layer], conv_w[layer], conv_b[layer],
                   conv_ln_g[layer], conv_ln_b[layer], q_norm_g[layer], k_norm_g[layer],
                   lambda_qk[layer], subln_g[layer], rel_bias, w_out[layer], norm_ffn_g[layer],
                   router_w[layer], router_b[layer], w_gate_up[layer], b_gate_up[layer],
                   w_down[layer], b_down[layer])
    return h
```

```python
import functools
import math

import jax
import jax.numpy as jnp
import numpy as np
from jax import lax
from jax.experimental import pallas as pl
from jax.experimental.pallas import tpu as pltpu
from jax.experimental.pallas import tpu_sc as plsc

F32 = jnp.float32
BF16 = jnp.bfloat16
I32 = jnp.int32
U32 = jnp.uint32

CHUNK = 64
CONV_WIDTH = 31
N_HEADS = 4
HEAD_DIM = 64
VALUE_DIM = 2 * HEAD_DIM
REL_BUCKETS = 32
REL_MAX_DIST = 128
N_EXPERTS = 32
TOP_K = 4
SWIGLU_LIMIT = 7.0
SWIGLU_ALPHA = 1.702
EPS = 1e-5
LOG2E = 1.4426950408889634

LANES = 128
SUBLANES = 8
MXU_DIM = 256
VMEM_LIMIT = 56 * 1024 * 1024

NEG_INF = float("-inf")


def _cparams(sem, vmem=VMEM_LIMIT, flags=None):
    return pltpu.CompilerParams(dimension_semantics=sem, vmem_limit_bytes=vmem, flags=flags)


def _dot(a, b):
    return jnp.dot(a, b, preferred_element_type=F32)


def _dot_nt(a, b):
    return lax.dot_general(a, b, (((1,), (1,)), ((), ())), preferred_element_type=F32)


def _pack_pair(lo, hi):
    lo_bits = lax.bitcast_convert_type(lo.astype(BF16).astype(F32), U32)
    hi_bits = lax.bitcast_convert_type(hi.astype(BF16).astype(F32), U32)
    return (lo_bits >> 16) | hi_bits


def _unpack_pair(w):
    return (lax.bitcast_convert_type(w << 16, F32),
            lax.bitcast_convert_type(w & jnp.uint32(0xFFFF0000), F32))


def _store_packed_rows(ref, lead, x):
    rows, k = x.shape[0], x.shape[1] // (2 * LANES)
    for j in range(k):
        w = _pack_pair(x[:, j * LANES:(j + 1) * LANES], x[:, (k + j) * LANES:(k + j + 1) * LANES])
        ref[(*lead, pl.ds(j, rows, stride=k), slice(None))] = w


def _load_packed_tile(ref, lead, rows, k, j):
    return _unpack_pair(ref[(*lead, pl.ds(j, rows, stride=k), slice(None))])


PROJ_TILE = 1024
def _in_proj_kernel(x_ref, g_ref, w_ref, u_ref, q_ref, k_ref, v_ref, *, d_conv, d_attn):
    x = x_ref[...]
    ms = jnp.mean(x * x, axis=-1, keepdims=True)
    y = (x * lax.rsqrt(ms + EPS) * g_ref[...]).astype(BF16)
    proj = _dot(y, w_ref[...])
    a = proj[:, :d_conv]
    g = proj[:, d_conv:2 * d_conv]
    u_ref[...] = a * jax.nn.sigmoid(g)
    o = 2 * d_conv
    q_ref[...] = proj[:, o:o + d_attn]
    k_ref[...] = proj[:, o + d_attn:o + 2 * d_attn]
    v_ref[...] = proj[:, o + 2 * d_attn:o + 3 * d_attn].astype(BF16)


def _in_proj(x2, g, w_bf, d_conv, d_attn, tm):
    n, d = x2.shape
    d_in = w_bf.shape[1]
    row = lambda i: (i, 0)
    fixed = lambda i: (0, 0)
    return pl.pallas_call(
        functools.partial(_in_proj_kernel, d_conv=d_conv, d_attn=d_attn),
        grid=(n // tm,),
        in_specs=[pl.BlockSpec((tm, d), row),
                  pl.BlockSpec((1, d), fixed),
                  pl.BlockSpec((d, d_in), fixed)],
        out_specs=[pl.BlockSpec((tm, d_conv), row),
                   pl.BlockSpec((tm, d_attn), row),
                   pl.BlockSpec((tm, d_attn), row),
                   pl.BlockSpec((tm, d_attn), row)],
        out_shape=[jax.ShapeDtypeStruct((n, d_conv), F32),
                   jax.ShapeDtypeStruct((n, d_attn), F32),
                   jax.ShapeDtypeStruct((n, d_attn), F32),
                   jax.ShapeDtypeStruct((n, d_attn), BF16)],
        compiler_params=_cparams(("parallel",)),
        name="in_proj",
    )(x2, g, w_bf)


CONV_PAD = 32
CONV_SEQ_TILE = 1024
CONV_TILE = 128
CONV_NORM_TILE = 256


def _conv_kernel(u_ref, prev_ref, w_ref, cb_ref, lg_ref, lb_ref, o_ref, sh_ref, y_ref):
    ts, c = u_ref.shape[1], u_ref.shape[2]
    plen = ts + CONV_PAD
    hist = prev_ref[0]
    hist = jnp.where(pl.program_id(1) > 0, hist, jnp.zeros_like(hist))
    sh_ref[0, pl.ds(0, CONV_PAD), :] = hist
    sh_ref[0, pl.ds(CONV_PAD, ts), :] = u_ref[0]
    sh_ref[0, pl.ds(plen, SUBLANES), :] = jnp.zeros((SUBLANES, c), F32)

    bt = CONV_PAD

    def shift(i, _):
        p0 = pl.multiple_of(i * bt, bt)
        win = sh_ref[0, pl.ds(p0, bt + SUBLANES), :]
        for r in range(1, SUBLANES):
            sh_ref[r, pl.ds(p0, bt), :] = win[r:r + bt]
        return 0

    lax.fori_loop(0, plen // bt, shift, 0)

    off0 = CONV_PAD - (CONV_WIDTH - 1)

    by_shift = {}
    for j in range(CONV_WIDTH):
        a, r = divmod(off0 + j, SUBLANES)
        by_shift.setdefault(r, []).append((a, j))

    for lt in range(c // LANES):
        lanes = slice(lt * LANES, (lt + 1) * LANES)
        taps = [w_ref[pl.ds(j, 1), lanes] for j in range(CONV_WIDTH)]

        def tap_body(i, _, lanes=lanes, taps=taps):
            t0 = pl.multiple_of(i * CONV_TILE, CONV_TILE)
            acc = jnp.zeros((CONV_TILE, LANES), F32)
            for r, group in by_shift.items():
                a_lo = min(a for a, _ in group)
                a_hi = max(a for a, _ in group)
                rows = CONV_TILE + (a_hi - a_lo) * SUBLANES
                start = pl.multiple_of(t0 + a_lo * SUBLANES, SUBLANES)
                win = sh_ref[r, pl.ds(start, rows), lanes]
                for a, j in group:
                    lo = (a - a_lo) * SUBLANES
                    acc = acc + win[lo:lo + CONV_TILE] * taps[j]
            y_ref[pl.ds(t0, CONV_TILE), lanes] = acc
            return 0

        lax.fori_loop(0, ts // CONV_TILE, tap_body, 0)

    def norm_body(i, _):
        t0 = pl.multiple_of(i * CONV_NORM_TILE, CONV_NORM_TILE)
        y = y_ref[pl.ds(t0, CONV_NORM_TILE), :] + cb_ref[...]
        mu = jnp.mean(y, axis=-1, keepdims=True)
        yc = y - mu
        var = jnp.mean(yc * yc, axis=-1, keepdims=True)
        z = yc * lax.rsqrt(var + EPS) * lg_ref[...] + lb_ref[...]
        o_ref[0, pl.ds(t0, CONV_NORM_TILE), :] = (z * jax.nn.sigmoid(z)).astype(o_ref.dtype)
        return 0

    lax.fori_loop(0, ts // CONV_NORM_TILE, norm_body, 0)


def _conv(u3, conv_w, conv_b, ln_g, ln_b):
    b, s, c = u3.shape
    ts = min(CONV_SEQ_TILE, s)
    hist_per_tile = ts // CONV_PAD
    fixed = lambda i, j: (0, 0)
    return pl.pallas_call(
        _conv_kernel,
        grid=(b, s // ts),
        in_specs=[pl.BlockSpec((1, ts, c), lambda i, j: (i, j, 0)),
                  pl.BlockSpec((1, CONV_PAD, c),
                               lambda i, j: (i, jnp.maximum(j * hist_per_tile - 1, 0), 0)),
                  pl.BlockSpec((CONV_WIDTH, c), fixed),
                  pl.BlockSpec((1, c), fixed),
                  pl.BlockSpec((1, c), fixed),
                  pl.BlockSpec((1, c), fixed)],
        out_specs=pl.BlockSpec((1, ts, c), lambda i, j: (i, j, 0)),
        out_shape=jax.ShapeDtypeStruct((b, s, c), BF16),
        scratch_shapes=[pltpu.VMEM((SUBLANES, ts + CONV_PAD + SUBLANES, c), F32),
                        pltpu.VMEM((ts, c), F32)],
        compiler_params=_cparams(("parallel", "parallel")),
        name="conv_mixer",
    )(u3, u3, conv_w, conv_b, ln_g, ln_b)


ATT_TILE = 256
ATT_SEQS = 2
FAR_BUCKET = REL_BUCKETS // 2 - 1


def _t5_bucket(rel):
    nb = REL_BUCKETS // 2
    max_exact = nb // 2
    ret = jnp.where(rel > 0, nb, 0)
    n = jnp.abs(rel)
    nf = jnp.maximum(n, 1).astype(jnp.float32)
    large = max_exact + (jnp.log(nf / max_exact) / math.log(REL_MAX_DIST / max_exact)
                         * (nb - max_exact)).astype(jnp.int32)
    large = jnp.minimum(large, nb - 1)
    return ret + jnp.where(n < max_exact, n, large)


def _near_buckets():
    assert ATT_TILE >= REL_MAX_DIST and ATT_TILE % CHUNK == 0
    qpos = jnp.arange(ATT_TILE, dtype=I32)[:, None]
    kpos = jnp.arange(ATT_TILE, dtype=I32)[None, :]
    prev = _t5_bucket(kpos - ATT_TILE - qpos)
    diag = _t5_bucket(kpos - qpos)
    diag = jnp.where(kpos // CHUNK <= qpos // CHUNK, diag, -1)
    return jnp.stack([prev, diag]).astype(I32)


def _attn_kernel(tab_ref, q_ref, k_ref, v_ref, bkt_ref, bd_ref, qg_ref, kg_ref, lqk_ref, sg_ref, o_ref,
                 qz_ref, kn_ref, v1_ref, bias_ref, *, seq, lambda_init):
    h = pl.program_id(0)
    t = ATT_TILE
    n_tiles = seq // t
    lane = lax.broadcasted_iota(I32, (1, VALUE_DIM), 1)
    first = lane < HEAD_DIM

    @pl.when(pl.program_id(1) == 0)
    def _():
        far = tab_ref[FAR_BUCKET, h]
        for d in range(2):
            bkt = bkt_ref[d]
            tile = jnp.full((t, t), NEG_INF, F32)
            for b in range(REL_BUCKETS):
                tile = jnp.where(bkt == b, (tab_ref[b, h] - far) * LOG2E, tile)
            bias_ref[d] = tile

    def half_norm(x, g):
        x2 = x * x
        hi = x2.astype(BF16)
        lo = (x2 - hi.astype(F32)).astype(BF16)
        ms = (_dot(hi, bd_ref[...]) + _dot(lo, bd_ref[...])) * (1.0 / HEAD_DIM)
        return x * lax.rsqrt(ms + EPS) * g

    q_gain = qg_ref[...] * (HEAD_DIM ** -0.5 * LOG2E)
    qg_maps = (jnp.where(first, q_gain, 0.0), jnp.where(first, 0.0, q_gain))

    n_seq = q_ref.shape[0]

    ones_col = jnp.broadcast_to(jnp.where(lane == 0, 1.0, 0.0).astype(BF16), (t, VALUE_DIM))
    for i in range(n_tiles):
        r0 = i * t
        for b in range(n_seq):
            qn = half_norm(q_ref[b, pl.ds(r0, t), :], 1.0)
            qz_ref[b, 0, pl.ds(r0, t), :] = (qn * qg_maps[0]).astype(BF16)
            qz_ref[b, 1, pl.ds(r0, t), :] = (qn * qg_maps[1]).astype(BF16)
            kn_ref[b, pl.ds(r0, t), :] = half_norm(k_ref[b, pl.ds(r0, t), :], kg_ref[...]).astype(BF16)
            v1_ref[b, pl.ds(r0, t), :] = jnp.concatenate([v_ref[b, pl.ds(r0, t), :], ones_col], axis=1)

    lqk = lqk_ref[...]
    lam = (jnp.exp(jnp.sum(lqk[0:1] * lqk[1:2], axis=-1, keepdims=True))
           - jnp.exp(jnp.sum(lqk[2:3] * lqk[3:4], axis=-1, keepdims=True)) + lambda_init)

    for i in range(n_tiles):
        q0, kend = i * t, (i + 1) * t
        for b in range(n_seq):
            keys = kn_ref[b, 0:kend, :]
            vals = v1_ref[b, 0:kend, :]
            maps = []
            for m in range(2):
                s = _dot_nt(qz_ref[b, m, q0:q0 + t, :], keys)
                parts = [s[:, kend - t:] + bias_ref[1]]
                if i >= 1:
                    parts.insert(0, s[:, kend - 2 * t:kend - t] + bias_ref[0])
                if i >= 2:
                    parts.insert(0, s[:, :kend - 2 * t])
                s = jnp.concatenate(parts, axis=1) if len(parts) > 1 else parts[0]
                p = jnp.exp2(s - jnp.max(s, axis=-1, keepdims=True))
                pv = _dot(p.astype(BF16), vals)
                maps.append(pv[:, :VALUE_DIM] / pv[:, VALUE_DIM:VALUE_DIM + 1])
            o = maps[0] - lam * maps[1]
            ms = jnp.mean(o * o, axis=-1, keepdims=True)
            o = o * lax.rsqrt(ms + EPS) * sg_ref[...] * (1.0 - lambda_init)
            o_ref[b, q0:q0 + t, :] = o.astype(o_ref.dtype)


def _attention(q3, k3, v3, rel_bias, q_g, k_g, lam_qk, subln_g, lambda_init):
    b, s, _ = q3.shape
    t = ATT_TILE
    nb = ATT_SEQS if b % ATT_SEQS == 0 else 1
    head = lambda j, i: (i, 0, j)
    fixed2 = lambda j, i: (0, 0)
    fixed3 = lambda j, i: (0, 0, 0)
    half = np.arange(VALUE_DIM) // HEAD_DIM
    blockdiag = jnp.asarray(half[:, None] == half[None, :], dtype=BF16)
    return pl.pallas_call(
        functools.partial(_attn_kernel, seq=s, lambda_init=lambda_init),
        grid=(N_HEADS, b // nb),
        in_specs=[pl.BlockSpec(memory_space=pltpu.SMEM),
                  pl.BlockSpec((nb, s, VALUE_DIM), head),
                  pl.BlockSpec((nb, s, VALUE_DIM), head),
                  pl.BlockSpec((nb, s, VALUE_DIM), head),
                  pl.BlockSpec((2, t, t), fixed3),
                  pl.BlockSpec((VALUE_DIM, VALUE_DIM), fixed2),
                  pl.BlockSpec((1, VALUE_DIM), fixed2),
                  pl.BlockSpec((1, VALUE_DIM), fixed2),
                  pl.BlockSpec((4, HEAD_DIM), fixed2),
                  pl.BlockSpec((1, VALUE_DIM), fixed2)],
        out_specs=pl.BlockSpec((nb, s, VALUE_DIM), head),
        out_shape=jax.ShapeDtypeStruct((b, s, N_HEADS * VALUE_DIM), BF16),
        scratch_shapes=[pltpu.VMEM((nb, 2, s, VALUE_DIM), BF16),
                        pltpu.VMEM((nb, s, VALUE_DIM), BF16),
                        pltpu.VMEM((nb, s, 2 * VALUE_DIM), BF16),
                        pltpu.VMEM((2, t, t), F32)],
        compiler_params=_cparams(("arbitrary", "arbitrary")),
        name="diff_attn",
    )(rel_bias, q3, k3, v3, _near_buckets(), blockdiag, q_g, k_g, lam_qk, subln_g)


def _out_proj_kernel(x_ref, c_ref, a_ref, wc_ref, wa_ref, g_ref, rw_ref, rb_ref, tri_ref,
                     h_ref, hn_ref, idx_ref, gate_ref, rank_ref, cnt_ref, carry_ref):
    @pl.when(pl.program_id(0) == 0)
    def _():
        carry_ref[...] = jnp.zeros_like(carry_ref)

    h = x_ref[...] + _dot(c_ref[...], wc_ref[...]) + _dot(a_ref[...], wa_ref[...])
    h_ref[...] = h
    ms = jnp.mean(h * h, axis=-1, keepdims=True)
    hn = h * lax.rsqrt(ms + EPS) * g_ref[...]
    _store_packed_rows(hn_ref, (), hn)

    logits = _dot(hn.astype(BF16), rw_ref[...]) + rb_ref[...]

    tm = logits.shape[0]
    lane = lax.broadcasted_iota(I32, (tm, LANES), 1).astype(F32)
    work = logits
    vals, idxs = [], []
    for _ in range(TOP_K):
        mx = jnp.max(work, axis=-1, keepdims=True)
        ix = jnp.min(jnp.where(work == mx, lane, float(LANES)), axis=-1, keepdims=True)
        vals.append(mx)
        idxs.append(ix)
        work = jnp.where(lane == ix, NEG_INF, work)
    exps = [jnp.exp(v - vals[0]) for v in vals]
    denom = exps[0]
    for e in exps[1:]:
        denom = denom + e

    sel = jnp.zeros((tm, LANES), F32)
    for ix in idxs:
        sel = sel + jnp.where(lane == ix, 1.0, 0.0)
    tb = tri_ref.shape[0]
    sel_bf = sel.astype(BF16)
    running = carry_ref[...]
    ranks = []
    for r0 in range(0, tm, tb):
        ranks.append(_dot(tri_ref[...], sel_bf[r0:r0 + tb]) + running)
        running = running + jnp.sum(sel[r0:r0 + tb], axis=0, keepdims=True)
    rank = jnp.concatenate(ranks, axis=0) if len(ranks) > 1 else ranks[0]
    carry_ref[...] = running
    cnt_ref[...] = carry_ref[...]

    idx_out = jnp.zeros((tm, LANES), F32)
    gate_out = jnp.zeros((tm, LANES), F32)
    rank_out = jnp.zeros((tm, LANES), F32)
    for k in range(TOP_K):
        rk = jnp.sum(jnp.where(lane == idxs[k], rank, 0.0), axis=-1, keepdims=True)
        idx_out = jnp.where(lane == k, idxs[k], idx_out)
        gate_out = jnp.where(lane == k, exps[k] / denom, gate_out)
        rank_out = jnp.where(lane == k, rk, rank_out)
    idx_ref[...] = jnp.transpose(idx_out)[:SUBLANES].astype(I32)
    rank_ref[...] = jnp.transpose(rank_out)[:SUBLANES].astype(I32)
    gate_ref[...] = gate_out


RANK_BLOCK = 256


def _out_proj(x2, conv_o, attn_o, wc, wa, g, rw, rb, tm):
    n, d = x2.shape
    dc, da = conv_o.shape[1], attn_o.shape[1]
    row = lambda i: (i, 0)
    fixed = lambda i: (0, 0)
    tb = min(RANK_BLOCK, tm)
    tri = jnp.tril(jnp.ones((tb, tb), F32), -1).astype(BF16)
    return pl.pallas_call(
        _out_proj_kernel,
        grid=(n // tm,),
        in_specs=[pl.BlockSpec((tm, d), row),
                  pl.BlockSpec((tm, dc), row),
                  pl.BlockSpec((tm, da), row),
                  pl.BlockSpec((dc, d), fixed),
                  pl.BlockSpec((da, d), fixed),
                  pl.BlockSpec((1, d), fixed),
                  pl.BlockSpec((d, LANES), fixed),
                  pl.BlockSpec((1, LANES), fixed),
                  pl.BlockSpec((tb, tb), fixed)],
        out_specs=[pl.BlockSpec((tm, d), row),
                   pl.BlockSpec((tm * d // (2 * LANES), LANES), row),
                   pl.BlockSpec((SUBLANES, tm), lambda i: (0, i)),
                   pl.BlockSpec((tm, LANES), row),
                   pl.BlockSpec((SUBLANES, tm), lambda i: (0, i)),
                   pl.BlockSpec((1, LANES), fixed)],
        out_shape=[jax.ShapeDtypeStruct((n, d), F32),
                   jax.ShapeDtypeStruct((n * d // (2 * LANES), LANES), U32),
                   jax.ShapeDtypeStruct((SUBLANES, n), I32),
                   jax.ShapeDtypeStruct((n, LANES), F32),
                   jax.ShapeDtypeStruct((SUBLANES, n), I32),
                   jax.ShapeDtypeStruct((1, LANES), F32)],
        scratch_shapes=[pltpu.VMEM((1, LANES), F32)],
        compiler_params=_cparams(("arbitrary",)),
        name="out_proj_router",
    )(x2, conv_o, attn_o, wc, wa, g, rw, rb, tri)


MOE_BLOCK = 1024
MOE_QUARTERS = 4


def _moe_kernel(be_ref, first_ref, wslot_ref, nxt_ref, quarters_ref, nused_ref,
                x_ref, wgu_hbm, wd_hbm, bgu_ref, bd_ref, perm_ref,
                y_ref, wgu_f32, wd_f32, wsem, wgu_bf, wd_bf, x_bf, act_bf):
    i = pl.program_id(0)
    bm = MOE_BLOCK
    n_used = nused_ref[0]
    d = wd_bf.shape[1]
    d_ff = wd_bf.shape[0]
    n_groups = d_ff // LANES

    def weight_copies(e, slot):
        return (pltpu.make_async_copy(wgu_hbm.at[e], wgu_f32.at[slot], wsem.at[0, slot]),
                pltpu.make_async_copy(wd_hbm.at[e], wd_f32.at[slot], wsem.at[1, slot]))

    @pl.when(i == 0)
    def _():
        for c in weight_copies(be_ref[0], 0):
            c.start()

    active = i < n_used

    @pl.when(jnp.logical_and(active, first_ref[i] == 1))
    def _():
        ws = wslot_ref[i]
        for c in weight_copies(be_ref[i], ws):
            c.wait()
        for g in range(n_groups):
            cols = pl.ds(g * MXU_DIM, MXU_DIM)
            wgu_bf[:, cols] = _dot(wgu_f32[ws, :, cols].astype(BF16), perm_ref[...]).astype(BF16)
        wd_bf[...] = wd_f32[ws].astype(BF16)

        @pl.when(nxt_ref[i] >= 0)
        def _():
            for c in weight_copies(nxt_ref[i], 1 - ws):
                c.start()

    def block(m):
        k = d // (2 * LANES)
        for j in range(k):
            lo, hi = _load_packed_tile(x_ref, (), m, k, j)
            x_bf[0:m, j * LANES:(j + 1) * LANES] = lo.astype(BF16)
            x_bf[0:m, (k + j) * LANES:(k + j + 1) * LANES] = hi.astype(BF16)
        for g in range(n_groups):
            gu = (_dot(x_bf[0:m, :], wgu_bf[:, g * MXU_DIM:(g + 1) * MXU_DIM])
                  + bgu_ref[0, :, g * MXU_DIM:(g + 1) * MXU_DIM])
            gate = jnp.minimum(gu[:, :LANES], SWIGLU_LIMIT)
            lin = jnp.clip(gu[:, LANES:], -SWIGLU_LIMIT, SWIGLU_LIMIT)
            act_bf[0:m, g * LANES:(g + 1) * LANES] = (
                gate * jax.nn.sigmoid(SWIGLU_ALPHA * gate) * (lin + 1.0)).astype(BF16)
        _store_packed_rows(y_ref, (), _dot(act_bf[0:m, :], wd_bf[...]) + bd_ref[0])
        if m < bm:
            y_ref[pl.ds(m * k, (bm - m) * k), :] = jnp.zeros(((bm - m) * k, LANES), y_ref.dtype)

    quarter = bm // MOE_QUARTERS
    for nq in range(1, MOE_QUARTERS + 1):
        @pl.when(jnp.logical_and(active, quarters_ref[i] == nq))
        def _(nq=nq):
            block(nq * quarter)

    @pl.when(i >= n_used)
    def _():
        y_ref[...] = jnp.zeros_like(y_ref)


def _deinterleave_perm():
    src = np.arange(MXU_DIM)
    dst = np.where(src % 2 == 0, src // 2, LANES + src // 2)
    p = np.zeros((MXU_DIM, MXU_DIM), np.float32)
    p[src, dst] = 1.0
    return jnp.asarray(p, dtype=BF16)


def _moe(plan, xs, w_gate_up, bgu_p, w_down, b_down):
    n_exp, d, d_gu = w_gate_up.shape
    d_ff = w_down.shape[1]
    bm = MOE_BLOCK
    nblk = plan[0].shape[0]
    slab = bm * d // (2 * LANES)
    by_expert = lambda i, be, *_: (be[i], 0, 0)
    grid_spec = pltpu.PrefetchScalarGridSpec(
        num_scalar_prefetch=6,
        grid=(nblk,),
        in_specs=[
            pl.BlockSpec((slab, LANES), lambda i, *s: (jnp.minimum(i, s[-1][0] - 1), 0)),
            pl.BlockSpec(memory_space=pl.ANY),
            pl.BlockSpec(memory_space=pl.ANY),
            pl.BlockSpec((1, 1, d_gu), by_expert),
            pl.BlockSpec((1, 1, d), by_expert),
            pl.BlockSpec((MXU_DIM, MXU_DIM), lambda i, *_: (0, 0)),
        ],
        out_specs=pl.BlockSpec((slab, LANES),
                               lambda i, *s: (jnp.where(i < s[-1][0], i, nblk - 1), 0)),
        scratch_shapes=[pltpu.VMEM((2, d, d_gu), F32),
                        pltpu.VMEM((2, d_ff, d), F32),
                        pltpu.SemaphoreType.DMA((2, 2)),
                        pltpu.VMEM((d, d_gu), BF16),
                        pltpu.VMEM((d_ff, d), BF16),
                        pltpu.VMEM((bm, d), BF16),
                        pltpu.VMEM((bm, d_ff), BF16)],
    )
    return pl.pallas_call(
        _moe_kernel,
        grid_spec=grid_spec,
        out_shape=jax.ShapeDtypeStruct((nblk * slab, LANES), U32),
        compiler_params=_cparams(("arbitrary",)),
        name="moe_experts",
    )(*plan, xs, w_gate_up, w_down, bgu_p, b_down, _deinterleave_perm())


SC_CORES = 2
SC_SUBCORES = 16
SC_CHUNK = 64


def _sc_mesh():
    return plsc.VectorSubcoreMesh(core_axis_name="c", subcore_axis_name="s")


def _sc_dispatch(rows3, pos_chunks, n_slots):
    n, r, _ = rows3.shape
    workers = SC_CORES * SC_SUBCORES
    per_worker = n // SC_CHUNK // workers

    assert per_worker % 2 == 0

    def body(rows_hbm, pos_hbm, out_hbm, idx_v, buf_a, buf_b, rsem_a, rsem_b, wsem):
        wid = lax.axis_index("s") * SC_CORES + lax.axis_index("c")
        base = wid * per_worker
        bufs, rsems = (buf_a, buf_b), (rsem_a, rsem_b)

        def read(chunk, s):
            return pltpu.make_async_copy(rows_hbm.at[pl.ds(chunk * SC_CHUNK, SC_CHUNK)], bufs[s], rsems[s])

        read(base, 0).start()

        @pl.loop(0, per_worker, step=2)
        def _(c):
            for s in range(2):
                chunk = base + c + s
                read(chunk, s).wait()

                @pl.when(c + s + 1 < per_worker)
                def _():
                    read(chunk + 1, 1 - s).start()

                pltpu.sync_copy(pos_hbm.at[chunk], idx_v)
                copies = [pltpu.async_copy(bufs[s], out_hbm.at[idx_v.at[k]], wsem) for k in range(TOP_K)]
                for cp in copies:
                    cp.wait()

    return pl.kernel(
        body,
        out_type=jax.ShapeDtypeStruct((n_slots, r, LANES), rows3.dtype),
        mesh=_sc_mesh(),
        scratch_types=[pltpu.VMEM((TOP_K, SC_CHUNK), I32),
                       pltpu.VMEM((SC_CHUNK, r, LANES), rows3.dtype),
                       pltpu.VMEM((SC_CHUNK, r, LANES), rows3.dtype),
                       pltpu.SemaphoreType.DMA,
                       pltpu.SemaphoreType.DMA,
                       pltpu.SemaphoreType.DMA],
        name="sc_dispatch",
    )(rows3, pos_chunks)


def _sc_gather(rows3, pos_chunks, n):
    _, r, _ = rows3.shape
    workers = SC_CORES * SC_SUBCORES
    per_worker = n // SC_CHUNK // workers

    def body(rows_hbm, pos_hbm, out_hbm, idx_v, buf_a, buf_b, sem_a, sem_b):
        wid = lax.axis_index("s") * SC_CORES + lax.axis_index("c")
        bufs, sems = (buf_a, buf_b), (sem_a, sem_b)

        @pl.loop(0, per_worker)
        def _(c):
            chunk = wid * per_worker + c
            pltpu.sync_copy(pos_hbm.at[chunk], idx_v)
            gathers = [None] * TOP_K
            gathers[0] = pltpu.async_copy(rows_hbm.at[idx_v.at[0]], bufs[0], sems[0])
            for k in range(TOP_K):
                if k + 1 < TOP_K:
                    nxt = (k + 1) % 2
                    gathers[k + 1] = pltpu.async_copy(rows_hbm.at[idx_v.at[k + 1]], bufs[nxt], sems[nxt])
                gathers[k].wait()
                pltpu.sync_copy(bufs[k % 2], out_hbm.at[k, pl.ds(chunk * SC_CHUNK, SC_CHUNK)])

    return pl.kernel(
        body,
        out_type=jax.ShapeDtypeStruct((TOP_K, n, r, LANES), rows3.dtype),
        mesh=_sc_mesh(),
        scratch_types=[pltpu.VMEM((TOP_K, SC_CHUNK), I32),
                       pltpu.VMEM((SC_CHUNK, r, LANES), rows3.dtype),
                       pltpu.VMEM((SC_CHUNK, r, LANES), rows3.dtype),
                       pltpu.SemaphoreType.DMA,
                       pltpu.SemaphoreType.DMA],
        name="sc_gather",
    )(rows3, pos_chunks)


COMBINE_TILE = 1024


def _combine_kernel(h_ref, gate_ref, y_ref, o_ref):
    tc, d = h_ref.shape
    kt = d // (2 * LANES)
    gates = gate_ref[...]
    for j in range(kt):
        lo_acc = h_ref[:, j * LANES:(j + 1) * LANES]
        hi_acc = h_ref[:, (kt + j) * LANES:(kt + j + 1) * LANES]
        for k in range(TOP_K):
            lo, hi = _load_packed_tile(y_ref, (k,), tc, kt, j)
            g = gates[:, k:k + 1]
            lo_acc = lo_acc + g * lo
            hi_acc = hi_acc + g * hi
        o_ref[:, j * LANES:(j + 1) * LANES] = lo_acc
        o_ref[:, (kt + j) * LANES:(kt + j + 1) * LANES] = hi_acc


def _combine(h, gates, y4):
    n, d = h.shape
    tc = COMBINE_TILE
    slab = tc * d // (2 * LANES)
    row = lambda i: (i, 0)
    return pl.pallas_call(
        _combine_kernel,
        grid=(n // tc,),
        in_specs=[pl.BlockSpec((tc, d), row),
                  pl.BlockSpec((tc, LANES), row),
                  pl.BlockSpec((TOP_K, slab, LANES), lambda i: (0, i, 0))],
        out_specs=pl.BlockSpec((tc, d), row),
        out_shape=jax.ShapeDtypeStruct((n, d), F32),
        input_output_aliases={0: 0},
        compiler_params=_cparams(("parallel",)),
        name="moe_combine",
    )(h, gates, y4)


def _layer(h3, layer, norm_mix_g, w_in, conv_w, conv_b, conv_ln_g, conv_ln_b, q_norm_g, k_norm_g,
           lambda_qk, subln_g, rel_bias, w_out, norm_ffn_g, router_w, router_b,
           w_gate_up, b_gate_up, w_down, b_down):
    b, s, d = h3.shape
    n = b * s
    d_conv = conv_w.shape[-1]
    d_attn = N_HEADS * VALUE_DIM
    lambda_init = 0.8 - 0.6 * math.exp(-0.3 * layer)
    x2 = h3.reshape(n, d)

    u, q, k, v = _in_proj(x2, norm_mix_g.reshape(1, d), w_in.astype(BF16), d_conv, d_attn,
                          tm=min(PROJ_TILE, n))
    conv_o = _conv(u.reshape(b, s, d_conv), conv_w, conv_b.reshape(1, d_conv),
                   conv_ln_g.reshape(1, d_conv), conv_ln_b.reshape(1, d_conv))
    attn_o = _attention(q.reshape(b, s, d_attn), k.reshape(b, s, d_attn), v.reshape(b, s, d_attn),
                        rel_bias, q_norm_g.reshape(1, VALUE_DIM), k_norm_g.reshape(1, VALUE_DIM),
                        lambda_qk, subln_g.reshape(1, VALUE_DIM), lambda_init)

    n_exp = router_w.shape[1]
    rw = jnp.zeros((d, LANES), BF16).at[:, :n_exp].set(router_w.astype(BF16))
    rb = jnp.full((1, LANES), NEG_INF, F32).at[0, :n_exp].set(router_b)
    w_out_bf = w_out.astype(BF16)
    hres, hn, idx, gates, rank, cnt = _out_proj(
        x2, conv_o.reshape(n, d_conv), attn_o.reshape(n, d_attn),
        w_out_bf[:d_conv], w_out_bf[d_conv:], norm_ffn_g.reshape(1, d), rw, rb,
        tm=min(PROJ_TILE, n))

    bm = MOE_BLOCK
    nblk = n * TOP_K // bm + n_exp
    counts = cnt[0, :n_exp].astype(I32)
    padded = (counts + bm - 1) // bm * bm
    eid = jnp.arange(n_exp, dtype=I32)
    pad_end = jnp.sum(jnp.where(eid[None, :] <= eid[:, None], padded[None, :], 0), axis=1)
    pad_start = pad_end - padded
    start_of = jnp.sum(jnp.where(idx[None, :TOP_K] == eid[:, None, None],
                                 pad_start[:, None, None], 0), axis=0)
    pos = start_of + rank[:TOP_K]
    pos_chunks = pos.reshape(TOP_K, n // SC_CHUNK, SC_CHUNK).transpose(1, 0, 2)
    block_start = jnp.arange(nblk, dtype=I32) * bm
    block_expert = jnp.minimum(jnp.sum(block_start[:, None] >= pad_end[None, :], axis=1),
                               n_exp - 1).astype(I32)
    n_used = (pad_end[-1:] // bm).astype(I32)
    blk = jnp.arange(nblk, dtype=I32)
    first = jnp.logical_and(
        jnp.concatenate([jnp.ones((1,), bool), block_expert[1:] != block_expert[:-1]]),
        blk < n_used[0]).astype(I32)
    opened = jnp.sum(jnp.where(blk[None, :] <= blk[:, None], first[None, :], 0), axis=1)
    wslot = ((opened - 1) % 2).astype(I32)
    later_used = jnp.logical_and(eid[None, :] > eid[:, None], counts[None, :] > 0)
    next_used = jnp.min(jnp.where(later_used, eid[None, :], n_exp), axis=1)
    next_used = jnp.where(next_used == n_exp, -1, next_used).astype(I32)
    next_of_block = jnp.sum(jnp.where(block_expert[:, None] == eid[None, :], next_used[None, :], 0),
                            axis=1).astype(I32)
    on_expert = block_expert[:, None] == eid[None, :]
    rows_in_block = jnp.clip(
        jnp.sum(jnp.where(on_expert, (counts + pad_start)[None, :], 0), axis=1) - block_start, 0, bm)
    quarter = bm // MOE_QUARTERS
    quarters = ((rows_in_block + quarter - 1) // quarter).astype(I32)
    plan = (block_expert, first, wslot, next_of_block, quarters, n_used)

    d_ff = w_down.shape[1]
    bgu_p = b_gate_up.reshape(n_exp, d_ff // LANES, LANES, 2).transpose(0, 1, 3, 2).reshape(n_exp, 1, 2 * d_ff)
    r = d // (2 * LANES)
    xs = _sc_dispatch(hn.reshape(n, r, LANES), pos_chunks, nblk * bm)
    yb = _moe(plan, xs.reshape(nblk * bm * r, LANES), w_gate_up, bgu_p, w_down,
              b_down.reshape(n_exp, 1, d))
    y4 = _sc_gather(yb.reshape(nblk * bm, r, LANES), pos_chunks, n)
    out = _combine(hres, gates, y4.reshape(TOP_K, n * r, LANES))
    return out.reshape(b, s, d)


def kernel(x, norm_mix_g, w_in, conv_w, conv_b, conv_ln_g, conv_ln_b, q_norm_g, k_norm_g, lambda_qk,
           subln_g, rel_bias, w_out, norm_ffn_g, router_w, router_b, w_gate_up, b_gate_up, w_down,
           b_down):
    h = x
    for layer in range(norm_mix_g.shape[0]):
        h = _layer(h, layer, norm_mix_g[layer], w_in[layer], conv_w[layer], conv_b[layer],
                   conv_ln_g[layer], conv_ln_b[layer], q_norm_g[layer], k_norm_g[layer],
                   lambda_qk[layer], subln_g[layer], rel_bias, w_out[layer], norm_ffn_g[layer],
                   router_w[layer], router_b[layer], w_gate_up[layer], b_gate_up[layer],
                   w_down[layer], b_down[layer])
    return h
```
